```python
import math
import jax
import jax.numpy as jnp
from jax import lax
import numpy as np


D_MODEL = 2048
BATCH = 1
SEQ = 16384
DEPTH = 2

NSA_HEADS = 8
NSA_KV_HEADS = 2
NSA_HEAD_DIM = 64
NSA_GROUP = NSA_HEADS // NSA_KV_HEADS
NSA_WIDTH = NSA_HEADS * NSA_HEAD_DIM
NSA_KV_WIDTH = NSA_KV_HEADS * NSA_HEAD_DIM
CMP_LEN = 32
CMP_STRIDE = 16
CMP_HIDDEN = 256
SLC_BLOCK = 64
SLC_TOPK = 16
WINDOW = 512
Q_BLOCK = 128
ROPE_THETA = 500000.0
ROPE_DIMS = NSA_HEAD_DIM // 4

S5_WIDTH = 512
S5_GROUP_CH = 16
S5_GROUPS = S5_WIDTH // S5_GROUP_CH
S5_STATE = 64

GDN_HEADS = 8
GDN_HEAD_DIM = 128
GDN_WIDTH = GDN_HEADS * GDN_HEAD_DIM
GDN_CONV = 4
GDN_CHUNK = 64

MIX_WIDTH = NSA_WIDTH + S5_WIDTH + GDN_WIDTH
D_FF = 5504
FFN_CONV = 3
EPS = 1e-6
NEG_BIG = -1e30

IN_SPLITS = (NSA_WIDTH,) + (NSA_KV_WIDTH,) * 6 + (3 * NSA_HEADS, S5_WIDTH, 3 * GDN_WIDTH, GDN_WIDTH, GDN_HEADS, GDN_HEADS)
IN_COLS = sum(IN_SPLITS)

kernel_name = 'hymba_nsa_s5_gdn_convffn'


def rmsnorm(x, g):
    xf = x.astype(jnp.float32)
    y = xf * lax.rsqrt(jnp.mean(xf * xf, axis=-1, keepdims=True) + EPS)
    return (y * g.astype(jnp.float32)).astype(x.dtype)


def l2norm(x):
    xf = x.astype(jnp.float32)
    return xf * lax.rsqrt(jnp.sum(xf * xf, axis=-1, keepdims=True) + EPS)


def rope_partial(x, pos):
    half = ROPE_DIMS // 2
    inv = ROPE_THETA ** (-jnp.arange(half, dtype=jnp.float32) / half)
    ang = pos.astype(jnp.float32)[:, None] * inv[None, :]
    cos = jnp.cos(ang)[:, None, :]
    sin = jnp.sin(ang)[:, None, :]
    xr = x[..., :ROPE_DIMS].astype(jnp.float32)
    x1, x2 = xr[..., :half], xr[..., half:]
    rot = jnp.concatenate([x1 * cos - x2 * sin, x2 * cos + x1 * sin], axis=-1)
    return jnp.concatenate([rot.astype(x.dtype), x[..., ROPE_DIMS:]], axis=-1)


def causal_dwconv(x, w):
    width = w.shape[0]
    L = x.shape[1]
    xp = jnp.pad(x, ((0, 0), (width - 1, 0), (0, 0)))
    return sum(xp[:, j:j + L] * w[j] for j in range(width))


def masked_softmax(s, mask):
    s = jnp.where(mask, s.astype(jnp.float32), NEG_BIG)
    p = jnp.where(mask, jnp.exp(s - jnp.max(s, axis=-1, keepdims=True)), 0.0)
    return p / jnp.maximum(jnp.sum(p, axis=-1, keepdims=True), 1e-30)


def compress_blocks(win, pe, w1, w2):
    h = win.astype(jnp.float32) + pe[:, None, :]
    h = jnp.einsum('bnlhd,ldf->bnhf', h, w1)
    return jnp.einsum('bnhf,fd->bnhd', jax.nn.gelu(h), w2)


def nsa_mixer(q, kc_raw, vc_raw, ks, vs, kw, vw, gate_logits,
              q_norm, kc_norm, ks_norm, kw_norm, cmp_pe, cmp_k_w1, cmp_k_w2, cmp_v_w1, cmp_v_w2):
    B_, L, _ = q.shape
    H, HK, G, DH = NSA_HEADS, NSA_KV_HEADS, NSA_GROUP, NSA_HEAD_DIM
    scale = DH ** -0.5
    pos = jnp.arange(L, dtype=jnp.float32)
    heads = lambda t, h: t.reshape(B_, L, h, DH)
    q = rope_partial(rmsnorm(heads(q, H), q_norm), pos)
    ks = rope_partial(rmsnorm(heads(ks, HK), ks_norm), pos)
    kw = rope_partial(rmsnorm(heads(kw, HK), kw_norm), pos)
    vs = heads(vs, HK)
    vw = heads(vw, HK)

    n_cmp = (L - CMP_LEN) // CMP_STRIDE + 1
    cmp_start = jnp.arange(n_cmp) * CMP_STRIDE
    win_idx = cmp_start[:, None] + jnp.arange(CMP_LEN)[None, :]
    kc = compress_blocks(heads(kc_raw, HK)[:, win_idx], cmp_pe, cmp_k_w1, cmp_k_w2)
    vc = compress_blocks(heads(vc_raw, HK)[:, win_idx], cmp_pe, cmp_v_w1, cmp_v_w2)
    kc = rope_partial(rmsnorm(kc, kc_norm), (cmp_start + CMP_LEN // 2).astype(jnp.float32))
    cmp_end = cmp_start + CMP_LEN - 1

    n_slc = L // SLC_BLOCK
    slc_start = jnp.arange(n_slc) * SLC_BLOCK
    sel_map = ((cmp_start[:, None] <= slc_start[None, :] + SLC_BLOCK - 1)
               & (cmp_end[:, None] >= slc_start[None, :])).astype(jnp.float32)
    n_sel = min(SLC_TOPK, n_slc)
    ks_blocks = ks.reshape(B_, n_slc, SLC_BLOCK, HK, DH).transpose(0, 3, 1, 2, 4)
    vs_blocks = vs.reshape(B_, n_slc, SLC_BLOCK, HK, DH).transpose(0, 3, 1, 2, 4)
    kw_pad = jnp.pad(kw, ((0, 0), (WINDOW, 0), (0, 0), (0, 0)))
    vw_pad = jnp.pad(vw, ((0, 0), (WINDOW, 0), (0, 0), (0, 0)))
    gates = jax.nn.sigmoid(gate_logits.astype(jnp.float32)).reshape(B_, L, HK, G, 3)
    b_idx = jnp.arange(B_)[:, None, None, None]
    h_idx = jnp.arange(HK)[None, None, :, None]
    j_idx = jnp.arange(n_slc)

    def query_block(qb_i):
        s = qb_i * Q_BLOCK
        t = s + jnp.arange(Q_BLOCK)
        qg = lax.dynamic_slice_in_dim(q, s, Q_BLOCK, axis=1).reshape(B_, Q_BLOCK, HK, G, DH)
        gb = lax.dynamic_slice_in_dim(gates, s, Q_BLOCK, axis=1)

        sc = jnp.einsum('bqhgd,bnhd->bhgqn', qg, kc) * scale
        pc = masked_softmax(sc, cmp_end[None, :] <= t[:, None])
        o_c = jnp.einsum('bhgqn,bnhd->bqhgd', pc, vc)

        imp = jnp.einsum('bhgqn,nj->bqhj', pc, sel_map)
        cur = t // SLC_BLOCK
        forced = (j_idx[None, :] == 0) | (j_idx[None, :] == cur[:, None]) | (j_idx[None, :] == cur[:, None] - 1)
        valid = j_idx[None, :] <= cur[:, None]
        imp = jnp.where(valid[None, :, None, :], jnp.where(forced[None, :, None, :], jnp.inf, imp), -jnp.inf)
        _, sel = lax.top_k(imp, n_sel)
        kb = ks_blocks[b_idx, h_idx, sel].reshape(B_, Q_BLOCK, HK, n_sel * SLC_BLOCK, DH)
        vb = vs_blocks[b_idx, h_idx, sel].reshape(B_, Q_BLOCK, HK, n_sel * SLC_BLOCK, DH)
        kpos = (sel[..., None] * SLC_BLOCK + jnp.arange(SLC_BLOCK)).reshape(B_, Q_BLOCK, HK, n_sel * SLC_BLOCK)
        mask_s = (kpos <= t[None, :, None, None]).transpose(0, 2, 1, 3)[:, :, None]
        ss = jnp.einsum('bqhgd,bqhkd->bhgqk', qg, kb) * scale
        o_s = jnp.einsum('bhgqk,bqhkd->bqhgd', masked_softmax(ss, mask_s), vb)

        kwb = lax.dynamic_slice_in_dim(kw_pad, s, Q_BLOCK + WINDOW, axis=1)
        vwb = lax.dynamic_slice_in_dim(vw_pad, s, Q_BLOCK + WINDOW, axis=1)
        kpos_w = s - WINDOW + jnp.arange(Q_BLOCK + WINDOW)
        mask_w = ((kpos_w[None, :] <= t[:, None]) & (kpos_w[None, :] > t[:, None] - WINDOW)
                  & (kpos_w[None, :] >= 0))
        sw = jnp.einsum('bqhgd,bkhd->bhgqk', qg, kwb) * scale
        o_w = jnp.einsum('bhgqk,bkhd->bqhgd', masked_softmax(sw, mask_w), vwb)

        out = gb[..., 0, None] * o_c + gb[..., 1, None] * o_s + gb[..., 2, None] * o_w
        return out.reshape(B_, Q_BLOCK, NSA_WIDTH)

    out = lax.map(query_block, jnp.arange(L // Q_BLOCK))
    return out.transpose(1, 0, 2, 3).reshape(B_, L, NSA_WIDTH).astype(q.dtype)


def s5_mixer(u, lam_re, lam_im, log_dt, b_re, b_im, c_re, c_im, d_skip, w_glu):
    B_, L, _ = u.shape
    f32 = lambda t: t.astype(jnp.float32)
    lam_re, lam_im, b_re, b_im, c_re, c_im = map(f32, (lam_re, lam_im, b_re, b_im, c_re, c_im))
    ug = f32(u).reshape(B_, L, S5_GROUPS, S5_GROUP_CH)
    dt = jnp.exp(f32(log_dt))[:, None]
    mag = jnp.exp(lam_re * dt)
    lb_re = mag * jnp.cos(lam_im * dt)
    lb_im = mag * jnp.sin(lam_im * dt)
    den = lam_re * lam_re + lam_im * lam_im
    nr, ni = lb_re - 1.0, lb_im
    coef_re = (nr * lam_re + ni * lam_im) / den
    coef_im = (ni * lam_re - nr * lam_im) / den
    bb_re = coef_re[..., None] * b_re - coef_im[..., None] * b_im
    bb_im = coef_re[..., None] * b_im + coef_im[..., None] * b_re
    bu_re = jnp.einsum('blgh,gph->blgp', ug, bb_re)
    bu_im = jnp.einsum('blgh,gph->blgp', ug, bb_im)
    a_re = jnp.broadcast_to(lb_re, bu_re.shape)
    a_im = jnp.broadcast_to(lb_im, bu_im.shape)

    def combine(e1, e2):
        a1r, a1i, b1r, b1i = e1
        a2r, a2i, b2r, b2i = e2
        return (a2r * a1r - a2i * a1i, a2r * a1i + a2i * a1r,
                a2r * b1r - a2i * b1i + b2r, a2r * b1i + a2i * b1r + b2i)

    _, _, xr, xi = lax.associative_scan(combine, (a_re, a_im, bu_re, bu_im), axis=1)
    y = jnp.einsum('blgp,ghp->blgh', xr, c_re) - jnp.einsum('blgp,ghp->blgh', xi, c_im)
    y = y + f32(d_skip).reshape(S5_GROUPS, S5_GROUP_CH) * ug
    y = jax.nn.gelu(y.reshape(B_, L, S5_WIDTH))
    y = y * jax.nn.sigmoid(y @ f32(w_glu))
    return y.astype(u.dtype)


def chunk_gated_delta_rule(q, k, v, g, beta):
    B_, L, H, DK = q.shape
    DV = v.shape[-1]
    C = GDN_CHUNK
    N = L // C

    def chunks(t):
        return t.astype(jnp.float32).reshape(B_, N, C, H, -1).transpose(1, 0, 3, 2, 4)

    q, k, v = chunks(q), chunks(k), chunks(v)
    beta = chunks(beta[..., None])[..., 0]
    gc = jnp.cumsum(chunks(g[..., None])[..., 0], axis=-1)
    causal = jnp.tril(jnp.ones((C, C), bool))
    strict = jnp.tril(jnp.ones((C, C), bool), -1)
    diff = gc[..., :, None] - gc[..., None, :]
    decay = jnp.where(causal, jnp.exp(jnp.where(causal, diff, 0.0)), 0.0)
    k_beta = k * beta[..., None]
    a_mat = jnp.where(strict, jnp.einsum('nbhcd,nbhed->nbhce', k_beta, k) * decay, 0.0)
    eye = jnp.broadcast_to(jnp.eye(C, dtype=jnp.float32), a_mat.shape)
    t_mat = lax.linalg.triangular_solve(a_mat, eye, left_side=True, lower=True, unit_diagonal=True)
    value = t_mat @ (v * beta[..., None])
    k_cum = t_mat @ (k_beta * jnp.exp(gc)[..., None])
    qk = jnp.einsum('nbhcd,nbhed->nbhce', q, k) * decay

    def step(state, inp):
        q_i, k_i, val_i, kc_i, qk_i, g_i = inp
        v_new = val_i - kc_i @ state
        o_i = (q_i * jnp.exp(g_i)[..., None]) @ state + qk_i @ v_new
        g_last = g_i[..., -1:]
        state = state * jnp.exp(g_last)[..., None] + jnp.einsum(
            'bhcd,bhce->bhde', k_i * jnp.exp(g_last - g_i)[..., None], v_new)
        return state, o_i

    state0 = jnp.zeros((B_, H, DK, DV), jnp.float32)
    _, o = lax.scan(step, state0, (q, k, value, k_cum, qk, gc))
    return o.transpose(1, 0, 3, 2, 4).reshape(B_, L, H, DV)


def gdn_mixer(qkv, z, a, b, conv_w, a_log, dt_bias, out_norm):
    B_, L, _ = qkv.shape
    qkv = jax.nn.silu(causal_dwconv(qkv, conv_w))
    q, k, v = jnp.split(qkv, 3, axis=-1)
    q = l2norm(q.reshape(B_, L, GDN_HEADS, GDN_HEAD_DIM)) * (GDN_HEAD_DIM ** -0.5)
    k = l2norm(k.reshape(B_, L, GDN_HEADS, GDN_HEAD_DIM))
    v = v.reshape(B_, L, GDN_HEADS, GDN_HEAD_DIM)
    beta = jax.nn.sigmoid(b.astype(jnp.float32))
    g = -jnp.exp(a_log.astype(jnp.float32)) * jax.nn.softplus(a.astype(jnp.float32) + dt_bias.astype(jnp.float32))
    o = chunk_gated_delta_rule(q, k, v, g, beta)
    o = rmsnorm(o, out_norm) * jax.nn.silu(z.astype(jnp.float32).reshape(B_, L, GDN_HEADS, GDN_HEAD_DIM))
    return o.reshape(B_, L, GDN_WIDTH).astype(qkv.dtype)


def conv_ffn(x, w_in, conv_w, conv_b, w_out):
    h = causal_dwconv(x @ w_in, conv_w) + conv_b
    gate, up = jnp.split(h, 2, axis=-1)
    return (jax.nn.silu(gate) * up) @ w_out


def setup_inputs(seed: int = 0) -> dict:
    key = jax.random.key(seed)
    keys = iter(jax.random.split(key, 48))

    def normal(shape, scale):
        return jax.random.normal(next(keys), shape, jnp.float32) * scale

    def gain(shape):
        return 1.0 + 0.02 * jax.random.normal(next(keys), shape, jnp.float32)

    def uniform(shape, lo, hi):
        return jax.random.uniform(next(keys), shape, jnp.float32, lo, hi)

    Ly, dh = DEPTH, NSA_HEAD_DIM
    n_idx = jnp.arange(S5_STATE, dtype=jnp.float32)
    dt_gdn = jnp.exp(uniform((Ly, GDN_HEADS), math.log(1e-3), math.log(1e-1)))
    return {
        'x': normal((BATCH, SEQ, D_MODEL), 1.0),
        'attn_norm': gain((Ly, D_MODEL)),
        'w_in': normal((Ly, D_MODEL, IN_COLS), D_MODEL ** -0.5),
        'nsa_q_norm': gain((Ly, dh)),
        'nsa_kc_norm': gain((Ly, dh)),
        'nsa_ks_norm': gain((Ly, dh)),
        'nsa_kw_norm': gain((Ly, dh)),
        'cmp_pe': normal((Ly, CMP_LEN, dh), 0.5),
        'cmp_k_w1': normal((Ly, CMP_LEN, dh, CMP_HIDDEN), (CMP_LEN * dh) ** -0.5),
        'cmp_k_w2': normal((Ly, CMP_HIDDEN, dh), CMP_HIDDEN ** -0.5),
        'cmp_v_w1': normal((Ly, CMP_LEN, dh, CMP_HIDDEN), (CMP_LEN * dh) ** -0.5),
        'cmp_v_w2': normal((Ly, CMP_HIDDEN, dh), CMP_HIDDEN ** -0.5),
        'nsa_out_norm': gain((Ly, NSA_WIDTH)),
        's5_lam_re': -0.5 + normal((Ly, S5_GROUPS, S5_STATE), 0.01),
        's5_lam_im': jnp.pi * n_idx + normal((Ly, S5_GROUPS, S5_STATE), 0.01),
        's5_log_dt': uniform((Ly, S5_GROUPS), math.log(1e-3), math.log(1e-1)),
        's5_b_re': normal((Ly, S5_GROUPS, S5_STATE, S5_GROUP_CH), (2 * S5_GROUP_CH) ** -0.5),
        's5_b_im': normal((Ly, S5_GROUPS, S5_STATE, S5_GROUP_CH), (2 * S5_GROUP_CH) ** -0.5),
        's5_c_re': normal((Ly, S5_GROUPS, S5_GROUP_CH, S5_STATE), S5_STATE ** -0.5),
        's5_c_im': normal((Ly, S5_GROUPS, S5_GROUP_CH, S5_STATE), S5_STATE ** -0.5),
        's5_d': normal((Ly, S5_WIDTH), 0.5),
        's5_w_glu': normal((Ly, S5_WIDTH, S5_WIDTH), S5_WIDTH ** -0.5),
        's5_out_norm': gain((Ly, S5_WIDTH)),
        'gdn_conv': normal((Ly, GDN_CONV, 3 * GDN_WIDTH), GDN_CONV ** -0.5),
        'gdn_a_log': jnp.log(uniform((Ly, GDN_HEADS), 1.0, 16.0)),
        'gdn_dt_bias': dt_gdn + jnp.log(-jnp.expm1(-dt_gdn)),
        'gdn_norm': gain((Ly, GDN_HEAD_DIM)),
        'w_out': normal((Ly, MIX_WIDTH, D_MODEL), MIX_WIDTH ** -0.5),
        'ffn_norm': gain((Ly, D_MODEL)),
        'ffn_w_in': normal((Ly, D_MODEL, 2 * D_FF), D_MODEL ** -0.5),
        'ffn_conv': normal((Ly, FFN_CONV, 2 * D_FF), FFN_CONV ** -0.5),
        'ffn_conv_b': normal((Ly, 2 * D_FF), 0.02),
        'ffn_w_out': normal((Ly, D_FF, D_MODEL), D_FF ** -0.5),
    }


def reference(x, attn_norm, w_in, nsa_q_norm, nsa_kc_norm, nsa_ks_norm, nsa_kw_norm, cmp_pe,
              cmp_k_w1, cmp_k_w2, cmp_v_w1, cmp_v_w2, nsa_out_norm, s5_lam_re, s5_lam_im, s5_log_dt,
              s5_b_re, s5_b_im, s5_c_re, s5_c_im, s5_d, s5_w_glu, s5_out_norm, gdn_conv, gdn_a_log,
              gdn_dt_bias, gdn_norm, w_out, ffn_norm, ffn_w_in, ffn_conv, ffn_conv_b, ffn_w_out):
    split_points = [int(p) for p in np.cumsum(IN_SPLITS)[:-1]]
    for l in range(DEPTH):
        h = rmsnorm(x, attn_norm[l])
        (q, kc, vc, ks, vs, kw, vw, nsa_gates, u, qkv, z, a, b) = jnp.split(h @ w_in[l], split_points, axis=-1)
        y_a = nsa_mixer(q, kc, vc, ks, vs, kw, vw, nsa_gates,
                        nsa_q_norm[l], nsa_kc_norm[l], nsa_ks_norm[l], nsa_kw_norm[l], cmp_pe[l],
                        cmp_k_w1[l], cmp_k_w2[l], cmp_v_w1[l], cmp_v_w2[l])
        y_b = s5_mixer(u, s5_lam_re[l], s5_lam_im[l], s5_log_dt[l], s5_b_re[l], s5_b_im[l],
                       s5_c_re[l], s5_c_im[l], s5_d[l], s5_w_glu[l])
        y_c = gdn_mixer(qkv, z, a, b, gdn_conv[l], gdn_a_log[l], gdn_dt_bias[l], gdn_norm[l])
        mix = jnp.concatenate([rmsnorm(y_a, nsa_out_norm[l]), rmsnorm(y_b, s5_out_norm[l]), y_c], axis=-1)
        x = x + (mix @ w_out[l]).astype(x.dtype)
        x = x + conv_ffn(rmsnorm(x, ffn_norm[l]), ffn_w_in[l], ffn_conv[l], ffn_conv_b[l], ffn_w_out[l]).astype(x.dtype)
    return x
```

```python
import functools
import math

import jax
import jax.numpy as jnp
import numpy as np
from jax import lax
from jax.experimental import pallas as pl
from jax.experimental.pallas import tpu as pltpu

F32 = jnp.float32
BF16 = jnp.bfloat16
HIGHEST = lax.Precision.HIGHEST

D_MODEL = 2048
NSA_HEADS = 8
NSA_KV = 2
NSA_G = NSA_HEADS // NSA_KV
NSA_DH = 64
CMP_LEN = 32
CMP_STRIDE = 16
SLC_BLOCK = 64
SLC_TOPK = 16
WINDOW = 512
ROPE_THETA = 500000.0
ROPE_DIMS = NSA_DH // 4
S5_WIDTH = 512
S5_CH = 16
S5_GROUPS = 32
S5_STATE = 64
S5_CHUNK = 32
GDN_HEADS = 8
GDN_DH = 128
GDN_WIDTH = 1024
GDN_CHUNK = 64
D_FF = 5504
D_FF_PAD = 5632
EPS = 1e-6
NEG_BIG = -1e30

LANES = 128
TQ = 128
TK = 512
VMEM_LIMIT = 56 * 1024 * 1024

C_Q, C_KS, C_KW, C_VS, C_VW = 0, 512, 640, 768, 896
C_Z, C_U, C_KC, C_VC, C_QKV, C_SM = 1024, 2048, 2560, 2688, 2816, 5888
NSA_PREP_W = C_Z
PROJ_COLS = 6144


def _cparams(sem):
    return pltpu.CompilerParams(dimension_semantics=sem, vmem_limit_bytes=VMEM_LIMIT)


def _dot(a, b, precision=None):
    return jnp.dot(a, b, preferred_element_type=F32, precision=precision)


def _dot_nt(a, b, precision=None):
    return lax.dot_general(a, b, (((1,), (1,)), ((), ())), preferred_element_type=F32, precision=precision)


def _gelu(x):
    return x * (0.5 * (1.0 + jnp.tanh(math.sqrt(2.0 / math.pi) * (x + 0.044715 * (x * x * x)))))


def _sigmoid(x):
    return 1.0 / (1.0 + jnp.exp(-x))


def _rms_matmul_kernel(x_ref, g_ref, w_ref, o_ref, xn_ref):
    @pl.when(pl.program_id(1) == 0)
    def _():
        xf = x_ref[...]
        ms = jnp.mean(xf * xf, axis=-1, keepdims=True)
        xn_ref[...] = (xf * lax.rsqrt(ms + EPS) * g_ref[...]).astype(BF16)

    o_ref[...] = _dot(xn_ref[...], w_ref[...])


def _rms_matmul(x, gain, w, tm=512, tn=1024):
    L, D = x.shape
    N = w.shape[1]
    return pl.pallas_call(
        _rms_matmul_kernel,
        grid=(L // tm, N // tn),
        in_specs=[
            pl.BlockSpec((tm, D), lambda i, j: (i, 0)),
            pl.BlockSpec((1, D), lambda i, j: (0, 0)),
            pl.BlockSpec((D, tn), lambda i, j: (0, j)),
        ],
        out_specs=pl.BlockSpec((tm, tn), lambda i, j: (i, j)),
        out_shape=jax.ShapeDtypeStruct((L, N), F32),
        scratch_shapes=[pltpu.VMEM((tm, D), BF16)],
        compiler_params=_cparams(("arbitrary", "arbitrary")),
        name="rms_proj",
    )(x, gain.reshape(1, D), w)


def _rope_slab(x, cos, sin):
    d = lax.broadcasted_iota(jnp.int32, x.shape, 1) & (NSA_DH - 1)
    half = ROPE_DIMS // 2
    partner = jnp.where(d < half, pltpu.roll(x, LANES - half, 1), pltpu.roll(x, half, 1))
    return x * cos + partner * sin


def _nsa_prep_kernel(x_ref, g_ref, cos_ref, sin_ref, bsum_ref, o_ref):
    cos = cos_ref[...]
    sin = sin_ref[...]
    bsum = bsum_ref[...]
    n_rot = (C_VS - C_Q) // LANES
    for s in range(NSA_PREP_W // LANES):
        sl = slice(s * LANES, (s + 1) * LANES)
        x = x_ref[:, sl]
        if s < n_rot:
            ms = _dot(x * x, bsum, precision=HIGHEST)
            y = x * lax.rsqrt(ms + EPS) * g_ref[:, sl]
            y = _rope_slab(y, cos, sin)
            if s < (C_KS - C_Q) // LANES:
                y = y * (NSA_DH ** -0.5)
            o_ref[:, sl] = y.astype(BF16)
        else:
            o_ref[:, sl] = x.astype(BF16)


def _nsa_prep(proj, gains, cos, sin, tm=512):
    L = proj.shape[0]
    W = NSA_PREP_W
    bsum = jnp.asarray(np.kron(np.eye(2), np.ones((NSA_DH, NSA_DH))) / NSA_DH, F32)
    return pl.pallas_call(
        _nsa_prep_kernel,
        grid=(L // tm,),
        in_specs=[
            pl.BlockSpec((tm, W), lambda i: (i, 0)),
            pl.BlockSpec((1, W), lambda i: (0, 0)),
            pl.BlockSpec((tm, LANES), lambda i: (i, 0)),
            pl.BlockSpec((tm, LANES), lambda i: (i, 0)),
            pl.BlockSpec((LANES, LANES), lambda i: (0, 0)),
        ],
        out_specs=pl.BlockSpec((tm, W), lambda i: (i, 0)),
        out_shape=jax.ShapeDtypeStruct((L, W), BF16),
        compiler_params=_cparams(("arbitrary",)),
        name="nsa_prep",
    )(proj, gains, cos, sin, bsum)


def _cmp_kernel(hbk_ref, hbv_ref, pea_ref, peb_ref, w1k_ref, w2k_ref, w1v_ref, w2v_ref,
                gk_ref, cos_ref, sin_ref, kc_ref, vc_ref):
    nb = hbk_ref.shape[1]
    half_w = hbk_ref.shape[2]
    pea = pea_ref[...]
    peb = peb_ref[...]

    def mlp(hb, w1_ref, w2_ref):
        a = (hb + pea).astype(BF16)
        b = (hb + peb).astype(BF16)
        p1 = _dot(a, w1_ref[0:half_w, :])
        p2 = _dot(b, w1_ref[half_w:2 * half_w, :])
        h = p1 + pltpu.roll(p2, nb - 1, 0)
        return _dot(_gelu(h).astype(BF16), w2_ref[...])

    rid = lax.broadcasted_iota(jnp.int32, (nb, LANES), 0)
    real = rid < nb - 1
    kc = jnp.where(real, mlp(hbk_ref[0], w1k_ref, w2k_ref), 0.0)
    vc = jnp.where(real, mlp(hbv_ref[0], w1v_ref, w2v_ref), 0.0)
    ms = jnp.sum(kc * kc, axis=-1, keepdims=True) * (1.0 / NSA_DH)
    kcn = kc * lax.rsqrt(ms + EPS) * gk_ref[...]
    kcn = _rope_slab(kcn, cos_ref[...], sin_ref[...])
    kc_ref[0] = kcn[:, 0:NSA_DH].astype(BF16)
    vc_ref[0] = vc[:, 0:NSA_DH].astype(BF16)


def _nsa_compress(hbk, hbv, pea, peb, w1k, w2k, w1v, w2v, gk, cos, sin):
    HK, nb, half_w = hbk.shape
    hid = w1k.shape[1]
    full = lambda shape: pl.BlockSpec(shape, lambda h: (0,) * len(shape))
    return pl.pallas_call(
        _cmp_kernel,
        grid=(HK,),
        in_specs=[
            pl.BlockSpec((1, nb, half_w), lambda h: (h, 0, 0)),
            pl.BlockSpec((1, nb, half_w), lambda h: (h, 0, 0)),
            full((1, half_w)), full((1, half_w)),
            full((2 * half_w, hid)), full((hid, LANES)),
            full((2 * half_w, hid)), full((hid, LANES)),
            full((1, LANES)), full((nb, LANES)), full((nb, LANES)),
        ],
        out_specs=[pl.BlockSpec((1, nb, NSA_DH), lambda h: (h, 0, 0)),
                   pl.BlockSpec((1, nb, NSA_DH), lambda h: (h, 0, 0))],
        out_shape=[jax.ShapeDtypeStruct((HK, nb, NSA_DH), BF16)] * 2,
        compiler_params=_cparams(("arbitrary",)),
        name="nsa_compress",
    )(hbk, hbv, pea, peb, w1k, w2k, w1v, w2v, gk, cos, sin)


def _nsa_attn_kernel(qT_ref, kc_ref, vcT_ref, smap_ref, ksaug_ref, vsT_ref, kwp_ref, vwTp_ref, gT_ref,
                     o_ref, qaug_ref, bias_ref, *, n_grp):
    i = pl.program_id(1)
    s0 = i * TQ
    nb = kc_ref.shape[1]
    ns = smap_ref.shape[0]
    GW = NSA_G * TQ
    qT = qT_ref[0, 0]
    tq = s0 + (lax.broadcasted_iota(jnp.int32, (1, GW), 1) & (TQ - 1))

    sc = _dot(kc_ref[0], qT)
    nrow = lax.broadcasted_iota(jnp.int32, (nb, GW), 0)
    maskc = (nrow * CMP_STRIDE + (CMP_LEN - 1)) <= tq
    sc = jnp.where(maskc, sc, NEG_BIG)
    mc = jnp.max(sc, axis=0, keepdims=True)
    pc = jnp.where(maskc, jnp.exp(sc - mc), 0.0)
    lc = jnp.sum(pc, axis=0, keepdims=True)
    pc = pc / jnp.maximum(lc, 1e-30)
    ocT = _dot(vcT_ref[0], pc.astype(BF16))

    pcs = pc[:, 0:TQ]
    for g in range(1, NSA_G):
        pcs = pcs + pc[:, g * TQ:(g + 1) * TQ]
    hi = pcs.astype(BF16)
    lo = (pcs - hi.astype(F32)).astype(BF16)
    smap = smap_ref[...]
    imp = _dot(smap, hi) + _dot(smap, lo)

    jrow = lax.broadcasted_iota(jnp.int32, (ns, TQ), 0)
    cur = (s0 + lax.broadcasted_iota(jnp.int32, (ns, TQ), 1)) // SLC_BLOCK
    val = jnp.where(jrow == 0, jnp.inf, jnp.where(jrow == cur, jnp.inf, jnp.where(jrow == cur - 1, jnp.inf, imp)))
    val = jnp.where(jrow <= cur, val, -jnp.inf)
    bias = jnp.full((ns, TQ), NEG_BIG, F32)
    for _ in range(min(SLC_TOPK, ns)):
        mx = jnp.max(val, axis=0, keepdims=True)
        idx = jnp.min(jnp.where(val == mx, jrow, ns), axis=0, keepdims=True)
        hit = jrow == idx
        bias = jnp.where(hit, 0.0, bias)
        val = jnp.where(hit, -jnp.inf, val)
    bias16 = bias.astype(BF16)
    for g in range(NSA_G):
        bias_ref[:, g * TQ:(g + 1) * TQ] = bias16
    qaug_ref[0:NSA_DH, :] = qT

    tiles_per_grp = (n_grp * SLC_BLOCK) // TK
    n_tiles = (s0 + TQ + TK - 1) // TK

    def tile_body(kt, carry):
        m, l, acc = carry

        @pl.when(kt % tiles_per_grp == 0)
        def _():
            j0 = pl.multiple_of((kt // tiles_per_grp) * n_grp, n_grp)
            qaug_ref[NSA_DH:NSA_DH + n_grp, :] = bias_ref[pl.ds(j0, n_grp), :]

        k0 = pl.multiple_of(kt * TK, TK)
        s = _dot(ksaug_ref[0, pl.ds(k0, TK), :], qaug_ref[...])
        kpos = k0 + lax.broadcasted_iota(jnp.int32, (TK, GW), 0)
        s = jnp.where(kpos <= tq, s, NEG_BIG)
        mn = jnp.maximum(m, jnp.max(s, axis=0, keepdims=True))
        alpha = jnp.exp(m - mn)
        p = jnp.exp(s - mn)
        l = alpha * l + jnp.sum(p, axis=0, keepdims=True)
        acc = alpha * acc + _dot(vsT_ref[0, :, pl.ds(k0, TK)], p.astype(BF16))
        return mn, l, acc

    init = (jnp.full((1, GW), NEG_BIG, F32), jnp.zeros((1, GW), F32), jnp.zeros((NSA_DH, GW), F32))
    _, ls, accs = lax.fori_loop(0, n_tiles, tile_body, init)
    osT = accs / ls

    WK = WINDOW + TQ
    r0 = pl.multiple_of(s0, TQ)
    sw = _dot(kwp_ref[0, pl.ds(r0, WK), :], qT)
    kposw = s0 - WINDOW + lax.broadcasted_iota(jnp.int32, (WK, GW), 0)
    sw = jnp.where(kposw <= tq, jnp.where(kposw > tq - WINDOW, jnp.where(kposw >= 0, sw, NEG_BIG), NEG_BIG), NEG_BIG)
    mw = jnp.max(sw, axis=0, keepdims=True)
    pw = jnp.exp(sw - mw)
    lw = jnp.sum(pw, axis=0, keepdims=True)
    owT = _dot(vwTp_ref[0, :, pl.ds(r0, WK)], pw.astype(BF16)) / lw

    gates = _sigmoid(gT_ref[0])
    for g in range(NSA_G):
        sl = slice(g * TQ, (g + 1) * TQ)
        o_ref[0, 0, :, sl] = (gates[3 * g:3 * g + 1] * ocT[:, sl] + gates[3 * g + 1:3 * g + 2] * osT[:, sl]
                              + gates[3 * g + 2:3 * g + 3] * owT[:, sl])


def _nsa_attention(qT, kc, vcT, smapT, ksaug, vsT, kwp, vwTp, gT, n_grp):
    HK, nQ, _, GW = qT.shape
    nb = kc.shape[1]
    ns = smapT.shape[0]
    L = vsT.shape[2]
    KA = ksaug.shape[2]
    per_hk = lambda shape: pl.BlockSpec((1,) + shape, lambda h, i: (h, 0, 0))
    return pl.pallas_call(
        functools.partial(_nsa_attn_kernel, n_grp=n_grp),
        grid=(HK, nQ),
        in_specs=[
            pl.BlockSpec((1, 1, NSA_DH, GW), lambda h, i: (h, i, 0, 0)),
            per_hk((nb, NSA_DH)),
            per_hk((NSA_DH, nb)),
            pl.BlockSpec((ns, nb), lambda h, i: (0, 0)),
            per_hk((L, KA)),
            per_hk((NSA_DH, L)),
            per_hk((L + WINDOW, NSA_DH)),
            per_hk((NSA_DH, L + WINDOW)),
            pl.BlockSpec((1, 16, TQ), lambda h, i: (h, 0, i)),
        ],
        out_specs=pl.BlockSpec((1, 1, NSA_DH, GW), lambda h, i: (h, i, 0, 0)),
        out_shape=jax.ShapeDtypeStruct((HK, nQ, NSA_DH, GW), F32),
        scratch_shapes=[pltpu.VMEM((KA, GW), BF16), pltpu.VMEM((ns, GW), BF16)],
        compiler_params=_cparams(("arbitrary", "arbitrary")),
        name="nsa_attn",
    )(qT, kc, vcT, smapT, ksaug, vsT, kwp, vwTp, gT)


def _s5_kernel(u_ref, lbr_ref, lbi_ref, ctr_ref, cti_ref, btr_ref, bti_ref, bsr_ref, bsi_ref, y_ref, lm_ref):
    T = S5_CHUNK
    W = T * S5_CH
    ar = lbr_ref[0]
    ai = lbi_ref[0]
    delta = lax.broadcasted_iota(jnp.int32, (W, S5_STATE), 0) // S5_CH

    def powers(e):
        pr = jnp.ones((W, S5_STATE), F32)
        pi = jnp.zeros((W, S5_STATE), F32)
        fr, fi = ar, ai
        for b in range(T.bit_length() - 1):
            bit = ((e >> b) & 1) == 1
            nr = pr * fr - pi * fi
            ni = pr * fi + pi * fr
            pr = jnp.where(bit, nr, pr)
            pi = jnp.where(bit, ni, pi)
            fr, fi = fr * fr - fi * fi, 2.0 * fr * fi
        return pr, pi, fr, fi

    pwr, pwi, aTr, aTi = powers(delta)
    rvr, rvi, _, _ = powers(T - 1 - delta)
    ctr, cti = ctr_ref[0], cti_ref[0]
    car = ctr * pwr - cti * pwi
    cai = ctr * pwi + cti * pwr

    kw = _dot_nt(bsr_ref[0], car, HIGHEST) - _dot_nt(bsi_ref[0], cai, HIGHEST)
    lane = lax.broadcasted_iota(jnp.int32, (S5_CH, W), 1)
    for tau in range(T):
        sh = S5_CH * tau
        blk = kw if tau == 0 else jnp.where(lane >= sh, pltpu.roll(kw, sh, 1), 0.0)
        lm_ref[sh:sh + S5_CH, :] = blk.astype(BF16)

    u = u_ref[0]
    nc = u.shape[0]
    y = _dot(u, lm_ref[...])

    btr, bti = btr_ref[0], bti_ref[0]
    sr = _dot(u, (rvr * btr - rvi * bti).astype(BF16))
    si = _dot(u, (rvr * bti + rvi * btr).astype(BF16))
    rowc = lax.broadcasted_iota(jnp.int32, (nc, S5_STATE), 0)
    fr, fi = aTr, aTi
    step = 1
    while step < nc:
        shr = jnp.where(rowc >= step, pltpu.roll(sr, step, 0), 0.0)
        shi = jnp.where(rowc >= step, pltpu.roll(si, step, 0), 0.0)
        sr, si = sr + fr * shr - fi * shi, si + fr * shi + fi * shr
        fr, fi = fr * fr - fi * fi, 2.0 * fr * fi
        step *= 2
    xr = jnp.where(rowc >= 1, pltpu.roll(sr, 1, 0), 0.0)
    xi = jnp.where(rowc >= 1, pltpu.roll(si, 1, 0), 0.0)
    c1r = car * ar - cai * ai
    c1i = car * ai + cai * ar
    y = y + _dot_nt(xr.astype(BF16), c1r.astype(BF16)) - _dot_nt(xi.astype(BF16), c1i.astype(BF16))
    y_ref[0] = y


def _s5_scan(ug, lbr, lbi, ctr, cti, btr, bti, bsr, bsi):
    G, nc, W = ug.shape
    grp = lambda shape: pl.BlockSpec((1,) + shape, lambda g: (g, 0, 0))
    return pl.pallas_call(
        _s5_kernel,
        grid=(G,),
        in_specs=[grp((nc, W)), grp((1, S5_STATE)), grp((1, S5_STATE)),
                  grp((W, S5_STATE)), grp((W, S5_STATE)), grp((W, S5_STATE)), grp((W, S5_STATE)),
                  grp((S5_CH, S5_STATE)), grp((S5_CH, S5_STATE))],
        out_specs=grp((nc, W)),
        out_shape=jax.ShapeDtypeStruct((G, nc, W), F32),
        scratch_shapes=[pltpu.VMEM((W, W), BF16)],
        compiler_params=_cparams(("arbitrary",)),
        name="s5_scan",
    )(ug, lbr, lbi, ctr, cti, btr, bti, bsr, bsi)


def _small_kernel(x_ref, p_ref, o_ref):
    x = x_ref[...]
    lane = lax.broadcasted_iota(jnp.int32, x.shape, 1)
    z = x + p_ref[0:1, :]
    softplus = jnp.maximum(z, 0.0) + jnp.log(1.0 + jnp.exp(-jnp.abs(z)))
    gdec = p_ref[1:2, :] * softplus
    beta = _sigmoid(x)
    o_ref[...] = jnp.where(lane < 24, x, jnp.where(lane < 32, gdec, beta))


def _small(proj, params, tm=1024):
    L = proj.shape[0]
    cb = C_SM // LANES
    return pl.pallas_call(
        _small_kernel,
        grid=(L // tm,),
        in_specs=[pl.BlockSpec((tm, LANES), lambda i: (i, cb)), pl.BlockSpec((8, LANES), lambda i: (0, 0))],
        out_specs=pl.BlockSpec((tm, LANES), lambda i: (i, 0)),
        out_shape=jax.ShapeDtypeStruct((L, LANES), F32),
        compiler_params=_cparams(("arbitrary",)),
        name="small_cols",
    )(proj, params)


def _gdn_prep_kernel(x_ref, cw_ref, o_ref, carry_ref):
    i = pl.program_id(0)
    j = pl.program_id(1)
    x = x_ref[...]
    tm = x.shape[0]
    @pl.when(i == 0)
    def _():
        carry_ref[j] = jnp.zeros((8, LANES), F32)

    prev = carry_ref[j]
    carry_ref[j] = x[tm - 8:tm, :]
    rid = lax.broadcasted_iota(jnp.int32, x.shape, 0)
    taps = cw_ref.shape[0]
    y = cw_ref[taps - 1:taps, :] * x
    for back in range(1, taps):
        sh = pltpu.roll(x, back, 0)
        for r in range(back):
            sh = jnp.where(rid == r, prev[8 - back + r:8 - back + r + 1, :], sh)
        y = y + cw_ref[taps - 1 - back:taps - back, :] * sh
    y = y * _sigmoid(y)
    ss = jnp.sum(y * y, axis=-1, keepdims=True)
    scale = jnp.where(j < GDN_HEADS, GDN_DH ** -0.5, 1.0)
    yn = y * lax.rsqrt(ss + EPS) * scale
    o_ref[...] = jnp.where(j < 2 * GDN_HEADS, yn, y)


def _gdn_prep(proj, conv_w, tm=1024):
    L = proj.shape[0]
    nslab = 3 * GDN_WIDTH // LANES
    cb = C_QKV // LANES
    return pl.pallas_call(
        _gdn_prep_kernel,
        grid=(L // tm, nslab),
        in_specs=[pl.BlockSpec((tm, LANES), lambda i, j: (i, cb + j)),
                  pl.BlockSpec((conv_w.shape[0], LANES), lambda i, j: (0, j))],
        out_specs=pl.BlockSpec((tm, LANES), lambda i, j: (i, j)),
        out_shape=jax.ShapeDtypeStruct((L, 3 * GDN_WIDTH), F32),
        scratch_shapes=[pltpu.VMEM((nslab, 8, LANES), F32)],
        compiler_params=_cparams(("arbitrary", "arbitrary")),
        name="gdn_prep",
    )(proj, conv_w)


GDN_CB = 8


def _gdn_local_kernel(q_ref, k_ref, v_ref, g_ref, b_ref, val_ref, kcum_ref, qg_ref, kd_ref, qk_ref, gl_ref):
    C = GDN_CHUNK
    ii = lax.broadcasted_iota(jnp.int32, (C, C), 0)
    jj = lax.broadcasted_iota(jnp.int32, (C, C), 1)
    causal = ii >= jj
    strict = ii > jj
    tril = causal.astype(F32)
    eye = (ii == jj).astype(F32)
    for c in range(GDN_CB):
        rs = slice(c * C, (c + 1) * C)
        q, k, v = q_ref[rs, :], k_ref[rs, :], v_ref[rs, :]
        g = g_ref[0, rs, :]
        beta = b_ref[0, rs, :]
        gsq = jnp.where(strict, jnp.broadcast_to(g, (C, C)), 0.0)
        diff = _dot(tril, gsq, precision=HIGHEST)
        decay = jnp.where(causal, jnp.exp(diff), 0.0)
        gc = _dot(tril, jnp.broadcast_to(g, (C, GDN_DH)), precision=HIGHEST)
        kb = k * beta
        a = jnp.where(strict, _dot_nt(kb, k, HIGHEST) * decay, 0.0)
        t = eye - a
        p = a
        for _ in range(int(math.log2(C)) - 1):
            p = _dot(p, p, precision=HIGHEST)
            t = t + _dot(t, p, precision=HIGHEST)
        tb = t.astype(BF16)
        egc = jnp.exp(gc)
        val_ref[rs, :] = _dot(tb, (v * beta).astype(BF16))
        kcum_ref[rs, :] = _dot(tb, (kb * egc).astype(BF16)).astype(BF16)
        qk_ref[0, rs, :] = (_dot_nt(q.astype(BF16), k.astype(BF16)) * decay).astype(BF16)
        glast = gc[C - 1:C, :]
        qg_ref[rs, :] = (q * egc).astype(BF16)
        kd_ref[rs, :] = (k * jnp.exp(glast - gc)).astype(BF16)
        gl_ref[0, c:c + 1, :] = jnp.exp(glast)


def _gdn_local(qkv, g3, b3):
    L = qkv.shape[0]
    H = GDN_HEADS
    R = GDN_CB * GDN_CHUNK
    col = lambda off: pl.BlockSpec((R, GDN_DH), lambda h, i: (i, off + h))
    vec = pl.BlockSpec((1, R, 1), lambda h, i: (h, i, 0))
    big = pl.BlockSpec((R, GDN_DH), lambda h, i: (i, h))
    return pl.pallas_call(
        _gdn_local_kernel,
        grid=(H, L // R),
        in_specs=[col(0), col(H), col(2 * H), vec, vec],
        out_specs=[big, big, big, big,
                   pl.BlockSpec((1, R, GDN_CHUNK), lambda h, i: (h, i, 0)),
                   pl.BlockSpec((1, GDN_CB, GDN_DH), lambda h, i: (h, i, 0))],
        out_shape=[jax.ShapeDtypeStruct((L, GDN_WIDTH), F32),
                   jax.ShapeDtypeStruct((L, GDN_WIDTH), BF16),
                   jax.ShapeDtypeStruct((L, GDN_WIDTH), BF16),
                   jax.ShapeDtypeStruct((L, GDN_WIDTH), BF16),
                   jax.ShapeDtypeStruct((H, L, GDN_CHUNK), BF16),
                   jax.ShapeDtypeStruct((H, L // GDN_CHUNK, GDN_DH), F32)],
        compiler_params=_cparams(("arbitrary", "arbitrary")),
        name="gdn_local",
    )(qkv, qkv, qkv, g3, b3)


def _gdn_scan_kernel(val_ref, kcum_ref, qg_ref, qk_ref, kdT_ref, gl_ref, z_ref, gn_ref, o_ref, st_ref):
    @pl.when(pl.program_id(0) == 0)
    def _():
        st_ref[...] = jnp.zeros_like(st_ref)

    C = GDN_CHUNK
    gn = gn_ref[...]
    for c in range(GDN_CB):
        rs = slice(c * C, (c + 1) * C)
        for h in range(GDN_HEADS):
            cs = slice(h * GDN_DH, (h + 1) * GDN_DH)
            state = st_ref[h]
            sb = state.astype(BF16)
            vnew = val_ref[rs, cs] - _dot(kcum_ref[rs, cs], sb)
            vb = vnew.astype(BF16)
            o = _dot(qg_ref[rs, cs], sb) + _dot(qk_ref[h, rs, :], vb)
            st_ref[h] = state * gl_ref[h, c:c + 1, :] + _dot(kdT_ref[h, c], vb)
            on = o * lax.rsqrt(jnp.mean(o * o, axis=-1, keepdims=True) + EPS) * gn
            z = z_ref[rs, cs]
            o_ref[rs, cs] = on * (z * _sigmoid(z))


def _gdn_scan(val, kcum, qg, qk, kdT, gl, proj, gnorm):
    L = val.shape[0]
    H = GDN_HEADS
    R = GDN_CB * GDN_CHUNK
    row = pl.BlockSpec((R, GDN_WIDTH), lambda i: (i, 0))
    zb = C_Z // GDN_WIDTH
    return pl.pallas_call(
        _gdn_scan_kernel,
        grid=(L // R,),
        in_specs=[row, row, row,
                  pl.BlockSpec((H, R, GDN_CHUNK), lambda i: (0, i, 0)),
                  pl.BlockSpec((H, GDN_CB, GDN_DH, GDN_CHUNK), lambda i: (0, i, 0, 0)),
                  pl.BlockSpec((H, GDN_CB, GDN_DH), lambda i: (0, i, 0)),
                  pl.BlockSpec((R, GDN_WIDTH), lambda i: (i, zb)),
                  pl.BlockSpec((1, GDN_DH), lambda i: (0, 0))],
        out_specs=row,
        out_shape=jax.ShapeDtypeStruct((L, GDN_WIDTH), F32),
        scratch_shapes=[pltpu.VMEM((H, GDN_DH, GDN_DH), F32)],
        compiler_params=_cparams(("arbitrary",)),
        name="gdn_scan",
    )(val, kcum, qg, qk, kdT, gl, proj, gnorm)


def _mix_kernel(ya_ref, ys_ref, u_ref, yc_ref, x_ref, ga_ref, gb_ref, d_ref, wglu_ref, wout_ref, o_ref):
    def rms(y, g):
        return y * lax.rsqrt(jnp.mean(y * y, axis=-1, keepdims=True) + EPS) * g

    a = rms(ya_ref[...], ga_ref[...]).astype(BF16)
    yb = _gelu(ys_ref[...] + d_ref[...] * u_ref[...])
    yb = yb * _sigmoid(_dot(yb.astype(BF16), wglu_ref[...]))
    b = rms(yb, gb_ref[...]).astype(BF16)
    na, nb = a.shape[1], b.shape[1]
    acc = x_ref[...] + _dot(a, wout_ref[0:na, :])
    acc = acc + _dot(b, wout_ref[na:na + nb, :])
    o_ref[...] = acc + _dot(yc_ref[...].astype(BF16), wout_ref[na + nb:, :])


def _mix(ya, ys, proj, yc, x, ga, gb, d, wglu, wout, tm=256):
    L, D = x.shape
    ub = C_U // S5_WIDTH
    full = lambda a: pl.BlockSpec(a.shape, lambda i: (0, 0))
    return pl.pallas_call(
        _mix_kernel,
        grid=(L // tm,),
        in_specs=[pl.BlockSpec((tm, ya.shape[1]), lambda i: (i, 0)),
                  pl.BlockSpec((tm, S5_WIDTH), lambda i: (i, 0)),
                  pl.BlockSpec((tm, S5_WIDTH), lambda i: (i, ub)),
                  pl.BlockSpec((tm, GDN_WIDTH), lambda i: (i, 0)),
                  pl.BlockSpec((tm, D), lambda i: (i, 0)),
                  full(ga), full(gb), full(d), full(wglu), full(wout)],
        out_specs=pl.BlockSpec((tm, D), lambda i: (i, 0)),
        out_shape=jax.ShapeDtypeStruct((L, D), F32),
        compiler_params=_cparams(("arbitrary",)),
        name="mix_out",
    )(ya, ys, proj, yc, x, ga, gb, d, wglu, wout)


def _ffn_in_kernel(x_ref, g_ref, wg_ref, wu_ref, cg_ref, cu_ref, bg_ref, bu_ref, o_ref, xn_ref, carry_ref):
    i = pl.program_id(0)
    j = pl.program_id(1)

    @pl.when(j == 0)
    def _():
        xf = x_ref[...]
        ms = jnp.mean(xf * xf, axis=-1, keepdims=True)
        xn_ref[...] = (xf * lax.rsqrt(ms + EPS) * g_ref[...]).astype(BF16)

    @pl.when(i == 0)
    def _():
        carry_ref[j] = jnp.zeros(carry_ref.shape[1:], F32)

    xn = xn_ref[...]
    tm = xn.shape[0]
    rid = lax.broadcasted_iota(jnp.int32, (tm, wg_ref.shape[1]), 0)

    def conv(h, slot, cw_ref, cb_ref):
        prev = carry_ref[j, slot]
        carry_ref[j, slot] = h[tm - 8:tm, :]
        taps = cw_ref.shape[0]
        y = cw_ref[taps - 1:taps, :] * h + cb_ref[...]
        for back in range(1, taps):
            sh = pltpu.roll(h, back, 0)
            for r in range(back):
                sh = jnp.where(rid == r, prev[8 - back + r:8 - back + r + 1, :], sh)
            y = y + cw_ref[taps - 1 - back:taps - back, :] * sh
        return y

    gate = conv(_dot(xn, wg_ref[...]), 0, cg_ref, bg_ref)
    up = conv(_dot(xn, wu_ref[...]), 1, cu_ref, bu_ref)
    o_ref[...] = (gate * _sigmoid(gate) * up).astype(BF16)


def _ffn_in(x, gain, wg, wu, cg, cu, bg, bu, tm=1024, tn=512):
    L, D = x.shape
    N = wg.shape[1]
    taps = cg.shape[0]
    wspec = pl.BlockSpec((D, tn), lambda i, j: (0, j))
    cspec = pl.BlockSpec((taps, tn), lambda i, j: (0, j))
    bspec = pl.BlockSpec((1, tn), lambda i, j: (0, j))
    return pl.pallas_call(
        _ffn_in_kernel,
        grid=(L // tm, N // tn),
        in_specs=[pl.BlockSpec((tm, D), lambda i, j: (i, 0)),
                  pl.BlockSpec((1, D), lambda i, j: (0, 0)),
                  wspec, wspec, cspec, cspec, bspec, bspec],
        out_specs=pl.BlockSpec((tm, tn), lambda i, j: (i, j)),
        out_shape=jax.ShapeDtypeStruct((L, N), BF16),
        scratch_shapes=[pltpu.VMEM((tm, D), BF16), pltpu.VMEM((N // tn, 2, 8, tn), F32)],
        compiler_params=_cparams(("arbitrary", "arbitrary")),
        name="ffn_in",
    )(x, gain.reshape(1, D), wg, wu, cg, cu, bg, bu)


def _ffn_out_kernel(a_ref, w_ref, x_ref, o_ref):
    @pl.when(pl.program_id(1) == 0)
    def _():
        o_ref[...] = x_ref[...]

    o_ref[...] += _dot(a_ref[...], w_ref[...])


def _ffn_out(act, w, x, tm=512, tk=512):
    L, D = x.shape
    K = act.shape[1]
    return pl.pallas_call(
        _ffn_out_kernel,
        grid=(L // tm, K // tk),
        in_specs=[pl.BlockSpec((tm, tk), lambda i, k: (i, k)),
                  pl.BlockSpec((tk, D), lambda i, k: (k, 0)),
                  pl.BlockSpec((tm, D), lambda i, k: (i, 0))],
        out_specs=pl.BlockSpec((tm, D), lambda i, k: (i, 0)),
        out_shape=jax.ShapeDtypeStruct((L, D), F32),
        compiler_params=_cparams(("arbitrary", "arbitrary")),
        name="ffn_out",
    )(act, w, x)


def _rope_tables(pos):
    half = ROPE_DIMS // 2
    inv = ROPE_THETA ** (-jnp.arange(half, dtype=F32) / half)
    ang = pos.astype(F32)[:, None] * inv[None, :]
    n = pos.shape[0]
    cos = jnp.concatenate([jnp.cos(ang), jnp.cos(ang), jnp.ones((n, NSA_DH - ROPE_DIMS), F32)], axis=1)
    sin = jnp.concatenate([-jnp.sin(ang), jnp.sin(ang), jnp.zeros((n, NSA_DH - ROPE_DIMS), F32)], axis=1)
    return jnp.tile(cos, (1, 2)), jnp.tile(sin, (1, 2))


def _permute_w_in(w):
    sp = np.cumsum([0, 512, 128, 128, 128, 128, 128, 128, 24, 512, 3072, 1024, 8, 8])
    q, kc, vc, ks, vs, kw, vw, gates, u, qkv, z, a, b = [w[:, sp[n]:sp[n + 1]] for n in range(13)]
    pad = jnp.zeros((w.shape[0], PROJ_COLS - C_SM - 40), w.dtype)
    return jnp.concatenate([q, ks, kw, vs, vw, z, u, kc, vc, qkv, gates, a, b, pad], axis=1)


def _nsa_mixer(proj, small, p, l):
    L = proj.shape[0]
    HK, G, DH = NSA_KV, NSA_G, NSA_DH
    nQ = L // TQ
    ns = L // SLC_BLOCK
    nb = L // CMP_STRIDE
    n_grp = min(64, ns)
    pos = jnp.arange(L)
    cos, sin = _rope_tables(pos)
    gains = jnp.concatenate([jnp.tile(p['nsa_q_norm'][l], NSA_HEADS), jnp.tile(p['nsa_ks_norm'][l], HK),
                             jnp.tile(p['nsa_kw_norm'][l], HK), jnp.ones((2 * HK * DH,), F32)]).reshape(1, -1)
    prep = _nsa_prep(proj, gains, cos, sin)

    heads = lambda off: prep[:, off:off + HK * DH].reshape(L, HK, DH).transpose(1, 0, 2)
    qT = prep[:, :NSA_HEADS * DH].reshape(nQ, TQ, HK, G, DH).transpose(2, 0, 4, 3, 1).reshape(HK, nQ, DH, G * TQ)
    ks, kw, vs, vw = heads(C_KS), heads(C_KW), heads(C_VS), heads(C_VW)
    onehot = jax.nn.one_hot((pos // SLC_BLOCK) % n_grp, n_grp, dtype=BF16)
    ksaug = jnp.concatenate([ks, jnp.broadcast_to(onehot, (HK, L, n_grp))], axis=2)
    vsT = vs.transpose(0, 2, 1)
    kwp = jnp.pad(kw, ((0, 0), (WINDOW, 0), (0, 0)))
    vwTp = jnp.pad(vw.transpose(0, 2, 1), ((0, 0), (0, 0), (WINDOW, 0)))

    half = lambda off: proj[:, off:off + HK * DH].reshape(nb, CMP_STRIDE, HK, DH).transpose(2, 0, 1, 3).reshape(
        HK, nb, CMP_STRIDE * DH)
    pe = p['cmp_pe'][l].reshape(2, 1, CMP_STRIDE * DH)
    w1 = lambda name: p[name][l].reshape(CMP_LEN * DH, -1).astype(BF16)
    w2 = lambda name: jnp.pad(p[name][l], ((0, 0), (0, LANES - DH))).astype(BF16)
    ccos, csin = _rope_tables(jnp.arange(nb) * CMP_STRIDE + CMP_LEN // 2)
    gk = jnp.pad(p['nsa_kc_norm'][l], (0, LANES - DH)).reshape(1, LANES)
    kc, vc = _nsa_compress(half(C_KC), half(C_VC), pe[0], pe[1], w1('cmp_k_w1'), w2('cmp_k_w2'),
                           w1('cmp_v_w1'), w2('cmp_v_w2'), gk, ccos, csin)
    vcT = vc.transpose(0, 2, 1)

    cmp_start = np.arange(nb) * CMP_STRIDE
    slc_start = np.arange(ns) * SLC_BLOCK
    smapT = ((cmp_start[None, :] <= slc_start[:, None] + SLC_BLOCK - 1)
             & (cmp_start[None, :] + CMP_LEN - 1 >= slc_start[:, None]))
    smapT[:, nb - 1] = False
    smapT = jnp.asarray(smapT, BF16)

    gT = jnp.pad(small[:, :HK * G * 3].reshape(L, HK, G * 3).transpose(1, 2, 0), ((0, 0), (0, 16 - G * 3), (0, 0)))
    oT = _nsa_attention(qT, kc, vcT, smapT, ksaug, vsT, kwp, vwTp, gT, n_grp)
    return oT.reshape(HK, nQ, DH, G, TQ).transpose(1, 4, 0, 3, 2).reshape(L, NSA_HEADS * DH)


def _s5_mixer(proj, p, l):
    L = proj.shape[0]
    T = S5_CHUNK
    nc = L // T
    G, H, P = S5_GROUPS, S5_CH, S5_STATE
    lam_re, lam_im = p['s5_lam_re'][l], p['s5_lam_im'][l]
    dt = jnp.exp(p['s5_log_dt'][l])[:, None]
    mag = jnp.exp(lam_re * dt)
    lb_re = mag * jnp.cos(lam_im * dt)
    lb_im = mag * jnp.sin(lam_im * dt)
    den = lam_re * lam_re + lam_im * lam_im
    nr, ni = lb_re - 1.0, lb_im
    coef_re = (nr * lam_re + ni * lam_im) / den
    coef_im = (ni * lam_re - nr * lam_im) / den
    b_re, b_im = p['s5_b_re'][l], p['s5_b_im'][l]
    bb_re = coef_re[..., None] * b_re - coef_im[..., None] * b_im
    bb_im = coef_re[..., None] * b_im + coef_im[..., None] * b_re
    bsr, bsi = bb_re.transpose(0, 2, 1), bb_im.transpose(0, 2, 1)
    tile_rows = lambda a: jnp.tile(a, (1, T, 1))
    u = proj[:, C_U:C_U + S5_WIDTH]
    ug = u.astype(BF16).reshape(nc, T, G, H).transpose(2, 0, 1, 3).reshape(G, nc, T * H)
    y = _s5_scan(ug, lb_re[:, None, :], lb_im[:, None, :], tile_rows(p['s5_c_re'][l]), tile_rows(p['s5_c_im'][l]),
                 tile_rows(bsr), tile_rows(bsi), bsr, bsi)
    return y.reshape(G, nc, T, H).transpose(1, 2, 0, 3).reshape(L, S5_WIDTH)


def _gdn_mixer(proj, small, p, l):
    L = proj.shape[0]
    H = GDN_HEADS
    qkv = _gdn_prep(proj, p['gdn_conv'][l])
    g3 = small[:, 24:32].T[:, :, None]
    b3 = small[:, 32:40].T[:, :, None]
    val, kcum, qg, kd, qk, gl = _gdn_local(qkv, g3, b3)
    kdT = kd.reshape(L // GDN_CHUNK, GDN_CHUNK, H, GDN_DH).transpose(2, 0, 3, 1)
    return _gdn_scan(val, kcum, qg, qk, kdT, gl, proj, p['gdn_norm'][l].reshape(1, GDN_DH))


def _pad_cols(a, n):
    return jnp.pad(a, ((0, 0), (0, n - a.shape[1])))


def _forward(x3, p):
    x = x3[0]
    depth = p['w_in'].shape[0]
    for l in range(depth):
        w_in = _permute_w_in(p['w_in'][l]).astype(BF16)
        proj = _rms_matmul(x, p['attn_norm'][l], w_in)
        sm_par = jnp.zeros((8, LANES), F32)
        sm_par = sm_par.at[0, 24:32].set(p['gdn_dt_bias'][l]).at[1, 24:32].set(-jnp.exp(p['gdn_a_log'][l]))
        small = _small(proj, sm_par)
        y_a = _nsa_mixer(proj, small, p, l)
        y_s = _s5_mixer(proj, p, l)
        y_c = _gdn_mixer(proj, small, p, l)
        x = _mix(y_a, y_s, proj, y_c, x, p['nsa_out_norm'][l].reshape(1, -1), p['s5_out_norm'][l].reshape(1, -1),
                 p['s5_d'][l].reshape(1, -1), p['s5_w_glu'][l].astype(BF16), p['w_out'][l].astype(BF16))
        wf = p['ffn_w_in'][l]
        cf = p['ffn_conv'][l]
        bf = p['ffn_conv_b'][l].reshape(1, -1)
        act = _ffn_in(x, p['ffn_norm'][l],
                      _pad_cols(wf[:, :D_FF], D_FF_PAD).astype(BF16), _pad_cols(wf[:, D_FF:], D_FF_PAD).astype(BF16),
                      _pad_cols(cf[:, :D_FF], D_FF_PAD), _pad_cols(cf[:, D_FF:], D_FF_PAD),
                      _pad_cols(bf[:, :D_FF], D_FF_PAD), _pad_cols(bf[:, D_FF:], D_FF_PAD))
        w_o = jnp.pad(p['ffn_w_out'][l], ((0, D_FF_PAD - D_FF), (0, 0))).astype(BF16)
        x = _ffn_out(act, w_o, x)
    return x[None]


_PARAM_NAMES = ('attn_norm', 'w_in', 'nsa_q_norm', 'nsa_kc_norm', 'nsa_ks_norm', 'nsa_kw_norm', 'cmp_pe',
                'cmp_k_w1', 'cmp_k_w2', 'cmp_v_w1', 'cmp_v_w2', 'nsa_out_norm', 's5_lam_re', 's5_lam_im',
                's5_log_dt', 's5_b_re', 's5_b_im', 's5_c_re', 's5_c_im', 's5_d', 's5_w_glu', 's5_out_norm',
                'gdn_conv', 'gdn_a_log', 'gdn_dt_bias', 'gdn_norm', 'w_out', 'ffn_norm', 'ffn_w_in', 'ffn_conv',
                'ffn_conv_b', 'ffn_w_out')


def kernel(x, attn_norm, w_in, nsa_q_norm, nsa_kc_norm, nsa_ks_norm, nsa_kw_norm, cmp_pe, cmp_k_w1, cmp_k_w2,
           cmp_v_w1, cmp_v_w2, nsa_out_norm, s5_lam_re, s5_lam_im, s5_log_dt, s5_b_re, s5_b_im, s5_c_re, s5_c_im,
           s5_d, s5_w_glu, s5_out_norm, gdn_conv, gdn_a_log, gdn_dt_bias, gdn_norm, w_out, ffn_norm, ffn_w_in,
           ffn_conv, ffn_conv_b, ffn_w_out):
    vals = (attn_norm, w_in, nsa_q_norm, nsa_kc_norm, nsa_ks_norm, nsa_kw_norm, cmp_pe, cmp_k_w1, cmp_k_w2,
            cmp_v_w1, cmp_v_w2, nsa_out_norm, s5_lam_re, s5_lam_im, s5_log_dt, s5_b_re, s5_b_im, s5_c_re, s5_c_im,
            s5_d, s5_w_glu, s5_out_norm, gdn_conv, gdn_a_log, gdn_dt_bias, gdn_norm, w_out, ffn_norm, ffn_w_in,
            ffn_conv, ffn_conv_b, ffn_w_out)
    return _forward(x, dict(zip(_PARAM_NAMES, vals)))
```

```python
import functools
import math

import jax
import jax.numpy as jnp
import numpy as np
from jax import lax
from jax.experimental import pallas as pl
from jax.experimental.pallas import tpu as pltpu

F32 = jnp.float32
BF16 = jnp.bfloat16
HIGHEST = lax.Precision.HIGHEST

D_MODEL = 2048
NSA_HEADS = 8
NSA_KV = 2
NSA_G = NSA_HEADS // NSA_KV
NSA_DH = 64
CMP_LEN = 32
CMP_STRIDE = 16
SLC_BLOCK = 64
SLC_TOPK = 16
WINDOW = 512
ROPE_THETA = 500000.0
ROPE_DIMS = NSA_DH // 4
S5_WIDTH = 512
S5_CH = 16
S5_GROUPS = 32
S5_STATE = 64
S5_CHUNK = 32
GDN_HEADS = 8
GDN_DH = 128
GDN_WIDTH = 1024
GDN_CHUNK = 64
D_FF = 5504
D_FF_PAD = 5632
EPS = 1e-6
NEG_BIG = -(2.0 ** 100)
LOG2_E = math.log2(math.e)
V_ROWS = 80
KW_LANES = 128
CMP_CHUNK = 256

LANES = 128
TQ = 128
TK = 512
VMEM_LIMIT = 56 * 1024 * 1024

C_Q, C_KS, C_KW, C_VS, C_VW = 0, 512, 640, 768, 896
C_Z, C_U, C_KC, C_VC, C_QKV, C_SM = 1024, 2048, 2560, 2688, 2816, 5888
NSA_PREP_W = C_Z
PROJ_COLS = 6144


def _cparams(sem):
    return pltpu.CompilerParams(dimension_semantics=sem, vmem_limit_bytes=VMEM_LIMIT)


def _dot(a, b, precision=None):
    return jnp.dot(a, b, preferred_element_type=F32, precision=precision)


def _dot_nt(a, b, precision=None):
    return lax.dot_general(a, b, (((1,), (1,)), ((), ())), preferred_element_type=F32, precision=precision)


def _gelu(x):
    return x * (0.5 * (1.0 + jnp.tanh(math.sqrt(2.0 / math.pi) * (x + 0.044715 * (x * x * x)))))


def _sigmoid(x):
    return 1.0 / (1.0 + jnp.exp(-x))


def _rms_matmul_kernel(x_ref, g_ref, w_ref, o_ref, xn_ref):
    @pl.when(pl.program_id(1) == 0)
    def _():
        xf = x_ref[...]
        ms = jnp.mean(xf * xf, axis=-1, keepdims=True)
        xn_ref[...] = (xf * lax.rsqrt(ms + EPS) * g_ref[...]).astype(BF16)

    o_ref[...] = _dot(xn_ref[...], w_ref[...])


def _rms_matmul(x, gain, w, tm=512, tn=1024):
    L, D = x.shape
    N = w.shape[1]
    return pl.pallas_call(
        _rms_matmul_kernel,
        grid=(L // tm, N // tn),
        in_specs=[
            pl.BlockSpec((tm, D), lambda i, j: (i, 0)),
            pl.BlockSpec((1, D), lambda i, j: (0, 0)),
            pl.BlockSpec((D, tn), lambda i, j: (0, j)),
        ],
        out_specs=pl.BlockSpec((tm, tn), lambda i, j: (i, j)),
        out_shape=jax.ShapeDtypeStruct((L, N), F32),
        scratch_shapes=[pltpu.VMEM((tm, D), BF16)],
        compiler_params=_cparams(("arbitrary", "arbitrary")),
        name="rms_proj",
    )(x, gain.reshape(1, D), w)


def _rope_slab(x, cos, sin):
    d = lax.broadcasted_iota(jnp.int32, x.shape, 1) & (NSA_DH - 1)
    half = ROPE_DIMS // 2
    partner = jnp.where(d < half, pltpu.roll(x, LANES - half, 1), pltpu.roll(x, half, 1))
    return x * cos + partner * sin


def _nsa_prep_kernel(x_ref, g_ref, cos_ref, sin_ref, bsum_ref, o_ref):
    cos = cos_ref[...]
    sin = sin_ref[...]
    bsum = bsum_ref[...]
    n_rot = (C_VS - C_Q) // LANES
    for s in range(NSA_PREP_W // LANES):
        sl = slice(s * LANES, (s + 1) * LANES)
        x = x_ref[:, sl]
        if s < n_rot:
            ms = _dot(x * x, bsum, precision=HIGHEST)
            y = x * lax.rsqrt(ms + EPS) * g_ref[:, sl]
            y = _rope_slab(y, cos, sin)
            if s < (C_KS - C_Q) // LANES:
                y = y * (NSA_DH ** -0.5 * LOG2_E)
            o_ref[:, sl] = y.astype(BF16)
        else:
            o_ref[:, sl] = x.astype(BF16)


def _nsa_prep(proj, gains, cos, sin, tm=512):
    L = proj.shape[0]
    W = NSA_PREP_W
    bsum = jnp.asarray(np.kron(np.eye(2), np.ones((NSA_DH, NSA_DH))) / NSA_DH, F32)
    return pl.pallas_call(
        _nsa_prep_kernel,
        grid=(L // tm,),
        in_specs=[
            pl.BlockSpec((tm, W), lambda i: (i, 0)),
            pl.BlockSpec((1, W), lambda i: (0, 0)),
            pl.BlockSpec((tm, LANES), lambda i: (i, 0)),
            pl.BlockSpec((tm, LANES), lambda i: (i, 0)),
            pl.BlockSpec((LANES, LANES), lambda i: (0, 0)),
        ],
        out_specs=pl.BlockSpec((tm, W), lambda i: (i, 0)),
        out_shape=jax.ShapeDtypeStruct((L, W), BF16),
        compiler_params=_cparams(("arbitrary",)),
        name="nsa_prep",
    )(proj, gains, cos, sin, bsum)


def _cmp_kernel(hbk_ref, hbv_ref, pea_ref, peb_ref, w1k_ref, w2k_ref, w1v_ref, w2v_ref,
                gk_ref, cos_ref, sin_ref, kc_ref, vc_ref):
    nb = hbk_ref.shape[1]
    half_w = hbk_ref.shape[2]
    pea = pea_ref[...]
    peb = peb_ref[...]

    def mlp(hb, w1_ref, w2_ref):
        a = (hb + pea).astype(BF16)
        b = (hb + peb).astype(BF16)
        p1 = _dot(a, w1_ref[0:half_w, :])
        p2 = _dot(b, w1_ref[half_w:2 * half_w, :])
        h = p1 + pltpu.roll(p2, nb - 1, 0)
        return _dot(_gelu(h).astype(BF16), w2_ref[...])

    rid = lax.broadcasted_iota(jnp.int32, (nb, LANES), 0)
    real = rid < nb - 1
    kc = jnp.where(real, mlp(hbk_ref[0], w1k_ref, w2k_ref), 0.0)
    vc = jnp.where(real, mlp(hbv_ref[0], w1v_ref, w2v_ref), 0.0)
    ms = jnp.sum(kc * kc, axis=-1, keepdims=True) * (1.0 / NSA_DH)
    kcn = kc * lax.rsqrt(ms + EPS) * gk_ref[...]
    kcn = _rope_slab(kcn, cos_ref[...], sin_ref[...])
    kc_ref[0] = kcn[:, 0:NSA_DH].astype(BF16)
    vc_ref[0] = vc[:, 0:NSA_DH].astype(BF16)


def _nsa_compress(hbk, hbv, pea, peb, w1k, w2k, w1v, w2v, gk, cos, sin):
    HK, nb, half_w = hbk.shape
    hid = w1k.shape[1]
    full = lambda shape: pl.BlockSpec(shape, lambda h: (0,) * len(shape))
    return pl.pallas_call(
        _cmp_kernel,
        grid=(HK,),
        in_specs=[
            pl.BlockSpec((1, nb, half_w), lambda h: (h, 0, 0)),
            pl.BlockSpec((1, nb, half_w), lambda h: (h, 0, 0)),
            full((1, half_w)), full((1, half_w)),
            full((2 * half_w, hid)), full((hid, LANES)),
            full((2 * half_w, hid)), full((hid, LANES)),
            full((1, LANES)), full((nb, LANES)), full((nb, LANES)),
        ],
        out_specs=[pl.BlockSpec((1, nb, NSA_DH), lambda h: (h, 0, 0)),
                   pl.BlockSpec((1, nb, NSA_DH), lambda h: (h, 0, 0))],
        out_shape=[jax.ShapeDtypeStruct((HK, nb, NSA_DH), BF16)] * 2,
        compiler_params=_cparams(("arbitrary",)),
        name="nsa_compress",
    )(hbk, hbv, pea, peb, w1k, w2k, w1v, w2v, gk, cos, sin)


def _nsa_attn_kernel(qT_ref, kc_ref, vcx_ref, smap_ref, ksaug_ref, vsx_ref, kwp_ref, vwxp_ref, gT_ref,
                     o_ref, qaug_ref, qwin_ref, ss_ref, mt_ref, m_ref, acc_ref, sc_ref, imp_ref, *, n_grp):
    i = pl.program_id(1)
    s0 = i * TQ
    nb = kc_ref.shape[1]
    ns = smap_ref.shape[0]
    GW = NSA_G * TQ
    qT = qT_ref[0, 0]
    tq = s0 + (lax.broadcasted_iota(jnp.int32, (1, GW), 1) & (TQ - 1))

    cch = min(CMP_CHUNK, nb)
    n_cch = (s0 + TQ - CMP_LEN) // (CMP_STRIDE * cch) + 1
    nthr = (tq - (CMP_LEN - 1)) // CMP_STRIDE

    def cmp_scores(c, mc):
        r0c = pl.multiple_of(c * cch, cch)
        s = _dot(kc_ref[0, pl.ds(r0c, cch), :], qT)
        nrow = r0c + lax.broadcasted_iota(jnp.int32, (cch, GW), 0)
        s = jnp.where(nrow <= nthr, s, NEG_BIG)
        sc_ref[pl.ds(r0c, cch), :] = s
        return jnp.maximum(mc, jnp.max(s, axis=0, keepdims=True))

    mc = lax.fori_loop(0, n_cch, cmp_scores, jnp.full((1, GW), NEG_BIG, F32))
    imp_ref[...] = jnp.zeros_like(imp_ref)
    ones16 = jnp.ones((16, cch), BF16)

    def cmp_accum(c, carry):
        ocx, llo = carry
        r0c = pl.multiple_of(c * cch, cch)
        p = jnp.exp2(sc_ref[pl.ds(r0c, cch), :] - mc)
        hi, lo = _split2(p)
        sm = smap_ref[:, pl.ds(r0c, cch)]
        imp_ref[...] += _dot(sm, hi) + _dot(sm, lo)
        return ocx + _dot(vcx_ref[0, :, pl.ds(r0c, cch)], hi), llo + _dot(ones16, lo)

    VX = vcx_ref.shape[1]
    ocx, llo = lax.fori_loop(0, n_cch, cmp_accum, (jnp.zeros((VX, GW), F32), jnp.zeros((16, GW), F32)))
    lc = ocx[NSA_DH:NSA_DH + 1] + llo[0:1]
    inv_l = jnp.where(tq >= CMP_LEN - 1, 1.0 / lc, 0.0)
    ocT = ocx[0:NSA_DH] * inv_l

    imp = imp_ref[:, 0:TQ] * inv_l[:, 0:TQ]
    for g in range(1, NSA_G):
        imp = imp + imp_ref[:, g * TQ:(g + 1) * TQ] * inv_l[:, g * TQ:(g + 1) * TQ]

    jrow = lax.broadcasted_iota(jnp.int32, (ns, TQ), 0)
    cur = (s0 + lax.broadcasted_iota(jnp.int32, (1, TQ), 1)) // SLC_BLOCK
    bias = jnp.where(jrow == 0, 0.0, jnp.where(jrow == cur, 0.0, jnp.where(jrow == cur - 1, 0.0, NEG_BIG)))
    val = jnp.where(jrow >= 1, jnp.where(jrow <= cur - 2, imp, -jnp.inf), -jnp.inf)
    for _ in range(min(SLC_TOPK, ns) - 3):
        mx = jnp.max(val, axis=0, keepdims=True)
        idx = jnp.min(jnp.where(val == mx, jrow, ns), axis=0, keepdims=True)
        hit = jrow == idx
        bias = jnp.where(hit, 0.0, bias)
        val = jnp.where(hit, -jnp.inf, val)
    bias16 = bias.astype(BF16)
    own = jrow // 2 == i
    bias16 = jnp.where(own, NEG_BIG, bias).astype(BF16)
    for grp in range(ns // n_grp):
        qaug_ref[grp, 0:NSA_DH, :] = qT
        for g in range(NSA_G):
            qaug_ref[grp, NSA_DH:NSA_DH + n_grp, g * TQ:(g + 1) * TQ] = bias16[grp * n_grp:(grp + 1) * n_grp, :]

    tiles_per_grp = (n_grp * SLC_BLOCK) // TK
    n_pairs = (s0 + 2 * TK - 1) // (2 * TK)

    def scores(kt, slot):
        k0 = pl.multiple_of(kt * TK, TK)
        s = _dot(ksaug_ref[0, pl.ds(k0, TK), :], qaug_ref[kt // tiles_per_grp])
        sb = s.astype(BF16)
        ss_ref[slot] = sb
        mt_ref[slot] = jnp.max(sb, axis=0, keepdims=True).astype(F32)

    def online_update(sb, mt, v):
        m = m_ref[...]
        mn = jnp.maximum(m, mt)
        acc_ref[...] = jnp.exp2(m - mn) * acc_ref[...] + _dot(v, jnp.exp2(sb - mn.astype(BF16)))
        m_ref[...] = mn

    def accumulate(kt, slot):
        k0 = pl.multiple_of(kt * TK, TK)
        online_update(ss_ref[slot], mt_ref[slot], vsx_ref[0, :, pl.ds(k0, TK)])

    m_ref[...] = jnp.full(m_ref.shape, NEG_BIG, F32)
    acc_ref[...] = jnp.zeros_like(acc_ref)

    @pl.when(n_pairs > 0)
    def _():
        scores(0, 0)

    def pair_body(j, carry):
        scores(2 * j + 1, 1)
        accumulate(2 * j, 0)
        scores(2 * j + 2, 0)
        accumulate(2 * j + 1, 1)
        return carry

    lax.fori_loop(0, n_pairs - 1, pair_body, 0)

    @pl.when(n_pairs > 0)
    def _():
        scores(2 * n_pairs - 1, 1)
        accumulate(2 * n_pairs - 2, 0)
        accumulate(2 * n_pairs - 1, 1)

    r0 = pl.multiple_of(s0, TQ)
    rel = lax.broadcasted_iota(jnp.int32, (TQ, GW), 0)
    qrel = tq - s0
    sd = _dot(ksaug_ref[0, pl.ds(r0, TQ), 0:NSA_DH], qT)
    sd16 = jnp.where(rel <= qrel, sd, NEG_BIG).astype(BF16)
    online_update(sd16, jnp.max(sd16, axis=0, keepdims=True).astype(F32), vsx_ref[0, :, pl.ds(r0, TQ)])
    osT = acc_ref[0:NSA_DH, :] / acc_ref[NSA_DH:NSA_DH + 1, :]

    qwin_ref[0:NSA_DH, :] = qT
    flag_row = lax.broadcasted_iota(jnp.int32, (KW_LANES - NSA_DH, GW), 0) == 0
    qwin_ref[NSA_DH:KW_LANES, :] = jnp.where(flag_row, NEG_BIG, 0.0).astype(BF16)
    qwin = qwin_ref[...]
    edge = _dot(kwp_ref[0, pl.ds(r0, TQ), :], qwin)
    mid = _dot(kwp_ref[0, pl.ds(r0 + TQ, WINDOW - TQ), :], qwin)
    diag = _dot(kwp_ref[0, pl.ds(r0 + WINDOW, TQ), :], qwin)
    edge16 = jnp.where(rel > qrel, edge, NEG_BIG).astype(BF16)
    diag16 = jnp.where(rel <= qrel, diag, NEG_BIG).astype(BF16)
    mid16 = mid.astype(BF16)
    mw = jnp.maximum(jnp.maximum(jnp.max(edge16, axis=0, keepdims=True), jnp.max(mid16, axis=0, keepdims=True)),
                     jnp.max(diag16, axis=0, keepdims=True))
    accw = (_dot(vwxp_ref[0, :, pl.ds(r0, TQ)], jnp.exp2(edge16 - mw))
            + _dot(vwxp_ref[0, :, pl.ds(r0 + TQ, WINDOW - TQ)], jnp.exp2(mid16 - mw))
            + _dot(vwxp_ref[0, :, pl.ds(r0 + WINDOW, TQ)], jnp.exp2(diag16 - mw)))
    owT = accw[0:NSA_DH] / accw[NSA_DH:NSA_DH + 1]

    gates = _sigmoid(gT_ref[0])
    for g in range(NSA_G):
        sl = slice(g * TQ, (g + 1) * TQ)
        o_ref[0, 0, :, sl] = (gates[3 * g:3 * g + 1] * ocT[:, sl] + gates[3 * g + 1:3 * g + 2] * osT[:, sl]
                              + gates[3 * g + 2:3 * g + 3] * owT[:, sl])


def _nsa_attention(qT, kc, vcx, smapT, ksaug, vsx, kwp, vwxp, gT, n_grp):
    HK, nQ, _, GW = qT.shape
    nb = kc.shape[1]
    ns = smapT.shape[0]
    L = vsx.shape[2]
    KA = ksaug.shape[2]
    scratch = [pltpu.VMEM((ns // n_grp, KA, GW), BF16),
               pltpu.VMEM((KW_LANES, GW), BF16),
               pltpu.VMEM((2, TK, GW), BF16),
               pltpu.VMEM((2, 1, GW), F32),
               pltpu.VMEM((1, GW), F32),
               pltpu.VMEM((V_ROWS, GW), F32),
               pltpu.VMEM((nb, GW), F32),
               pltpu.VMEM((ns, GW), F32)]
    per_hk = lambda shape: pl.BlockSpec((1,) + shape, lambda h, i: (h, 0, 0))
    return pl.pallas_call(
        functools.partial(_nsa_attn_kernel, n_grp=n_grp),
        grid=(HK, nQ),
        in_specs=[
            pl.BlockSpec((1, 1, NSA_DH, GW), lambda h, i: (h, i, 0, 0)),
            per_hk((nb, NSA_DH)),
            per_hk((V_ROWS, nb)),
            pl.BlockSpec((ns, nb), lambda h, i: (0, 0)),
            per_hk((L, KA)),
            per_hk((V_ROWS, L)),
            per_hk((L + WINDOW, KW_LANES)),
            per_hk((V_ROWS, L + WINDOW)),
            pl.BlockSpec((1, 16, TQ), lambda h, i: (h, 0, i)),
        ],
        out_specs=pl.BlockSpec((1, 1, NSA_DH, GW), lambda h, i: (h, i, 0, 0)),
        out_shape=jax.ShapeDtypeStruct((HK, nQ, NSA_DH, GW), F32),
        scratch_shapes=scratch,
        compiler_params=_cparams(("arbitrary", "arbitrary")),
        name="nsa_attn",
    )(qT, kc, vcx, smapT, ksaug, vsx, kwp, vwxp, gT)


def _s5_kernel(u_ref, lbr_ref, lbi_ref, ctr_ref, cti_ref, btr_ref, bti_ref, bsr_ref, bsi_ref, y_ref, lm_ref):
    T = S5_CHUNK
    W = T * S5_CH
    ar = lbr_ref[0]
    ai = lbi_ref[0]
    delta = lax.broadcasted_iota(jnp.int32, (W, S5_STATE), 0) // S5_CH

    def powers(e):
        pr = jnp.ones((W, S5_STATE), F32)
        pi = jnp.zeros((W, S5_STATE), F32)
        fr, fi = ar, ai
        for b in range(T.bit_length() - 1):
            bit = ((e >> b) & 1) == 1
            nr = pr * fr - pi * fi
            ni = pr * fi + pi * fr
            pr = jnp.where(bit, nr, pr)
            pi = jnp.where(bit, ni, pi)
            fr, fi = fr * fr - fi * fi, 2.0 * fr * fi
        return pr, pi, fr, fi

    pwr, pwi, aTr, aTi = powers(delta)
    rvr, rvi, _, _ = powers(T - 1 - delta)
    ctr, cti = ctr_ref[0], cti_ref[0]
    car = ctr * pwr - cti * pwi
    cai = ctr * pwi + cti * pwr

    kw = _dot_nt(bsr_ref[0], car, HIGHEST) - _dot_nt(bsi_ref[0], cai, HIGHEST)
    lane = lax.broadcasted_iota(jnp.int32, (S5_CH, W), 1)
    for tau in range(T):
        sh = S5_CH * tau
        blk = kw if tau == 0 else jnp.where(lane >= sh, pltpu.roll(kw, sh, 1), 0.0)
        lm_ref[sh:sh + S5_CH, :] = blk.astype(BF16)

    u = u_ref[0]
    nc = u.shape[0]
    y = _dot(u, lm_ref[...])

    btr, bti = btr_ref[0], bti_ref[0]
    sr = _dot(u, (rvr * btr - rvi * bti).astype(BF16))
    si = _dot(u, (rvr * bti + rvi * btr).astype(BF16))
    rowc = lax.broadcasted_iota(jnp.int32, (nc, S5_STATE), 0)
    fr, fi = aTr, aTi
    step = 1
    while step < nc:
        shr = jnp.where(rowc >= step, pltpu.roll(sr, step, 0), 0.0)
        shi = jnp.where(rowc >= step, pltpu.roll(si, step, 0), 0.0)
        sr, si = sr + fr * shr - fi * shi, si + fr * shi + fi * shr
        fr, fi = fr * fr - fi * fi, 2.0 * fr * fi
        step *= 2
    xr = jnp.where(rowc >= 1, pltpu.roll(sr, 1, 0), 0.0)
    xi = jnp.where(rowc >= 1, pltpu.roll(si, 1, 0), 0.0)
    c1r = car * ar - cai * ai
    c1i = car * ai + cai * ar
    y = y + _dot_nt(xr.astype(BF16), c1r.astype(BF16)) - _dot_nt(xi.astype(BF16), c1i.astype(BF16))
    y_ref[0] = y


def _s5_scan(ug, lbr, lbi, ctr, cti, btr, bti, bsr, bsi):
    G, nc, W = ug.shape
    grp = lambda shape: pl.BlockSpec((1,) + shape, lambda g: (g, 0, 0))
    return pl.pallas_call(
        _s5_kernel,
        grid=(G,),
        in_specs=[grp((nc, W)), grp((1, S5_STATE)), grp((1, S5_STATE)),
                  grp((W, S5_STATE)), grp((W, S5_STATE)), grp((W, S5_STATE)), grp((W, S5_STATE)),
                  grp((S5_CH, S5_STATE)), grp((S5_CH, S5_STATE))],
        out_specs=grp((nc, W)),
        out_shape=jax.ShapeDtypeStruct((G, nc, W), F32),
        scratch_shapes=[pltpu.VMEM((W, W), BF16)],
        compiler_params=_cparams(("arbitrary",)),
        name="s5_scan",
    )(ug, lbr, lbi, ctr, cti, btr, bti, bsr, bsi)


def _small_kernel(x_ref, p_ref, o_ref):
    x = x_ref[...]
    lane = lax.broadcasted_iota(jnp.int32, x.shape, 1)
    z = x + p_ref[0:1, :]
    softplus = jnp.maximum(z, 0.0) + jnp.log(1.0 + jnp.exp(-jnp.abs(z)))
    gdec = p_ref[1:2, :] * softplus
    beta = _sigmoid(x)
    o_ref[...] = jnp.where(lane < 24, x, jnp.where(lane < 32, gdec, beta))


def _small(proj, params, tm=1024):
    L = proj.shape[0]
    cb = C_SM // LANES
    return pl.pallas_call(
        _small_kernel,
        grid=(L // tm,),
        in_specs=[pl.BlockSpec((tm, LANES), lambda i: (i, cb)), pl.BlockSpec((8, LANES), lambda i: (0, 0))],
        out_specs=pl.BlockSpec((tm, LANES), lambda i: (i, 0)),
        out_shape=jax.ShapeDtypeStruct((L, LANES), F32),
        compiler_params=_cparams(("arbitrary",)),
        name="small_cols",
    )(proj, params)


def _gdn_prep_kernel(x_ref, cw_ref, o_ref, carry_ref):
    i = pl.program_id(0)
    j = pl.program_id(1)
    x = x_ref[...]
    tm = x.shape[0]
    @pl.when(i == 0)
    def _():
        carry_ref[j] = jnp.zeros((8, LANES), F32)

    prev = carry_ref[j]
    carry_ref[j] = x[tm - 8:tm, :]
    rid = lax.broadcasted_iota(jnp.int32, x.shape, 0)
    taps = cw_ref.shape[0]
    y = cw_ref[taps - 1:taps, :] * x
    for back in range(1, taps):
        sh = pltpu.roll(x, back, 0)
        for r in range(back):
            sh = jnp.where(rid == r, prev[8 - back + r:8 - back + r + 1, :], sh)
        y = y + cw_ref[taps - 1 - back:taps - back, :] * sh
    y = y * _sigmoid(y)
    ss = jnp.sum(y * y, axis=-1, keepdims=True)
    scale = jnp.where(j < GDN_HEADS, GDN_DH ** -0.5, 1.0)
    yn = y * lax.rsqrt(ss + EPS) * scale
    o_ref[...] = jnp.where(j < 2 * GDN_HEADS, yn, y)


def _gdn_prep(proj, conv_w, tm=1024):
    L = proj.shape[0]
    nslab = 3 * GDN_WIDTH // LANES
    cb = C_QKV // LANES
    return pl.pallas_call(
        _gdn_prep_kernel,
        grid=(L // tm, nslab),
        in_specs=[pl.BlockSpec((tm, LANES), lambda i, j: (i, cb + j)),
                  pl.BlockSpec((conv_w.shape[0], LANES), lambda i, j: (0, j))],
        out_specs=pl.BlockSpec((tm, LANES), lambda i, j: (i, j)),
        out_shape=jax.ShapeDtypeStruct((L, 3 * GDN_WIDTH), F32),
        scratch_shapes=[pltpu.VMEM((nslab, 8, LANES), F32)],
        compiler_params=_cparams(("arbitrary", "arbitrary")),
        name="gdn_prep",
    )(proj, conv_w)


GDN_CB = 8


def _split2(x):
    hi = x.astype(BF16)
    return hi, (x - hi.astype(F32)).astype(BF16)


def _dot_split(a, b):
    ah, al = _split2(a)
    bh, bl = _split2(b)
    return _dot(ah, bh) + (_dot(ah, bl) + _dot(al, bh))


def _gdn_local_kernel(q_ref, k_ref, v_ref, g_ref, b_ref, val_ref, kcum_ref, qg_ref, kd_ref, qk_ref, gl_ref):
    C = GDN_CHUNK
    chunks = range(GDN_CB)
    rows = [slice(c * C, (c + 1) * C) for c in chunks]
    ii = lax.broadcasted_iota(jnp.int32, (C, C), 0)
    jj = lax.broadcasted_iota(jnp.int32, (C, C), 1)
    causal = ii >= jj
    strict = ii > jj
    eye = (ii == jj).astype(F32)
    tril16 = causal.astype(BF16)
    row2 = lax.broadcasted_iota(jnp.int32, (C, 2 * C), 0)
    col2 = lax.broadcasted_iota(jnp.int32, (C, 2 * C), 1)
    keep = row2 > jnp.where(col2 < C, -1, col2 - C)

    k = [k_ref[rs, :] for rs in rows]
    beta = [b_ref[0, rs, :] for rs in rows]
    res = []
    for rs in rows:
        x = jnp.where(keep, jnp.broadcast_to(g_ref[0, rs, :], (C, 2 * C)), 0.0)
        hi = x.astype(BF16)
        r1 = x - hi.astype(F32)
        mid = r1.astype(BF16)
        lo = (r1 - mid.astype(F32)).astype(BF16)
        res.append(_dot(tril16, hi) + (_dot(tril16, mid) + _dot(tril16, lo)))
    gc = [jnp.broadcast_to(r[:, 0:1], (C, GDN_DH)) for r in res]
    decay = [jnp.where(causal, jnp.exp(r[:, C:2 * C]), 0.0) for r in res]
    kb = [kc * bc for kc, bc in zip(k, beta)]
    k16 = [kc.astype(BF16) for kc in k]
    a = [jnp.where(strict, _dot_nt(kbc.astype(BF16), kc16) * dc, 0.0) for kbc, kc16, dc in zip(kb, k16, decay)]
    t = [eye - ac for ac in a]
    p = a
    for _ in range(int(math.log2(C)) - 1):
        p = [_dot_split(pc, pc) for pc in p]
        t = [tc + _dot_split(tc, pc) for tc, pc in zip(t, p)]
    t16 = [tc.astype(BF16) for tc in t]
    egc = [jnp.exp(gcc) for gcc in gc]
    val = [_dot(tc, (v_ref[rs, :] * bc).astype(BF16)) for tc, rs, bc in zip(t16, rows, beta)]
    kcum = [_dot(tc, (kbc * ec).astype(BF16)) for tc, kbc, ec in zip(t16, kb, egc)]
    qk = [_dot_nt(q_ref[rs, :].astype(BF16), kc16) * dc for rs, kc16, dc in zip(rows, k16, decay)]
    for c in chunks:
        rs = rows[c]
        val_ref[rs, :] = val[c]
        kcum_ref[rs, :] = kcum[c].astype(BF16)
        qk_ref[0, rs, :] = qk[c].astype(BF16)
        glast = gc[c][C - 1:C, :]
        qg_ref[rs, :] = (q_ref[rs, :] * egc[c]).astype(BF16)
        kd_ref[rs, :] = (k[c] * jnp.exp(glast - gc[c])).astype(BF16)
        gl_ref[0, c:c + 1, :] = jnp.exp(glast)


def _gdn_local(qkv, g3, b3):
    L = qkv.shape[0]
    H = GDN_HEADS
    R = GDN_CB * GDN_CHUNK
    col = lambda off: pl.BlockSpec((R, GDN_DH), lambda h, i: (i, off + h))
    vec = pl.BlockSpec((1, R, 1), lambda h, i: (h, i, 0))
    big = pl.BlockSpec((R, GDN_DH), lambda h, i: (i, h))
    return pl.pallas_call(
        _gdn_local_kernel,
        grid=(H, L // R),
        in_specs=[col(0), col(H), col(2 * H), vec, vec],
        out_specs=[big, big, big, big,
                   pl.BlockSpec((1, R, GDN_CHUNK), lambda h, i: (h, i, 0)),
                   pl.BlockSpec((1, GDN_CB, GDN_DH), lambda h, i: (h, i, 0))],
        out_shape=[jax.ShapeDtypeStruct((L, GDN_WIDTH), F32),
                   jax.ShapeDtypeStruct((L, GDN_WIDTH), BF16),
                   jax.ShapeDtypeStruct((L, GDN_WIDTH), BF16),
                   jax.ShapeDtypeStruct((L, GDN_WIDTH), BF16),
                   jax.ShapeDtypeStruct((H, L, GDN_CHUNK), BF16),
                   jax.ShapeDtypeStruct((H, L // GDN_CHUNK, GDN_DH), F32)],
        compiler_params=_cparams(("arbitrary", "arbitrary")),
        name="gdn_local",
    )(qkv, qkv, qkv, g3, b3)


def _gdn_scan_kernel(val_ref, kcum_ref, qg_ref, qk_ref, kdT_ref, gl_ref, z_ref, gn_ref, o_ref, st_ref):
    @pl.when(pl.program_id(0) == 0)
    def _():
        st_ref[...] = jnp.zeros_like(st_ref)

    C = GDN_CHUNK
    gn = gn_ref[...]
    heads = range(GDN_HEADS)
    cols = [slice(h * GDN_DH, (h + 1) * GDN_DH) for h in heads]
    state = [st_ref[h] for h in heads]
    for c in range(GDN_CB):
        rs = slice(c * C, (c + 1) * C)
        sb = [s.astype(BF16) for s in state]
        kcs = [_dot(kcum_ref[rs, cs], s) for cs, s in zip(cols, sb)]
        qgs = [_dot(qg_ref[rs, cs], s) for cs, s in zip(cols, sb)]
        vb = [(val_ref[rs, cs] - x).astype(BF16) for cs, x in zip(cols, kcs)]
        o = [x + _dot(qk_ref[h, rs, :], v) for h, x, v in zip(heads, qgs, vb)]
        state = [s * gl_ref[h, c:c + 1, :] + _dot(kdT_ref[h, c], v) for h, s, v in zip(heads, state, vb)]
        for h in heads:
            on = o[h] * lax.rsqrt(jnp.mean(o[h] * o[h], axis=-1, keepdims=True) + EPS) * gn
            z = z_ref[rs, cols[h]]
            o_ref[rs, cols[h]] = on * (z * _sigmoid(z))
    for h in heads:
        st_ref[h] = state[h]


def _gdn_scan(val, kcum, qg, qk, kdT, gl, proj, gnorm):
    L = val.shape[0]
    H = GDN_HEADS
    R = GDN_CB * GDN_CHUNK
    row = pl.BlockSpec((R, GDN_WIDTH), lambda i: (i, 0))
    zb = C_Z // GDN_WIDTH
    return pl.pallas_call(
        _gdn_scan_kernel,
        grid=(L // R,),
        in_specs=[row, row, row,
                  pl.BlockSpec((H, R, GDN_CHUNK), lambda i: (0, i, 0)),
                  pl.BlockSpec((H, GDN_CB, GDN_DH, GDN_CHUNK), lambda i: (0, i, 0, 0)),
                  pl.BlockSpec((H, GDN_CB, GDN_DH), lambda i: (0, i, 0)),
                  pl.BlockSpec((R, GDN_WIDTH), lambda i: (i, zb)),
                  pl.BlockSpec((1, GDN_DH), lambda i: (0, 0))],
        out_specs=row,
        out_shape=jax.ShapeDtypeStruct((L, GDN_WIDTH), F32),
        scratch_shapes=[pltpu.VMEM((H, GDN_DH, GDN_DH), F32)],
        compiler_params=_cparams(("arbitrary",)),
        name="gdn_scan",
    )(val, kcum, qg, qk, kdT, gl, proj, gnorm)


def _mix_kernel(ya_ref, ys_ref, u_ref, yc_ref, x_ref, ga_ref, gb_ref, d_ref, wglu_ref, wout_ref, o_ref):
    def rms(y, g):
        return y * lax.rsqrt(jnp.mean(y * y, axis=-1, keepdims=True) + EPS) * g

    a = rms(ya_ref[...], ga_ref[...]).astype(BF16)
    yb = _gelu(ys_ref[...] + d_ref[...] * u_ref[...])
    yb = yb * _sigmoid(_dot(yb.astype(BF16), wglu_ref[...]))
    b = rms(yb, gb_ref[...]).astype(BF16)
    na, nb = a.shape[1], b.shape[1]
    acc = x_ref[...] + _dot(a, wout_ref[0:na, :])
    acc = acc + _dot(b, wout_ref[na:na + nb, :])
    o_ref[...] = acc + _dot(yc_ref[...].astype(BF16), wout_ref[na + nb:, :])


def _mix(ya, ys, proj, yc, x, ga, gb, d, wglu, wout, tm=256):
    L, D = x.shape
    ub = C_U // S5_WIDTH
    full = lambda a: pl.BlockSpec(a.shape, lambda i: (0, 0))
    return pl.pallas_call(
        _mix_kernel,
        grid=(L // tm,),
        in_specs=[pl.BlockSpec((tm, ya.shape[1]), lambda i: (i, 0)),
                  pl.BlockSpec((tm, S5_WIDTH), lambda i: (i, 0)),
                  pl.BlockSpec((tm, S5_WIDTH), lambda i: (i, ub)),
                  pl.BlockSpec((tm, GDN_WIDTH), lambda i: (i, 0)),
                  pl.BlockSpec((tm, D), lambda i: (i, 0)),
                  full(ga), full(gb), full(d), full(wglu), full(wout)],
        out_specs=pl.BlockSpec((tm, D), lambda i: (i, 0)),
        out_shape=jax.ShapeDtypeStruct((L, D), F32),
        compiler_params=_cparams(("arbitrary",)),
        name="mix_out",
    )(ya, ys, proj, yc, x, ga, gb, d, wglu, wout)


def _ffn_in_kernel(x_ref, g_ref, wg_ref, wu_ref, cg_ref, cu_ref, bg_ref, bu_ref, o_ref, xn_ref, carry_ref):
    i = pl.program_id(0)
    j = pl.program_id(1)

    @pl.when(j == 0)
    def _():
        xf = x_ref[...]
        ms = jnp.mean(xf * xf, axis=-1, keepdims=True)
        xn_ref[...] = (xf * lax.rsqrt(ms + EPS) * g_ref[...]).astype(BF16)

    @pl.when(i == 0)
    def _():
        carry_ref[j] = jnp.zeros(carry_ref.shape[1:], F32)

    xn = xn_ref[...]
    tm = xn.shape[0]
    rid = lax.broadcasted_iota(jnp.int32, (tm, wg_ref.shape[1]), 0)

    def conv(h, slot, cw_ref, cb_ref):
        prev = carry_ref[j, slot]
        carry_ref[j, slot] = h[tm - 8:tm, :]
        taps = cw_ref.shape[0]
        y = cw_ref[taps - 1:taps, :] * h + cb_ref[...]
        for back in range(1, taps):
            sh = pltpu.roll(h, back, 0)
            for r in range(back):
                sh = jnp.where(rid == r, prev[8 - back + r:8 - back + r + 1, :], sh)
            y = y + cw_ref[taps - 1 - back:taps - back, :] * sh
        return y

    gate = conv(_dot(xn, wg_ref[...]), 0, cg_ref, bg_ref)
    up = conv(_dot(xn, wu_ref[...]), 1, cu_ref, bu_ref)
    o_ref[...] = (gate * _sigmoid(gate) * up).astype(BF16)


def _ffn_in(x, gain, wg, wu, cg, cu, bg, bu, tm=1024, tn=512):
    L, D = x.shape
    N = wg.shape[1]
    taps = cg.shape[0]
    wspec = pl.BlockSpec((D, tn), lambda i, j: (0, j))
    cspec = pl.BlockSpec((taps, tn), lambda i, j: (0, j))
    bspec = pl.BlockSpec((1, tn), lambda i, j: (0, j))
    return pl.pallas_call(
        _ffn_in_kernel,
        grid=(L // tm, N // tn),
        in_specs=[pl.BlockSpec((tm, D), lambda i, j: (i, 0)),
                  pl.BlockSpec((1, D), lambda i, j: (0, 0)),
                  wspec, wspec, cspec, cspec, bspec, bspec],
        out_specs=pl.BlockSpec((tm, tn), lambda i, j: (i, j)),
        out_shape=jax.ShapeDtypeStruct((L, N), BF16),
        scratch_shapes=[pltpu.VMEM((tm, D), BF16), pltpu.VMEM((N // tn, 2, 8, tn), F32)],
        compiler_params=_cparams(("arbitrary", "arbitrary")),
        name="ffn_in",
    )(x, gain.reshape(1, D), wg, wu, cg, cu, bg, bu)


def _ffn_out_kernel(a_ref, w_ref, x_ref, o_ref):
    @pl.when(pl.program_id(1) == 0)
    def _():
        o_ref[...] = x_ref[...]

    o_ref[...] += _dot(a_ref[...], w_ref[...])


def _ffn_out(act, w, x, tm=512, tk=512):
    L, D = x.shape
    K = act.shape[1]
    return pl.pallas_call(
        _ffn_out_kernel,
        grid=(L // tm, K // tk),
        in_specs=[pl.BlockSpec((tm, tk), lambda i, k: (i, k)),
                  pl.BlockSpec((tk, D), lambda i, k: (k, 0)),
                  pl.BlockSpec((tm, D), lambda i, k: (i, 0))],
        out_specs=pl.BlockSpec((tm, D), lambda i, k: (i, 0)),
        out_shape=jax.ShapeDtypeStruct((L, D), F32),
        compiler_params=_cparams(("arbitrary", "arbitrary")),
        name="ffn_out",
    )(act, w, x)


def _rope_tables(pos):
    half = ROPE_DIMS // 2
    inv = ROPE_THETA ** (-jnp.arange(half, dtype=F32) / half)
    ang = pos.astype(F32)[:, None] * inv[None, :]
    n = pos.shape[0]
    cos = jnp.concatenate([jnp.cos(ang), jnp.cos(ang), jnp.ones((n, NSA_DH - ROPE_DIMS), F32)], axis=1)
    sin = jnp.concatenate([-jnp.sin(ang), jnp.sin(ang), jnp.zeros((n, NSA_DH - ROPE_DIMS), F32)], axis=1)
    return jnp.tile(cos, (1, 2)), jnp.tile(sin, (1, 2))


def _permute_w_in(w):
    sp = np.cumsum([0, 512, 128, 128, 128, 128, 128, 128, 24, 512, 3072, 1024, 8, 8])
    q, kc, vc, ks, vs, kw, vw, gates, u, qkv, z, a, b = [w[:, sp[n]:sp[n + 1]] for n in range(13)]
    pad = jnp.zeros((w.shape[0], PROJ_COLS - C_SM - 40), w.dtype)
    return jnp.concatenate([q, ks, kw, vs, vw, z, u, kc, vc, qkv, gates, a, b, pad], axis=1)


def _nsa_mixer(proj, small, p, l):
    L = proj.shape[0]
    HK, G, DH = NSA_KV, NSA_G, NSA_DH
    nQ = L // TQ
    ns = L // SLC_BLOCK
    nb = L // CMP_STRIDE
    n_grp = min(64, ns)
    pos = jnp.arange(L)
    cos, sin = _rope_tables(pos)
    gains = jnp.concatenate([jnp.tile(p['nsa_q_norm'][l], NSA_HEADS), jnp.tile(p['nsa_ks_norm'][l], HK),
                             jnp.tile(p['nsa_kw_norm'][l], HK), jnp.ones((2 * HK * DH,), F32)]).reshape(1, -1)
    prep = _nsa_prep(proj, gains, cos, sin)

    heads = lambda off: prep[:, off:off + HK * DH].reshape(L, HK, DH).transpose(1, 0, 2)
    qT = prep[:, :NSA_HEADS * DH].reshape(nQ, TQ, HK, G, DH).transpose(2, 0, 4, 3, 1).reshape(HK, nQ, DH, G * TQ)
    ks, kw, vs, vw = heads(C_KS), heads(C_KW), heads(C_VS), heads(C_VW)
    onehot = jax.nn.one_hot((pos // SLC_BLOCK) % n_grp, n_grp, dtype=BF16)
    ksaug = jnp.concatenate([ks, jnp.broadcast_to(onehot, (HK, L, n_grp))], axis=2)
    ones_rows = jnp.concatenate([jnp.ones((HK, 1, L), BF16), jnp.zeros((HK, V_ROWS - DH - 1, L), BF16)], axis=1)
    vsx = jnp.concatenate([vs.transpose(0, 2, 1), ones_rows], axis=1)
    kwp = jnp.pad(kw, ((0, 0), (WINDOW, 0), (0, KW_LANES - DH)))
    kwp = kwp.at[:, :WINDOW, DH].set(1.0)
    vwxp = jnp.pad(jnp.concatenate([vw.transpose(0, 2, 1), ones_rows], axis=1), ((0, 0), (0, 0), (WINDOW, 0)))

    half = lambda off: proj[:, off:off + HK * DH].reshape(nb, CMP_STRIDE, HK, DH).transpose(2, 0, 1, 3).reshape(
        HK, nb, CMP_STRIDE * DH)
    pe = p['cmp_pe'][l].reshape(2, 1, CMP_STRIDE * DH)
    w1 = lambda name: p[name][l].reshape(CMP_LEN * DH, -1).astype(BF16)
    w2 = lambda name: jnp.pad(p[name][l], ((0, 0), (0, LANES - DH))).astype(BF16)
    ccos, csin = _rope_tables(jnp.arange(nb) * CMP_STRIDE + CMP_LEN // 2)
    gk = jnp.pad(p['nsa_kc_norm'][l], (0, LANES - DH)).reshape(1, LANES)
    kc, vc = _nsa_compress(half(C_KC), half(C_VC), pe[0], pe[1], w1('cmp_k_w1'), w2('cmp_k_w2'),
                           w1('cmp_v_w1'), w2('cmp_v_w2'), gk, ccos, csin)
    vcx = jnp.concatenate([vc.transpose(0, 2, 1), ones_rows[:, :, :nb]], axis=1)

    cmp_start = np.arange(nb) * CMP_STRIDE
    slc_start = np.arange(ns) * SLC_BLOCK
    smapT = ((cmp_start[None, :] <= slc_start[:, None] + SLC_BLOCK - 1)
             & (cmp_start[None, :] + CMP_LEN - 1 >= slc_start[:, None]))
    smapT[:, nb - 1] = False
    smapT = jnp.asarray(smapT, BF16)

    gT = jnp.pad(small[:, :HK * G * 3].reshape(L, HK, G * 3).transpose(1, 2, 0), ((0, 0), (0, 16 - G * 3), (0, 0)))
    oT = _nsa_attention(qT, kc, vcx, smapT, ksaug, vsx, kwp, vwxp, gT, n_grp)
    return oT.reshape(HK, nQ, DH, G, TQ).transpose(1, 4, 0, 3, 2).reshape(L, NSA_HEADS * DH)


def _s5_mixer(proj, p, l):
    L = proj.shape[0]
    T = S5_CHUNK
    nc = L // T
    G, H, P = S5_GROUPS, S5_CH, S5_STATE
    lam_re, lam_im = p['s5_lam_re'][l], p['s5_lam_im'][l]
    dt = jnp.exp(p['s5_log_dt'][l])[:, None]
    mag = jnp.exp(lam_re * dt)
    lb_re = mag * jnp.cos(lam_im * dt)
    lb_im = mag * jnp.sin(lam_im * dt)
    den = lam_re * lam_re + lam_im * lam_im
    nr, ni = lb_re - 1.0, lb_im
    coef_re = (nr * lam_re + ni * lam_im) / den
    coef_im = (ni * lam_re - nr * lam_im) / den
    b_re, b_im = p['s5_b_re'][l], p['s5_b_im'][l]
    bb_re = coef_re[..., None] * b_re - coef_im[..., None] * b_im
    bb_im = coef_re[..., None] * b_im + coef_im[..., None] * b_re
    bsr, bsi = bb_re.transpose(0, 2, 1), bb_im.transpose(0, 2, 1)
    tile_rows = lambda a: jnp.tile(a, (1, T, 1))
    u = proj[:, C_U:C_U + S5_WIDTH]
    ug = u.astype(BF16).reshape(nc, T, G, H).transpose(2, 0, 1, 3).reshape(G, nc, T * H)
    y = _s5_scan(ug, lb_re[:, None, :], lb_im[:, None, :], tile_rows(p['s5_c_re'][l]), tile_rows(p['s5_c_im'][l]),
                 tile_rows(bsr), tile_rows(bsi), bsr, bsi)
    return y.reshape(G, nc, T, H).transpose(1, 2, 0, 3).reshape(L, S5_WIDTH)


def _gdn_mixer(proj, small, p, l):
    L = proj.shape[0]
    H = GDN_HEADS
    qkv = _gdn_prep(proj, p['gdn_conv'][l])
    g3 = small[:, 24:32].T[:, :, None]
    b3 = small[:, 32:40].T[:, :, None]
    val, kcum, qg, kd, qk, gl = _gdn_local(qkv, g3, b3)
    kdT = kd.reshape(L // GDN_CHUNK, GDN_CHUNK, H, GDN_DH).transpose(2, 0, 3, 1)
    return _gdn_scan(val, kcum, qg, qk, kdT, gl, proj, p['gdn_norm'][l].reshape(1, GDN_DH))


def _pad_cols(a, n):
    return jnp.pad(a, ((0, 0), (0, n - a.shape[1])))


def _forward(x3, p):
    x = x3[0]
    depth = p['w_in'].shape[0]
    for l in range(depth):
        w_in = _permute_w_in(p['w_in'][l]).astype(BF16)
        proj = _rms_matmul(x, p['attn_norm'][l], w_in)
        sm_par = jnp.zeros((8, LANES), F32)
        sm_par = sm_par.at[0, 24:32].set(p['gdn_dt_bias'][l]).at[1, 24:32].set(-jnp.exp(p['gdn_a_log'][l]))
        small = _small(proj, sm_par)
        y_a = _nsa_mixer(proj, small, p, l)
        y_s = _s5_mixer(proj, p, l)
        y_c = _gdn_mixer(proj, small, p, l)
        x = _mix(y_a, y_s, proj, y_c, x, p['nsa_out_norm'][l].reshape(1, -1), p['s5_out_norm'][l].reshape(1, -1),
                 p['s5_d'][l].reshape(1, -1), p['s5_w_glu'][l].astype(BF16), p['w_out'][l].astype(BF16))
        wf = p['ffn_w_in'][l]
        cf = p['ffn_conv'][l]
        bf = p['ffn_conv_b'][l].reshape(1, -1)
        act = _ffn_in(x, p['ffn_norm'][l],
                      _pad_cols(wf[:, :D_FF], D_FF_PAD).astype(BF16), _pad_cols(wf[:, D_FF:], D_FF_PAD).astype(BF16),
                      _pad_cols(cf[:, :D_FF], D_FF_PAD), _pad_cols(cf[:, D_FF:], D_FF_PAD),
                      _pad_cols(bf[:, :D_FF], D_FF_PAD), _pad_cols(bf[:, D_FF:], D_FF_PAD))
        w_o = jnp.pad(p['ffn_w_out'][l], ((0, D_FF_PAD - D_FF), (0, 0))).astype(BF16)
        x = _ffn_out(act, w_o, x)
    return x[None]


_PARAM_NAMES = ('attn_norm', 'w_in', 'nsa_q_norm', 'nsa_kc_norm', 'nsa_ks_norm', 'nsa_kw_norm', 'cmp_pe',
                'cmp_k_w1', 'cmp_k_w2', 'cmp_v_w1', 'cmp_v_w2', 'nsa_out_norm', 's5_lam_re', 's5_lam_im',
                's5_log_dt', 's5_b_re', 's5_b_im', 's5_c_re', 's5_c_im', 's5_d', 's5_w_glu', 's5_out_norm',
                'gdn_conv', 'gdn_a_log', 'gdn_dt_bias', 'gdn_norm', 'w_out', 'ffn_norm', 'ffn_w_in', 'ffn_conv',
                'ffn_conv_b', 'ffn_w_out')


def kernel(x, attn_norm, w_in, nsa_q_norm, nsa_kc_norm, nsa_ks_norm, nsa_kw_norm, cmp_pe, cmp_k_w1, cmp_k_w2,
           cmp_v_w1, cmp_v_w2, nsa_out_norm, s5_lam_re, s5_lam_im, s5_log_dt, s5_b_re, s5_b_im, s5_c_re, s5_c_im,
           s5_d, s5_w_glu, s5_out_norm, gdn_conv, gdn_a_log, gdn_dt_bias, gdn_norm, w_out, ffn_norm, ffn_w_in,
           ffn_conv, ffn_conv_b, ffn_w_out):
    vals = (attn_norm, w_in, nsa_q_norm, nsa_kc_norm, nsa_ks_norm, nsa_kw_norm, cmp_pe, cmp_k_w1, cmp_k_w2,
            cmp_v_w1, cmp_v_w2, nsa_out_norm, s5_lam_re, s5_lam_im, s5_log_dt, s5_b_re, s5_b_im, s5_c_re, s5_c_im,
            s5_d, s5_w_glu, s5_out_norm, gdn_conv, gdn_a_log, gdn_dt_bias, gdn_norm, w_out, ffn_norm, ffn_w_in,
            ffn_conv, ffn_conv_b, ffn_w_out)
    return _forward(x, dict(zip(_PARAM_NAMES, vals)))
```

```python
import functools
import math

import jax
import jax.numpy as jnp
import numpy as np
from jax import lax
from jax.experimental import pallas as pl
from jax.experimental.pallas import tpu as pltpu

F32 = jnp.float32
BF16 = jnp.bfloat16
HIGHEST = lax.Precision.HIGHEST

D_MODEL = 2048
NSA_HEADS = 8
NSA_KV = 2
NSA_G = NSA_HEADS // NSA_KV
NSA_DH = 64
CMP_LEN = 32
CMP_STRIDE = 16
SLC_BLOCK = 64
SLC_TOPK = 16
WINDOW = 512
ROPE_THETA = 500000.0
ROPE_DIMS = NSA_DH // 4
S5_WIDTH = 512
S5_CH = 16
S5_GROUPS = 32
S5_STATE = 64
S5_CHUNK = 32
GDN_HEADS = 8
GDN_DH = 128
GDN_WIDTH = 1024
GDN_CHUNK = 64
D_FF = 5504
D_FF_PAD = 5632
EPS = 1e-6
NEG_BIG = -(2.0 ** 100)
LOG2_E = math.log2(math.e)
V_ROWS = 80
KW_LANES = 128
CMP_CHUNK = 256

LANES = 128
TQ = 128
TK = 512
VMEM_LIMIT = 56 * 1024 * 1024

C_Q, C_KS, C_KW, C_VS, C_VW = 0, 512, 640, 768, 896
C_Z, C_QKV, C_U, C_KC, C_VC, C_SM = 1024, 2048, 5120, 5632, 5760, 5888
NSA_PREP_W = C_Z
PROJ_COLS = 6144
SM_GATES, SM_G, SM_BETA, SM_END = 0, 24, 32, 40


def _cparams(sem):
    return pltpu.CompilerParams(dimension_semantics=sem, vmem_limit_bytes=VMEM_LIMIT)


def _dot(a, b, precision=None):
    return jnp.dot(a, b, preferred_element_type=F32, precision=precision)


def _dot_nt(a, b, precision=None):
    return lax.dot_general(a, b, (((1,), (1,)), ((), ())), preferred_element_type=F32, precision=precision)


def _gelu(x):
    return x * (0.5 * (1.0 + jnp.tanh(math.sqrt(2.0 / math.pi) * (x + 0.044715 * (x * x * x)))))


def _sigmoid(x):
    return 1.0 / (1.0 + jnp.exp(-x))


def _rms_matmul_kernel(x_ref, g_ref, w_ref, o_ref, xn_ref):
    @pl.when(pl.program_id(1) == 0)
    def _():
        xf = x_ref[...]
        ms = jnp.mean(xf * xf, axis=-1, keepdims=True)
        xn_ref[...] = (xf * lax.rsqrt(ms + EPS) * g_ref[...]).astype(BF16)

    o_ref[...] = _dot(xn_ref[...], w_ref[...])


def _rms_matmul(x, gain, w, tm=512, tn=1024):
    L, D = x.shape
    N = w.shape[1]
    return pl.pallas_call(
        _rms_matmul_kernel,
        grid=(L // tm, N // tn),
        in_specs=[
            pl.BlockSpec((tm, D), lambda i, j: (i, 0)),
            pl.BlockSpec((1, D), lambda i, j: (0, 0)),
            pl.BlockSpec((D, tn), lambda i, j: (0, j)),
        ],
        out_specs=pl.BlockSpec((tm, tn), lambda i, j: (i, j)),
        out_shape=jax.ShapeDtypeStruct((L, N), F32),
        scratch_shapes=[pltpu.VMEM((tm, D), BF16)],
        compiler_params=_cparams(("arbitrary", "arbitrary")),
        name="rms_proj",
    )(x, gain.reshape(1, D), w)


def _rope_slab(x, cos, sin):
    d = lax.broadcasted_iota(jnp.int32, x.shape, 1) & (NSA_DH - 1)
    half = ROPE_DIMS // 2
    partner = jnp.where(d < half, pltpu.roll(x, LANES - half, 1), pltpu.roll(x, half, 1))
    return x * cos + partner * sin


def _nsa_prep_kernel(x_ref, g_ref, cos_ref, sin_ref, bsum_ref, o_ref):
    cos = cos_ref[...]
    sin = sin_ref[...]
    bsum = bsum_ref[...]
    n_rot = (C_VS - C_Q) // LANES
    for s in range(NSA_PREP_W // LANES):
        sl = slice(s * LANES, (s + 1) * LANES)
        x = x_ref[:, sl]
        if s < n_rot:
            ms = _dot(x * x, bsum, precision=HIGHEST)
            y = x * lax.rsqrt(ms + EPS) * g_ref[:, sl]
            y = _rope_slab(y, cos, sin)
            if s < (C_KS - C_Q) // LANES:
                y = y * (NSA_DH ** -0.5 * LOG2_E)
            o_ref[:, sl] = y.astype(BF16)
        else:
            o_ref[:, sl] = x.astype(BF16)


def _nsa_prep(proj, gains, cos, sin, tm=512):
    L = proj.shape[0]
    W = NSA_PREP_W
    bsum = jnp.asarray(np.kron(np.eye(2), np.ones((NSA_DH, NSA_DH))) / NSA_DH, F32)
    return pl.pallas_call(
        _nsa_prep_kernel,
        grid=(L // tm,),
        in_specs=[
            pl.BlockSpec((tm, W), lambda i: (i, 0)),
            pl.BlockSpec((1, W), lambda i: (0, 0)),
            pl.BlockSpec((tm, LANES), lambda i: (i, 0)),
            pl.BlockSpec((tm, LANES), lambda i: (i, 0)),
            pl.BlockSpec((LANES, LANES), lambda i: (0, 0)),
        ],
        out_specs=pl.BlockSpec((tm, W), lambda i: (i, 0)),
        out_shape=jax.ShapeDtypeStruct((L, W), BF16),
        compiler_params=_cparams(("arbitrary",)),
        name="nsa_prep",
    )(proj, gains, cos, sin, bsum)


def _cmp_kernel(hbk_ref, hbv_ref, pea_ref, peb_ref, w1k_ref, w2k_ref, w1v_ref, w2v_ref,
                gk_ref, cos_ref, sin_ref, kc_ref, vc_ref):
    nb = hbk_ref.shape[1]
    half_w = hbk_ref.shape[2]
    pea = pea_ref[...]
    peb = peb_ref[...]

    def mlp(hb, w1_ref, w2_ref):
        a = (hb + pea).astype(BF16)
        b = (hb + peb).astype(BF16)
        p1 = _dot(a, w1_ref[0:half_w, :])
        p2 = _dot(b, w1_ref[half_w:2 * half_w, :])
        h = p1 + pltpu.roll(p2, nb - 1, 0)
        return _dot(_gelu(h).astype(BF16), w2_ref[...])

    rid = lax.broadcasted_iota(jnp.int32, (nb, LANES), 0)
    real = rid < nb - 1
    kc = jnp.where(real, mlp(hbk_ref[0], w1k_ref, w2k_ref), 0.0)
    vc = jnp.where(real, mlp(hbv_ref[0], w1v_ref, w2v_ref), 0.0)
    ms = jnp.sum(kc * kc, axis=-1, keepdims=True) * (1.0 / NSA_DH)
    kcn = kc * lax.rsqrt(ms + EPS) * gk_ref[...]
    kcn = _rope_slab(kcn, cos_ref[...], sin_ref[...])
    kc_ref[0] = kcn[:, 0:NSA_DH].astype(BF16)
    vc_ref[0] = vc[:, 0:NSA_DH].astype(BF16)


def _nsa_compress(hbk, hbv, pea, peb, w1k, w2k, w1v, w2v, gk, cos, sin):
    HK, nb, half_w = hbk.shape
    hid = w1k.shape[1]
    full = lambda shape: pl.BlockSpec(shape, lambda h: (0,) * len(shape))
    return pl.pallas_call(
        _cmp_kernel,
        grid=(HK,),
        in_specs=[
            pl.BlockSpec((1, nb, half_w), lambda h: (h, 0, 0)),
            pl.BlockSpec((1, nb, half_w), lambda h: (h, 0, 0)),
            full((1, half_w)), full((1, half_w)),
            full((2 * half_w, hid)), full((hid, LANES)),
            full((2 * half_w, hid)), full((hid, LANES)),
            full((1, LANES)), full((nb, LANES)), full((nb, LANES)),
        ],
        out_specs=[pl.BlockSpec((1, nb, NSA_DH), lambda h: (h, 0, 0)),
                   pl.BlockSpec((1, nb, NSA_DH), lambda h: (h, 0, 0))],
        out_shape=[jax.ShapeDtypeStruct((HK, nb, NSA_DH), BF16)] * 2,
        compiler_params=_cparams(("arbitrary",)),
        name="nsa_compress",
    )(hbk, hbv, pea, peb, w1k, w2k, w1v, w2v, gk, cos, sin)


def _nsa_attn_kernel(qT_ref, kc_ref, vcx_ref, smap_ref, ksaug_ref, vsx_ref, kwp_ref, vwxp_ref, gT_ref,
                     o_ref, qaug_ref, qwin_ref, ss_ref, mt_ref, m_ref, acc_ref, sc_ref, imp_ref, *, n_grp):
    i = pl.program_id(0)
    s0 = i * TQ
    hks = range(qT_ref.shape[0])
    nb = kc_ref.shape[1]
    ns = smap_ref.shape[0]
    GW = NSA_G * TQ
    qT = [qT_ref[h, 0] for h in hks]
    tq = s0 + (lax.broadcasted_iota(jnp.int32, (1, GW), 1) & (TQ - 1))

    cch = min(CMP_CHUNK, nb)
    n_cch = (s0 + TQ - CMP_LEN) // (CMP_STRIDE * cch) + 1
    nthr = (tq - (CMP_LEN - 1)) // CMP_STRIDE

    def cmp_scores(c, mcs):
        r0c = pl.multiple_of(c * cch, cch)
        s = [_dot(kc_ref[h, pl.ds(r0c, cch), :], qT[h]) for h in hks]
        nrow = r0c + lax.broadcasted_iota(jnp.int32, (cch, GW), 0)
        out = []
        for h in hks:
            sh = jnp.where(nrow <= nthr, s[h], NEG_BIG)
            sc_ref[h, pl.ds(r0c, cch), :] = sh
            out.append(jnp.maximum(mcs[h], jnp.max(sh, axis=0, keepdims=True)))
        return tuple(out)

    mc = lax.fori_loop(0, n_cch, cmp_scores, tuple(jnp.full((1, GW), NEG_BIG, F32) for _ in hks))
    imp_ref[...] = jnp.zeros_like(imp_ref)
    ones16 = jnp.ones((16, cch), BF16)
    VX = vcx_ref.shape[1]

    def cmp_accum(c, carry):
        r0c = pl.multiple_of(c * cch, cch)
        sm = smap_ref[:, pl.ds(r0c, cch)]
        parts = [_split2(jnp.exp2(sc_ref[h, pl.ds(r0c, cch), :] - mc[h])) for h in hks]
        for h in hks:
            imp_ref[h] += _dot(sm, parts[h][0]) + _dot(sm, parts[h][1])
        return tuple((carry[h][0] + _dot(vcx_ref[h, :, pl.ds(r0c, cch)], parts[h][0]),
                      carry[h][1] + _dot(ones16, parts[h][1])) for h in hks)

    cacc = lax.fori_loop(0, n_cch, cmp_accum,
                         tuple((jnp.zeros((VX, GW), F32), jnp.zeros((16, GW), F32)) for _ in hks))
    inv_l = [jnp.where(tq >= CMP_LEN - 1, 1.0 / (cacc[h][0][NSA_DH:NSA_DH + 1] + cacc[h][1][0:1]), 0.0) for h in hks]
    ocT = [cacc[h][0][0:NSA_DH] * inv_l[h] for h in hks]

    imp = []
    for h in hks:
        tot = imp_ref[h, :, 0:TQ] * inv_l[h][:, 0:TQ]
        for g in range(1, NSA_G):
            tot = tot + imp_ref[h, :, g * TQ:(g + 1) * TQ] * inv_l[h][:, g * TQ:(g + 1) * TQ]
        imp.append(tot)

    jrow = lax.broadcasted_iota(jnp.int32, (ns, TQ), 0)
    cur = (s0 + lax.broadcasted_iota(jnp.int32, (1, TQ), 1)) // SLC_BLOCK
    forced = jnp.where(jrow == 0, 0.0, jnp.where(jrow == cur, 0.0, jnp.where(jrow == cur - 1, 0.0, NEG_BIG)))
    val = [jnp.where(jrow >= 1, jnp.where(jrow <= cur - 2, imp[h], -jnp.inf), -jnp.inf) for h in hks]
    for _ in range(min(SLC_TOPK, ns) - 3):
        mx = [jnp.max(val[h], axis=0, keepdims=True) for h in hks]
        idx = [jnp.min(jnp.where(val[h] == mx[h], jrow, ns), axis=0, keepdims=True) for h in hks]
        val = [jnp.where(jrow == idx[h], -jnp.inf, val[h]) for h in hks]
    bias = [jnp.where(jrow >= 1, jnp.where(jrow <= cur - 2, jnp.where(val[h] == -jnp.inf, 0.0, NEG_BIG), forced), forced)
            for h in hks]
    own = jrow // 2 == i
    for h in hks:
        bias16 = jnp.where(own, NEG_BIG, bias[h]).astype(BF16)
        for grp in range(ns // n_grp):
            qaug_ref[h, grp, 0:NSA_DH, :] = qT[h]
            for g in range(NSA_G):
                qaug_ref[h, grp, NSA_DH:NSA_DH + n_grp, g * TQ:(g + 1) * TQ] = bias16[grp * n_grp:(grp + 1) * n_grp, :]

    tiles_per_grp = (n_grp * SLC_BLOCK) // TK
    n_pairs = (s0 + 2 * TK - 1) // (2 * TK)

    def scores(kt, slot):
        k0 = pl.multiple_of(kt * TK, TK)
        grp = kt // tiles_per_grp
        s = [_dot(ksaug_ref[h, pl.ds(k0, TK), :], qaug_ref[h, grp]) for h in hks]
        for h in hks:
            sb = s[h].astype(BF16)
            ss_ref[h, slot] = sb
            mt_ref[h, slot] = jnp.max(sb, axis=0, keepdims=True).astype(F32)

    def online_update(sb, mt, v):
        m = [m_ref[h] for h in hks]
        mn = [jnp.maximum(m[h], mt[h]) for h in hks]
        pv = [_dot(v[h], jnp.exp2(sb[h] - mn[h].astype(BF16))) for h in hks]
        for h in hks:
            acc_ref[h] = jnp.exp2(m[h] - mn[h]) * acc_ref[h] + pv[h]
            m_ref[h] = mn[h]

    def accumulate(kt, slot):
        k0 = pl.multiple_of(kt * TK, TK)
        online_update([ss_ref[h, slot] for h in hks], [mt_ref[h, slot] for h in hks],
                      [vsx_ref[h, :, pl.ds(k0, TK)] for h in hks])

    m_ref[...] = jnp.full(m_ref.shape, NEG_BIG, F32)
    acc_ref[...] = jnp.zeros_like(acc_ref)

    @pl.when(n_pairs > 0)
    def _():
        scores(0, 0)

    def pair_body(j, carry):
        scores(2 * j + 1, 1)
        accumulate(2 * j, 0)
        scores(2 * j + 2, 0)
        accumulate(2 * j + 1, 1)
        return carry

    lax.fori_loop(0, n_pairs - 1, pair_body, 0)

    @pl.when(n_pairs > 0)
    def _():
        scores(2 * n_pairs - 1, 1)
        accumulate(2 * n_pairs - 2, 0)
        accumulate(2 * n_pairs - 1, 1)

    r0 = pl.multiple_of(s0, TQ)
    rel = lax.broadcasted_iota(jnp.int32, (TQ, GW), 0)
    qrel = tq - s0
    sd = [_dot(ksaug_ref[h, pl.ds(r0, TQ), 0:NSA_DH], qT[h]) for h in hks]
    sd16 = [jnp.where(rel <= qrel, sd[h], NEG_BIG).astype(BF16) for h in hks]
    online_update(sd16, [jnp.max(sd16[h], axis=0, keepdims=True).astype(F32) for h in hks],
                  [vsx_ref[h, :, pl.ds(r0, TQ)] for h in hks])
    osT = [acc_ref[h, 0:NSA_DH, :] / acc_ref[h, NSA_DH:NSA_DH + 1, :] for h in hks]

    flag_row = lax.broadcasted_iota(jnp.int32, (KW_LANES - NSA_DH, GW), 0) == 0
    for h in hks:
        qwin_ref[h, 0:NSA_DH, :] = qT[h]
        qwin_ref[h, NSA_DH:KW_LANES, :] = jnp.where(flag_row, NEG_BIG, 0.0).astype(BF16)
    qwin = [qwin_ref[h] for h in hks]
    edge = [_dot(kwp_ref[h, pl.ds(r0, TQ), :], qwin[h]) for h in hks]
    mid = [_dot(kwp_ref[h, pl.ds(r0 + TQ, WINDOW - TQ), :], qwin[h]) for h in hks]
    diag = [_dot(kwp_ref[h, pl.ds(r0 + WINDOW, TQ), :], qwin[h]) for h in hks]
    edge16 = [jnp.where(rel > qrel, edge[h], NEG_BIG).astype(BF16) for h in hks]
    diag16 = [jnp.where(rel <= qrel, diag[h], NEG_BIG).astype(BF16) for h in hks]
    mid16 = [mid[h].astype(BF16) for h in hks]
    mw = [jnp.maximum(jnp.maximum(jnp.max(edge16[h], axis=0, keepdims=True), jnp.max(mid16[h], axis=0, keepdims=True)),
                      jnp.max(diag16[h], axis=0, keepdims=True)) for h in hks]
    accw = [(_dot(vwxp_ref[h, :, pl.ds(r0, TQ)], jnp.exp2(edge16[h] - mw[h]))
             + _dot(vwxp_ref[h, :, pl.ds(r0 + TQ, WINDOW - TQ)], jnp.exp2(mid16[h] - mw[h]))
             + _dot(vwxp_ref[h, :, pl.ds(r0 + WINDOW, TQ)], jnp.exp2(diag16[h] - mw[h]))) for h in hks]

    for h in hks:
        owT = accw[h][0:NSA_DH] / accw[h][NSA_DH:NSA_DH + 1]
        gates = _sigmoid(gT_ref[h])
        for g in range(NSA_G):
            sl = slice(g * TQ, (g + 1) * TQ)
            o_ref[h, 0, :, sl] = (gates[3 * g:3 * g + 1] * ocT[h][:, sl] + gates[3 * g + 1:3 * g + 2] * osT[h][:, sl]
                                  + gates[3 * g + 2:3 * g + 3] * owT[:, sl])


def _nsa_attention(qT, kc, vcx, smapT, ksaug, vsx, kwp, vwxp, gT, n_grp):
    HK, nQ, _, GW = qT.shape
    nb = kc.shape[1]
    ns = smapT.shape[0]
    L = vsx.shape[2]
    KA = ksaug.shape[2]
    scratch = [pltpu.VMEM((HK, ns // n_grp, KA, GW), BF16),
               pltpu.VMEM((HK, KW_LANES, GW), BF16),
               pltpu.VMEM((HK, 2, TK, GW), BF16),
               pltpu.VMEM((HK, 2, 1, GW), F32),
               pltpu.VMEM((HK, 1, GW), F32),
               pltpu.VMEM((HK, V_ROWS, GW), F32),
               pltpu.VMEM((HK, nb, GW), F32),
               pltpu.VMEM((HK, ns, GW), F32)]
    resident = lambda a: pl.BlockSpec(a.shape, lambda i: (0,) * a.ndim, pipeline_mode=pl.Buffered(1))
    return pl.pallas_call(
        functools.partial(_nsa_attn_kernel, n_grp=n_grp),
        grid=(nQ,),
        in_specs=[
            pl.BlockSpec((HK, 1, NSA_DH, GW), lambda i: (0, i, 0, 0)),
            resident(kc), resident(vcx), resident(smapT), resident(ksaug), resident(vsx), resident(kwp),
            resident(vwxp),
            pl.BlockSpec((HK, 16, TQ), lambda i: (0, 0, i)),
        ],
        out_specs=pl.BlockSpec((HK, 1, NSA_DH, GW), lambda i: (0, i, 0, 0)),
        out_shape=jax.ShapeDtypeStruct((HK, nQ, NSA_DH, GW), F32),
        scratch_shapes=scratch,
        compiler_params=_cparams(("arbitrary",)),
        name="nsa_attn",
    )(qT, kc, vcx, smapT, ksaug, vsx, kwp, vwxp, gT)


def _s5_kernel(u_ref, lbr_ref, lbi_ref, ctr_ref, cti_ref, btr_ref, bti_ref, bsr_ref, bsi_ref, y_ref, lm_ref):
    T = S5_CHUNK
    W = T * S5_CH
    ar = lbr_ref[0]
    ai = lbi_ref[0]
    delta = lax.broadcasted_iota(jnp.int32, (W, S5_STATE), 0) // S5_CH

    def powers(e):
        pr = jnp.ones((W, S5_STATE), F32)
        pi = jnp.zeros((W, S5_STATE), F32)
        fr, fi = ar, ai
        for b in range(T.bit_length() - 1):
            bit = ((e >> b) & 1) == 1
            nr = pr * fr - pi * fi
            ni = pr * fi + pi * fr
            pr = jnp.where(bit, nr, pr)
            pi = jnp.where(bit, ni, pi)
            fr, fi = fr * fr - fi * fi, 2.0 * fr * fi
        return pr, pi, fr, fi

    pwr, pwi, aTr, aTi = powers(delta)
    rvr, rvi, _, _ = powers(T - 1 - delta)
    ctr, cti = ctr_ref[0], cti_ref[0]
    car = ctr * pwr - cti * pwi
    cai = ctr * pwi + cti * pwr

    kw = _dot_nt(bsr_ref[0], car, HIGHEST) - _dot_nt(bsi_ref[0], cai, HIGHEST)
    lane = lax.broadcasted_iota(jnp.int32, (S5_CH, W), 1)
    for tau in range(T):
        sh = S5_CH * tau
        blk = kw if tau == 0 else jnp.where(lane >= sh, pltpu.roll(kw, sh, 1), 0.0)
        lm_ref[sh:sh + S5_CH, :] = blk.astype(BF16)

    u = u_ref[0]
    nc = u.shape[0]
    y = _dot(u, lm_ref[...])

    btr, bti = btr_ref[0], bti_ref[0]
    sr = _dot(u, (rvr * btr - rvi * bti).astype(BF16))
    si = _dot(u, (rvr * bti + rvi * btr).astype(BF16))
    rowc = lax.broadcasted_iota(jnp.int32, (nc, S5_STATE), 0)
    fr, fi = aTr, aTi
    step = 1
    while step < nc:
        shr = jnp.where(rowc >= step, pltpu.roll(sr, step, 0), 0.0)
        shi = jnp.where(rowc >= step, pltpu.roll(si, step, 0), 0.0)
        sr, si = sr + fr * shr - fi * shi, si + fr * shi + fi * shr
        fr, fi = fr * fr - fi * fi, 2.0 * fr * fi
        step *= 2
    xr = jnp.where(rowc >= 1, pltpu.roll(sr, 1, 0), 0.0)
    xi = jnp.where(rowc >= 1, pltpu.roll(si, 1, 0), 0.0)
    c1r = car * ar - cai * ai
    c1i = car * ai + cai * ar
    y = y + _dot_nt(xr.astype(BF16), c1r.astype(BF16)) - _dot_nt(xi.astype(BF16), c1i.astype(BF16))
    y_ref[0] = y


def _s5_scan(ug, lbr, lbi, ctr, cti, btr, bti, bsr, bsi):
    G, nc, W = ug.shape
    grp = lambda shape: pl.BlockSpec((1,) + shape, lambda g: (g, 0, 0))
    return pl.pallas_call(
        _s5_kernel,
        grid=(G,),
        in_specs=[grp((nc, W)), grp((1, S5_STATE)), grp((1, S5_STATE)),
                  grp((W, S5_STATE)), grp((W, S5_STATE)), grp((W, S5_STATE)), grp((W, S5_STATE)),
                  grp((S5_CH, S5_STATE)), grp((S5_CH, S5_STATE))],
        out_specs=grp((nc, W)),
        out_shape=jax.ShapeDtypeStruct((G, nc, W), F32),
        scratch_shapes=[pltpu.VMEM((W, W), BF16)],
        compiler_params=_cparams(("arbitrary",)),
        name="s5_scan",
    )(ug, lbr, lbi, ctr, cti, btr, bti, bsr, bsi)


def _small_kernel(x_ref, p_ref, o_ref):
    x = x_ref[...]
    lane = lax.broadcasted_iota(jnp.int32, x.shape, 1)
    z = x + p_ref[0:1, :]
    softplus = jnp.maximum(z, 0.0) + jnp.log(1.0 + jnp.exp(-jnp.abs(z)))
    gdec = p_ref[1:2, :] * softplus
    beta = _sigmoid(x)
    o_ref[...] = jnp.where(lane < SM_G, x, jnp.where(lane < SM_BETA, gdec, beta))


def _small(proj, params, tm=1024):
    L = proj.shape[0]
    cb = C_SM // LANES
    return pl.pallas_call(
        _small_kernel,
        grid=(L // tm,),
        in_specs=[pl.BlockSpec((tm, LANES), lambda i: (i, cb)), pl.BlockSpec((8, LANES), lambda i: (0, 0))],
        out_specs=pl.BlockSpec((tm, LANES), lambda i: (i, 0)),
        out_shape=jax.ShapeDtypeStruct((L, LANES), F32),
        compiler_params=_cparams(("arbitrary",)),
        name="small_cols",
    )(proj, params)


def _gdn_prep_kernel(x_ref, cw_ref, o_ref, carry_ref):
    i = pl.program_id(0)
    j = pl.program_id(1)
    tm = x_ref.shape[0]

    @pl.when(i == 0)
    def _():
        carry_ref[j] = jnp.zeros(carry_ref.shape[1:], F32)

    rid = lax.broadcasted_iota(jnp.int32, (tm, GDN_DH), 0)
    taps = cw_ref.shape[0]
    scale = jnp.where(j == 0, GDN_DH ** -0.5, 1.0)
    for h in range(GDN_HEADS):
        sl = slice(h * GDN_DH, (h + 1) * GDN_DH)
        x = x_ref[:, sl]
        prev = carry_ref[j, :, sl]
        carry_ref[j, :, sl] = x[tm - 8:tm, :]
        y = cw_ref[taps - 1:taps, sl] * x
        for back in range(1, taps):
            sh = pltpu.roll(x, back, 0)
            for r in range(back):
                sh = jnp.where(rid == r, prev[8 - back + r:8 - back + r + 1, :], sh)
            y = y + cw_ref[taps - 1 - back:taps - back, sl] * sh
        y = y * _sigmoid(y)
        ss = jnp.sum(y * y, axis=-1, keepdims=True)
        yn = y * (lax.rsqrt(ss + EPS) * scale)
        o_ref[:, sl] = jnp.where(j < 2, yn, y)


def _gdn_prep(proj, conv_w, tm=512):
    L = proj.shape[0]
    cb = C_QKV // GDN_WIDTH
    return pl.pallas_call(
        _gdn_prep_kernel,
        grid=(L // tm, 3),
        in_specs=[pl.BlockSpec((tm, GDN_WIDTH), lambda i, j: (i, cb + j)),
                  pl.BlockSpec((conv_w.shape[0], GDN_WIDTH), lambda i, j: (0, j))],
        out_specs=pl.BlockSpec((tm, GDN_WIDTH), lambda i, j: (i, j)),
        out_shape=jax.ShapeDtypeStruct((L, 3 * GDN_WIDTH), F32),
        scratch_shapes=[pltpu.VMEM((3, 8, GDN_WIDTH), F32)],
        compiler_params=_cparams(("arbitrary", "arbitrary")),
        name="gdn_prep",
    )(proj, conv_w)


GDN_CB = 8


def _split2(x):
    hi = x.astype(BF16)
    return hi, (x - hi.astype(F32)).astype(BF16)


def _dot_split(a, b):
    ah, al = _split2(a)
    bh, bl = _split2(b)
    return _dot(ah, bh) + (_dot(ah, bl) + _dot(al, bh))


def _gdn_local_kernel(q_ref, k_ref, v_ref, sm_ref, val_ref, kcum_ref, qg_ref, kd_ref, qk_ref, gl_ref):
    C = GDN_CHUNK
    chunks = range(GDN_CB)
    rows = [slice(c * C, (c + 1) * C) for c in chunks]
    ii = lax.broadcasted_iota(jnp.int32, (C, C), 0)
    jj = lax.broadcasted_iota(jnp.int32, (C, C), 1)
    causal = ii >= jj
    strict = ii > jj
    eye = (ii == jj).astype(F32)
    tril16 = causal.astype(BF16)
    row2 = lax.broadcasted_iota(jnp.int32, (C, 2 * C), 0)
    col2 = lax.broadcasted_iota(jnp.int32, (C, 2 * C), 1)
    keep = row2 > jnp.where(col2 < C, -1, col2 - C)

    head = pl.program_id(0)
    sm = sm_ref[...]
    sm_lane = lax.broadcasted_iota(jnp.int32, sm.shape, 1)
    g_col = jnp.sum(jnp.where(sm_lane == SM_G + head, sm, 0.0), axis=1, keepdims=True)
    b_col = jnp.sum(jnp.where(sm_lane == SM_BETA + head, sm, 0.0), axis=1, keepdims=True)

    k = [k_ref[rs, :] for rs in rows]
    beta = [b_col[rs, :] for rs in rows]
    res = []
    for rs in rows:
        x = jnp.where(keep, jnp.broadcast_to(g_col[rs, :], (C, 2 * C)), 0.0)
        hi = x.astype(BF16)
        r1 = x - hi.astype(F32)
        mid = r1.astype(BF16)
        lo = (r1 - mid.astype(F32)).astype(BF16)
        res.append(_dot(tril16, hi) + (_dot(tril16, mid) + _dot(tril16, lo)))
    gc = [jnp.broadcast_to(r[:, 0:1], (C, GDN_DH)) for r in res]
    decay = [jnp.where(causal, jnp.exp(r[:, C:2 * C]), 0.0) for r in res]
    kb = [kc * bc for kc, bc in zip(k, beta)]
    k16 = [kc.astype(BF16) for kc in k]
    a = [jnp.where(strict, _dot_nt(kbc.astype(BF16), kc16) * dc, 0.0) for kbc, kc16, dc in zip(kb, k16, decay)]
    t = [eye - ac for ac in a]
    p = a
    for _ in range(int(math.log2(C)) - 1):
        p = [_dot_split(pc, pc) for pc in p]
        t = [tc + _dot_split(tc, pc) for tc, pc in zip(t, p)]
    t16 = [tc.astype(BF16) for tc in t]
    egc = [jnp.exp(gcc) for gcc in gc]
    val = [_dot(tc, (v_ref[rs, :] * bc).astype(BF16)) for tc, rs, bc in zip(t16, rows, beta)]
    kcum = [_dot(tc, (kbc * ec).astype(BF16)) for tc, kbc, ec in zip(t16, kb, egc)]
    qk = [_dot_nt(q_ref[rs, :].astype(BF16), kc16) * dc for rs, kc16, dc in zip(rows, k16, decay)]
    for c in chunks:
        rs = rows[c]
        val_ref[rs, :] = val[c]
        kcum_ref[rs, :] = kcum[c].astype(BF16)
        qk_ref[0, rs, :] = qk[c].astype(BF16)
        glast = gc[c][C - 1:C, :]
        qg_ref[rs, :] = (q_ref[rs, :] * egc[c]).astype(BF16)
        kd_ref[rs, :] = (k[c] * jnp.exp(glast - gc[c])).astype(BF16)
        gl_ref[0, c:c + 1, :] = jnp.exp(glast)


def _gdn_local(qkv, small):
    L = qkv.shape[0]
    H = GDN_HEADS
    R = GDN_CB * GDN_CHUNK
    col = lambda off: pl.BlockSpec((R, GDN_DH), lambda h, i: (i, off + h))
    big = pl.BlockSpec((R, GDN_DH), lambda h, i: (i, h))
    return pl.pallas_call(
        _gdn_local_kernel,
        grid=(H, L // R),
        in_specs=[col(0), col(H), col(2 * H), pl.BlockSpec((R, LANES), lambda h, i: (i, 0))],
        out_specs=[big, big, big, big,
                   pl.BlockSpec((1, R, GDN_CHUNK), lambda h, i: (h, i, 0)),
                   pl.BlockSpec((1, GDN_CB, GDN_DH), lambda h, i: (h, i, 0))],
        out_shape=[jax.ShapeDtypeStruct((L, GDN_WIDTH), F32),
                   jax.ShapeDtypeStruct((L, GDN_WIDTH), BF16),
                   jax.ShapeDtypeStruct((L, GDN_WIDTH), BF16),
                   jax.ShapeDtypeStruct((L, GDN_WIDTH), BF16),
                   jax.ShapeDtypeStruct((H, L, GDN_CHUNK), BF16),
                   jax.ShapeDtypeStruct((H, L // GDN_CHUNK, GDN_DH), F32)],
        compiler_params=_cparams(("arbitrary", "arbitrary")),
        name="gdn_local",
    )(qkv, qkv, qkv, small)


def _gdn_scan_kernel(val_ref, kcum_ref, qg_ref, qk_ref, kdT_ref, gl_ref, z_ref, gn_ref, o_ref, st_ref):
    @pl.when(pl.program_id(0) == 0)
    def _():
        st_ref[...] = jnp.zeros_like(st_ref)

    C = GDN_CHUNK
    gn = gn_ref[...]
    heads = range(GDN_HEADS)
    cols = [slice(h * GDN_DH, (h + 1) * GDN_DH) for h in heads]
    state = [st_ref[h] for h in heads]
    for c in range(GDN_CB):
        rs = slice(c * C, (c + 1) * C)
        sb = [s.astype(BF16) for s in state]
        kcs = [_dot(kcum_ref[rs, cs], s) for cs, s in zip(cols, sb)]
        qgs = [_dot(qg_ref[rs, cs], s) for cs, s in zip(cols, sb)]
        vb = [(val_ref[rs, cs] - x).astype(BF16) for cs, x in zip(cols, kcs)]
        o = [x + _dot(qk_ref[h, rs, :], v) for h, x, v in zip(heads, qgs, vb)]
        state = [s * gl_ref[h, c:c + 1, :] + _dot(kdT_ref[h, c], v) for h, s, v in zip(heads, state, vb)]
        for h in heads:
            on = o[h] * lax.rsqrt(jnp.mean(o[h] * o[h], axis=-1, keepdims=True) + EPS) * gn
            z = z_ref[rs, cols[h]]
            o_ref[rs, cols[h]] = on * (z * _sigmoid(z))
    for h in heads:
        st_ref[h] = state[h]


def _gdn_scan(val, kcum, qg, qk, kdT, gl, proj, gnorm):
    L = val.shape[0]
    H = GDN_HEADS
    R = GDN_CB * GDN_CHUNK
    row = pl.BlockSpec((R, GDN_WIDTH), lambda i: (i, 0))
    zb = C_Z // GDN_WIDTH
    return pl.pallas_call(
        _gdn_scan_kernel,
        grid=(L // R,),
        in_specs=[row, row, row,
                  pl.BlockSpec((H, R, GDN_CHUNK), lambda i: (0, i, 0)),
                  pl.BlockSpec((H, GDN_CB, GDN_DH, GDN_CHUNK), lambda i: (0, i, 0, 0)),
                  pl.BlockSpec((H, GDN_CB, GDN_DH), lambda i: (0, i, 0)),
                  pl.BlockSpec((R, GDN_WIDTH), lambda i: (i, zb)),
                  pl.BlockSpec((1, GDN_DH), lambda i: (0, 0))],
        out_specs=row,
        out_shape=jax.ShapeDtypeStruct((L, GDN_WIDTH), F32),
        scratch_shapes=[pltpu.VMEM((H, GDN_DH, GDN_DH), F32)],
        compiler_params=_cparams(("arbitrary",)),
        name="gdn_scan",
    )(val, kcum, qg, qk, kdT, gl, proj, gnorm)


def _mix_kernel(ya_ref, ys_ref, u_ref, yc_ref, x_ref, ga_ref, gb_ref, d_ref, wglu_ref, wout_ref, o_ref):
    def rms(y, g):
        return y * lax.rsqrt(jnp.mean(y * y, axis=-1, keepdims=True) + EPS) * g

    a = rms(ya_ref[...], ga_ref[...]).astype(BF16)
    yb = _gelu(ys_ref[...] + d_ref[...] * u_ref[...])
    yb = yb * _sigmoid(_dot(yb.astype(BF16), wglu_ref[...]))
    b = rms(yb, gb_ref[...]).astype(BF16)
    na, nb = a.shape[1], b.shape[1]
    acc = x_ref[...] + _dot(a, wout_ref[0:na, :])
    acc = acc + _dot(b, wout_ref[na:na + nb, :])
    o_ref[...] = acc + _dot(yc_ref[...].astype(BF16), wout_ref[na + nb:, :])


def _mix(ya, ys, proj, yc, x, ga, gb, d, wglu, wout, tm=256):
    L, D = x.shape
    ub = C_U // S5_WIDTH
    full = lambda a: pl.BlockSpec(a.shape, lambda i: (0, 0))
    return pl.pallas_call(
        _mix_kernel,
        grid=(L // tm,),
        in_specs=[pl.BlockSpec((tm, ya.shape[1]), lambda i: (i, 0)),
                  pl.BlockSpec((tm, S5_WIDTH), lambda i: (i, 0)),
                  pl.BlockSpec((tm, S5_WIDTH), lambda i: (i, ub)),
                  pl.BlockSpec((tm, GDN_WIDTH), lambda i: (i, 0)),
                  pl.BlockSpec((tm, D), lambda i: (i, 0)),
                  full(ga), full(gb), full(d), full(wglu), full(wout)],
        out_specs=pl.BlockSpec((tm, D), lambda i: (i, 0)),
        out_shape=jax.ShapeDtypeStruct((L, D), F32),
        compiler_params=_cparams(("arbitrary",)),
        name="mix_out",
    )(ya, ys, proj, yc, x, ga, gb, d, wglu, wout)


def _ffn_in_kernel(x_ref, g_ref, wg_ref, wu_ref, cg_ref, cu_ref, bg_ref, bu_ref, o_ref, xn_ref, carry_ref):
    i = pl.program_id(0)
    j = pl.program_id(1)

    @pl.when(j == 0)
    def _():
        xf = x_ref[...]
        ms = jnp.mean(xf * xf, axis=-1, keepdims=True)
        xn_ref[...] = (xf * lax.rsqrt(ms + EPS) * g_ref[...]).astype(BF16)

    @pl.when(i == 0)
    def _():
        carry_ref[j] = jnp.zeros(carry_ref.shape[1:], F32)

    xn = xn_ref[...]
    tm = xn.shape[0]
    rid = lax.broadcasted_iota(jnp.int32, (tm, wg_ref.shape[1]), 0)

    def conv(h, slot, cw_ref, cb_ref):
        prev = carry_ref[j, slot]
        carry_ref[j, slot] = h[tm - 8:tm, :]
        taps = cw_ref.shape[0]
        y = cw_ref[taps - 1:taps, :] * h + cb_ref[...]
        for back in range(1, taps):
            sh = pltpu.roll(h, back, 0)
            for r in range(back):
                sh = jnp.where(rid == r, prev[8 - back + r:8 - back + r + 1, :], sh)
            y = y + cw_ref[taps - 1 - back:taps - back, :] * sh
        return y

    gate = conv(_dot(xn, wg_ref[...]), 0, cg_ref, bg_ref)
    up = conv(_dot(xn, wu_ref[...]), 1, cu_ref, bu_ref)
    o_ref[...] = (gate * _sigmoid(gate) * up).astype(BF16)


def _ffn_in(x, gain, wg, wu, cg, cu, bg, bu, tm=1024, tn=512):
    L, D = x.shape
    N = wg.shape[1]
    taps = cg.shape[0]
    wspec = pl.BlockSpec((D, tn), lambda i, j: (0, j))
    cspec = pl.BlockSpec((taps, tn), lambda i, j: (0, j))
    bspec = pl.BlockSpec((1, tn), lambda i, j: (0, j))
    return pl.pallas_call(
        _ffn_in_kernel,
        grid=(L // tm, N // tn),
        in_specs=[pl.BlockSpec((tm, D), lambda i, j: (i, 0)),
                  pl.BlockSpec((1, D), lambda i, j: (0, 0)),
                  wspec, wspec, cspec, cspec, bspec, bspec],
        out_specs=pl.BlockSpec((tm, tn), lambda i, j: (i, j)),
        out_shape=jax.ShapeDtypeStruct((L, N), BF16),
        scratch_shapes=[pltpu.VMEM((tm, D), BF16), pltpu.VMEM((N // tn, 2, 8, tn), F32)],
        compiler_params=_cparams(("arbitrary", "arbitrary")),
        name="ffn_in",
    )(x, gain.reshape(1, D), wg, wu, cg, cu, bg, bu)


def _ffn_out_kernel(a_ref, w_ref, x_ref, o_ref):
    @pl.when(pl.program_id(1) == 0)
    def _():
        o_ref[...] = x_ref[...]

    o_ref[...] += _dot(a_ref[...], w_ref[...])


def _ffn_out(act, w, x, tm=512, tk=512):
    L, D = x.shape
    K = act.shape[1]
    return pl.pallas_call(
        _ffn_out_kernel,
        grid=(L // tm, K // tk),
        in_specs=[pl.BlockSpec((tm, tk), lambda i, k: (i, k)),
                  pl.BlockSpec((tk, D), lambda i, k: (k, 0)),
                  pl.BlockSpec((tm, D), lambda i, k: (i, 0))],
        out_specs=pl.BlockSpec((tm, D), lambda i, k: (i, 0)),
        out_shape=jax.ShapeDtypeStruct((L, D), F32),
        compiler_params=_cparams(("arbitrary", "arbitrary")),
        name="ffn_out",
    )(act, w, x)


def _rope_tables(pos):
    half = ROPE_DIMS // 2
    inv = ROPE_THETA ** (-jnp.arange(half, dtype=F32) / half)
    ang = pos.astype(F32)[:, None] * inv[None, :]
    n = pos.shape[0]
    cos = jnp.concatenate([jnp.cos(ang), jnp.cos(ang), jnp.ones((n, NSA_DH - ROPE_DIMS), F32)], axis=1)
    sin = jnp.concatenate([-jnp.sin(ang), jnp.sin(ang), jnp.zeros((n, NSA_DH - ROPE_DIMS), F32)], axis=1)
    return jnp.tile(cos, (1, 2)), jnp.tile(sin, (1, 2))


def _permute_w_in(w):
    sp = np.cumsum([0, 512, 128, 128, 128, 128, 128, 128, 24, 512, 3072, 1024, 8, 8])
    q, kc, vc, ks, vs, kw, vw, gates, u, qkv, z, a, b = [w[:, sp[n]:sp[n + 1]] for n in range(13)]
    pad = jnp.zeros((w.shape[0], PROJ_COLS - C_SM - 40), w.dtype)
    return jnp.concatenate([q, ks, kw, vs, vw, z, qkv, u, kc, vc, gates, a, b, pad], axis=1)


def _nsa_mixer(proj, small, p, l):
    L = proj.shape[0]
    HK, G, DH = NSA_KV, NSA_G, NSA_DH
    nQ = L // TQ
    ns = L // SLC_BLOCK
    nb = L // CMP_STRIDE
    n_grp = min(64, ns)
    pos = jnp.arange(L)
    cos, sin = _rope_tables(pos)
    gains = jnp.concatenate([jnp.tile(p['nsa_q_norm'][l], NSA_HEADS), jnp.tile(p['nsa_ks_norm'][l], HK),
                             jnp.tile(p['nsa_kw_norm'][l], HK), jnp.ones((2 * HK * DH,), F32)]).reshape(1, -1)
    prep = _nsa_prep(proj, gains, cos, sin)

    heads = lambda off: prep[:, off:off + HK * DH].reshape(L, HK, DH).transpose(1, 0, 2)
    qT = prep[:, :NSA_HEADS * DH].reshape(nQ, TQ, HK, G, DH).transpose(2, 0, 4, 3, 1).reshape(HK, nQ, DH, G * TQ)
    ks, kw, vs, vw = heads(C_KS), heads(C_KW), heads(C_VS), heads(C_VW)
    onehot = jax.nn.one_hot((pos // SLC_BLOCK) % n_grp, n_grp, dtype=BF16)
    ksaug = jnp.concatenate([ks, jnp.broadcast_to(onehot, (HK, L, n_grp))], axis=2)
    ones_rows = jnp.concatenate([jnp.ones((HK, 1, L), BF16), jnp.zeros((HK, V_ROWS - DH - 1, L), BF16)], axis=1)
    vsx = jnp.concatenate([vs.transpose(0, 2, 1), ones_rows], axis=1)
    kwp = jnp.pad(kw, ((0, 0), (WINDOW, 0), (0, KW_LANES - DH)))
    kwp = kwp.at[:, :WINDOW, DH].set(1.0)
    vwxp = jnp.pad(jnp.concatenate([vw.transpose(0, 2, 1), ones_rows], axis=1), ((0, 0), (0, 0), (WINDOW, 0)))

    half = lambda off: proj[:, off:off + HK * DH].reshape(nb, CMP_STRIDE, HK, DH).transpose(2, 0, 1, 3).reshape(
        HK, nb, CMP_STRIDE * DH)
    pe = p['cmp_pe'][l].reshape(2, 1, CMP_STRIDE * DH)
    w1 = lambda name: p[name][l].reshape(CMP_LEN * DH, -1).astype(BF16)
    w2 = lambda name: jnp.pad(p[name][l], ((0, 0), (0, LANES - DH))).astype(BF16)
    ccos, csin = _rope_tables(jnp.arange(nb) * CMP_STRIDE + CMP_LEN // 2)
    gk = jnp.pad(p['nsa_kc_norm'][l], (0, LANES - DH)).reshape(1, LANES)
    kc, vc = _nsa_compress(half(C_KC), half(C_VC), pe[0], pe[1], w1('cmp_k_w1'), w2('cmp_k_w2'),
                           w1('cmp_v_w1'), w2('cmp_v_w2'), gk, ccos, csin)
    vcx = jnp.concatenate([vc.transpose(0, 2, 1), ones_rows[:, :, :nb]], axis=1)

    cmp_start = np.arange(nb) * CMP_STRIDE
    slc_start = np.arange(ns) * SLC_BLOCK
    smapT = ((cmp_start[None, :] <= slc_start[:, None] + SLC_BLOCK - 1)
             & (cmp_start[None, :] + CMP_LEN - 1 >= slc_start[:, None]))
    smapT[:, nb - 1] = False
    smapT = jnp.asarray(smapT, BF16)

    gT = jnp.pad(small[:, :HK * G * 3].reshape(L, HK, G * 3).transpose(1, 2, 0), ((0, 0), (0, 16 - G * 3), (0, 0)))
    oT = _nsa_attention(qT, kc, vcx, smapT, ksaug, vsx, kwp, vwxp, gT, n_grp)
    return oT.reshape(HK, nQ, DH, G, TQ).transpose(1, 4, 0, 3, 2).reshape(L, NSA_HEADS * DH)


def _s5_mixer(proj, p, l):
    L = proj.shape[0]
    T = S5_CHUNK
    nc = L // T
    G, H, P = S5_GROUPS, S5_CH, S5_STATE
    lam_re, lam_im = p['s5_lam_re'][l], p['s5_lam_im'][l]
    dt = jnp.exp(p['s5_log_dt'][l])[:, None]
    mag = jnp.exp(lam_re * dt)
    lb_re = mag * jnp.cos(lam_im * dt)
    lb_im = mag * jnp.sin(lam_im * dt)
    den = lam_re * lam_re + lam_im * lam_im
    nr, ni = lb_re - 1.0, lb_im
    coef_re = (nr * lam_re + ni * lam_im) / den
    coef_im = (ni * lam_re - nr * lam_im) / den
    b_re, b_im = p['s5_b_re'][l], p['s5_b_im'][l]
    bb_re = coef_re[..., None] * b_re - coef_im[..., None] * b_im
    bb_im = coef_re[..., None] * b_im + coef_im[..., None] * b_re
    bsr, bsi = bb_re.transpose(0, 2, 1), bb_im.transpose(0, 2, 1)
    tile_rows = lambda a: jnp.tile(a, (1, T, 1))
    u = proj[:, C_U:C_U + S5_WIDTH]
    ug = u.astype(BF16).reshape(nc, T, G, H).transpose(2, 0, 1, 3).reshape(G, nc, T * H)
    y = _s5_scan(ug, lb_re[:, None, :], lb_im[:, None, :], tile_rows(p['s5_c_re'][l]), tile_rows(p['s5_c_im'][l]),
                 tile_rows(bsr), tile_rows(bsi), bsr, bsi)
    return y.reshape(G, nc, T, H).transpose(1, 2, 0, 3).reshape(L, S5_WIDTH)


def _gdn_mixer(proj, small, p, l):
    L = proj.shape[0]
    H = GDN_HEADS
    qkv = _gdn_prep(proj, p['gdn_conv'][l])
    val, kcum, qg, kd, qk, gl = _gdn_local(qkv, small)
    kdT = kd.reshape(L // GDN_CHUNK, GDN_CHUNK, H, GDN_DH).transpose(2, 0, 3, 1)
    return _gdn_scan(val, kcum, qg, qk, kdT, gl, proj, p['gdn_norm'][l].reshape(1, GDN_DH))


def _pad_cols(a, n):
    return jnp.pad(a, ((0, 0), (0, n - a.shape[1])))


def _forward(x3, p):
    x = x3[0]
    depth = p['w_in'].shape[0]
    for l in range(depth):
        w_in = _permute_w_in(p['w_in'][l]).astype(BF16)
        proj = _rms_matmul(x, p['attn_norm'][l], w_in)
        sm_par = jnp.zeros((8, LANES), F32)
        sm_par = sm_par.at[0, SM_G:SM_BETA].set(p['gdn_dt_bias'][l]).at[1, SM_G:SM_BETA].set(-jnp.exp(p['gdn_a_log'][l]))
        small = _small(proj, sm_par)
        y_a = _nsa_mixer(proj, small, p, l)
        y_s = _s5_mixer(proj, p, l)
        y_c = _gdn_mixer(proj, small, p, l)
        x = _mix(y_a, y_s, proj, y_c, x, p['nsa_out_norm'][l].reshape(1, -1), p['s5_out_norm'][l].reshape(1, -1),
                 p['s5_d'][l].reshape(1, -1), p['s5_w_glu'][l].astype(BF16), p['w_out'][l].astype(BF16))
        wf = p['ffn_w_in'][l]
        cf = p['ffn_conv'][l]
        bf = p['ffn_conv_b'][l].reshape(1, -1)
        act = _ffn_in(x, p['ffn_norm'][l],
                      _pad_cols(wf[:, :D_FF], D_FF_PAD).astype(BF16), _pad_cols(wf[:, D_FF:], D_FF_PAD).astype(BF16),
                      _pad_cols(cf[:, :D_FF], D_FF_PAD), _pad_cols(cf[:, D_FF:], D_FF_PAD),
                      _pad_cols(bf[:, :D_FF], D_FF_PAD), _pad_cols(bf[:, D_FF:], D_FF_PAD))
        w_o = jnp.pad(p['ffn_w_out'][l], ((0, D_FF_PAD - D_FF), (0, 0))).astype(BF16)
        x = _ffn_out(act, w_o, x)
    return x[None]


_PARAM_NAMES = ('attn_norm', 'w_in', 'nsa_q_norm', 'nsa_kc_norm', 'nsa_ks_norm', 'nsa_kw_norm', 'cmp_pe',
                'cmp_k_w1', 'cmp_k_w2', 'cmp_v_w1', 'cmp_v_w2', 'nsa_out_norm', 's5_lam_re', 's5_lam_im',
                's5_log_dt', 's5_b_re', 's5_b_im', 's5_c_re', 's5_c_im', 's5_d', 's5_w_glu', 's5_out_norm',
                'gdn_conv', 'gdn_a_log', 'gdn_dt_bias', 'gdn_norm', 'w_out', 'ffn_norm', 'ffn_w_in', 'ffn_conv',
                'ffn_conv_b', 'ffn_w_out')


def kernel(x, attn_norm, w_in, nsa_q_norm, nsa_kc_norm, nsa_ks_norm, nsa_kw_norm, cmp_pe, cmp_k_w1, cmp_k_w2,
           cmp_v_w1, cmp_v_w2, nsa_out_norm, s5_lam_re, s5_lam_im, s5_log_dt, s5_b_re, s5_b_im, s5_c_re, s5_c_im,
           s5_d, s5_w_glu, s5_out_norm, gdn_conv, gdn_a_log, gdn_dt_bias, gdn_norm, w_out, ffn_norm, ffn_w_in,
           ffn_conv, ffn_conv_b, ffn_w_out):
    vals = (attn_norm, w_in, nsa_q_norm, nsa_kc_norm, nsa_ks_norm, nsa_kw_norm, cmp_pe, cmp_k_w1, cmp_k_w2,
            cmp_v_w1, cmp_v_w2, nsa_out_norm, s5_lam_re, s5_lam_im, s5_log_dt, s5_b_re, s5_b_im, s5_c_re, s5_c_im,
            s5_d, s5_w_glu, s5_out_norm, gdn_conv, gdn_a_log, gdn_dt_bias, gdn_norm, w_out, ffn_norm, ffn_w_in,
            ffn_conv, ffn_conv_b, ffn_w_out)
    return _forward(x, dict(zip(_PARAM_NAMES, vals)))
```

```python
import functools
import math

import jax
import jax.numpy as jnp
import numpy as np
from jax import lax
from jax.experimental import pallas as pl
from jax.experimental.pallas import tpu as pltpu

F32 = jnp.float32
BF16 = jnp.bfloat16
HIGHEST = lax.Precision.HIGHEST

D_MODEL = 2048
NSA_HEADS = 8
NSA_KV = 2
NSA_G = NSA_HEADS // NSA_KV
NSA_DH = 64
CMP_LEN = 32
CMP_STRIDE = 16
SLC_BLOCK = 64
SLC_TOPK = 16
WINDOW = 512
ROPE_THETA = 500000.0
ROPE_DIMS = NSA_DH // 4
S5_WIDTH = 512
S5_CH = 16
S5_GROUPS = 32
S5_STATE = 64
S5_CHUNK = 32
GDN_HEADS = 8
GDN_DH = 128
GDN_WIDTH = 1024
GDN_CHUNK = 64
D_FF = 5504
D_FF_PAD = 5632
EPS = 1e-6
NEG_BIG = -(2.0 ** 100)
LOG2_E = math.log2(math.e)
V_ROWS = 80
KW_LANES = 128
CMP_CHUNK = 256

LANES = 128
TQ = 128
TK = 512
VMEM_LIMIT = 56 * 1024 * 1024

C_Q, C_KS, C_KW, C_VS, C_VW = 0, 512, 640, 768, 896
C_Z, C_QKV, C_U, C_KC, C_VC, C_SM = 1024, 2048, 5120, 5632, 5760, 5888
NSA_PREP_W = C_Z
PROJ_COLS = 6144
SM_GATES, SM_G, SM_BETA, SM_END = 0, 24, 32, 40


def _cparams(sem):
    return pltpu.CompilerParams(dimension_semantics=sem, vmem_limit_bytes=VMEM_LIMIT)


def _dot(a, b, precision=None):
    return jnp.dot(a, b, preferred_element_type=F32, precision=precision)


def _dot_nt(a, b, precision=None):
    return lax.dot_general(a, b, (((1,), (1,)), ((), ())), preferred_element_type=F32, precision=precision)


def _gelu(x):
    return x * (0.5 * (1.0 + jnp.tanh(math.sqrt(2.0 / math.pi) * (x + 0.044715 * (x * x * x)))))


def _sigmoid(x):
    return 1.0 / (1.0 + jnp.exp(-x))


def _rms_matmul_kernel(x_ref, g_ref, w_ref, o_ref, xn_ref):
    @pl.when(pl.program_id(1) == 0)
    def _():
        xf = x_ref[...]
        ms = jnp.mean(xf * xf, axis=-1, keepdims=True)
        xn_ref[...] = (xf * lax.rsqrt(ms + EPS) * g_ref[...]).astype(BF16)

    o_ref[...] = _dot(xn_ref[...], w_ref[...])


def _rms_matmul(x, gain, w, tm=512, tn=2048):
    L, D = x.shape
    N = w.shape[1]
    return pl.pallas_call(
        _rms_matmul_kernel,
        grid=(L // tm, N // tn),
        in_specs=[
            pl.BlockSpec((tm, D), lambda i, j: (i, 0)),
            pl.BlockSpec((1, D), lambda i, j: (0, 0)),
            pl.BlockSpec((D, tn), lambda i, j: (0, j)),
        ],
        out_specs=pl.BlockSpec((tm, tn), lambda i, j: (i, j)),
        out_shape=jax.ShapeDtypeStruct((L, N), F32),
        scratch_shapes=[pltpu.VMEM((tm, D), BF16)],
        compiler_params=_cparams(("arbitrary", "arbitrary")),
        name="rms_proj",
    )(x, gain.reshape(1, D), w)


def _rope_slab(x, cos, sin):
    d = lax.broadcasted_iota(jnp.int32, x.shape, 1) & (NSA_DH - 1)
    half = ROPE_DIMS // 2
    partner = jnp.where(d < half, pltpu.roll(x, LANES - half, 1), pltpu.roll(x, half, 1))
    return x * cos + partner * sin


def _nsa_prep_kernel(x_ref, g_ref, cos_ref, sin_ref, bsum_ref, qT_ref, ksaug_ref, kwa_ref, vsx_ref, vwx_ref, *, n_grp):
    i = pl.program_id(0)
    tm = x_ref.shape[0]
    cos = cos_ref[...]
    sin = sin_ref[...]
    bsum = bsum_ref[...]
    lane = lax.broadcasted_iota(jnp.int32, (tm, LANES), 1)
    tiles = range(tm // TQ)

    def normed(s):
        sl = slice(s * LANES, (s + 1) * LANES)
        x = x_ref[:, sl]
        ms = _dot(x * x, bsum, precision=HIGHEST)
        return _rope_slab(x * lax.rsqrt(ms + EPS) * g_ref[:, sl], cos, sin)

    def per_head(y):
        return y, pltpu.roll(y, NSA_DH, 1)

    for s in range((C_KS - C_Q) // LANES):
        y = normed(s) * (NSA_DH ** -0.5 * LOG2_E)
        hk, g0 = divmod(s, NSA_G // 2)
        for t in tiles:
            yT = y[t * TQ:(t + 1) * TQ, :].T.astype(BF16)
            for half in range(2):
                g = 2 * g0 + half
                qT_ref[hk, t, :, g * TQ:(g + 1) * TQ] = yT[half * NSA_DH:(half + 1) * NSA_DH, :]

    blk = ((i * tm + lax.broadcasted_iota(jnp.int32, (tm, LANES), 0)) // SLC_BLOCK) % n_grp
    onehot = jnp.where(lane - NSA_DH == blk, 1.0, 0.0)
    for hk, y in enumerate(per_head(normed(C_KS // LANES))):
        ksaug_ref[hk] = jnp.where(lane < NSA_DH, y, onehot).astype(BF16)
    for hk, y in enumerate(per_head(normed(C_KW // LANES))):
        kwa_ref[hk] = jnp.where(lane < NSA_DH, y, 0.0).astype(BF16)

    ones_rows = jnp.where(lax.broadcasted_iota(jnp.int32, (V_ROWS - NSA_DH, tm), 0) == 0, 1.0, 0.0).astype(BF16)
    for c0, v_ref in ((C_VS, vsx_ref), (C_VW, vwx_ref)):
        x = x_ref[:, c0:c0 + LANES]
        for t in tiles:
            xT = x[t * TQ:(t + 1) * TQ, :].T.astype(BF16)
            for hk in range(NSA_KV):
                v_ref[hk, 0:NSA_DH, t * TQ:(t + 1) * TQ] = xT[hk * NSA_DH:(hk + 1) * NSA_DH, :]
        for hk in range(NSA_KV):
            v_ref[hk, NSA_DH:V_ROWS, :] = ones_rows


def _nsa_prep(proj, gains, cos, sin, n_grp, tm=512):
    L = proj.shape[0]
    W = NSA_PREP_W
    HK = NSA_KV
    GW = NSA_G * TQ
    bsum = jnp.asarray(np.kron(np.eye(2), np.ones((NSA_DH, NSA_DH))) / NSA_DH, F32)
    rows = pl.BlockSpec((HK, tm, LANES), lambda i: (0, i, 0))
    vals = pl.BlockSpec((HK, V_ROWS, tm), lambda i: (0, 0, i))
    return pl.pallas_call(
        functools.partial(_nsa_prep_kernel, n_grp=n_grp),
        grid=(L // tm,),
        in_specs=[
            pl.BlockSpec((tm, W), lambda i: (i, 0)),
            pl.BlockSpec((1, W), lambda i: (0, 0)),
            pl.BlockSpec((tm, LANES), lambda i: (i, 0)),
            pl.BlockSpec((tm, LANES), lambda i: (i, 0)),
            pl.BlockSpec((LANES, LANES), lambda i: (0, 0)),
        ],
        out_specs=[pl.BlockSpec((HK, tm // TQ, NSA_DH, GW), lambda i: (0, i, 0, 0)), rows, rows, vals, vals],
        out_shape=[jax.ShapeDtypeStruct((HK, L // TQ, NSA_DH, GW), BF16),
                   jax.ShapeDtypeStruct((HK, L, LANES), BF16), jax.ShapeDtypeStruct((HK, L, LANES), BF16),
                   jax.ShapeDtypeStruct((HK, V_ROWS, L), BF16), jax.ShapeDtypeStruct((HK, V_ROWS, L), BF16)],
        compiler_params=_cparams(("arbitrary",)),
        name="nsa_prep",
    )(proj, gains, cos, sin, bsum)


def _cmp_kernel(xk_ref, xv_ref, pe_ref, w1k_ref, w2k_ref, w1v_ref, w2v_ref, gk_ref, cos_ref, sin_ref, bsum_ref,
                kc_ref, vcx_ref):
    nb = kc_ref.shape[1]
    half_w = CMP_STRIDE * LANES

    def mlp(x_ref, w1_ref, w2_ref):
        hb = jnp.concatenate([x_ref[pl.ds(l, nb, stride=CMP_STRIDE), :] for l in range(CMP_STRIDE)], axis=1)
        a = (hb + pe_ref[:, 0:half_w]).astype(BF16)
        b = (hb + pe_ref[:, half_w:2 * half_w]).astype(BF16)
        p1 = _dot(a, w1_ref[0:half_w, :])
        p2 = _dot(b, w1_ref[half_w:2 * half_w, :])
        h = p1 + pltpu.roll(p2, nb - 1, 0)
        return _dot(_gelu(h).astype(BF16), w2_ref[...])

    rid = lax.broadcasted_iota(jnp.int32, (nb, LANES), 0)
    real = rid < nb - 1
    kc = jnp.where(real, mlp(xk_ref, w1k_ref, w2k_ref), 0.0)
    vc = jnp.where(real, mlp(xv_ref, w1v_ref, w2v_ref), 0.0)
    ms = _dot(kc * kc, bsum_ref[...], precision=HIGHEST)
    kcn = _rope_slab(kc * lax.rsqrt(ms + EPS) * gk_ref[...], cos_ref[...], sin_ref[...])
    kc_ref[0] = kcn[:, 0:NSA_DH].astype(BF16)
    kc_ref[1] = pltpu.roll(kcn, NSA_DH, 1)[:, 0:NSA_DH].astype(BF16)
    ones_rows = jnp.where(lax.broadcasted_iota(jnp.int32, (V_ROWS - NSA_DH, nb), 0) == 0, 1.0, 0.0).astype(BF16)
    for t in range(nb // LANES):
        vT = vc[t * LANES:(t + 1) * LANES, :].T.astype(BF16)
        for hk in range(NSA_KV):
            vcx_ref[hk, 0:NSA_DH, t * LANES:(t + 1) * LANES] = vT[hk * NSA_DH:(hk + 1) * NSA_DH, :]
    for hk in range(NSA_KV):
        vcx_ref[hk, NSA_DH:V_ROWS, :] = ones_rows


def _nsa_compress(proj, pe, w1k, w2k, w1v, w2v, gk, cos, sin):
    L = proj.shape[0]
    nb = L // CMP_STRIDE
    bsum = jnp.asarray(np.kron(np.eye(2), np.ones((NSA_DH, NSA_DH))) / NSA_DH, F32)
    once = lambda shape, idx: pl.BlockSpec(shape, lambda i: idx, pipeline_mode=pl.Buffered(1))
    full = lambda a: once(a.shape, (0,) * a.ndim)
    return pl.pallas_call(
        _cmp_kernel,
        grid=(1,),
        in_specs=[once((L, LANES), (0, C_KC // LANES)), once((L, LANES), (0, C_VC // LANES)),
                  full(pe), full(w1k), full(w2k), full(w1v), full(w2v), full(gk), full(cos), full(sin), full(bsum)],
        out_specs=[pl.BlockSpec((NSA_KV, nb, NSA_DH), lambda i: (0, 0, 0)),
                   pl.BlockSpec((NSA_KV, V_ROWS, nb), lambda i: (0, 0, 0))],
        out_shape=[jax.ShapeDtypeStruct((NSA_KV, nb, NSA_DH), BF16),
                   jax.ShapeDtypeStruct((NSA_KV, V_ROWS, nb), BF16)],
        compiler_params=_cparams(("arbitrary",)),
        name="nsa_compress",
    )(proj, proj, pe, w1k, w2k, w1v, w2v, gk, cos, sin, bsum)


def _nsa_attn_kernel(qT_ref, kc_ref, vcx_ref, smap_ref, ksaug_ref, vsx_ref, kwp_ref, vwxp_ref, gT_ref,
                     o_ref, qaug_ref, qwin_ref, ss_ref, mt_ref, m_ref, acc_ref, sc_ref, imp_ref, *, n_grp):
    i = pl.program_id(0)
    s0 = i * TQ
    hks = range(qT_ref.shape[0])
    nb = kc_ref.shape[1]
    ns = smap_ref.shape[0]
    GW = NSA_G * TQ
    qT = [qT_ref[h, 0] for h in hks]
    tq = s0 + (lax.broadcasted_iota(jnp.int32, (1, GW), 1) & (TQ - 1))

    cch = min(CMP_CHUNK, nb)
    n_cch = (s0 + TQ - CMP_LEN) // (CMP_STRIDE * cch) + 1
    nthr = (tq - (CMP_LEN - 1)) // CMP_STRIDE

    def cmp_scores(c, mcs):
        r0c = pl.multiple_of(c * cch, cch)
        s = [_dot(kc_ref[h, pl.ds(r0c, cch), :], qT[h]) for h in hks]
        nrow = r0c + lax.broadcasted_iota(jnp.int32, (cch, GW), 0)
        out = []
        for h in hks:
            sh = jnp.where(nrow <= nthr, s[h], NEG_BIG)
            sc_ref[h, pl.ds(r0c, cch), :] = sh
            out.append(jnp.maximum(mcs[h], jnp.max(sh, axis=0, keepdims=True)))
        return tuple(out)

    mc = lax.fori_loop(0, n_cch, cmp_scores, tuple(jnp.full((1, GW), NEG_BIG, F32) for _ in hks))
    imp_ref[...] = jnp.zeros_like(imp_ref)
    ones16 = jnp.ones((16, cch), BF16)
    VX = vcx_ref.shape[1]

    def cmp_accum(c, carry):
        r0c = pl.multiple_of(c * cch, cch)
        sm = smap_ref[:, pl.ds(r0c, cch)]
        parts = [_split2(jnp.exp2(sc_ref[h, pl.ds(r0c, cch), :] - mc[h])) for h in hks]
        for h in hks:
            imp_ref[h] += _dot(sm, parts[h][0]) + _dot(sm, parts[h][1])
        return tuple((carry[h][0] + _dot(vcx_ref[h, :, pl.ds(r0c, cch)], parts[h][0]),
                      carry[h][1] + _dot(ones16, parts[h][1])) for h in hks)

    cacc = lax.fori_loop(0, n_cch, cmp_accum,
                         tuple((jnp.zeros((VX, GW), F32), jnp.zeros((16, GW), F32)) for _ in hks))
    inv_l = [jnp.where(tq >= CMP_LEN - 1, 1.0 / (cacc[h][0][NSA_DH:NSA_DH + 1] + cacc[h][1][0:1]), 0.0) for h in hks]
    ocT = [cacc[h][0][0:NSA_DH] * inv_l[h] for h in hks]

    imp = []
    for h in hks:
        tot = imp_ref[h, :, 0:TQ] * inv_l[h][:, 0:TQ]
        for g in range(1, NSA_G):
            tot = tot + imp_ref[h, :, g * TQ:(g + 1) * TQ] * inv_l[h][:, g * TQ:(g + 1) * TQ]
        imp.append(tot)

    jrow = lax.broadcasted_iota(jnp.int32, (ns, TQ), 0)
    cur = (s0 + lax.broadcasted_iota(jnp.int32, (1, TQ), 1)) // SLC_BLOCK
    forced = jnp.where(jrow == 0, 0.0, jnp.where(jrow == cur, 0.0, jnp.where(jrow == cur - 1, 0.0, NEG_BIG)))
    val = [jnp.where(jrow >= 1, jnp.where(jrow <= cur - 2, imp[h], -jnp.inf), -jnp.inf) for h in hks]
    for _ in range(min(SLC_TOPK, ns) - 3):
        mx = [jnp.max(val[h], axis=0, keepdims=True) for h in hks]
        idx = [jnp.min(jnp.where(val[h] == mx[h], jrow, ns), axis=0, keepdims=True) for h in hks]
        val = [jnp.where(jrow == idx[h], -jnp.inf, val[h]) for h in hks]
    bias = [jnp.where(jrow >= 1, jnp.where(jrow <= cur - 2, jnp.where(val[h] == -jnp.inf, 0.0, NEG_BIG), forced), forced)
            for h in hks]
    own = jrow // 2 == i
    for h in hks:
        bias16 = jnp.where(own, NEG_BIG, bias[h]).astype(BF16)
        for grp in range(ns // n_grp):
            qaug_ref[h, grp, 0:NSA_DH, :] = qT[h]
            for g in range(NSA_G):
                qaug_ref[h, grp, NSA_DH:NSA_DH + n_grp, g * TQ:(g + 1) * TQ] = bias16[grp * n_grp:(grp + 1) * n_grp, :]

    tiles_per_grp = (n_grp * SLC_BLOCK) // TK
    n_pairs = (s0 + 2 * TK - 1) // (2 * TK)

    def scores(kt, slot):
        k0 = pl.multiple_of(kt * TK, TK)
        grp = kt // tiles_per_grp
        s = [_dot(ksaug_ref[h, pl.ds(k0, TK), :], qaug_ref[h, grp]) for h in hks]
        for h in hks:
            sb = s[h].astype(BF16)
            ss_ref[h, slot] = sb
            mt_ref[h, slot] = jnp.max(sb, axis=0, keepdims=True).astype(F32)

    def online_update(sb, mt, v):
        m = [m_ref[h] for h in hks]
        mn = [jnp.maximum(m[h], mt[h]) for h in hks]
        pv = [_dot(v[h], jnp.exp2(sb[h] - mn[h].astype(BF16))) for h in hks]
        for h in hks:
            acc_ref[h] = jnp.exp2(m[h] - mn[h]) * acc_ref[h] + pv[h]
            m_ref[h] = mn[h]

    def accumulate(kt, slot):
        k0 = pl.multiple_of(kt * TK, TK)
        online_update([ss_ref[h, slot] for h in hks], [mt_ref[h, slot] for h in hks],
                      [vsx_ref[h, :, pl.ds(k0, TK)] for h in hks])

    m_ref[...] = jnp.full(m_ref.shape, NEG_BIG, F32)
    acc_ref[...] = jnp.zeros_like(acc_ref)

    @pl.when(n_pairs > 0)
    def _():
        scores(0, 0)

    def pair_body(j, carry):
        scores(2 * j + 1, 1)
        accumulate(2 * j, 0)
        scores(2 * j + 2, 0)
        accumulate(2 * j + 1, 1)
        return carry

    lax.fori_loop(0, n_pairs - 1, pair_body, 0)

    @pl.when(n_pairs > 0)
    def _():
        scores(2 * n_pairs - 1, 1)
        accumulate(2 * n_pairs - 2, 0)
        accumulate(2 * n_pairs - 1, 1)

    r0 = pl.multiple_of(s0, TQ)
    rel = lax.broadcasted_iota(jnp.int32, (TQ, GW), 0)
    qrel = tq - s0
    sd = [_dot(ksaug_ref[h, pl.ds(r0, TQ), 0:NSA_DH], qT[h]) for h in hks]
    sd16 = [jnp.where(rel <= qrel, sd[h], NEG_BIG).astype(BF16) for h in hks]
    online_update(sd16, [jnp.max(sd16[h], axis=0, keepdims=True).astype(F32) for h in hks],
                  [vsx_ref[h, :, pl.ds(r0, TQ)] for h in hks])
    osT = [acc_ref[h, 0:NSA_DH, :] / acc_ref[h, NSA_DH:NSA_DH + 1, :] for h in hks]

    flag_row = lax.broadcasted_iota(jnp.int32, (KW_LANES - NSA_DH, GW), 0) == 0
    for h in hks:
        qwin_ref[h, 0:NSA_DH, :] = qT[h]
        qwin_ref[h, NSA_DH:KW_LANES, :] = jnp.where(flag_row, NEG_BIG, 0.0).astype(BF16)
    qwin = [qwin_ref[h] for h in hks]
    edge = [_dot(kwp_ref[h, pl.ds(r0, TQ), :], qwin[h]) for h in hks]
    mid = [_dot(kwp_ref[h, pl.ds(r0 + TQ, WINDOW - TQ), :], qwin[h]) for h in hks]
    diag = [_dot(kwp_ref[h, pl.ds(r0 + WINDOW, TQ), :], qwin[h]) for h in hks]
    edge16 = [jnp.where(rel > qrel, edge[h], NEG_BIG).astype(BF16) for h in hks]
    diag16 = [jnp.where(rel <= qrel, diag[h], NEG_BIG).astype(BF16) for h in hks]
    mid16 = [mid[h].astype(BF16) for h in hks]
    mw = [jnp.maximum(jnp.maximum(jnp.max(edge16[h], axis=0, keepdims=True), jnp.max(mid16[h], axis=0, keepdims=True)),
                      jnp.max(diag16[h], axis=0, keepdims=True)) for h in hks]
    accw = [(_dot(vwxp_ref[h, :, pl.ds(r0, TQ)], jnp.exp2(edge16[h] - mw[h]))
             + _dot(vwxp_ref[h, :, pl.ds(r0 + TQ, WINDOW - TQ)], jnp.exp2(mid16[h] - mw[h]))
             + _dot(vwxp_ref[h, :, pl.ds(r0 + WINDOW, TQ)], jnp.exp2(diag16[h] - mw[h]))) for h in hks]

    for h in hks:
        owT = accw[h][0:NSA_DH] / accw[h][NSA_DH:NSA_DH + 1]
        gates = _sigmoid(gT_ref[h])
        outs = []
        for g in range(NSA_G):
            sl = slice(g * TQ, (g + 1) * TQ)
            outs.append(gates[3 * g:3 * g + 1] * ocT[h][:, sl] + gates[3 * g + 1:3 * g + 2] * osT[h][:, sl]
                        + gates[3 * g + 2:3 * g + 3] * owT[:, sl])
        for g in range(0, NSA_G, 2):
            c0 = (h * NSA_G + g) * NSA_DH
            o_ref[:, c0:c0 + 2 * NSA_DH] = jnp.concatenate([outs[g], outs[g + 1]], axis=0).T


def _nsa_attention(qT, kc, vcx, smapT, ksaug, vsx, kwp, vwxp, gT, n_grp):
    HK, nQ, _, GW = qT.shape
    nb = kc.shape[1]
    ns = smapT.shape[0]
    L = vsx.shape[2]
    KA = ksaug.shape[2]
    scratch = [pltpu.VMEM((HK, ns // n_grp, KA, GW), BF16),
               pltpu.VMEM((HK, KW_LANES, GW), BF16),
               pltpu.VMEM((HK, 2, TK, GW), BF16),
               pltpu.VMEM((HK, 2, 1, GW), F32),
               pltpu.VMEM((HK, 1, GW), F32),
               pltpu.VMEM((HK, V_ROWS, GW), F32),
               pltpu.VMEM((HK, nb, GW), F32),
               pltpu.VMEM((HK, ns, GW), F32)]
    resident = lambda a: pl.BlockSpec(a.shape, lambda i: (0,) * a.ndim, pipeline_mode=pl.Buffered(1))
    return pl.pallas_call(
        functools.partial(_nsa_attn_kernel, n_grp=n_grp),
        grid=(nQ,),
        in_specs=[
            pl.BlockSpec((HK, 1, NSA_DH, GW), lambda i: (0, i, 0, 0)),
            resident(kc), resident(vcx), resident(smapT), resident(ksaug), resident(vsx), resident(kwp),
            resident(vwxp),
            pl.BlockSpec((HK, 16, TQ), lambda i: (0, 0, i)),
        ],
        out_specs=pl.BlockSpec((TQ, NSA_HEADS * NSA_DH), lambda i: (i, 0)),
        out_shape=jax.ShapeDtypeStruct((nQ * TQ, NSA_HEADS * NSA_DH), F32),
        scratch_shapes=scratch,
        compiler_params=_cparams(("arbitrary",)),
        name="nsa_attn",
    )(qT, kc, vcx, smapT, ksaug, vsx, kwp, vwxp, gT)


def _s5_kernel(u_ref, lbr_ref, lbi_ref, ctr_ref, cti_ref, btr_ref, bti_ref, bsr_ref, bsi_ref, y_ref, lm_ref):
    T = S5_CHUNK
    W = T * S5_CH
    ar = lbr_ref[0]
    ai = lbi_ref[0]
    delta = lax.broadcasted_iota(jnp.int32, (W, S5_STATE), 0) // S5_CH

    def powers(e):
        pr = jnp.ones((W, S5_STATE), F32)
        pi = jnp.zeros((W, S5_STATE), F32)
        fr, fi = ar, ai
        for b in range(T.bit_length() - 1):
            bit = ((e >> b) & 1) == 1
            nr = pr * fr - pi * fi
            ni = pr * fi + pi * fr
            pr = jnp.where(bit, nr, pr)
            pi = jnp.where(bit, ni, pi)
            fr, fi = fr * fr - fi * fi, 2.0 * fr * fi
        return pr, pi, fr, fi

    pwr, pwi, aTr, aTi = powers(delta)
    rvr, rvi, _, _ = powers(T - 1 - delta)
    ctr, cti = ctr_ref[0], cti_ref[0]
    car = ctr * pwr - cti * pwi
    cai = ctr * pwi + cti * pwr

    kw = _dot_nt(bsr_ref[0], car, HIGHEST) - _dot_nt(bsi_ref[0], cai, HIGHEST)
    lane = lax.broadcasted_iota(jnp.int32, (S5_CH, W), 1)
    for tau in range(T):
        sh = S5_CH * tau
        blk = kw if tau == 0 else jnp.where(lane >= sh, pltpu.roll(kw, sh, 1), 0.0)
        lm_ref[sh:sh + S5_CH, :] = blk.astype(BF16)

    u = u_ref[0]
    nc = u.shape[0]
    y = _dot(u, lm_ref[...])

    btr, bti = btr_ref[0], bti_ref[0]
    sr = _dot(u, (rvr * btr - rvi * bti).astype(BF16))
    si = _dot(u, (rvr * bti + rvi * btr).astype(BF16))
    rowc = lax.broadcasted_iota(jnp.int32, (nc, S5_STATE), 0)
    fr, fi = aTr, aTi
    step = 1
    while step < nc:
        shr = jnp.where(rowc >= step, pltpu.roll(sr, step, 0), 0.0)
        shi = jnp.where(rowc >= step, pltpu.roll(si, step, 0), 0.0)
        sr, si = sr + fr * shr - fi * shi, si + fr * shi + fi * shr
        fr, fi = fr * fr - fi * fi, 2.0 * fr * fi
        step *= 2
    xr = jnp.where(rowc >= 1, pltpu.roll(sr, 1, 0), 0.0)
    xi = jnp.where(rowc >= 1, pltpu.roll(si, 1, 0), 0.0)
    c1r = car * ar - cai * ai
    c1i = car * ai + cai * ar
    y = y + _dot_nt(xr.astype(BF16), c1r.astype(BF16)) - _dot_nt(xi.astype(BF16), c1i.astype(BF16))
    y_ref[0] = y


def _s5_scan(ug, lbr, lbi, ctr, cti, btr, bti, bsr, bsi):
    G, nc, W = ug.shape
    grp = lambda shape: pl.BlockSpec((1,) + shape, lambda g: (g, 0, 0))
    return pl.pallas_call(
        _s5_kernel,
        grid=(G,),
        in_specs=[grp((nc, W)), grp((1, S5_STATE)), grp((1, S5_STATE)),
                  grp((W, S5_STATE)), grp((W, S5_STATE)), grp((W, S5_STATE)), grp((W, S5_STATE)),
                  grp((S5_CH, S5_STATE)), grp((S5_CH, S5_STATE))],
        out_specs=grp((nc, W)),
        out_shape=jax.ShapeDtypeStruct((G, nc, W), F32),
        scratch_shapes=[pltpu.VMEM((W, W), BF16)],
        compiler_params=_cparams(("arbitrary",)),
        name="s5_scan",
    )(ug, lbr, lbi, ctr, cti, btr, bti, bsr, bsi)


def _small_kernel(x_ref, p_ref, o_ref):
    x = x_ref[...]
    lane = lax.broadcasted_iota(jnp.int32, x.shape, 1)
    z = x + p_ref[0:1, :]
    softplus = jnp.maximum(z, 0.0) + jnp.log(1.0 + jnp.exp(-jnp.abs(z)))
    gdec = p_ref[1:2, :] * softplus
    beta = _sigmoid(x)
    o_ref[...] = jnp.where(lane < SM_G, x, jnp.where(lane < SM_BETA, gdec, beta))


def _small(proj, params, tm=1024):
    L = proj.shape[0]
    cb = C_SM // LANES
    return pl.pallas_call(
        _small_kernel,
        grid=(L // tm,),
        in_specs=[pl.BlockSpec((tm, LANES), lambda i: (i, cb)), pl.BlockSpec((8, LANES), lambda i: (0, 0))],
        out_specs=pl.BlockSpec((tm, LANES), lambda i: (i, 0)),
        out_shape=jax.ShapeDtypeStruct((L, LANES), F32),
        compiler_params=_cparams(("arbitrary",)),
        name="small_cols",
    )(proj, params)


def _gdn_prep_kernel(x_ref, cw_ref, o_ref, carry_ref):
    i = pl.program_id(0)
    j = pl.program_id(1)
    tm = x_ref.shape[0]

    @pl.when(i == 0)
    def _():
        carry_ref[j] = jnp.zeros(carry_ref.shape[1:], F32)

    rid = lax.broadcasted_iota(jnp.int32, (tm, GDN_DH), 0)
    taps = cw_ref.shape[0]
    scale = jnp.where(j == 0, GDN_DH ** -0.5, 1.0)
    for h in range(GDN_HEADS):
        sl = slice(h * GDN_DH, (h + 1) * GDN_DH)
        x = x_ref[:, sl]
        prev = carry_ref[j, :, sl]
        carry_ref[j, :, sl] = x[tm - 8:tm, :]
        y = cw_ref[taps - 1:taps, sl] * x
        for back in range(1, taps):
            sh = pltpu.roll(x, back, 0)
            for r in range(back):
                sh = jnp.where(rid == r, prev[8 - back + r:8 - back + r + 1, :], sh)
            y = y + cw_ref[taps - 1 - back:taps - back, sl] * sh
        y = y * _sigmoid(y)
        ss = jnp.sum(y * y, axis=-1, keepdims=True)
        yn = y * (lax.rsqrt(ss + EPS) * scale)
        o_ref[:, sl] = jnp.where(j < 2, yn, y)


def _gdn_prep(proj, conv_w, tm=512):
    L = proj.shape[0]
    cb = C_QKV // GDN_WIDTH
    return pl.pallas_call(
        _gdn_prep_kernel,
        grid=(L // tm, 3),
        in_specs=[pl.BlockSpec((tm, GDN_WIDTH), lambda i, j: (i, cb + j)),
                  pl.BlockSpec((conv_w.shape[0], GDN_WIDTH), lambda i, j: (0, j))],
        out_specs=pl.BlockSpec((tm, GDN_WIDTH), lambda i, j: (i, j)),
        out_shape=jax.ShapeDtypeStruct((L, 3 * GDN_WIDTH), F32),
        scratch_shapes=[pltpu.VMEM((3, 8, GDN_WIDTH), F32)],
        compiler_params=_cparams(("arbitrary", "arbitrary")),
        name="gdn_prep",
    )(proj, conv_w)


GDN_CB = 8


def _split2(x):
    hi = x.astype(BF16)
    return hi, (x - hi.astype(F32)).astype(BF16)


def _dot_split(a, b):
    ah, al = _split2(a)
    bh, bl = _split2(b)
    return _dot(ah, bh) + (_dot(ah, bl) + _dot(al, bh))


def _gdn_local_kernel(q_ref, k_ref, v_ref, sm_ref, val_ref, kcum_ref, qg_ref, kdT_ref, qk_ref, gl_ref):
    C = GDN_CHUNK
    chunks = range(GDN_CB)
    rows = [slice(c * C, (c + 1) * C) for c in chunks]
    ii = lax.broadcasted_iota(jnp.int32, (C, C), 0)
    jj = lax.broadcasted_iota(jnp.int32, (C, C), 1)
    causal = ii >= jj
    strict = ii > jj
    eye = (ii == jj).astype(F32)
    tril16 = causal.astype(BF16)
    row2 = lax.broadcasted_iota(jnp.int32, (C, 2 * C), 0)
    col2 = lax.broadcasted_iota(jnp.int32, (C, 2 * C), 1)
    keep = row2 > jnp.where(col2 < C, -1, col2 - C)

    head = pl.program_id(0)
    sm = sm_ref[...]
    sm_lane = lax.broadcasted_iota(jnp.int32, sm.shape, 1)
    g_col = jnp.sum(jnp.where(sm_lane == SM_G + head, sm, 0.0), axis=1, keepdims=True)
    b_col = jnp.sum(jnp.where(sm_lane == SM_BETA + head, sm, 0.0), axis=1, keepdims=True)

    k = [k_ref[rs, :] for rs in rows]
    beta = [b_col[rs, :] for rs in rows]
    res = []
    for rs in rows:
        x = jnp.where(keep, jnp.broadcast_to(g_col[rs, :], (C, 2 * C)), 0.0)
        hi = x.astype(BF16)
        r1 = x - hi.astype(F32)
        mid = r1.astype(BF16)
        lo = (r1 - mid.astype(F32)).astype(BF16)
        res.append(_dot(tril16, hi) + (_dot(tril16, mid) + _dot(tril16, lo)))
    gc = [jnp.broadcast_to(r[:, 0:1], (C, GDN_DH)) for r in res]
    decay = [jnp.where(causal, jnp.exp(r[:, C:2 * C]), 0.0) for r in res]
    kb = [kc * bc for kc, bc in zip(k, beta)]
    k16 = [kc.astype(BF16) for kc in k]
    a = [jnp.where(strict, _dot_nt(kbc.astype(BF16), kc16) * dc, 0.0) for kbc, kc16, dc in zip(kb, k16, decay)]
    t = [eye - ac for ac in a]
    p = a
    for _ in range(int(math.log2(C)) - 1):
        p = [_dot_split(pc, pc) for pc in p]
        t = [tc + _dot_split(tc, pc) for tc, pc in zip(t, p)]
    t16 = [tc.astype(BF16) for tc in t]
    egc = [jnp.exp(gcc) for gcc in gc]
    val = [_dot(tc, (v_ref[rs, :] * bc).astype(BF16)) for tc, rs, bc in zip(t16, rows, beta)]
    kcum = [_dot(tc, (kbc * ec).astype(BF16)) for tc, kbc, ec in zip(t16, kb, egc)]
    qk = [_dot_nt(q_ref[rs, :].astype(BF16), kc16) * dc for rs, kc16, dc in zip(rows, k16, decay)]
    kd = []
    for c in chunks:
        rs = rows[c]
        val_ref[rs, :] = val[c]
        kcum_ref[rs, :] = kcum[c].astype(BF16)
        qk_ref[0, rs, :] = qk[c].astype(BF16)
        glast = gc[c][C - 1:C, :]
        qg_ref[rs, :] = (q_ref[rs, :] * egc[c]).astype(BF16)
        kd.append(k[c] * jnp.exp(glast - gc[c]))
        gl_ref[0, c:c + 1, :] = jnp.exp(glast)
    for c in range(0, GDN_CB, 2):
        kdT_ref[0, c // 2] = jnp.concatenate([kd[c], kd[c + 1]], axis=0).T.astype(BF16)


def _gdn_local(qkv, small):
    L = qkv.shape[0]
    H = GDN_HEADS
    R = GDN_CB * GDN_CHUNK
    col = lambda off: pl.BlockSpec((R, GDN_DH), lambda h, i: (i, off + h))
    big = pl.BlockSpec((R, GDN_DH), lambda h, i: (i, h))
    return pl.pallas_call(
        _gdn_local_kernel,
        grid=(H, L // R),
        in_specs=[col(0), col(H), col(2 * H), pl.BlockSpec((R, LANES), lambda h, i: (i, 0))],
        out_specs=[big, big, big,
                   pl.BlockSpec((1, GDN_CB // 2, GDN_DH, 2 * GDN_CHUNK), lambda h, i: (h, i, 0, 0)),
                   pl.BlockSpec((1, R, GDN_CHUNK), lambda h, i: (h, i, 0)),
                   pl.BlockSpec((1, GDN_CB, GDN_DH), lambda h, i: (h, i, 0))],
        out_shape=[jax.ShapeDtypeStruct((L, GDN_WIDTH), F32),
                   jax.ShapeDtypeStruct((L, GDN_WIDTH), BF16),
                   jax.ShapeDtypeStruct((L, GDN_WIDTH), BF16),
                   jax.ShapeDtypeStruct((H, L // (2 * GDN_CHUNK), GDN_DH, 2 * GDN_CHUNK), BF16),
                   jax.ShapeDtypeStruct((H, L, GDN_CHUNK), BF16),
                   jax.ShapeDtypeStruct((H, L // GDN_CHUNK, GDN_DH), F32)],
        compiler_params=_cparams(("arbitrary", "arbitrary")),
        name="gdn_local",
    )(qkv, qkv, qkv, small)


def _gdn_scan_kernel(val_ref, kcum_ref, qg_ref, qk_ref, kdT_ref, gl_ref, z_ref, gn_ref, o_ref, st_ref):
    @pl.when(pl.program_id(0) == 0)
    def _():
        st_ref[...] = jnp.zeros_like(st_ref)

    C = GDN_CHUNK
    gn = gn_ref[...]
    heads = range(GDN_HEADS)
    cols = [slice(h * GDN_DH, (h + 1) * GDN_DH) for h in heads]
    state = [st_ref[h] for h in heads]
    for c in range(GDN_CB):
        rs = slice(c * C, (c + 1) * C)
        sb = [s.astype(BF16) for s in state]
        kcs = [_dot(kcum_ref[rs, cs], s) for cs, s in zip(cols, sb)]
        qgs = [_dot(qg_ref[rs, cs], s) for cs, s in zip(cols, sb)]
        vb = [(val_ref[rs, cs] - x).astype(BF16) for cs, x in zip(cols, kcs)]
        o = [x + _dot(qk_ref[h, rs, :], v) for h, x, v in zip(heads, qgs, vb)]
        kcols = slice((c % 2) * C, (c % 2 + 1) * C)
        state = [s * gl_ref[h, c:c + 1, :] + _dot(kdT_ref[h, c // 2, :, kcols], v)
                 for h, s, v in zip(heads, state, vb)]
        for h in heads:
            on = o[h] * lax.rsqrt(jnp.mean(o[h] * o[h], axis=-1, keepdims=True) + EPS) * gn
            z = z_ref[rs, cols[h]]
            o_ref[rs, cols[h]] = on * (z * _sigmoid(z))
    for h in heads:
        st_ref[h] = state[h]


def _gdn_scan(val, kcum, qg, qk, kdT, gl, proj, gnorm):
    L = val.shape[0]
    H = GDN_HEADS
    R = GDN_CB * GDN_CHUNK
    row = pl.BlockSpec((R, GDN_WIDTH), lambda i: (i, 0))
    zb = C_Z // GDN_WIDTH
    return pl.pallas_call(
        _gdn_scan_kernel,
        grid=(L // R,),
        in_specs=[row, row, row,
                  pl.BlockSpec((H, R, GDN_CHUNK), lambda i: (0, i, 0)),
                  pl.BlockSpec((H, GDN_CB // 2, GDN_DH, 2 * GDN_CHUNK), lambda i: (0, i, 0, 0)),
                  pl.BlockSpec((H, GDN_CB, GDN_DH), lambda i: (0, i, 0)),
                  pl.BlockSpec((R, GDN_WIDTH), lambda i: (i, zb)),
                  pl.BlockSpec((1, GDN_DH), lambda i: (0, 0))],
        out_specs=row,
        out_shape=jax.ShapeDtypeStruct((L, GDN_WIDTH), F32),
        scratch_shapes=[pltpu.VMEM((H, GDN_DH, GDN_DH), F32)],
        compiler_params=_cparams(("arbitrary",)),
        name="gdn_scan",
    )(val, kcum, qg, qk, kdT, gl, proj, gnorm)


def _mix_kernel(ya_ref, ys_ref, u_ref, yc_ref, x_ref, ga_ref, gb_ref, d_ref, wglu_ref, wout_ref, o_ref):
    def rms(y, g):
        return y * lax.rsqrt(jnp.mean(y * y, axis=-1, keepdims=True) + EPS) * g

    a = rms(ya_ref[...], ga_ref[...]).astype(BF16)
    yb = _gelu(ys_ref[...] + d_ref[...] * u_ref[...])
    yb = yb * _sigmoid(_dot(yb.astype(BF16), wglu_ref[...]))
    b = rms(yb, gb_ref[...]).astype(BF16)
    na, nb = a.shape[1], b.shape[1]
    acc = x_ref[...] + _dot(a, wout_ref[0:na, :])
    acc = acc + _dot(b, wout_ref[na:na + nb, :])
    o_ref[...] = acc + _dot(yc_ref[...].astype(BF16), wout_ref[na + nb:, :])


def _mix(ya, ys, proj, yc, x, ga, gb, d, wglu, wout, tm=256):
    L, D = x.shape
    ub = C_U // S5_WIDTH
    full = lambda a: pl.BlockSpec(a.shape, lambda i: (0, 0))
    return pl.pallas_call(
        _mix_kernel,
        grid=(L // tm,),
        in_specs=[pl.BlockSpec((tm, ya.shape[1]), lambda i: (i, 0)),
                  pl.BlockSpec((tm, S5_WIDTH), lambda i: (i, 0)),
                  pl.BlockSpec((tm, S5_WIDTH), lambda i: (i, ub)),
                  pl.BlockSpec((tm, GDN_WIDTH), lambda i: (i, 0)),
                  pl.BlockSpec((tm, D), lambda i: (i, 0)),
                  full(ga), full(gb), full(d), full(wglu), full(wout)],
        out_specs=pl.BlockSpec((tm, D), lambda i: (i, 0)),
        out_shape=jax.ShapeDtypeStruct((L, D), F32),
        compiler_params=_cparams(("arbitrary",)),
        name="mix_out",
    )(ya, ys, proj, yc, x, ga, gb, d, wglu, wout)


def _ffn_in_kernel(x_ref, g_ref, wg_ref, wu_ref, cg_ref, cu_ref, bg_ref, bu_ref, o_ref, xn_ref, carry_ref, hs_ref):
    i = pl.program_id(0)
    j = pl.program_id(1)

    @pl.when(j == 0)
    def _():
        xf = x_ref[...]
        ms = jnp.mean(xf * xf, axis=-1, keepdims=True)
        xn_ref[...] = (xf * lax.rsqrt(ms + EPS) * g_ref[...]).astype(BF16)

    @pl.when(i == 0)
    def _():
        carry_ref[j] = jnp.zeros(carry_ref.shape[1:], F32)

    xn = xn_ref[...]
    tm = xn.shape[0]

    def conv(h, slot, cw_ref, cb_ref):
        hs_ref[slot, 0:8, :] = carry_ref[j, slot]
        hs_ref[slot, 8:tm + 8, :] = h
        carry_ref[j, slot] = h[tm - 8:tm, :]
        taps = cw_ref.shape[0]
        y = cw_ref[taps - 1:taps, :] * h + cb_ref[...]
        for back in range(1, taps):
            y = y + cw_ref[taps - 1 - back:taps - back, :] * hs_ref[slot, 8 - back:8 - back + tm, :]
        return y

    gate = conv(_dot(xn, wg_ref[...]), 0, cg_ref, bg_ref)
    up = conv(_dot(xn, wu_ref[...]), 1, cu_ref, bu_ref)
    half_gate = 0.5 * gate
    o_ref[...] = (half_gate * (1.0 + jnp.tanh(half_gate)) * up).astype(BF16)


def _ffn_in(x, gain, wg, wu, cg, cu, bg, bu, tm=1024, tn=512):
    L, D = x.shape
    N = wg.shape[1]
    taps = cg.shape[0]
    wspec = pl.BlockSpec((D, tn), lambda i, j: (0, j))
    cspec = pl.BlockSpec((taps, tn), lambda i, j: (0, j))
    bspec = pl.BlockSpec((1, tn), lambda i, j: (0, j))
    return pl.pallas_call(
        _ffn_in_kernel,
        grid=(L // tm, N // tn),
        in_specs=[pl.BlockSpec((tm, D), lambda i, j: (i, 0)),
                  pl.BlockSpec((1, D), lambda i, j: (0, 0)),
                  wspec, wspec, cspec, cspec, bspec, bspec],
        out_specs=pl.BlockSpec((tm, tn), lambda i, j: (i, j)),
        out_shape=jax.ShapeDtypeStruct((L, N), BF16),
        scratch_shapes=[pltpu.VMEM((tm, D), BF16), pltpu.VMEM((N // tn, 2, 8, tn), F32),
                        pltpu.VMEM((2, tm + 8, tn), F32)],
        compiler_params=_cparams(("arbitrary", "arbitrary")),
        name="ffn_in",
    )(x, gain.reshape(1, D), wg, wu, cg, cu, bg, bu)


def _ffn_out_kernel(a_ref, w_ref, x_ref, o_ref):
    @pl.when(pl.program_id(1) == 0)
    def _():
        o_ref[...] = x_ref[...]

    o_ref[...] += _dot(a_ref[...], w_ref[...])


def _ffn_out(act, w, x, tm=1024, tk=512):
    L, D = x.shape
    K = act.shape[1]
    return pl.pallas_call(
        _ffn_out_kernel,
        grid=(L // tm, K // tk),
        in_specs=[pl.BlockSpec((tm, tk), lambda i, k: (i, k)),
                  pl.BlockSpec((tk, D), lambda i, k: (k, 0)),
                  pl.BlockSpec((tm, D), lambda i, k: (i, 0))],
        out_specs=pl.BlockSpec((tm, D), lambda i, k: (i, 0)),
        out_shape=jax.ShapeDtypeStruct((L, D), F32),
        compiler_params=_cparams(("arbitrary", "arbitrary")),
        name="ffn_out",
    )(act, w, x)


def _rope_tables(pos):
    half = ROPE_DIMS // 2
    inv = ROPE_THETA ** (-jnp.arange(half, dtype=F32) / half)
    ang = pos.astype(F32)[:, None] * inv[None, :]
    n = pos.shape[0]
    cos = jnp.concatenate([jnp.cos(ang), jnp.cos(ang), jnp.ones((n, NSA_DH - ROPE_DIMS), F32)], axis=1)
    sin = jnp.concatenate([-jnp.sin(ang), jnp.sin(ang), jnp.zeros((n, NSA_DH - ROPE_DIMS), F32)], axis=1)
    return jnp.tile(cos, (1, 2)), jnp.tile(sin, (1, 2))


def _permute_w_in(w):
    sp = np.cumsum([0, 512, 128, 128, 128, 128, 128, 128, 24, 512, 3072, 1024, 8, 8])
    q, kc, vc, ks, vs, kw, vw, gates, u, qkv, z, a, b = [w[:, sp[n]:sp[n + 1]] for n in range(13)]
    pad = jnp.zeros((w.shape[0], PROJ_COLS - C_SM - 40), w.dtype)
    return jnp.concatenate([q, ks, kw, vs, vw, z, qkv, u, kc, vc, gates, a, b, pad], axis=1)


def _nsa_mixer(proj, small, p, l):
    L = proj.shape[0]
    HK, G, DH = NSA_KV, NSA_G, NSA_DH
    nQ = L // TQ
    ns = L // SLC_BLOCK
    nb = L // CMP_STRIDE
    n_grp = LANES - DH
    assert ns % n_grp == 0 and (n_grp * SLC_BLOCK) % (2 * TK) == 0
    cos, sin = _rope_tables(jnp.arange(L))
    gains = jnp.concatenate([jnp.tile(p['nsa_q_norm'][l], NSA_HEADS), jnp.tile(p['nsa_ks_norm'][l], HK),
                             jnp.tile(p['nsa_kw_norm'][l], HK), jnp.ones((2 * HK * DH,), F32)]).reshape(1, -1)
    qT, ksaug, kwa, vsx, vwx = _nsa_prep(proj, gains, cos, sin, n_grp)
    kpad = jnp.zeros((HK, WINDOW, KW_LANES), BF16).at[:, :, DH].set(1.0)
    kwp = jnp.concatenate([kpad, kwa], axis=1)
    vwxp = jnp.pad(vwx, ((0, 0), (0, 0), (WINDOW, 0)))

    eye = jnp.eye(HK, dtype=F32)
    pe = jnp.tile(p['cmp_pe'][l][:, None, :], (1, HK, 1)).reshape(1, CMP_LEN * HK * DH)
    w1 = lambda name: jnp.einsum('ldf,hg->lhdgf', p[name][l], eye).reshape(CMP_LEN * HK * DH, -1).astype(BF16)
    w2 = lambda name: jnp.einsum('fd,hg->hfgd', p[name][l], eye).reshape(-1, HK * DH).astype(BF16)
    ccos, csin = _rope_tables(jnp.arange(nb) * CMP_STRIDE + CMP_LEN // 2)
    gk = jnp.tile(p['nsa_kc_norm'][l], HK).reshape(1, LANES)
    kc, vcx = _nsa_compress(proj, pe, w1('cmp_k_w1'), w2('cmp_k_w2'), w1('cmp_v_w1'), w2('cmp_v_w2'), gk, ccos, csin)

    cmp_start = np.arange(nb) * CMP_STRIDE
    slc_start = np.arange(ns) * SLC_BLOCK
    smapT = ((cmp_start[None, :] <= slc_start[:, None] + SLC_BLOCK - 1)
             & (cmp_start[None, :] + CMP_LEN - 1 >= slc_start[:, None]))
    smapT[:, nb - 1] = False
    smapT = jnp.asarray(smapT, BF16)

    gT = jnp.pad(small[:, :HK * G * 3].reshape(L, HK, G * 3).transpose(1, 2, 0), ((0, 0), (0, 16 - G * 3), (0, 0)))
    return _nsa_attention(qT, kc, vcx, smapT, ksaug, vsx, kwp, vwxp, gT, n_grp)


def _s5_mixer(proj, p, l):
    L = proj.shape[0]
    T = S5_CHUNK
    nc = L // T
    G, H, P = S5_GROUPS, S5_CH, S5_STATE
    lam_re, lam_im = p['s5_lam_re'][l], p['s5_lam_im'][l]
    dt = jnp.exp(p['s5_log_dt'][l])[:, None]
    mag = jnp.exp(lam_re * dt)
    lb_re = mag * jnp.cos(lam_im * dt)
    lb_im = mag * jnp.sin(lam_im * dt)
    den = lam_re * lam_re + lam_im * lam_im
    nr, ni = lb_re - 1.0, lb_im
    coef_re = (nr * lam_re + ni * lam_im) / den
    coef_im = (ni * lam_re - nr * lam_im) / den
    b_re, b_im = p['s5_b_re'][l], p['s5_b_im'][l]
    bb_re = coef_re[..., None] * b_re - coef_im[..., None] * b_im
    bb_im = coef_re[..., None] * b_im + coef_im[..., None] * b_re
    bsr, bsi = bb_re.transpose(0, 2, 1), bb_im.transpose(0, 2, 1)
    tile_rows = lambda a: jnp.tile(a, (1, T, 1))
    u = proj[:, C_U:C_U + S5_WIDTH]
    ug = u.astype(BF16).reshape(nc, T, G, H).transpose(2, 0, 1, 3).reshape(G, nc, T * H)
    y = _s5_scan(ug, lb_re[:, None, :], lb_im[:, None, :], tile_rows(p['s5_c_re'][l]), tile_rows(p['s5_c_im'][l]),
                 tile_rows(bsr), tile_rows(bsi), bsr, bsi)
    return y.reshape(G, nc, T, H).transpose(1, 2, 0, 3).reshape(L, S5_WIDTH)


def _gdn_mixer(proj, small, p, l):
    L = proj.shape[0]
    H = GDN_HEADS
    qkv = _gdn_prep(proj, p['gdn_conv'][l])
    val, kcum, qg, kdT, qk, gl = _gdn_local(qkv, small)
    return _gdn_scan(val, kcum, qg, qk, kdT, gl, proj, p['gdn_norm'][l].reshape(1, GDN_DH))


def _pad_cols(a, n):
    return jnp.pad(a, ((0, 0), (0, n - a.shape[1])))


def _forward(x3, p):
    x = x3[0]
    depth = p['w_in'].shape[0]
    for l in range(depth):
        w_in = _permute_w_in(p['w_in'][l]).astype(BF16)
        proj = _rms_matmul(x, p['attn_norm'][l], w_in)
        sm_par = jnp.zeros((8, LANES), F32)
        sm_par = sm_par.at[0, SM_G:SM_BETA].set(p['gdn_dt_bias'][l]).at[1, SM_G:SM_BETA].set(-jnp.exp(p['gdn_a_log'][l]))
        small = _small(proj, sm_par)
        y_a = _nsa_mixer(proj, small, p, l)
        y_s = _s5_mixer(proj, p, l)
        y_c = _gdn_mixer(proj, small, p, l)
        x = _mix(y_a, y_s, proj, y_c, x, p['nsa_out_norm'][l].reshape(1, -1), p['s5_out_norm'][l].reshape(1, -1),
                 p['s5_d'][l].reshape(1, -1), p['s5_w_glu'][l].astype(BF16), p['w_out'][l].astype(BF16))
        wf = p['ffn_w_in'][l]
        cf = p['ffn_conv'][l]
        bf = p['ffn_conv_b'][l].reshape(1, -1)
        act = _ffn_in(x, p['ffn_norm'][l],
                      _pad_cols(wf[:, :D_FF], D_FF_PAD).astype(BF16), _pad_cols(wf[:, D_FF:], D_FF_PAD).astype(BF16),
                      _pad_cols(cf[:, :D_FF], D_FF_PAD), _pad_cols(cf[:, D_FF:], D_FF_PAD),
                      _pad_cols(bf[:, :D_FF], D_FF_PAD), _pad_cols(bf[:, D_FF:], D_FF_PAD))
        w_o = jnp.pad(p['ffn_w_out'][l], ((0, D_FF_PAD - D_FF), (0, 0))).astype(BF16)
        x = _ffn_out(act, w_o, x)
    return x[None]


_PARAM_NAMES = ('attn_norm', 'w_in', 'nsa_q_norm', 'nsa_kc_norm', 'nsa_ks_norm', 'nsa_kw_norm', 'cmp_pe',
                'cmp_k_w1', 'cmp_k_w2', 'cmp_v_w1', 'cmp_v_w2', 'nsa_out_norm', 's5_lam_re', 's5_lam_im',
                's5_log_dt', 's5_b_re', 's5_b_im', 's5_c_re', 's5_c_im', 's5_d', 's5_w_glu', 's5_out_norm',
                'gdn_conv', 'gdn_a_log', 'gdn_dt_bias', 'gdn_norm', 'w_out', 'ffn_norm', 'ffn_w_in', 'ffn_conv',
                'ffn_conv_b', 'ffn_w_out')


def kernel(x, attn_norm, w_in, nsa_q_norm, nsa_kc_norm, nsa_ks_norm, nsa_kw_norm, cmp_pe, cmp_k_w1, cmp_k_w2,
           cmp_v_w1, cmp_v_w2, nsa_out_norm, s5_lam_re, s5_lam_im, s5_log_dt, s5_b_re, s5_b_im, s5_c_re, s5_c_im,
           s5_d, s5_w_glu, s5_out_norm, gdn_conv, gdn_a_log, gdn_dt_bias, gdn_norm, w_out, ffn_norm, ffn_w_in,
           ffn_conv, ffn_conv_b, ffn_w_out):
    vals = (attn_norm, w_in, nsa_q_norm, nsa_kc_norm, nsa_ks_norm, nsa_kw_norm, cmp_pe, cmp_k_w1, cmp_k_w2,
            cmp_v_w1, cmp_v_w2, nsa_out_norm, s5_lam_re, s5_lam_im, s5_log_dt, s5_b_re, s5_b_im, s5_c_re, s5_c_im,
            s5_d, s5_w_glu, s5_out_norm, gdn_conv, gdn_a_log, gdn_dt_bias, gdn_norm, w_out, ffn_norm, ffn_w_in,
            ffn_conv, ffn_conv_b, ffn_w_out)
    return _forward(x, dict(zip(_PARAM_NAMES, vals)))
```

```python
import functools
import math

import jax
import jax.numpy as jnp
import numpy as np
from jax import lax
from jax.experimental import pallas as pl
from jax.experimental.pallas import tpu as pltpu

F32 = jnp.float32
BF16 = jnp.bfloat16
HIGHEST = lax.Precision.HIGHEST

D_MODEL = 2048
NSA_HEADS = 8
NSA_KV = 2
NSA_G = NSA_HEADS // NSA_KV
NSA_DH = 64
CMP_LEN = 32
CMP_STRIDE = 16
SLC_BLOCK = 64
SLC_TOPK = 16
WINDOW = 512
ROPE_THETA = 500000.0
ROPE_DIMS = NSA_DH // 4
S5_WIDTH = 512
S5_CH = 16
S5_GROUPS = 32
S5_STATE = 64
S5_CHUNK = 32
GDN_HEADS = 8
GDN_DH = 128
GDN_WIDTH = 1024
GDN_CHUNK = 64
D_FF = 5504
D_FF_PAD = 5632
EPS = 1e-6
NEG_BIG = -(2.0 ** 100)
LOG2_E = math.log2(math.e)
V_ROWS = 80
KW_LANES = 128
CMP_CHUNK = 256

LANES = 128
TQ = 128
TK = 512
VMEM_LIMIT = 56 * 1024 * 1024

C_Q, C_KS, C_KW, C_VS, C_VW = 0, 512, 640, 768, 896
C_Z, C_QKV, C_U, C_KC, C_VC, C_SM = 1024, 2048, 5120, 5632, 5760, 5888
NSA_PREP_W = C_Z
PROJ_COLS = 6144
SM_GATES, SM_G, SM_BETA, SM_END = 0, 24, 32, 40


def _cparams(sem):
    return pltpu.CompilerParams(dimension_semantics=sem, vmem_limit_bytes=VMEM_LIMIT)


def _dot(a, b, precision=None):
    return jnp.dot(a, b, preferred_element_type=F32, precision=precision)


def _dot_nt(a, b, precision=None):
    return lax.dot_general(a, b, (((1,), (1,)), ((), ())), preferred_element_type=F32, precision=precision)


def _gelu(x):
    return x * (0.5 * (1.0 + jnp.tanh(math.sqrt(2.0 / math.pi) * (x + 0.044715 * (x * x * x)))))


def _sigmoid(x):
    return 1.0 / (1.0 + jnp.exp(-x))


def _rms_matmul_kernel(x_ref, g_ref, w_ref, o_ref, xn_ref):
    @pl.when(pl.program_id(1) == 0)
    def _():
        xf = x_ref[...]
        ms = jnp.mean(xf * xf, axis=-1, keepdims=True)
        xn_ref[...] = (xf * lax.rsqrt(ms + EPS) * g_ref[...]).astype(BF16)

    o_ref[...] = _dot(xn_ref[...], w_ref[...])


def _rms_matmul(x, gain, w, tm=512, tn=2048):
    L, D = x.shape
    N = w.shape[1]
    return pl.pallas_call(
        _rms_matmul_kernel,
        grid=(L // tm, N // tn),
        in_specs=[
            pl.BlockSpec((tm, D), lambda i, j: (i, 0)),
            pl.BlockSpec((1, D), lambda i, j: (0, 0)),
            pl.BlockSpec((D, tn), lambda i, j: (0, j)),
        ],
        out_specs=pl.BlockSpec((tm, tn), lambda i, j: (i, j)),
        out_shape=jax.ShapeDtypeStruct((L, N), F32),
        scratch_shapes=[pltpu.VMEM((tm, D), BF16)],
        compiler_params=_cparams(("arbitrary", "arbitrary")),
        name="rms_proj",
    )(x, gain.reshape(1, D), w)


def _rope_slab(x, cos, sin):
    d = lax.broadcasted_iota(jnp.int32, x.shape, 1) & (NSA_DH - 1)
    half = ROPE_DIMS // 2
    partner = jnp.where(d < half, pltpu.roll(x, LANES - half, 1), pltpu.roll(x, half, 1))
    return x * cos + partner * sin


def _nsa_prep_kernel(x_ref, g_ref, cos_ref, sin_ref, bsum_ref, qT_ref, ksaug_ref, kwa_ref, vsx_ref, vwx_ref, *, n_grp):
    i = pl.program_id(0)
    tm = x_ref.shape[0]
    cos = cos_ref[...]
    sin = sin_ref[...]
    bsum = bsum_ref[...]
    lane = lax.broadcasted_iota(jnp.int32, (tm, LANES), 1)
    tiles = range(tm // TQ)

    def normed(s):
        sl = slice(s * LANES, (s + 1) * LANES)
        x = x_ref[:, sl]
        ms = _dot(x * x, bsum, precision=HIGHEST)
        return _rope_slab(x * lax.rsqrt(ms + EPS) * g_ref[:, sl], cos, sin)

    def per_head(y):
        return y, pltpu.roll(y, NSA_DH, 1)

    for s in range((C_KS - C_Q) // LANES):
        y = normed(s) * (NSA_DH ** -0.5 * LOG2_E)
        hk, g0 = divmod(s, NSA_G // 2)
        for t in tiles:
            yT = y[t * TQ:(t + 1) * TQ, :].T.astype(BF16)
            for half in range(2):
                g = 2 * g0 + half
                qT_ref[hk, t, :, g * TQ:(g + 1) * TQ] = yT[half * NSA_DH:(half + 1) * NSA_DH, :]

    blk = ((i * tm + lax.broadcasted_iota(jnp.int32, (tm, LANES), 0)) // SLC_BLOCK) % n_grp
    onehot = jnp.where(lane - NSA_DH == blk, 1.0, 0.0)
    for hk, y in enumerate(per_head(normed(C_KS // LANES))):
        ksaug_ref[hk] = jnp.where(lane < NSA_DH, y, onehot).astype(BF16)
    for hk, y in enumerate(per_head(normed(C_KW // LANES))):
        kwa_ref[hk] = jnp.where(lane < NSA_DH, y, 0.0).astype(BF16)

    ones_rows = jnp.where(lax.broadcasted_iota(jnp.int32, (V_ROWS - NSA_DH, tm), 0) == 0, 1.0, 0.0).astype(BF16)
    for c0, v_ref in ((C_VS, vsx_ref), (C_VW, vwx_ref)):
        x = x_ref[:, c0:c0 + LANES]
        for t in tiles:
            xT = x[t * TQ:(t + 1) * TQ, :].T.astype(BF16)
            for hk in range(NSA_KV):
                v_ref[hk, 0:NSA_DH, t * TQ:(t + 1) * TQ] = xT[hk * NSA_DH:(hk + 1) * NSA_DH, :]
        for hk in range(NSA_KV):
            v_ref[hk, NSA_DH:V_ROWS, :] = ones_rows


def _nsa_prep(proj, gains, cos, sin, n_grp, tm=512):
    L = proj.shape[0]
    W = NSA_PREP_W
    HK = NSA_KV
    GW = NSA_G * TQ
    bsum = jnp.asarray(np.kron(np.eye(2), np.ones((NSA_DH, NSA_DH))) / NSA_DH, F32)
    rows = pl.BlockSpec((HK, tm, LANES), lambda i: (0, i, 0))
    vals = pl.BlockSpec((HK, V_ROWS, tm), lambda i: (0, 0, i))
    return pl.pallas_call(
        functools.partial(_nsa_prep_kernel, n_grp=n_grp),
        grid=(L // tm,),
        in_specs=[
            pl.BlockSpec((tm, W), lambda i: (i, 0)),
            pl.BlockSpec((1, W), lambda i: (0, 0)),
            pl.BlockSpec((tm, LANES), lambda i: (i, 0)),
            pl.BlockSpec((tm, LANES), lambda i: (i, 0)),
            pl.BlockSpec((LANES, LANES), lambda i: (0, 0)),
        ],
        out_specs=[pl.BlockSpec((HK, tm // TQ, NSA_DH, GW), lambda i: (0, i, 0, 0)), rows, rows, vals, vals],
        out_shape=[jax.ShapeDtypeStruct((HK, L // TQ, NSA_DH, GW), BF16),
                   jax.ShapeDtypeStruct((HK, L, LANES), BF16), jax.ShapeDtypeStruct((HK, L, LANES), BF16),
                   jax.ShapeDtypeStruct((HK, V_ROWS, L), BF16), jax.ShapeDtypeStruct((HK, V_ROWS, L), BF16)],
        compiler_params=_cparams(("arbitrary",)),
        name="nsa_prep",
    )(proj, gains, cos, sin, bsum)


def _cmp_kernel(xk_ref, xv_ref, pe_ref, w1k_ref, w2k_ref, w1v_ref, w2v_ref, gk_ref, cos_ref, sin_ref, bsum_ref,
                kc_ref, vcx_ref):
    nb = kc_ref.shape[1]
    half_w = CMP_STRIDE * LANES

    def mlp(x_ref, w1_ref, w2_ref):
        hb = jnp.concatenate([x_ref[pl.ds(l, nb, stride=CMP_STRIDE), :] for l in range(CMP_STRIDE)], axis=1)
        a = (hb + pe_ref[:, 0:half_w]).astype(BF16)
        b = (hb + pe_ref[:, half_w:2 * half_w]).astype(BF16)
        p1 = _dot(a, w1_ref[0:half_w, :])
        p2 = _dot(b, w1_ref[half_w:2 * half_w, :])
        h = p1 + pltpu.roll(p2, nb - 1, 0)
        return _dot(_gelu(h).astype(BF16), w2_ref[...])

    rid = lax.broadcasted_iota(jnp.int32, (nb, LANES), 0)
    real = rid < nb - 1
    kc = jnp.where(real, mlp(xk_ref, w1k_ref, w2k_ref), 0.0)
    vc = jnp.where(real, mlp(xv_ref, w1v_ref, w2v_ref), 0.0)
    ms = _dot(kc * kc, bsum_ref[...], precision=HIGHEST)
    kcn = _rope_slab(kc * lax.rsqrt(ms + EPS) * gk_ref[...], cos_ref[...], sin_ref[...])
    kc_ref[0] = kcn[:, 0:NSA_DH].astype(BF16)
    kc_ref[1] = pltpu.roll(kcn, NSA_DH, 1)[:, 0:NSA_DH].astype(BF16)
    ones_rows = jnp.where(lax.broadcasted_iota(jnp.int32, (V_ROWS - NSA_DH, nb), 0) == 0, 1.0, 0.0).astype(BF16)
    for t in range(nb // LANES):
        vT = vc[t * LANES:(t + 1) * LANES, :].T.astype(BF16)
        for hk in range(NSA_KV):
            vcx_ref[hk, 0:NSA_DH, t * LANES:(t + 1) * LANES] = vT[hk * NSA_DH:(hk + 1) * NSA_DH, :]
    for hk in range(NSA_KV):
        vcx_ref[hk, NSA_DH:V_ROWS, :] = ones_rows


def _nsa_compress(proj, pe, w1k, w2k, w1v, w2v, gk, cos, sin):
    L = proj.shape[0]
    nb = L // CMP_STRIDE
    bsum = jnp.asarray(np.kron(np.eye(2), np.ones((NSA_DH, NSA_DH))) / NSA_DH, F32)
    once = lambda shape, idx: pl.BlockSpec(shape, lambda i: idx, pipeline_mode=pl.Buffered(1))
    full = lambda a: once(a.shape, (0,) * a.ndim)
    return pl.pallas_call(
        _cmp_kernel,
        grid=(1,),
        in_specs=[once((L, LANES), (0, C_KC // LANES)), once((L, LANES), (0, C_VC // LANES)),
                  full(pe), full(w1k), full(w2k), full(w1v), full(w2v), full(gk), full(cos), full(sin), full(bsum)],
        out_specs=[pl.BlockSpec((NSA_KV, nb, NSA_DH), lambda i: (0, 0, 0)),
                   pl.BlockSpec((NSA_KV, V_ROWS, nb), lambda i: (0, 0, 0))],
        out_shape=[jax.ShapeDtypeStruct((NSA_KV, nb, NSA_DH), BF16),
                   jax.ShapeDtypeStruct((NSA_KV, V_ROWS, nb), BF16)],
        compiler_params=_cparams(("arbitrary",)),
        name="nsa_compress",
    )(proj, proj, pe, w1k, w2k, w1v, w2v, gk, cos, sin, bsum)


def _nsa_attn_kernel(qT_ref, kc_ref, vcx_ref, smap_ref, ksaug_ref, vsx_ref, kwp_ref, vwxp_ref, gT_ref,
                     o_ref, qaug_ref, qwin_ref, ss_ref, mt_ref, m_ref, acc_ref, sc_ref, imp_ref, *, n_grp):
    i = pl.program_id(0)
    s0 = i * TQ
    hks = range(qT_ref.shape[0])
    nb = kc_ref.shape[1]
    ns = smap_ref.shape[0]
    GW = NSA_G * TQ
    qT = [qT_ref[h, 0] for h in hks]
    tq = s0 + (lax.broadcasted_iota(jnp.int32, (1, GW), 1) & (TQ - 1))

    cch = min(CMP_CHUNK, nb)
    n_cch = (s0 + TQ - CMP_LEN) // (CMP_STRIDE * cch) + 1
    nthr = (tq - (CMP_LEN - 1)) // CMP_STRIDE

    def cmp_scores(c, mcs):
        r0c = pl.multiple_of(c * cch, cch)
        s = [_dot(kc_ref[h, pl.ds(r0c, cch), :], qT[h]) for h in hks]
        nrow = r0c + lax.broadcasted_iota(jnp.int32, (cch, GW), 0)
        out = []
        for h in hks:
            sh = jnp.where(nrow <= nthr, s[h], NEG_BIG)
            sc_ref[h, pl.ds(r0c, cch), :] = sh
            out.append(jnp.maximum(mcs[h], jnp.max(sh, axis=0, keepdims=True)))
        return tuple(out)

    mc = lax.fori_loop(0, n_cch, cmp_scores, tuple(jnp.full((1, GW), NEG_BIG, F32) for _ in hks))
    imp_ref[...] = jnp.zeros_like(imp_ref)
    ones16 = jnp.ones((16, cch), BF16)
    VX = vcx_ref.shape[1]

    def cmp_accum(c, carry):
        r0c = pl.multiple_of(c * cch, cch)
        sm = smap_ref[:, pl.ds(r0c, cch)]
        parts = [_split2(jnp.exp2(sc_ref[h, pl.ds(r0c, cch), :] - mc[h])) for h in hks]
        for h in hks:
            imp_ref[h] += _dot(sm, parts[h][0]) + _dot(sm, parts[h][1])
        return tuple((carry[h][0] + _dot(vcx_ref[h, :, pl.ds(r0c, cch)], parts[h][0]),
                      carry[h][1] + _dot(ones16, parts[h][1])) for h in hks)

    cacc = lax.fori_loop(0, n_cch, cmp_accum,
                         tuple((jnp.zeros((VX, GW), F32), jnp.zeros((16, GW), F32)) for _ in hks))
    inv_l = [jnp.where(tq >= CMP_LEN - 1, 1.0 / (cacc[h][0][NSA_DH:NSA_DH + 1] + cacc[h][1][0:1]), 0.0) for h in hks]
    ocT = [cacc[h][0][0:NSA_DH] * inv_l[h] for h in hks]

    imp = []
    for h in hks:
        tot = imp_ref[h, :, 0:TQ] * inv_l[h][:, 0:TQ]
        for g in range(1, NSA_G):
            tot = tot + imp_ref[h, :, g * TQ:(g + 1) * TQ] * inv_l[h][:, g * TQ:(g + 1) * TQ]
        imp.append(tot)

    jrow = lax.broadcasted_iota(jnp.int32, (ns, TQ), 0)
    cur = (s0 + lax.broadcasted_iota(jnp.int32, (1, TQ), 1)) // SLC_BLOCK
    forced = jnp.where(jrow == 0, 0.0, jnp.where(jrow == cur, 0.0, jnp.where(jrow == cur - 1, 0.0, NEG_BIG)))
    val = [jnp.where(jrow >= 1, jnp.where(jrow <= cur - 2, imp[h], -jnp.inf), -jnp.inf) for h in hks]
    for _ in range(min(SLC_TOPK, ns) - 3):
        mx = [jnp.max(val[h], axis=0, keepdims=True) for h in hks]
        idx = [jnp.min(jnp.where(val[h] == mx[h], jrow, ns), axis=0, keepdims=True) for h in hks]
        val = [jnp.where(jrow == idx[h], -jnp.inf, val[h]) for h in hks]
    bias = [jnp.where(jrow >= 1, jnp.where(jrow <= cur - 2, jnp.where(val[h] == -jnp.inf, 0.0, NEG_BIG), forced), forced)
            for h in hks]
    own = jrow // 2 == i
    for h in hks:
        bias16 = jnp.where(own, NEG_BIG, bias[h]).astype(BF16)
        for grp in range(ns // n_grp):
            qaug_ref[h, grp, 0:NSA_DH, :] = qT[h]
            for g in range(NSA_G):
                qaug_ref[h, grp, NSA_DH:NSA_DH + n_grp, g * TQ:(g + 1) * TQ] = bias16[grp * n_grp:(grp + 1) * n_grp, :]

    tiles_per_grp = (n_grp * SLC_BLOCK) // TK
    n_pairs = (s0 + 2 * TK - 1) // (2 * TK)

    def scores(kt, slot):
        k0 = pl.multiple_of(kt * TK, TK)
        grp = kt // tiles_per_grp
        s = [_dot(ksaug_ref[h, pl.ds(k0, TK), :], qaug_ref[h, grp]) for h in hks]
        for h in hks:
            sb = s[h].astype(BF16)
            ss_ref[h, slot] = sb
            mt_ref[h, slot] = jnp.max(sb, axis=0, keepdims=True).astype(F32)

    def online_update(sb, mt, v):
        m = [m_ref[h] for h in hks]
        mn = [jnp.maximum(m[h], mt[h]) for h in hks]
        pv = [_dot(v[h], jnp.exp2(sb[h] - mn[h].astype(BF16))) for h in hks]
        for h in hks:
            acc_ref[h] = jnp.exp2(m[h] - mn[h]) * acc_ref[h] + pv[h]
            m_ref[h] = mn[h]

    def accumulate(kt, slot):
        k0 = pl.multiple_of(kt * TK, TK)
        online_update([ss_ref[h, slot] for h in hks], [mt_ref[h, slot] for h in hks],
                      [vsx_ref[h, :, pl.ds(k0, TK)] for h in hks])

    m_ref[...] = jnp.full(m_ref.shape, NEG_BIG, F32)
    acc_ref[...] = jnp.zeros_like(acc_ref)

    @pl.when(n_pairs > 0)
    def _():
        scores(0, 0)

    def pair_body(j, carry):
        scores(2 * j + 1, 1)
        accumulate(2 * j, 0)
        scores(2 * j + 2, 0)
        accumulate(2 * j + 1, 1)
        return carry

    lax.fori_loop(0, n_pairs - 1, pair_body, 0)

    @pl.when(n_pairs > 0)
    def _():
        scores(2 * n_pairs - 1, 1)
        accumulate(2 * n_pairs - 2, 0)
        accumulate(2 * n_pairs - 1, 1)

    r0 = pl.multiple_of(s0, TQ)
    rel = lax.broadcasted_iota(jnp.int32, (TQ, GW), 0)
    qrel = tq - s0
    sd = [_dot(ksaug_ref[h, pl.ds(r0, TQ), 0:NSA_DH], qT[h]) for h in hks]
    sd16 = [jnp.where(rel <= qrel, sd[h], NEG_BIG).astype(BF16) for h in hks]
    online_update(sd16, [jnp.max(sd16[h], axis=0, keepdims=True).astype(F32) for h in hks],
                  [vsx_ref[h, :, pl.ds(r0, TQ)] for h in hks])
    osT = [acc_ref[h, 0:NSA_DH, :] / acc_ref[h, NSA_DH:NSA_DH + 1, :] for h in hks]

    flag_row = lax.broadcasted_iota(jnp.int32, (KW_LANES - NSA_DH, GW), 0) == 0
    for h in hks:
        qwin_ref[h, 0:NSA_DH, :] = qT[h]
        qwin_ref[h, NSA_DH:KW_LANES, :] = jnp.where(flag_row, NEG_BIG, 0.0).astype(BF16)
    qwin = [qwin_ref[h] for h in hks]
    edge = [_dot(kwp_ref[h, pl.ds(r0, TQ), :], qwin[h]) for h in hks]
    mid = [_dot(kwp_ref[h, pl.ds(r0 + TQ, WINDOW - TQ), :], qwin[h]) for h in hks]
    diag = [_dot(kwp_ref[h, pl.ds(r0 + WINDOW, TQ), :], qwin[h]) for h in hks]
    edge16 = [jnp.where(rel > qrel, edge[h], NEG_BIG).astype(BF16) for h in hks]
    diag16 = [jnp.where(rel <= qrel, diag[h], NEG_BIG).astype(BF16) for h in hks]
    mid16 = [mid[h].astype(BF16) for h in hks]
    mw = [jnp.maximum(jnp.maximum(jnp.max(edge16[h], axis=0, keepdims=True), jnp.max(mid16[h], axis=0, keepdims=True)),
                      jnp.max(diag16[h], axis=0, keepdims=True)) for h in hks]
    accw = [(_dot(vwxp_ref[h, :, pl.ds(r0, TQ)], jnp.exp2(edge16[h] - mw[h]))
             + _dot(vwxp_ref[h, :, pl.ds(r0 + TQ, WINDOW - TQ)], jnp.exp2(mid16[h] - mw[h]))
             + _dot(vwxp_ref[h, :, pl.ds(r0 + WINDOW, TQ)], jnp.exp2(diag16[h] - mw[h]))) for h in hks]

    for h in hks:
        owT = accw[h][0:NSA_DH] / accw[h][NSA_DH:NSA_DH + 1]
        gates = _sigmoid(gT_ref[h])
        outs = []
        for g in range(NSA_G):
            sl = slice(g * TQ, (g + 1) * TQ)
            outs.append(gates[3 * g:3 * g + 1] * ocT[h][:, sl] + gates[3 * g + 1:3 * g + 2] * osT[h][:, sl]
                        + gates[3 * g + 2:3 * g + 3] * owT[:, sl])
        for g in range(0, NSA_G, 2):
            c0 = (h * NSA_G + g) * NSA_DH
            o_ref[:, c0:c0 + 2 * NSA_DH] = jnp.concatenate([outs[g], outs[g + 1]], axis=0).T


def _nsa_attention(qT, kc, vcx, smapT, ksaug, vsx, kwp, vwxp, gT, n_grp):
    HK, nQ, _, GW = qT.shape
    nb = kc.shape[1]
    ns = smapT.shape[0]
    L = vsx.shape[2]
    KA = ksaug.shape[2]
    scratch = [pltpu.VMEM((HK, ns // n_grp, KA, GW), BF16),
               pltpu.VMEM((HK, KW_LANES, GW), BF16),
               pltpu.VMEM((HK, 2, TK, GW), BF16),
               pltpu.VMEM((HK, 2, 1, GW), F32),
               pltpu.VMEM((HK, 1, GW), F32),
               pltpu.VMEM((HK, V_ROWS, GW), F32),
               pltpu.VMEM((HK, nb, GW), F32),
               pltpu.VMEM((HK, ns, GW), F32)]
    resident = lambda a: pl.BlockSpec(a.shape, lambda i: (0,) * a.ndim, pipeline_mode=pl.Buffered(1))
    return pl.pallas_call(
        functools.partial(_nsa_attn_kernel, n_grp=n_grp),
        grid=(nQ,),
        in_specs=[
            pl.BlockSpec((HK, 1, NSA_DH, GW), lambda i: (0, i, 0, 0)),
            resident(kc), resident(vcx), resident(smapT), resident(ksaug), resident(vsx), resident(kwp),
            resident(vwxp),
            pl.BlockSpec((HK, 16, TQ), lambda i: (0, 0, i)),
        ],
        out_specs=pl.BlockSpec((TQ, NSA_HEADS * NSA_DH), lambda i: (i, 0)),
        out_shape=jax.ShapeDtypeStruct((nQ * TQ, NSA_HEADS * NSA_DH), F32),
        scratch_shapes=scratch,
        compiler_params=_cparams(("arbitrary",)),
        name="nsa_attn",
    )(qT, kc, vcx, smapT, ksaug, vsx, kwp, vwxp, gT)


def _s5_kernel(u_ref, lbr_ref, lbi_ref, ctr_ref, cti_ref, btr_ref, bti_ref, bsr_ref, bsi_ref, y_ref, lm_ref):
    T = S5_CHUNK
    W = T * S5_CH
    ar = lbr_ref[0]
    ai = lbi_ref[0]
    delta = lax.broadcasted_iota(jnp.int32, (W, S5_STATE), 0) // S5_CH

    def powers(e):
        pr = jnp.ones((W, S5_STATE), F32)
        pi = jnp.zeros((W, S5_STATE), F32)
        fr, fi = ar, ai
        for b in range(T.bit_length() - 1):
            bit = ((e >> b) & 1) == 1
            nr = pr * fr - pi * fi
            ni = pr * fi + pi * fr
            pr = jnp.where(bit, nr, pr)
            pi = jnp.where(bit, ni, pi)
            fr, fi = fr * fr - fi * fi, 2.0 * fr * fi
        return pr, pi, fr, fi

    pwr, pwi, aTr, aTi = powers(delta)
    rvr, rvi, _, _ = powers(T - 1 - delta)
    ctr, cti = ctr_ref[0], cti_ref[0]
    car = ctr * pwr - cti * pwi
    cai = ctr * pwi + cti * pwr

    kw = _dot_nt(bsr_ref[0], car, HIGHEST) - _dot_nt(bsi_ref[0], cai, HIGHEST)
    lane = lax.broadcasted_iota(jnp.int32, (S5_CH, W), 1)
    for tau in range(T):
        sh = S5_CH * tau
        blk = kw if tau == 0 else jnp.where(lane >= sh, pltpu.roll(kw, sh, 1), 0.0)
        lm_ref[sh:sh + S5_CH, :] = blk.astype(BF16)

    u = u_ref[0]
    nc = u.shape[0]
    y = _dot(u, lm_ref[...])

    btr, bti = btr_ref[0], bti_ref[0]
    sr = _dot(u, (rvr * btr - rvi * bti).astype(BF16))
    si = _dot(u, (rvr * bti + rvi * btr).astype(BF16))
    rowc = lax.broadcasted_iota(jnp.int32, (nc, S5_STATE), 0)
    fr, fi = aTr, aTi
    step = 1
    while step < nc:
        shr = jnp.where(rowc >= step, pltpu.roll(sr, step, 0), 0.0)
        shi = jnp.where(rowc >= step, pltpu.roll(si, step, 0), 0.0)
        sr, si = sr + fr * shr - fi * shi, si + fr * shi + fi * shr
        fr, fi = fr * fr - fi * fi, 2.0 * fr * fi
        step *= 2
    xr = jnp.where(rowc >= 1, pltpu.roll(sr, 1, 0), 0.0)
    xi = jnp.where(rowc >= 1, pltpu.roll(si, 1, 0), 0.0)
    c1r = car * ar - cai * ai
    c1i = car * ai + cai * ar
    y = y + _dot_nt(xr.astype(BF16), c1r.astype(BF16)) - _dot_nt(xi.astype(BF16), c1i.astype(BF16))
    y_ref[0] = y


def _s5_scan(ug, lbr, lbi, ctr, cti, btr, bti, bsr, bsi):
    G, nc, W = ug.shape
    grp = lambda shape: pl.BlockSpec((1,) + shape, lambda g: (g, 0, 0))
    return pl.pallas_call(
        _s5_kernel,
        grid=(G,),
        in_specs=[grp((nc, W)), grp((1, S5_STATE)), grp((1, S5_STATE)),
                  grp((W, S5_STATE)), grp((W, S5_STATE)), grp((W, S5_STATE)), grp((W, S5_STATE)),
                  grp((S5_CH, S5_STATE)), grp((S5_CH, S5_STATE))],
        out_specs=grp((nc, W)),
        out_shape=jax.ShapeDtypeStruct((G, nc, W), F32),
        scratch_shapes=[pltpu.VMEM((W, W), BF16)],
        compiler_params=_cparams(("arbitrary",)),
        name="s5_scan",
    )(ug, lbr, lbi, ctr, cti, btr, bti, bsr, bsi)


def _small_kernel(x_ref, p_ref, o_ref):
    x = x_ref[...]
    lane = lax.broadcasted_iota(jnp.int32, x.shape, 1)
    z = x + p_ref[0:1, :]
    softplus = jnp.maximum(z, 0.0) + jnp.log(1.0 + jnp.exp(-jnp.abs(z)))
    gdec = p_ref[1:2, :] * softplus
    beta = _sigmoid(x)
    o_ref[...] = jnp.where(lane < SM_G, x, jnp.where(lane < SM_BETA, gdec, beta))


def _small(proj, params, tm=1024):
    L = proj.shape[0]
    cb = C_SM // LANES
    return pl.pallas_call(
        _small_kernel,
        grid=(L // tm,),
        in_specs=[pl.BlockSpec((tm, LANES), lambda i: (i, cb)), pl.BlockSpec((8, LANES), lambda i: (0, 0))],
        out_specs=pl.BlockSpec((tm, LANES), lambda i: (i, 0)),
        out_shape=jax.ShapeDtypeStruct((L, LANES), F32),
        compiler_params=_cparams(("arbitrary",)),
        name="small_cols",
    )(proj, params)


GDN_CB = 8


def _split2(x):
    hi = x.astype(BF16)
    return hi, (x - hi.astype(F32)).astype(BF16)


def _dot_split(a, b):
    ah, al = _split2(a)
    bh, bl = _split2(b)
    return _dot(ah, bh) + (_dot(ah, bl) + _dot(al, bh))


def _gdn_local_kernel(xq_ref, xk_ref, xv_ref, cwq_ref, cwk_ref, cwv_ref, sm_ref,
                      val_ref, kcum_ref, qg_ref, kdT_ref, qk_ref, gl_ref, carry_ref, hs_ref):
    C = GDN_CHUNK
    R = xq_ref.shape[0]

    @pl.when(pl.program_id(1) == 0)
    def _():
        carry_ref[...] = jnp.zeros_like(carry_ref)

    def conv_silu(x_ref, cw_ref, slot):
        x = x_ref[...]
        hs_ref[slot, 0:8, :] = carry_ref[slot]
        hs_ref[slot, 8:R + 8, :] = x
        carry_ref[slot] = x[R - 8:R, :]
        taps = cw_ref.shape[0]
        y = cw_ref[taps - 1:taps, :] * x
        for back in range(1, taps):
            y = y + cw_ref[taps - 1 - back:taps - back, :] * hs_ref[slot, 8 - back:8 - back + R, :]
        half_y = 0.5 * y
        return half_y * (1.0 + jnp.tanh(half_y))

    def l2norm(y):
        return y * lax.rsqrt(jnp.sum(y * y, axis=-1, keepdims=True) + EPS)

    q_all = l2norm(conv_silu(xq_ref, cwq_ref, 0)) * (GDN_DH ** -0.5)
    k_all = l2norm(conv_silu(xk_ref, cwk_ref, 1))
    v_all = conv_silu(xv_ref, cwv_ref, 2)
    chunks = range(GDN_CB)
    rows = [slice(c * C, (c + 1) * C) for c in chunks]
    ii = lax.broadcasted_iota(jnp.int32, (C, C), 0)
    jj = lax.broadcasted_iota(jnp.int32, (C, C), 1)
    causal = ii >= jj
    strict = ii > jj
    eye = (ii == jj).astype(F32)
    tril16 = causal.astype(BF16)
    row2 = lax.broadcasted_iota(jnp.int32, (C, 2 * C), 0)
    col2 = lax.broadcasted_iota(jnp.int32, (C, 2 * C), 1)
    keep = row2 > jnp.where(col2 < C, -1, col2 - C)

    head = pl.program_id(0)
    sm = sm_ref[...]
    sm_lane = lax.broadcasted_iota(jnp.int32, sm.shape, 1)
    g_col = jnp.sum(jnp.where(sm_lane == SM_G + head, sm, 0.0), axis=1, keepdims=True)
    b_col = jnp.sum(jnp.where(sm_lane == SM_BETA + head, sm, 0.0), axis=1, keepdims=True)

    k = [k_all[rs, :] for rs in rows]
    beta = [b_col[rs, :] for rs in rows]
    res = []
    for rs in rows:
        x = jnp.where(keep, jnp.broadcast_to(g_col[rs, :], (C, 2 * C)), 0.0)
        hi = x.astype(BF16)
        r1 = x - hi.astype(F32)
        mid = r1.astype(BF16)
        lo = (r1 - mid.astype(F32)).astype(BF16)
        res.append(_dot(tril16, hi) + (_dot(tril16, mid) + _dot(tril16, lo)))
    gc = [jnp.broadcast_to(r[:, 0:1], (C, GDN_DH)) for r in res]
    decay = [jnp.where(causal, jnp.exp(r[:, C:2 * C]), 0.0) for r in res]
    kb = [kc * bc for kc, bc in zip(k, beta)]
    k16 = [kc.astype(BF16) for kc in k]
    a = [jnp.where(strict, _dot_nt(kbc.astype(BF16), kc16) * dc, 0.0) for kbc, kc16, dc in zip(kb, k16, decay)]
    t = [eye - ac for ac in a]
    p = a
    for _ in range(int(math.log2(C)) - 1):
        p = [_dot_split(pc, pc) for pc in p]
        t = [tc + _dot_split(tc, pc) for tc, pc in zip(t, p)]
    t16 = [tc.astype(BF16) for tc in t]
    egc = [jnp.exp(gcc) for gcc in gc]
    val = [_dot(tc, (v_all[rs, :] * bc).astype(BF16)) for tc, rs, bc in zip(t16, rows, beta)]
    kcum = [_dot(tc, (kbc * ec).astype(BF16)) for tc, kbc, ec in zip(t16, kb, egc)]
    qk = [_dot_nt(q_all[rs, :].astype(BF16), kc16) * dc for rs, kc16, dc in zip(rows, k16, decay)]
    kd = []
    for c in chunks:
        rs = rows[c]
        val_ref[rs, :] = val[c]
        kcum_ref[rs, :] = kcum[c].astype(BF16)
        qk_ref[0, rs, :] = qk[c].astype(BF16)
        glast = gc[c][C - 1:C, :]
        qg_ref[rs, :] = (q_all[rs, :] * egc[c]).astype(BF16)
        kd.append(k[c] * jnp.exp(glast - gc[c]))
        gl_ref[0, c:c + 1, :] = jnp.exp(glast)
    for c in range(0, GDN_CB, 2):
        kdT_ref[0, c // 2] = jnp.concatenate([kd[c], kd[c + 1]], axis=0).T.astype(BF16)


def _gdn_local(proj, conv_w, small):
    L = proj.shape[0]
    H = GDN_HEADS
    R = GDN_CB * GDN_CHUNK
    cb = C_QKV // GDN_DH
    taps = conv_w.shape[0]
    col = lambda off: pl.BlockSpec((R, GDN_DH), lambda h, i: (i, cb + off + h))
    cw = lambda off: pl.BlockSpec((taps, GDN_DH), lambda h, i: (0, off + h))
    big = pl.BlockSpec((R, GDN_DH), lambda h, i: (i, h))
    return pl.pallas_call(
        _gdn_local_kernel,
        grid=(H, L // R),
        in_specs=[col(0), col(H), col(2 * H), cw(0), cw(H), cw(2 * H),
                  pl.BlockSpec((R, LANES), lambda h, i: (i, 0))],
        out_specs=[big, big, big,
                   pl.BlockSpec((1, GDN_CB // 2, GDN_DH, 2 * GDN_CHUNK), lambda h, i: (h, i, 0, 0)),
                   pl.BlockSpec((1, R, GDN_CHUNK), lambda h, i: (h, i, 0)),
                   pl.BlockSpec((1, GDN_CB, GDN_DH), lambda h, i: (h, i, 0))],
        out_shape=[jax.ShapeDtypeStruct((L, GDN_WIDTH), F32),
                   jax.ShapeDtypeStruct((L, GDN_WIDTH), BF16),
                   jax.ShapeDtypeStruct((L, GDN_WIDTH), BF16),
                   jax.ShapeDtypeStruct((H, L // (2 * GDN_CHUNK), GDN_DH, 2 * GDN_CHUNK), BF16),
                   jax.ShapeDtypeStruct((H, L, GDN_CHUNK), BF16),
                   jax.ShapeDtypeStruct((H, L // GDN_CHUNK, GDN_DH), F32)],
        scratch_shapes=[pltpu.VMEM((3, 8, GDN_DH), F32), pltpu.VMEM((3, R + 8, GDN_DH), F32)],
        compiler_params=_cparams(("arbitrary", "arbitrary")),
        name="gdn_local",
    )(proj, proj, proj, conv_w, conv_w, conv_w, small)


def _gdn_scan_kernel(val_ref, kcum_ref, qg_ref, qk_ref, kdT_ref, gl_ref, z_ref, gn_ref, o_ref, st_ref):
    @pl.when(pl.program_id(0) == 0)
    def _():
        st_ref[...] = jnp.zeros_like(st_ref)

    C = GDN_CHUNK
    gn = gn_ref[...]
    heads = range(GDN_HEADS)
    cols = [slice(h * GDN_DH, (h + 1) * GDN_DH) for h in heads]
    state = [st_ref[h] for h in heads]
    for c in range(GDN_CB):
        rs = slice(c * C, (c + 1) * C)
        sb = [s.astype(BF16) for s in state]
        kcs = [_dot(kcum_ref[rs, cs], s) for cs, s in zip(cols, sb)]
        qgs = [_dot(qg_ref[rs, cs], s) for cs, s in zip(cols, sb)]
        vb = [(val_ref[rs, cs] - x).astype(BF16) for cs, x in zip(cols, kcs)]
        o = [x + _dot(qk_ref[h, rs, :], v) for h, x, v in zip(heads, qgs, vb)]
        kcols = slice((c % 2) * C, (c % 2 + 1) * C)
        state = [s * gl_ref[h, c:c + 1, :] + _dot(kdT_ref[h, c // 2, :, kcols], v)
                 for h, s, v in zip(heads, state, vb)]
        for h in heads:
            on = o[h] * lax.rsqrt(jnp.mean(o[h] * o[h], axis=-1, keepdims=True) + EPS) * gn
            z = z_ref[rs, cols[h]]
            o_ref[rs, cols[h]] = on * (z * _sigmoid(z))
    for h in heads:
        st_ref[h] = state[h]


def _gdn_scan(val, kcum, qg, qk, kdT, gl, proj, gnorm):
    L = val.shape[0]
    H = GDN_HEADS
    R = GDN_CB * GDN_CHUNK
    row = pl.BlockSpec((R, GDN_WIDTH), lambda i: (i, 0))
    zb = C_Z // GDN_WIDTH
    return pl.pallas_call(
        _gdn_scan_kernel,
        grid=(L // R,),
        in_specs=[row, row, row,
                  pl.BlockSpec((H, R, GDN_CHUNK), lambda i: (0, i, 0)),
                  pl.BlockSpec((H, GDN_CB // 2, GDN_DH, 2 * GDN_CHUNK), lambda i: (0, i, 0, 0)),
                  pl.BlockSpec((H, GDN_CB, GDN_DH), lambda i: (0, i, 0)),
                  pl.BlockSpec((R, GDN_WIDTH), lambda i: (i, zb)),
                  pl.BlockSpec((1, GDN_DH), lambda i: (0, 0))],
        out_specs=row,
        out_shape=jax.ShapeDtypeStruct((L, GDN_WIDTH), F32),
        scratch_shapes=[pltpu.VMEM((H, GDN_DH, GDN_DH), F32)],
        compiler_params=_cparams(("arbitrary",)),
        name="gdn_scan",
    )(val, kcum, qg, qk, kdT, gl, proj, gnorm)


def _mix_kernel(ya_ref, ys_ref, u_ref, yc_ref, x_ref, ga_ref, gb_ref, d_ref, wglu_ref, wout_ref, o_ref):
    def rms(y, g):
        return y * lax.rsqrt(jnp.mean(y * y, axis=-1, keepdims=True) + EPS) * g

    a = rms(ya_ref[...], ga_ref[...]).astype(BF16)
    yb = _gelu(ys_ref[...] + d_ref[...] * u_ref[...])
    yb = yb * _sigmoid(_dot(yb.astype(BF16), wglu_ref[...]))
    b = rms(yb, gb_ref[...]).astype(BF16)
    na, nb = a.shape[1], b.shape[1]
    acc = x_ref[...] + _dot(a, wout_ref[0:na, :])
    acc = acc + _dot(b, wout_ref[na:na + nb, :])
    o_ref[...] = acc + _dot(yc_ref[...].astype(BF16), wout_ref[na + nb:, :])


def _mix(ya, ys, proj, yc, x, ga, gb, d, wglu, wout, tm=256):
    L, D = x.shape
    ub = C_U // S5_WIDTH
    full = lambda a: pl.BlockSpec(a.shape, lambda i: (0, 0))
    return pl.pallas_call(
        _mix_kernel,
        grid=(L // tm,),
        in_specs=[pl.BlockSpec((tm, ya.shape[1]), lambda i: (i, 0)),
                  pl.BlockSpec((tm, S5_WIDTH), lambda i: (i, 0)),
                  pl.BlockSpec((tm, S5_WIDTH), lambda i: (i, ub)),
                  pl.BlockSpec((tm, GDN_WIDTH), lambda i: (i, 0)),
                  pl.BlockSpec((tm, D), lambda i: (i, 0)),
                  full(ga), full(gb), full(d), full(wglu), full(wout)],
        out_specs=pl.BlockSpec((tm, D), lambda i: (i, 0)),
        out_shape=jax.ShapeDtypeStruct((L, D), F32),
        compiler_params=_cparams(("arbitrary",)),
        name="mix_out",
    )(ya, ys, proj, yc, x, ga, gb, d, wglu, wout)


def _ffn_in_kernel(x_ref, g_ref, wg_ref, wu_ref, cg_ref, cu_ref, bg_ref, bu_ref, o_ref, xn_ref, carry_ref, hs_ref):
    i = pl.program_id(0)
    j = pl.program_id(1)

    @pl.when(j == 0)
    def _():
        xf = x_ref[...]
        ms = jnp.mean(xf * xf, axis=-1, keepdims=True)
        xn_ref[...] = (xf * lax.rsqrt(ms + EPS) * g_ref[...]).astype(BF16)

    @pl.when(i == 0)
    def _():
        carry_ref[j] = jnp.zeros(carry_ref.shape[1:], F32)

    xn = xn_ref[...]
    tm = xn.shape[0]

    def conv(h, slot, cw_ref, cb_ref):
        hs_ref[slot, 0:8, :] = carry_ref[j, slot]
        hs_ref[slot, 8:tm + 8, :] = h
        carry_ref[j, slot] = h[tm - 8:tm, :]
        taps = cw_ref.shape[0]
        y = cw_ref[taps - 1:taps, :] * h + cb_ref[...]
        for back in range(1, taps):
            y = y + cw_ref[taps - 1 - back:taps - back, :] * hs_ref[slot, 8 - back:8 - back + tm, :]
        return y

    gate = conv(_dot(xn, wg_ref[...]), 0, cg_ref, bg_ref)
    up = conv(_dot(xn, wu_ref[...]), 1, cu_ref, bu_ref)
    half_gate = 0.5 * gate
    o_ref[...] = (half_gate * (1.0 + jnp.tanh(half_gate)) * up).astype(BF16)


def _ffn_in(x, gain, w, cw, cb, tm=1024, tn=512):
    L, D = x.shape
    N = w.shape[1] // 2
    nj = N // tn
    taps = cw.shape[0]
    halves = lambda rows: (pl.BlockSpec((rows, tn), lambda i, j: (0, j)), pl.BlockSpec((rows, tn), lambda i, j: (0, nj + j)))
    wg, wu = halves(D)
    cg, cu = halves(taps)
    bg, bu = halves(1)
    return pl.pallas_call(
        _ffn_in_kernel,
        grid=(L // tm, nj),
        in_specs=[pl.BlockSpec((tm, D), lambda i, j: (i, 0)),
                  pl.BlockSpec((1, D), lambda i, j: (0, 0)),
                  wg, wu, cg, cu, bg, bu],
        out_specs=pl.BlockSpec((tm, tn), lambda i, j: (i, j)),
        out_shape=jax.ShapeDtypeStruct((L, N), BF16),
        scratch_shapes=[pltpu.VMEM((tm, D), BF16), pltpu.VMEM((nj, 2, 8, tn), F32),
                        pltpu.VMEM((2, tm + 8, tn), F32)],
        compiler_params=_cparams(("arbitrary", "arbitrary")),
        name="ffn_in",
    )(x, gain.reshape(1, D), w, w, cw, cw, cb, cb)


def _ffn_out_kernel(a_ref, w_ref, x_ref, o_ref):
    @pl.when(pl.program_id(1) == 0)
    def _():
        o_ref[...] = x_ref[...]

    o_ref[...] += _dot(a_ref[...], w_ref[...])


def _ffn_out(act, w, x, tm=1024, tk=512):
    L, D = x.shape
    K = act.shape[1]
    return pl.pallas_call(
        _ffn_out_kernel,
        grid=(L // tm, K // tk),
        in_specs=[pl.BlockSpec((tm, tk), lambda i, k: (i, k)),
                  pl.BlockSpec((tk, D), lambda i, k: (k, 0)),
                  pl.BlockSpec((tm, D), lambda i, k: (i, 0))],
        out_specs=pl.BlockSpec((tm, D), lambda i, k: (i, 0)),
        out_shape=jax.ShapeDtypeStruct((L, D), F32),
        compiler_params=_cparams(("arbitrary", "arbitrary")),
        name="ffn_out",
    )(act, w, x)


def _rope_tables(pos):
    half = ROPE_DIMS // 2
    inv = ROPE_THETA ** (-jnp.arange(half, dtype=F32) / half)
    ang = pos.astype(F32)[:, None] * inv[None, :]
    n = pos.shape[0]
    cos = jnp.concatenate([jnp.cos(ang), jnp.cos(ang), jnp.ones((n, NSA_DH - ROPE_DIMS), F32)], axis=1)
    sin = jnp.concatenate([-jnp.sin(ang), jnp.sin(ang), jnp.zeros((n, NSA_DH - ROPE_DIMS), F32)], axis=1)
    return jnp.tile(cos, (1, 2)), jnp.tile(sin, (1, 2))


def _permute_w_in(w):
    sp = np.cumsum([0, 512, 128, 128, 128, 128, 128, 128, 24, 512, 3072, 1024, 8, 8])
    q, kc, vc, ks, vs, kw, vw, gates, u, qkv, z, a, b = [w[:, sp[n]:sp[n + 1]] for n in range(13)]
    pad = jnp.zeros((w.shape[0], PROJ_COLS - C_SM - 40), w.dtype)
    return jnp.concatenate([q, ks, kw, vs, vw, z, qkv, u, kc, vc, gates, a, b, pad], axis=1)


def _nsa_mixer(proj, small, p, l):
    L = proj.shape[0]
    HK, G, DH = NSA_KV, NSA_G, NSA_DH
    nQ = L // TQ
    ns = L // SLC_BLOCK
    nb = L // CMP_STRIDE
    n_grp = LANES - DH
    assert ns % n_grp == 0 and (n_grp * SLC_BLOCK) % (2 * TK) == 0
    cos, sin = _rope_tables(jnp.arange(L))
    gains = jnp.concatenate([jnp.tile(p['nsa_q_norm'][l], NSA_HEADS), jnp.tile(p['nsa_ks_norm'][l], HK),
                             jnp.tile(p['nsa_kw_norm'][l], HK), jnp.ones((2 * HK * DH,), F32)]).reshape(1, -1)
    qT, ksaug, kwa, vsx, vwx = _nsa_prep(proj, gains, cos, sin, n_grp)
    kpad = jnp.zeros((HK, WINDOW, KW_LANES), BF16).at[:, :, DH].set(1.0)
    kwp = jnp.concatenate([kpad, kwa], axis=1)
    vwxp = jnp.pad(vwx, ((0, 0), (0, 0), (WINDOW, 0)))

    eye = jnp.eye(HK, dtype=F32)
    pe = jnp.tile(p['cmp_pe'][l][:, None, :], (1, HK, 1)).reshape(1, CMP_LEN * HK * DH)
    w1 = lambda name: jnp.einsum('ldf,hg->lhdgf', p[name][l], eye).reshape(CMP_LEN * HK * DH, -1).astype(BF16)
    w2 = lambda name: jnp.einsum('fd,hg->hfgd', p[name][l], eye).reshape(-1, HK * DH).astype(BF16)
    ccos, csin = _rope_tables(jnp.arange(nb) * CMP_STRIDE + CMP_LEN // 2)
    gk = jnp.tile(p['nsa_kc_norm'][l], HK).reshape(1, LANES)
    kc, vcx = _nsa_compress(proj, pe, w1('cmp_k_w1'), w2('cmp_k_w2'), w1('cmp_v_w1'), w2('cmp_v_w2'), gk, ccos, csin)

    cmp_start = np.arange(nb) * CMP_STRIDE
    slc_start = np.arange(ns) * SLC_BLOCK
    smapT = ((cmp_start[None, :] <= slc_start[:, None] + SLC_BLOCK - 1)
             & (cmp_start[None, :] + CMP_LEN - 1 >= slc_start[:, None]))
    smapT[:, nb - 1] = False
    smapT = jnp.asarray(smapT, BF16)

    gT = jnp.pad(small[:, :HK * G * 3].reshape(L, HK, G * 3).transpose(1, 2, 0), ((0, 0), (0, 16 - G * 3), (0, 0)))
    return _nsa_attention(qT, kc, vcx, smapT, ksaug, vsx, kwp, vwxp, gT, n_grp)


def _s5_mixer(proj, p, l):
    L = proj.shape[0]
    T = S5_CHUNK
    nc = L // T
    G, H, P = S5_GROUPS, S5_CH, S5_STATE
    lam_re, lam_im = p['s5_lam_re'][l], p['s5_lam_im'][l]
    dt = jnp.exp(p['s5_log_dt'][l])[:, None]
    mag = jnp.exp(lam_re * dt)
    lb_re = mag * jnp.cos(lam_im * dt)
    lb_im = mag * jnp.sin(lam_im * dt)
    den = lam_re * lam_re + lam_im * lam_im
    nr, ni = lb_re - 1.0, lb_im
    coef_re = (nr * lam_re + ni * lam_im) / den
    coef_im = (ni * lam_re - nr * lam_im) / den
    b_re, b_im = p['s5_b_re'][l], p['s5_b_im'][l]
    bb_re = coef_re[..., None] * b_re - coef_im[..., None] * b_im
    bb_im = coef_re[..., None] * b_im + coef_im[..., None] * b_re
    bsr, bsi = bb_re.transpose(0, 2, 1), bb_im.transpose(0, 2, 1)
    tile_rows = lambda a: jnp.tile(a, (1, T, 1))
    u = proj[:, C_U:C_U + S5_WIDTH]
    ug = u.astype(BF16).reshape(nc, T, G, H).transpose(2, 0, 1, 3).reshape(G, nc, T * H)
    y = _s5_scan(ug, lb_re[:, None, :], lb_im[:, None, :], tile_rows(p['s5_c_re'][l]), tile_rows(p['s5_c_im'][l]),
                 tile_rows(bsr), tile_rows(bsi), bsr, bsi)
    return y.reshape(G, nc, T, H).transpose(1, 2, 0, 3).reshape(L, S5_WIDTH)


def _gdn_mixer(proj, small, p, l):
    L = proj.shape[0]
    H = GDN_HEADS
    val, kcum, qg, kdT, qk, gl = _gdn_local(proj, p['gdn_conv'][l], small)
    return _gdn_scan(val, kcum, qg, qk, kdT, gl, proj, p['gdn_norm'][l].reshape(1, GDN_DH))


def _pad_halves(a):
    pad = jnp.zeros((a.shape[0], D_FF_PAD - D_FF), a.dtype)
    return jnp.concatenate([a[:, :D_FF], pad, a[:, D_FF:], pad], axis=1)


def _forward(x3, p):
    x = x3.reshape(x3.shape[1:])
    depth = p['w_in'].shape[0]
    for l in range(depth):
        w_in = _permute_w_in(p['w_in'][l]).astype(BF16)
        proj = _rms_matmul(x, p['attn_norm'][l], w_in)
        sm_par = jnp.zeros((8, LANES), F32)
        sm_par = sm_par.at[0, SM_G:SM_BETA].set(p['gdn_dt_bias'][l]).at[1, SM_G:SM_BETA].set(-jnp.exp(p['gdn_a_log'][l]))
        small = _small(proj, sm_par)
        y_a = _nsa_mixer(proj, small, p, l)
        y_s = _s5_mixer(proj, p, l)
        y_c = _gdn_mixer(proj, small, p, l)
        x = _mix(y_a, y_s, proj, y_c, x, p['nsa_out_norm'][l].reshape(1, -1), p['s5_out_norm'][l].reshape(1, -1),
                 p['s5_d'][l].reshape(1, -1), p['s5_w_glu'][l].astype(BF16), p['w_out'][l].astype(BF16))
        act = _ffn_in(x, p['ffn_norm'][l], _pad_halves(p['ffn_w_in'][l].astype(BF16)), _pad_halves(p['ffn_conv'][l]),
                      _pad_halves(p['ffn_conv_b'][l].reshape(1, -1)))
        w_o = jnp.pad(p['ffn_w_out'][l].astype(BF16), ((0, D_FF_PAD - D_FF), (0, 0)))
        x = _ffn_out(act, w_o, x)
    return x.reshape(x3.shape)


_PARAM_NAMES = ('attn_norm', 'w_in', 'nsa_q_norm', 'nsa_kc_norm', 'nsa_ks_norm', 'nsa_kw_norm', 'cmp_pe',
                'cmp_k_w1', 'cmp_k_w2', 'cmp_v_w1', 'cmp_v_w2', 'nsa_out_norm', 's5_lam_re', 's5_lam_im',
                's5_log_dt', 's5_b_re', 's5_b_im', 's5_c_re', 's5_c_im', 's5_d', 's5_w_glu', 's5_out_norm',
                'gdn_conv', 'gdn_a_log', 'gdn_dt_bias', 'gdn_norm', 'w_out', 'ffn_norm', 'ffn_w_in', 'ffn_conv',
                'ffn_conv_b', 'ffn_w_out')


def kernel(x, attn_norm, w_in, nsa_q_norm, nsa_kc_norm, nsa_ks_norm, nsa_kw_norm, cmp_pe, cmp_k_w1, cmp_k_w2,
           cmp_v_w1, cmp_v_w2, nsa_out_norm, s5_lam_re, s5_lam_im, s5_log_dt, s5_b_re, s5_b_im, s5_c_re, s5_c_im,
           s5_d, s5_w_glu, s5_out_norm, gdn_conv, gdn_a_log, gdn_dt_bias, gdn_norm, w_out, ffn_norm, ffn_w_in,
           ffn_conv, ffn_conv_b, ffn_w_out):
    vals = (attn_norm, w_in, nsa_q_norm, nsa_kc_norm, nsa_ks_norm, nsa_kw_norm, cmp_pe, cmp_k_w1, cmp_k_w2,
            cmp_v_w1, cmp_v_w2, nsa_out_norm, s5_lam_re, s5_lam_im, s5_log_dt, s5_b_re, s5_b_im, s5_c_re, s5_c_im,
            s5_d, s5_w_glu, s5_out_norm, gdn_conv, gdn_a_log, gdn_dt_bias, gdn_norm, w_out, ffn_norm, ffn_w_in,
            ffn_conv, ffn_conv_b, ffn_w_out)
    return _forward(x, dict(zip(_PARAM_NAMES, vals)))
```

```python
import functools
import math

import jax
import jax.numpy as jnp
import numpy as np
from jax import lax
from jax.experimental import pallas as pl
from jax.experimental.pallas import tpu as pltpu

F32 = jnp.float32
BF16 = jnp.bfloat16
HIGHEST = lax.Precision.HIGHEST

D_MODEL = 2048
NSA_HEADS = 8
NSA_KV = 2
NSA_G = NSA_HEADS // NSA_KV
NSA_DH = 64
CMP_LEN = 32
CMP_STRIDE = 16
SLC_BLOCK = 64
SLC_TOPK = 16
WINDOW = 512
ROPE_THETA = 500000.0
ROPE_DIMS = NSA_DH // 4
S5_WIDTH = 512
S5_CH = 16
S5_GROUPS = 32
S5_STATE = 64
S5_CHUNK = 32
GDN_HEADS = 8
GDN_DH = 128
GDN_WIDTH = 1024
GDN_CHUNK = 64
D_FF = 5504
D_FF_PAD = 5632
EPS = 1e-6
NEG_BIG = -(2.0 ** 100)
LOG2_E = math.log2(math.e)
V_ROWS = 80
KW_LANES = 128
CMP_CHUNK = 256
SEL_UNROLL = 2

LANES = 128
TQ = 128
TK = 512
VMEM_LIMIT = 56 * 1024 * 1024

C_Q, C_KS, C_KW, C_VS, C_VW = 0, 512, 640, 768, 896
C_Z, C_QKV, C_U, C_KC, C_VC, C_SM = 1024, 2048, 5120, 5632, 5760, 5888
NSA_PREP_W = C_Z
PROJ_COLS = 6144
SM_GATES, SM_G, SM_BETA, SM_END = 0, 24, 32, 40


def _cparams(sem):
    return pltpu.CompilerParams(dimension_semantics=sem, vmem_limit_bytes=VMEM_LIMIT)


def _dot(a, b, precision=None):
    return jnp.dot(a, b, preferred_element_type=F32, precision=precision)


def _dot_nt(a, b, precision=None):
    return lax.dot_general(a, b, (((1,), (1,)), ((), ())), preferred_element_type=F32, precision=precision)


def _gelu(x):
    return x * (0.5 * (1.0 + jnp.tanh(math.sqrt(2.0 / math.pi) * (x + 0.044715 * (x * x * x)))))


def _sigmoid(x):
    return 1.0 / (1.0 + jnp.exp(-x))


def _rms_matmul_kernel(x_ref, g_ref, w_ref, o_ref, xn_ref):
    @pl.when(pl.program_id(1) == 0)
    def _():
        xf = x_ref[...]
        ms = jnp.mean(xf * xf, axis=-1, keepdims=True)
        xn_ref[...] = (xf * lax.rsqrt(ms + EPS) * g_ref[...]).astype(BF16)

    o_ref[...] = _dot(xn_ref[...], w_ref[...])


def _rms_matmul(x, gain, w, tm=512, tn=2048):
    L, D = x.shape
    N = w.shape[1]
    return pl.pallas_call(
        _rms_matmul_kernel,
        grid=(L // tm, N // tn),
        in_specs=[
            pl.BlockSpec((tm, D), lambda i, j: (i, 0)),
            pl.BlockSpec((1, D), lambda i, j: (0, 0)),
            pl.BlockSpec((D, tn), lambda i, j: (0, j)),
        ],
        out_specs=pl.BlockSpec((tm, tn), lambda i, j: (i, j)),
        out_shape=jax.ShapeDtypeStruct((L, N), F32),
        scratch_shapes=[pltpu.VMEM((tm, D), BF16)],
        compiler_params=_cparams(("arbitrary", "arbitrary")),
        name="rms_proj",
    )(x, gain.reshape(1, D), w)


def _rope_slab(x, cos, sin):
    d = lax.broadcasted_iota(jnp.int32, x.shape, 1) & (NSA_DH - 1)
    half = ROPE_DIMS // 2
    partner = jnp.where(d < half, pltpu.roll(x, LANES - half, 1), pltpu.roll(x, half, 1))
    return x * cos + partner * sin


def _nsa_prep_kernel(x_ref, g_ref, cos_ref, sin_ref, bsum_ref, qT_ref, ksaug_ref, kwa_ref, vsx_ref, vwx_ref, *, n_grp):
    i = pl.program_id(0)
    tm = x_ref.shape[0]
    cos = cos_ref[...]
    sin = sin_ref[...]
    bsum = bsum_ref[...]
    lane = lax.broadcasted_iota(jnp.int32, (tm, LANES), 1)
    tiles = range(tm // TQ)

    def normed(s):
        sl = slice(s * LANES, (s + 1) * LANES)
        x = x_ref[:, sl]
        ms = _dot(x * x, bsum, precision=HIGHEST)
        return _rope_slab(x * lax.rsqrt(ms + EPS) * g_ref[:, sl], cos, sin)

    def per_head(y):
        return y, pltpu.roll(y, NSA_DH, 1)

    for s in range((C_KS - C_Q) // LANES):
        y = normed(s) * (NSA_DH ** -0.5 * LOG2_E)
        hk, g0 = divmod(s, NSA_G // 2)
        for t in tiles:
            yT = y[t * TQ:(t + 1) * TQ, :].T.astype(BF16)
            for half in range(2):
                g = 2 * g0 + half
                qT_ref[hk, t, :, g * TQ:(g + 1) * TQ] = yT[half * NSA_DH:(half + 1) * NSA_DH, :]

    blk = ((i * tm + lax.broadcasted_iota(jnp.int32, (tm, LANES), 0)) // SLC_BLOCK) % n_grp
    onehot = jnp.where(lane - NSA_DH == blk, 1.0, 0.0)
    for hk, y in enumerate(per_head(normed(C_KS // LANES))):
        ksaug_ref[hk] = jnp.where(lane < NSA_DH, y, onehot).astype(BF16)
    for hk, y in enumerate(per_head(normed(C_KW // LANES))):
        kwa_ref[hk] = jnp.where(lane < NSA_DH, y, 0.0).astype(BF16)

    ones_rows = jnp.where(lax.broadcasted_iota(jnp.int32, (V_ROWS - NSA_DH, tm), 0) == 0, 1.0, 0.0).astype(BF16)
    for c0, v_ref in ((C_VS, vsx_ref), (C_VW, vwx_ref)):
        x = x_ref[:, c0:c0 + LANES]
        for t in tiles:
            xT = x[t * TQ:(t + 1) * TQ, :].T.astype(BF16)
            for hk in range(NSA_KV):
                v_ref[hk, 0:NSA_DH, t * TQ:(t + 1) * TQ] = xT[hk * NSA_DH:(hk + 1) * NSA_DH, :]
        for hk in range(NSA_KV):
            v_ref[hk, NSA_DH:V_ROWS, :] = ones_rows


def _nsa_prep(proj, gains, cos, sin, n_grp, tm=512):
    L = proj.shape[0]
    W = NSA_PREP_W
    HK = NSA_KV
    GW = NSA_G * TQ
    bsum = jnp.asarray(np.kron(np.eye(2), np.ones((NSA_DH, NSA_DH))) / NSA_DH, F32)
    rows = pl.BlockSpec((HK, tm, LANES), lambda i: (0, i, 0))
    vals = pl.BlockSpec((HK, V_ROWS, tm), lambda i: (0, 0, i))
    return pl.pallas_call(
        functools.partial(_nsa_prep_kernel, n_grp=n_grp),
        grid=(L // tm,),
        in_specs=[
            pl.BlockSpec((tm, W), lambda i: (i, 0)),
            pl.BlockSpec((1, W), lambda i: (0, 0)),
            pl.BlockSpec((tm, LANES), lambda i: (i, 0)),
            pl.BlockSpec((tm, LANES), lambda i: (i, 0)),
            pl.BlockSpec((LANES, LANES), lambda i: (0, 0)),
        ],
        out_specs=[pl.BlockSpec((HK, tm // TQ, NSA_DH, GW), lambda i: (0, i, 0, 0)), rows, rows, vals, vals],
        out_shape=[jax.ShapeDtypeStruct((HK, L // TQ, NSA_DH, GW), BF16),
                   jax.ShapeDtypeStruct((HK, L, LANES), BF16), jax.ShapeDtypeStruct((HK, L, LANES), BF16),
                   jax.ShapeDtypeStruct((HK, V_ROWS, L), BF16), jax.ShapeDtypeStruct((HK, V_ROWS, L), BF16)],
        compiler_params=_cparams(("arbitrary",)),
        name="nsa_prep",
    )(proj, gains, cos, sin, bsum)


def _cmp_kernel(xk_ref, xv_ref, pe_ref, w1k_ref, w2k_ref, w1v_ref, w2v_ref, gk_ref, cos_ref, sin_ref, bsum_ref,
                kc_ref, vcx_ref):
    nb = kc_ref.shape[1]
    half_w = CMP_STRIDE * LANES

    def mlp(x_ref, w1_ref, w2_ref):
        hb = jnp.concatenate([x_ref[pl.ds(l, nb, stride=CMP_STRIDE), :] for l in range(CMP_STRIDE)], axis=1)
        a = (hb + pe_ref[:, 0:half_w]).astype(BF16)
        b = (hb + pe_ref[:, half_w:2 * half_w]).astype(BF16)
        p1 = _dot(a, w1_ref[0:half_w, :])
        p2 = _dot(b, w1_ref[half_w:2 * half_w, :])
        h = p1 + pltpu.roll(p2, nb - 1, 0)
        return _dot(_gelu(h).astype(BF16), w2_ref[...])

    rid = lax.broadcasted_iota(jnp.int32, (nb, LANES), 0)
    real = rid < nb - 1
    kc = jnp.where(real, mlp(xk_ref, w1k_ref, w2k_ref), 0.0)
    vc = jnp.where(real, mlp(xv_ref, w1v_ref, w2v_ref), 0.0)
    ms = _dot(kc * kc, bsum_ref[...], precision=HIGHEST)
    kcn = _rope_slab(kc * lax.rsqrt(ms + EPS) * gk_ref[...], cos_ref[...], sin_ref[...])
    kc_ref[0] = kcn[:, 0:NSA_DH].astype(BF16)
    kc_ref[1] = pltpu.roll(kcn, NSA_DH, 1)[:, 0:NSA_DH].astype(BF16)
    ones_rows = jnp.where(lax.broadcasted_iota(jnp.int32, (V_ROWS - NSA_DH, nb), 0) == 0, 1.0, 0.0).astype(BF16)
    for t in range(nb // LANES):
        vT = vc[t * LANES:(t + 1) * LANES, :].T.astype(BF16)
        for hk in range(NSA_KV):
            vcx_ref[hk, 0:NSA_DH, t * LANES:(t + 1) * LANES] = vT[hk * NSA_DH:(hk + 1) * NSA_DH, :]
    for hk in range(NSA_KV):
        vcx_ref[hk, NSA_DH:V_ROWS, :] = ones_rows


def _nsa_compress(proj, pe, w1k, w2k, w1v, w2v, gk, cos, sin):
    L = proj.shape[0]
    nb = L // CMP_STRIDE
    bsum = jnp.asarray(np.kron(np.eye(2), np.ones((NSA_DH, NSA_DH))) / NSA_DH, F32)
    once = lambda shape, idx: pl.BlockSpec(shape, lambda i: idx, pipeline_mode=pl.Buffered(1))
    full = lambda a: once(a.shape, (0,) * a.ndim)
    return pl.pallas_call(
        _cmp_kernel,
        grid=(1,),
        in_specs=[once((L, LANES), (0, C_KC // LANES)), once((L, LANES), (0, C_VC // LANES)),
                  full(pe), full(w1k), full(w2k), full(w1v), full(w2v), full(gk), full(cos), full(sin), full(bsum)],
        out_specs=[pl.BlockSpec((NSA_KV, nb, NSA_DH), lambda i: (0, 0, 0)),
                   pl.BlockSpec((NSA_KV, V_ROWS, nb), lambda i: (0, 0, 0))],
        out_shape=[jax.ShapeDtypeStruct((NSA_KV, nb, NSA_DH), BF16),
                   jax.ShapeDtypeStruct((NSA_KV, V_ROWS, nb), BF16)],
        compiler_params=_cparams(("arbitrary",)),
        name="nsa_compress",
    )(proj, proj, pe, w1k, w2k, w1v, w2v, gk, cos, sin, bsum)


def _nsa_attn_kernel(qT_ref, kc_ref, vcx_ref, smap_ref, ksaug_ref, vsx_ref, kwp_ref, vwxp_ref, gT_ref,
                     o_ref, qaug_ref, qwin_ref, ss_ref, mt_ref, m_ref, acc_ref, sc_ref, imp_ref, *, n_grp):
    i = pl.program_id(0)
    s0 = i * TQ
    hks = range(qT_ref.shape[0])
    nb = kc_ref.shape[1]
    ns = qaug_ref.shape[1] * n_grp
    GW = NSA_G * TQ
    qT = [qT_ref[h, 0] for h in hks]
    tq = s0 + (lax.broadcasted_iota(jnp.int32, (1, GW), 1) & (TQ - 1))

    cch = min(CMP_CHUNK, nb)
    n_cch = (s0 + TQ - CMP_LEN) // (CMP_STRIDE * cch) + 1
    nthr = (tq - (CMP_LEN - 1)) // CMP_STRIDE

    def cmp_scores(c, mcs):
        r0c = pl.multiple_of(c * cch, cch)
        s = [_dot(kc_ref[h, pl.ds(r0c, cch), :], qT[h]) for h in hks]
        nrow = r0c + lax.broadcasted_iota(jnp.int32, (cch, GW), 0)
        out = []
        for h in hks:
            sh = jnp.where(nrow <= nthr, s[h], NEG_BIG)
            sc_ref[h, pl.ds(r0c, cch), :] = sh
            out.append(jnp.maximum(mcs[h], jnp.max(sh, axis=0, keepdims=True)))
        return tuple(out)

    mc = lax.fori_loop(0, n_cch, cmp_scores, tuple(jnp.full((1, GW), NEG_BIG, F32) for _ in hks))
    imp_ref[...] = jnp.zeros_like(imp_ref)
    ones16 = jnp.ones((16, cch), BF16)
    VX = vcx_ref.shape[1]

    sm = smap_ref[...]
    sm_rows = sm.shape[0]
    blocks_per_chunk = cch * CMP_STRIDE // SLC_BLOCK

    def cmp_accum(c, carry):
        r0c = pl.multiple_of(c * cch, cch)
        j0 = pl.multiple_of(c * blocks_per_chunk, blocks_per_chunk)
        parts = [_split2(jnp.exp2(sc_ref[h, pl.ds(r0c, cch), :] - mc[h])) for h in hks]
        for h in hks:
            imp_ref[h, pl.ds(j0, sm_rows), :] += _dot(sm, parts[h][0]) + _dot(sm, parts[h][1])
        return tuple((carry[h][0] + _dot(vcx_ref[h, :, pl.ds(r0c, cch)], parts[h][0]),
                      carry[h][1] + _dot(ones16, parts[h][1])) for h in hks)

    cacc = lax.fori_loop(0, n_cch, cmp_accum,
                         tuple((jnp.zeros((VX, GW), F32), jnp.zeros((16, GW), F32)) for _ in hks))
    inv_l = [jnp.where(tq >= CMP_LEN - 1, 1.0 / (cacc[h][0][NSA_DH:NSA_DH + 1] + cacc[h][1][0:1]), 0.0) for h in hks]
    ocT = [cacc[h][0][0:NSA_DH] * inv_l[h] for h in hks]

    r0 = pl.multiple_of(s0, TQ)
    rel = lax.broadcasted_iota(jnp.int32, (TQ, GW), 0)
    qrel = tq - s0
    flag_row = lax.broadcasted_iota(jnp.int32, (KW_LANES - NSA_DH, GW), 0) == 0
    for h in hks:
        qwin_ref[h, 0:NSA_DH, :] = qT[h]
        qwin_ref[h, NSA_DH:KW_LANES, :] = jnp.where(flag_row, NEG_BIG, 0.0).astype(BF16)
    qwin = [qwin_ref[h] for h in hks]
    edge = [_dot(kwp_ref[h, pl.ds(r0, TQ), :], qwin[h]) for h in hks]
    mid = [_dot(kwp_ref[h, pl.ds(r0 + TQ, WINDOW - TQ), :], qwin[h]) for h in hks]
    diag = [_dot(kwp_ref[h, pl.ds(r0 + WINDOW, TQ), :], qwin[h]) for h in hks]
    edge16 = [jnp.where(rel > qrel, edge[h], NEG_BIG).astype(BF16) for h in hks]
    diag16 = [jnp.where(rel <= qrel, diag[h], NEG_BIG).astype(BF16) for h in hks]
    mid16 = [mid[h].astype(BF16) for h in hks]
    mw = [jnp.maximum(jnp.maximum(jnp.max(edge16[h], axis=0, keepdims=True), jnp.max(mid16[h], axis=0, keepdims=True)),
                      jnp.max(diag16[h], axis=0, keepdims=True)) for h in hks]
    accw = [(_dot(vwxp_ref[h, :, pl.ds(r0, TQ)], jnp.exp2(edge16[h] - mw[h]))
             + _dot(vwxp_ref[h, :, pl.ds(r0 + TQ, WINDOW - TQ)], jnp.exp2(mid16[h] - mw[h]))
             + _dot(vwxp_ref[h, :, pl.ds(r0 + WINDOW, TQ)], jnp.exp2(diag16[h] - mw[h]))) for h in hks]
    owT = [accw[h][0:NSA_DH] / accw[h][NSA_DH:NSA_DH + 1] for h in hks]
    sd = [_dot(ksaug_ref[h, pl.ds(r0, TQ), 0:NSA_DH], qT[h]) for h in hks]
    sd16 = [jnp.where(rel <= qrel, sd[h], NEG_BIG).astype(BF16) for h in hks]

    imp = []
    for h in hks:
        tot = imp_ref[h, 0:ns, 0:TQ] * inv_l[h][:, 0:TQ]
        for g in range(1, NSA_G):
            tot = tot + imp_ref[h, 0:ns, g * TQ:(g + 1) * TQ] * inv_l[h][:, g * TQ:(g + 1) * TQ]
        imp.append(tot)

    jrow = lax.broadcasted_iota(jnp.int32, (ns, TQ), 0)
    cur = (s0 + lax.broadcasted_iota(jnp.int32, (1, TQ), 1)) // SLC_BLOCK
    forced = jnp.where(jrow == 0, 0.0, jnp.where(jrow == cur, 0.0, jnp.where(jrow == cur - 1, 0.0, NEG_BIG)))
    val = [jnp.where(jrow >= 1, jnp.where(jrow <= cur - 2, imp[h], -jnp.inf), -jnp.inf) for h in hks]
    for _ in range(min(SLC_TOPK, ns) - 3):
        mx = [jnp.max(val[h], axis=0, keepdims=True) for h in hks]
        idx = [jnp.min(jnp.where(val[h] == mx[h], jrow, ns), axis=0, keepdims=True) for h in hks]
        val = [jnp.where(jrow == idx[h], -jnp.inf, val[h]) for h in hks]
    bias = [jnp.where(jrow >= 1, jnp.where(jrow <= cur - 2, jnp.where(val[h] == -jnp.inf, 0.0, NEG_BIG), forced), forced)
            for h in hks]
    own = jrow // 2 == i
    for h in hks:
        bias16 = jnp.where(own, NEG_BIG, bias[h]).astype(BF16)
        for grp in range(ns // n_grp):
            qaug_ref[h, grp, 0:NSA_DH, :] = qT[h]
            for g in range(NSA_G):
                qaug_ref[h, grp, NSA_DH:NSA_DH + n_grp, g * TQ:(g + 1) * TQ] = bias16[grp * n_grp:(grp + 1) * n_grp, :]

    tiles_per_grp = (n_grp * SLC_BLOCK) // TK
    n_body = (s0 + SEL_UNROLL * TK - 1) // (SEL_UNROLL * TK)

    def scores(kt, slot):
        k0 = pl.multiple_of(kt * TK, TK)
        grp = kt // tiles_per_grp
        s = [_dot(ksaug_ref[h, pl.ds(k0, TK), :], qaug_ref[h, grp]) for h in hks]
        for h in hks:
            sb = s[h].astype(BF16)
            ss_ref[h, slot] = sb
            mt_ref[h, slot] = jnp.max(sb, axis=0, keepdims=True).astype(F32)

    def online_update(sb, mt, v):
        m = [m_ref[h] for h in hks]
        mn = [jnp.maximum(m[h], mt[h]) for h in hks]
        pv = [_dot(v[h], jnp.exp2(sb[h] - mn[h].astype(BF16))) for h in hks]
        for h in hks:
            acc_ref[h] = jnp.exp2(m[h] - mn[h]) * acc_ref[h] + pv[h]
            m_ref[h] = mn[h]

    def accumulate(kt, slot):
        k0 = pl.multiple_of(kt * TK, TK)
        online_update([ss_ref[h, slot] for h in hks], [mt_ref[h, slot] for h in hks],
                      [vsx_ref[h, :, pl.ds(k0, TK)] for h in hks])

    m_ref[...] = jnp.full(m_ref.shape, NEG_BIG, F32)
    acc_ref[...] = jnp.zeros_like(acc_ref)

    @pl.when(n_body > 0)
    def _():
        scores(0, 0)

    def tile_group(base, last):
        for u in range(SEL_UNROLL):
            if not (last and u == SEL_UNROLL - 1):
                scores(base + u + 1, (u + 1) % 2)
            accumulate(base + u, u % 2)

    def loop_body(j, carry):
        tile_group(SEL_UNROLL * j, last=False)
        return carry

    lax.fori_loop(0, n_body - 1, loop_body, 0)

    @pl.when(n_body > 0)
    def _():
        tile_group(SEL_UNROLL * (n_body - 1), last=True)

    online_update(sd16, [jnp.max(sd16[h], axis=0, keepdims=True).astype(F32) for h in hks],
                  [vsx_ref[h, :, pl.ds(r0, TQ)] for h in hks])
    osT = [acc_ref[h, 0:NSA_DH, :] / acc_ref[h, NSA_DH:NSA_DH + 1, :] for h in hks]

    for h in hks:
        gates = _sigmoid(gT_ref[h])
        outs = []
        for g in range(NSA_G):
            sl = slice(g * TQ, (g + 1) * TQ)
            outs.append(gates[3 * g:3 * g + 1] * ocT[h][:, sl] + gates[3 * g + 1:3 * g + 2] * osT[h][:, sl]
                        + gates[3 * g + 2:3 * g + 3] * owT[h][:, sl])
        for g in range(0, NSA_G, 2):
            c0 = (h * NSA_G + g) * NSA_DH
            o_ref[:, c0:c0 + 2 * NSA_DH] = jnp.concatenate([outs[g], outs[g + 1]], axis=0).T


def _nsa_attention(qT, kc, vcx, smapT, ksaug, vsx, kwp, vwxp, gT, n_grp):
    HK, nQ, _, GW = qT.shape
    nb = kc.shape[1]
    L = vsx.shape[2]
    ns = L // SLC_BLOCK
    KA = ksaug.shape[2]
    scratch = [pltpu.VMEM((HK, ns // n_grp, KA, GW), BF16),
               pltpu.VMEM((HK, KW_LANES, GW), BF16),
               pltpu.VMEM((HK, 2, TK, GW), BF16),
               pltpu.VMEM((HK, 2, 1, GW), F32),
               pltpu.VMEM((HK, 1, GW), F32),
               pltpu.VMEM((HK, V_ROWS, GW), F32),
               pltpu.VMEM((HK, nb, GW), F32),
               pltpu.VMEM((HK, ns + 8, GW), F32)]
    resident = lambda a: pl.BlockSpec(a.shape, lambda i: (0,) * a.ndim, pipeline_mode=pl.Buffered(1))
    return pl.pallas_call(
        functools.partial(_nsa_attn_kernel, n_grp=n_grp),
        grid=(nQ,),
        in_specs=[
            pl.BlockSpec((HK, 1, NSA_DH, GW), lambda i: (0, i, 0, 0)),
            resident(kc), resident(vcx), resident(smapT), resident(ksaug), resident(vsx), resident(kwp),
            resident(vwxp),
            pl.BlockSpec((HK, 16, TQ), lambda i: (0, 0, i)),
        ],
        out_specs=pl.BlockSpec((TQ, NSA_HEADS * NSA_DH), lambda i: (i, 0)),
        out_shape=jax.ShapeDtypeStruct((nQ * TQ, NSA_HEADS * NSA_DH), F32),
        scratch_shapes=scratch,
        compiler_params=_cparams(("arbitrary",)),
        name="nsa_attn",
    )(qT, kc, vcx, smapT, ksaug, vsx, kwp, vwxp, gT)


def _s5_kernel(u_ref, lbr_ref, lbi_ref, ctr_ref, cti_ref, btr_ref, bti_ref, bsr_ref, bsi_ref, y_ref, lm_ref):
    T = S5_CHUNK
    W = T * S5_CH
    ar = lbr_ref[0]
    ai = lbi_ref[0]
    delta = lax.broadcasted_iota(jnp.int32, (W, S5_STATE), 0) // S5_CH

    def powers(e):
        pr = jnp.ones((W, S5_STATE), F32)
        pi = jnp.zeros((W, S5_STATE), F32)
        fr, fi = ar, ai
        for b in range(T.bit_length() - 1):
            bit = ((e >> b) & 1) == 1
            nr = pr * fr - pi * fi
            ni = pr * fi + pi * fr
            pr = jnp.where(bit, nr, pr)
            pi = jnp.where(bit, ni, pi)
            fr, fi = fr * fr - fi * fi, 2.0 * fr * fi
        return pr, pi, fr, fi

    pwr, pwi, aTr, aTi = powers(delta)
    rvr, rvi, _, _ = powers(T - 1 - delta)
    ctr, cti = ctr_ref[0], cti_ref[0]
    car = ctr * pwr - cti * pwi
    cai = ctr * pwi + cti * pwr

    kw = _dot_nt(bsr_ref[0], car, HIGHEST) - _dot_nt(bsi_ref[0], cai, HIGHEST)
    lane = lax.broadcasted_iota(jnp.int32, (S5_CH, W), 1)
    for tau in range(T):
        sh = S5_CH * tau
        blk = kw if tau == 0 else jnp.where(lane >= sh, pltpu.roll(kw, sh, 1), 0.0)
        lm_ref[sh:sh + S5_CH, :] = blk.astype(BF16)

    u = u_ref[0]
    nc = u.shape[0]
    y = _dot(u, lm_ref[...])

    btr, bti = btr_ref[0], bti_ref[0]
    sr = _dot(u, (rvr * btr - rvi * bti).astype(BF16))
    si = _dot(u, (rvr * bti + rvi * btr).astype(BF16))
    rowc = lax.broadcasted_iota(jnp.int32, (nc, S5_STATE), 0)
    fr, fi = aTr, aTi
    step = 1
    while step < nc:
        shr = jnp.where(rowc >= step, pltpu.roll(sr, step, 0), 0.0)
        shi = jnp.where(rowc >= step, pltpu.roll(si, step, 0), 0.0)
        sr, si = sr + fr * shr - fi * shi, si + fr * shi + fi * shr
        fr, fi = fr * fr - fi * fi, 2.0 * fr * fi
        step *= 2
    xr = jnp.where(rowc >= 1, pltpu.roll(sr, 1, 0), 0.0)
    xi = jnp.where(rowc >= 1, pltpu.roll(si, 1, 0), 0.0)
    c1r = car * ar - cai * ai
    c1i = car * ai + cai * ar
    y = y + _dot_nt(xr.astype(BF16), c1r.astype(BF16)) - _dot_nt(xi.astype(BF16), c1i.astype(BF16))
    y_ref[0] = y


def _s5_scan(ug, lbr, lbi, ctr, cti, btr, bti, bsr, bsi):
    G, nc, W = ug.shape
    grp = lambda shape: pl.BlockSpec((1,) + shape, lambda g: (g, 0, 0))
    return pl.pallas_call(
        _s5_kernel,
        grid=(G,),
        in_specs=[grp((nc, W)), grp((1, S5_STATE)), grp((1, S5_STATE)),
                  grp((W, S5_STATE)), grp((W, S5_STATE)), grp((W, S5_STATE)), grp((W, S5_STATE)),
                  grp((S5_CH, S5_STATE)), grp((S5_CH, S5_STATE))],
        out_specs=grp((nc, W)),
        out_shape=jax.ShapeDtypeStruct((G, nc, W), F32),
        scratch_shapes=[pltpu.VMEM((W, W), BF16)],
        compiler_params=_cparams(("arbitrary",)),
        name="s5_scan",
    )(ug, lbr, lbi, ctr, cti, btr, bti, bsr, bsi)


def _small_kernel(x_ref, p_ref, o_ref):
    x = x_ref[...]
    lane = lax.broadcasted_iota(jnp.int32, x.shape, 1)
    z = x + p_ref[0:1, :]
    softplus = jnp.maximum(z, 0.0) + jnp.log(1.0 + jnp.exp(-jnp.abs(z)))
    gdec = p_ref[1:2, :] * softplus
    beta = _sigmoid(x)
    o_ref[...] = jnp.where(lane < SM_G, x, jnp.where(lane < SM_BETA, gdec, beta))


def _small(proj, params, tm=1024):
    L = proj.shape[0]
    cb = C_SM // LANES
    return pl.pallas_call(
        _small_kernel,
        grid=(L // tm,),
        in_specs=[pl.BlockSpec((tm, LANES), lambda i: (i, cb)), pl.BlockSpec((8, LANES), lambda i: (0, 0))],
        out_specs=pl.BlockSpec((tm, LANES), lambda i: (i, 0)),
        out_shape=jax.ShapeDtypeStruct((L, LANES), F32),
        compiler_params=_cparams(("arbitrary",)),
        name="small_cols",
    )(proj, params)


GDN_CB = 8


def _split2(x):
    hi = x.astype(BF16)
    return hi, (x - hi.astype(F32)).astype(BF16)


def _dot_split(a, b):
    ah, al = _split2(a)
    bh, bl = _split2(b)
    return _dot(ah, bh) + (_dot(ah, bl) + _dot(al, bh))


def _gdn_local_kernel(xq_ref, xk_ref, xv_ref, cwq_ref, cwk_ref, cwv_ref, sm_ref,
                      val_ref, kcum_ref, qg_ref, kdT_ref, qk_ref, gl_ref, carry_ref, hs_ref):
    C = GDN_CHUNK
    R = xq_ref.shape[0]

    @pl.when(pl.program_id(1) == 0)
    def _():
        carry_ref[...] = jnp.zeros_like(carry_ref)

    def conv_silu(x_ref, cw_ref, slot):
        x = x_ref[...]
        hs_ref[slot, 0:8, :] = carry_ref[slot]
        hs_ref[slot, 8:R + 8, :] = x
        carry_ref[slot] = x[R - 8:R, :]
        taps = cw_ref.shape[0]
        y = cw_ref[taps - 1:taps, :] * x
        for back in range(1, taps):
            y = y + cw_ref[taps - 1 - back:taps - back, :] * hs_ref[slot, 8 - back:8 - back + R, :]
        half_y = 0.5 * y
        return half_y * (1.0 + jnp.tanh(half_y))

    def l2norm(y):
        return y * lax.rsqrt(jnp.sum(y * y, axis=-1, keepdims=True) + EPS)

    q_all = l2norm(conv_silu(xq_ref, cwq_ref, 0)) * (GDN_DH ** -0.5)
    k_all = l2norm(conv_silu(xk_ref, cwk_ref, 1))
    v_all = conv_silu(xv_ref, cwv_ref, 2)
    chunks = range(GDN_CB)
    rows = [slice(c * C, (c + 1) * C) for c in chunks]
    ii = lax.broadcasted_iota(jnp.int32, (C, C), 0)
    jj = lax.broadcasted_iota(jnp.int32, (C, C), 1)
    causal = ii >= jj
    strict = ii > jj
    eye = (ii == jj).astype(F32)
    tril16 = causal.astype(BF16)
    row2 = lax.broadcasted_iota(jnp.int32, (C, 2 * C), 0)
    col2 = lax.broadcasted_iota(jnp.int32, (C, 2 * C), 1)
    keep = row2 > jnp.where(col2 < C, -1, col2 - C)

    head = pl.program_id(0)
    sm = sm_ref[...]
    sm_lane = lax.broadcasted_iota(jnp.int32, sm.shape, 1)
    g_col = jnp.sum(jnp.where(sm_lane == SM_G + head, sm, 0.0), axis=1, keepdims=True)
    b_col = jnp.sum(jnp.where(sm_lane == SM_BETA + head, sm, 0.0), axis=1, keepdims=True)

    k = [k_all[rs, :] for rs in rows]
    beta = [b_col[rs, :] for rs in rows]
    res = []
    for rs in rows:
        x = jnp.where(keep, jnp.broadcast_to(g_col[rs, :], (C, 2 * C)), 0.0)
        hi = x.astype(BF16)
        r1 = x - hi.astype(F32)
        mid = r1.astype(BF16)
        lo = (r1 - mid.astype(F32)).astype(BF16)
        res.append(_dot(tril16, hi) + (_dot(tril16, mid) + _dot(tril16, lo)))
    gc = [jnp.broadcast_to(r[:, 0:1], (C, GDN_DH)) for r in res]
    decay = [jnp.where(causal, jnp.exp(r[:, C:2 * C]), 0.0) for r in res]
    kb = [kc * bc for kc, bc in zip(k, beta)]
    k16 = [kc.astype(BF16) for kc in k]
    a = [jnp.where(strict, _dot_nt(kbc.astype(BF16), kc16) * dc, 0.0) for kbc, kc16, dc in zip(kb, k16, decay)]
    t = [eye - ac for ac in a]
    p = a
    for _ in range(int(math.log2(C)) - 1):
        p = [_dot_split(pc, pc) for pc in p]
        t = [tc + _dot_split(tc, pc) for tc, pc in zip(t, p)]
    t16 = [tc.astype(BF16) for tc in t]
    egc = [jnp.exp(gcc) for gcc in gc]
    val = [_dot(tc, (v_all[rs, :] * bc).astype(BF16)) for tc, rs, bc in zip(t16, rows, beta)]
    kcum = [_dot(tc, (kbc * ec).astype(BF16)) for tc, kbc, ec in zip(t16, kb, egc)]
    qk = [_dot_nt(q_all[rs, :].astype(BF16), kc16) * dc for rs, kc16, dc in zip(rows, k16, decay)]
    kd = []
    for c in chunks:
        rs = rows[c]
        val_ref[rs, :] = val[c]
        kcum_ref[rs, :] = kcum[c].astype(BF16)
        qk_ref[0, rs, :] = qk[c].astype(BF16)
        glast = gc[c][C - 1:C, :]
        qg_ref[rs, :] = (q_all[rs, :] * egc[c]).astype(BF16)
        kd.append(k[c] * jnp.exp(glast - gc[c]))
        gl_ref[0, c:c + 1, :] = jnp.exp(glast)
    for c in range(0, GDN_CB, 2):
        kdT_ref[0, c // 2] = jnp.concatenate([kd[c], kd[c + 1]], axis=0).T.astype(BF16)


def _gdn_local(proj, conv_w, small):
    L = proj.shape[0]
    H = GDN_HEADS
    R = GDN_CB * GDN_CHUNK
    cb = C_QKV // GDN_DH
    taps = conv_w.shape[0]
    col = lambda off: pl.BlockSpec((R, GDN_DH), lambda h, i: (i, cb + off + h))
    cw = lambda off: pl.BlockSpec((taps, GDN_DH), lambda h, i: (0, off + h))
    big = pl.BlockSpec((R, GDN_DH), lambda h, i: (i, h))
    return pl.pallas_call(
        _gdn_local_kernel,
        grid=(H, L // R),
        in_specs=[col(0), col(H), col(2 * H), cw(0), cw(H), cw(2 * H),
                  pl.BlockSpec((R, LANES), lambda h, i: (i, 0))],
        out_specs=[big, big, big,
                   pl.BlockSpec((1, GDN_CB // 2, GDN_DH, 2 * GDN_CHUNK), lambda h, i: (h, i, 0, 0)),
                   pl.BlockSpec((1, R, GDN_CHUNK), lambda h, i: (h, i, 0)),
                   pl.BlockSpec((1, GDN_CB, GDN_DH), lambda h, i: (h, i, 0))],
        out_shape=[jax.ShapeDtypeStruct((L, GDN_WIDTH), F32),
                   jax.ShapeDtypeStruct((L, GDN_WIDTH), BF16),
                   jax.ShapeDtypeStruct((L, GDN_WIDTH), BF16),
                   jax.ShapeDtypeStruct((H, L // (2 * GDN_CHUNK), GDN_DH, 2 * GDN_CHUNK), BF16),
                   jax.ShapeDtypeStruct((H, L, GDN_CHUNK), BF16),
                   jax.ShapeDtypeStruct((H, L // GDN_CHUNK, GDN_DH), F32)],
        scratch_shapes=[pltpu.VMEM((3, 8, GDN_DH), F32), pltpu.VMEM((3, R + 8, GDN_DH), F32)],
        compiler_params=_cparams(("arbitrary", "arbitrary")),
        name="gdn_local",
    )(proj, proj, proj, conv_w, conv_w, conv_w, small)


def _gdn_scan_kernel(val_ref, kcum_ref, qg_ref, qk_ref, kdT_ref, gl_ref, z_ref, gn_ref, o_ref, st_ref):
    @pl.when(pl.program_id(0) == 0)
    def _():
        st_ref[...] = jnp.zeros_like(st_ref)

    C = GDN_CHUNK
    gn = gn_ref[...]
    heads = range(GDN_HEADS)
    cols = [slice(h * GDN_DH, (h + 1) * GDN_DH) for h in heads]
    state = [st_ref[h] for h in heads]
    for c in range(GDN_CB):
        rs = slice(c * C, (c + 1) * C)
        sb = [s.astype(BF16) for s in state]
        kcs = [_dot(kcum_ref[rs, cs], s) for cs, s in zip(cols, sb)]
        qgs = [_dot(qg_ref[rs, cs], s) for cs, s in zip(cols, sb)]
        vb = [(val_ref[rs, cs] - x).astype(BF16) for cs, x in zip(cols, kcs)]
        o = [x + _dot(qk_ref[h, rs, :], v) for h, x, v in zip(heads, qgs, vb)]
        kcols = slice((c % 2) * C, (c % 2 + 1) * C)
        state = [s * gl_ref[h, c:c + 1, :] + _dot(kdT_ref[h, c // 2, :, kcols], v)
                 for h, s, v in zip(heads, state, vb)]
        for h in heads:
            on = o[h] * lax.rsqrt(jnp.mean(o[h] * o[h], axis=-1, keepdims=True) + EPS) * gn
            z = z_ref[rs, cols[h]]
            o_ref[rs, cols[h]] = on * (z * _sigmoid(z))
    for h in heads:
        st_ref[h] = state[h]


def _gdn_scan(val, kcum, qg, qk, kdT, gl, proj, gnorm):
    L = val.shape[0]
    H = GDN_HEADS
    R = GDN_CB * GDN_CHUNK
    row = pl.BlockSpec((R, GDN_WIDTH), lambda i: (i, 0))
    zb = C_Z // GDN_WIDTH
    return pl.pallas_call(
        _gdn_scan_kernel,
        grid=(L // R,),
        in_specs=[row, row, row,
                  pl.BlockSpec((H, R, GDN_CHUNK), lambda i: (0, i, 0)),
                  pl.BlockSpec((H, GDN_CB // 2, GDN_DH, 2 * GDN_CHUNK), lambda i: (0, i, 0, 0)),
                  pl.BlockSpec((H, GDN_CB, GDN_DH), lambda i: (0, i, 0)),
                  pl.BlockSpec((R, GDN_WIDTH), lambda i: (i, zb)),
                  pl.BlockSpec((1, GDN_DH), lambda i: (0, 0))],
        out_specs=row,
        out_shape=jax.ShapeDtypeStruct((L, GDN_WIDTH), F32),
        scratch_shapes=[pltpu.VMEM((H, GDN_DH, GDN_DH), F32)],
        compiler_params=_cparams(("arbitrary",)),
        name="gdn_scan",
    )(val, kcum, qg, qk, kdT, gl, proj, gnorm)


def _mix_kernel(ya_ref, ys_ref, u_ref, yc_ref, x_ref, ga_ref, gb_ref, d_ref, wglu_ref, wout_ref, o_ref):
    def rms(y, g):
        return y * lax.rsqrt(jnp.mean(y * y, axis=-1, keepdims=True) + EPS) * g

    a = rms(ya_ref[...], ga_ref[...]).astype(BF16)
    yb = _gelu(ys_ref[...] + d_ref[...] * u_ref[...])
    yb = yb * _sigmoid(_dot(yb.astype(BF16), wglu_ref[...]))
    b = rms(yb, gb_ref[...]).astype(BF16)
    na, nb = a.shape[1], b.shape[1]
    acc = x_ref[...] + _dot(a, wout_ref[0:na, :])
    acc = acc + _dot(b, wout_ref[na:na + nb, :])
    o_ref[...] = acc + _dot(yc_ref[...].astype(BF16), wout_ref[na + nb:, :])


def _mix(ya, ys, proj, yc, x, ga, gb, d, wglu, wout, tm=256):
    L, D = x.shape
    ub = C_U // S5_WIDTH
    full = lambda a: pl.BlockSpec(a.shape, lambda i: (0, 0))
    return pl.pallas_call(
        _mix_kernel,
        grid=(L // tm,),
        in_specs=[pl.BlockSpec((tm, ya.shape[1]), lambda i: (i, 0)),
                  pl.BlockSpec((tm, S5_WIDTH), lambda i: (i, 0)),
                  pl.BlockSpec((tm, S5_WIDTH), lambda i: (i, ub)),
                  pl.BlockSpec((tm, GDN_WIDTH), lambda i: (i, 0)),
                  pl.BlockSpec((tm, D), lambda i: (i, 0)),
                  full(ga), full(gb), full(d), full(wglu), full(wout)],
        out_specs=pl.BlockSpec((tm, D), lambda i: (i, 0)),
        out_shape=jax.ShapeDtypeStruct((L, D), F32),
        compiler_params=_cparams(("arbitrary",)),
        name="mix_out",
    )(ya, ys, proj, yc, x, ga, gb, d, wglu, wout)


def _ffn_in_kernel(x_ref, g_ref, wg_ref, wu_ref, cg_ref, cu_ref, bg_ref, bu_ref, o_ref, xn_ref, carry_ref, hs_ref):
    i = pl.program_id(0)
    j = pl.program_id(1)

    @pl.when(j == 0)
    def _():
        xf = x_ref[...]
        ms = jnp.mean(xf * xf, axis=-1, keepdims=True)
        xn_ref[...] = (xf * lax.rsqrt(ms + EPS) * g_ref[...]).astype(BF16)

    @pl.when(i == 0)
    def _():
        carry_ref[j] = jnp.zeros(carry_ref.shape[1:], F32)

    xn = xn_ref[...]
    tm = xn.shape[0]

    def conv(h, slot, cw_ref, cb_ref):
        hs_ref[slot, 0:8, :] = carry_ref[j, slot]
        hs_ref[slot, 8:tm + 8, :] = h
        carry_ref[j, slot] = h[tm - 8:tm, :]
        taps = cw_ref.shape[0]
        y = cw_ref[taps - 1:taps, :] * h + cb_ref[...]
        for back in range(1, taps):
            y = y + cw_ref[taps - 1 - back:taps - back, :] * hs_ref[slot, 8 - back:8 - back + tm, :]
        return y

    gate = conv(_dot(xn, wg_ref[...]), 0, cg_ref, bg_ref)
    up = conv(_dot(xn, wu_ref[...]), 1, cu_ref, bu_ref)
    half_gate = 0.5 * gate
    o_ref[...] = (half_gate * (1.0 + jnp.tanh(half_gate)) * up).astype(BF16)


def _ffn_in(x, gain, w, cw, cb, tm=1024, tn=512):
    L, D = x.shape
    N = w.shape[1] // 2
    nj = N // tn
    taps = cw.shape[0]
    halves = lambda rows: (pl.BlockSpec((rows, tn), lambda i, j: (0, j)), pl.BlockSpec((rows, tn), lambda i, j: (0, nj + j)))
    wg, wu = halves(D)
    cg, cu = halves(taps)
    bg, bu = halves(1)
    return pl.pallas_call(
        _ffn_in_kernel,
        grid=(L // tm, nj),
        in_specs=[pl.BlockSpec((tm, D), lambda i, j: (i, 0)),
                  pl.BlockSpec((1, D), lambda i, j: (0, 0)),
                  wg, wu, cg, cu, bg, bu],
        out_specs=pl.BlockSpec((tm, tn), lambda i, j: (i, j)),
        out_shape=jax.ShapeDtypeStruct((L, N), BF16),
        scratch_shapes=[pltpu.VMEM((tm, D), BF16), pltpu.VMEM((nj, 2, 8, tn), F32),
                        pltpu.VMEM((2, tm + 8, tn), F32)],
        compiler_params=_cparams(("arbitrary", "arbitrary")),
        name="ffn_in",
    )(x, gain.reshape(1, D), w, w, cw, cw, cb, cb)


def _ffn_out_kernel(a_ref, w_ref, x_ref, o_ref):
    @pl.when(pl.program_id(1) == 0)
    def _():
        o_ref[...] = x_ref[...]

    o_ref[...] += _dot(a_ref[...], w_ref[...])


def _ffn_out(act, w, x, tm=1024, tk=512):
    L, D = x.shape
    K = act.shape[1]
    return pl.pallas_call(
        _ffn_out_kernel,
        grid=(L // tm, K // tk),
        in_specs=[pl.BlockSpec((tm, tk), lambda i, k: (i, k)),
                  pl.BlockSpec((tk, D), lambda i, k: (k, 0)),
                  pl.BlockSpec((tm, D), lambda i, k: (i, 0))],
        out_specs=pl.BlockSpec((tm, D), lambda i, k: (i, 0)),
        out_shape=jax.ShapeDtypeStruct((L, D), F32),
        compiler_params=_cparams(("arbitrary", "arbitrary")),
        name="ffn_out",
    )(act, w, x)


def _rope_tables(pos):
    half = ROPE_DIMS // 2
    inv = ROPE_THETA ** (-jnp.arange(half, dtype=F32) / half)
    ang = pos.astype(F32)[:, None] * inv[None, :]
    n = pos.shape[0]
    cos = jnp.concatenate([jnp.cos(ang), jnp.cos(ang), jnp.ones((n, NSA_DH - ROPE_DIMS), F32)], axis=1)
    sin = jnp.concatenate([-jnp.sin(ang), jnp.sin(ang), jnp.zeros((n, NSA_DH - ROPE_DIMS), F32)], axis=1)
    return jnp.tile(cos, (1, 2)), jnp.tile(sin, (1, 2))


def _permute_w_in(w):
    sp = np.cumsum([0, 512, 128, 128, 128, 128, 128, 128, 24, 512, 3072, 1024, 8, 8])
    q, kc, vc, ks, vs, kw, vw, gates, u, qkv, z, a, b = [w[:, sp[n]:sp[n + 1]] for n in range(13)]
    pad = jnp.zeros((w.shape[0], PROJ_COLS - C_SM - 40), w.dtype)
    return jnp.concatenate([q, ks, kw, vs, vw, z, qkv, u, kc, vc, gates, a, b, pad], axis=1)


def _nsa_mixer(proj, small, p, l):
    L = proj.shape[0]
    HK, G, DH = NSA_KV, NSA_G, NSA_DH
    nQ = L // TQ
    ns = L // SLC_BLOCK
    nb = L // CMP_STRIDE
    n_grp = LANES - DH
    assert ns % n_grp == 0 and (n_grp * SLC_BLOCK) % (SEL_UNROLL * TK) == 0 and SEL_UNROLL % 2 == 0
    cos, sin = _rope_tables(jnp.arange(L))
    gains = jnp.concatenate([jnp.tile(p['nsa_q_norm'][l], NSA_HEADS), jnp.tile(p['nsa_ks_norm'][l], HK),
                             jnp.tile(p['nsa_kw_norm'][l], HK), jnp.ones((2 * HK * DH,), F32)]).reshape(1, -1)
    qT, ksaug, kwa, vsx, vwx = _nsa_prep(proj, gains, cos, sin, n_grp)
    kpad = jnp.zeros((HK, WINDOW, KW_LANES), BF16).at[:, :, DH].set(1.0)
    kwp = jnp.concatenate([kpad, kwa], axis=1)
    vwxp = jnp.pad(vwx, ((0, 0), (0, 0), (WINDOW, 0)))

    eye = jnp.eye(HK, dtype=F32)
    pe = jnp.tile(p['cmp_pe'][l][:, None, :], (1, HK, 1)).reshape(1, CMP_LEN * HK * DH)
    w1 = lambda name: jnp.einsum('ldf,hg->lhdgf', p[name][l], eye).reshape(CMP_LEN * HK * DH, -1).astype(BF16)
    w2 = lambda name: jnp.einsum('fd,hg->hfgd', p[name][l], eye).reshape(-1, HK * DH).astype(BF16)
    ccos, csin = _rope_tables(jnp.arange(nb) * CMP_STRIDE + CMP_LEN // 2)
    gk = jnp.tile(p['nsa_kc_norm'][l], HK).reshape(1, LANES)
    kc, vcx = _nsa_compress(proj, pe, w1('cmp_k_w1'), w2('cmp_k_w2'), w1('cmp_v_w1'), w2('cmp_v_w2'), gk, ccos, csin)

    cch = min(CMP_CHUNK, nb)
    cmp_start = np.arange(cch) * CMP_STRIDE
    slc_start = np.arange(cch * CMP_STRIDE // SLC_BLOCK + 8) * SLC_BLOCK
    smapT = jnp.asarray((cmp_start[None, :] <= slc_start[:, None] + SLC_BLOCK - 1)
                        & (cmp_start[None, :] + CMP_LEN - 1 >= slc_start[:, None]), BF16)

    gT = jnp.pad(small[:, :HK * G * 3].reshape(L, HK, G * 3).transpose(1, 2, 0), ((0, 0), (0, 16 - G * 3), (0, 0)))
    return _nsa_attention(qT, kc, vcx, smapT, ksaug, vsx, kwp, vwxp, gT, n_grp)


def _s5_mixer(proj, p, l):
    L = proj.shape[0]
    T = S5_CHUNK
    nc = L // T
    G, H, P = S5_GROUPS, S5_CH, S5_STATE
    lam_re, lam_im = p['s5_lam_re'][l], p['s5_lam_im'][l]
    dt = jnp.exp(p['s5_log_dt'][l])[:, None]
    mag = jnp.exp(lam_re * dt)
    lb_re = mag * jnp.cos(lam_im * dt)
    lb_im = mag * jnp.sin(lam_im * dt)
    den = lam_re * lam_re + lam_im * lam_im
    nr, ni = lb_re - 1.0, lb_im
    coef_re = (nr * lam_re + ni * lam_im) / den
    coef_im = (ni * lam_re - nr * lam_im) / den
    b_re, b_im = p['s5_b_re'][l], p['s5_b_im'][l]
    bb_re = coef_re[..., None] * b_re - coef_im[..., None] * b_im
    bb_im = coef_re[..., None] * b_im + coef_im[..., None] * b_re
    bsr, bsi = bb_re.transpose(0, 2, 1), bb_im.transpose(0, 2, 1)
    tile_rows = lambda a: jnp.tile(a, (1, T, 1))
    u = proj[:, C_U:C_U + S5_WIDTH]
    ug = u.astype(BF16).reshape(nc, T, G, H).transpose(2, 0, 1, 3).reshape(G, nc, T * H)
    y = _s5_scan(ug, lb_re[:, None, :], lb_im[:, None, :], tile_rows(p['s5_c_re'][l]), tile_rows(p['s5_c_im'][l]),
                 tile_rows(bsr), tile_rows(bsi), bsr, bsi)
    return y.reshape(G, nc, T, H).transpose(1, 2, 0, 3).reshape(L, S5_WIDTH)


def _gdn_mixer(proj, small, p, l):
    L = proj.shape[0]
    H = GDN_HEADS
    val, kcum, qg, kdT, qk, gl = _gdn_local(proj, p['gdn_conv'][l], small)
    return _gdn_scan(val, kcum, qg, qk, kdT, gl, proj, p['gdn_norm'][l].reshape(1, GDN_DH))


def _pad_halves(a):
    pad = jnp.zeros((a.shape[0], D_FF_PAD - D_FF), a.dtype)
    return jnp.concatenate([a[:, :D_FF], pad, a[:, D_FF:], pad], axis=1)


def _forward(x3, p):
    x = x3.reshape(x3.shape[1:])
    depth = p['w_in'].shape[0]
    for l in range(depth):
        w_in = _permute_w_in(p['w_in'][l].astype(BF16))
        proj = _rms_matmul(x, p['attn_norm'][l], w_in)
        sm_par = jnp.zeros((8, LANES), F32)
        sm_par = sm_par.at[0, SM_G:SM_BETA].set(p['gdn_dt_bias'][l]).at[1, SM_G:SM_BETA].set(-jnp.exp(p['gdn_a_log'][l]))
        small = _small(proj, sm_par)
        y_a = _nsa_mixer(proj, small, p, l)
        y_s = _s5_mixer(proj, p, l)
        y_c = _gdn_mixer(proj, small, p, l)
        x = _mix(y_a, y_s, proj, y_c, x, p['nsa_out_norm'][l].reshape(1, -1), p['s5_out_norm'][l].reshape(1, -1),
                 p['s5_d'][l].reshape(1, -1), p['s5_w_glu'][l].astype(BF16), p['w_out'][l].astype(BF16))
        act = _ffn_in(x, p['ffn_norm'][l], _pad_halves(p['ffn_w_in'][l].astype(BF16)), _pad_halves(p['ffn_conv'][l]),
                      _pad_halves(p['ffn_conv_b'][l].reshape(1, -1)))
        w_o = jnp.pad(p['ffn_w_out'][l].astype(BF16), ((0, D_FF_PAD - D_FF), (0, 0)))
        x = _ffn_out(act, w_o, x)
    return x.reshape(x3.shape)


_PARAM_NAMES = ('attn_norm', 'w_in', 'nsa_q_norm', 'nsa_kc_norm', 'nsa_ks_norm', 'nsa_kw_norm', 'cmp_pe',
                'cmp_k_w1', 'cmp_k_w2', 'cmp_v_w1', 'cmp_v_w2', 'nsa_out_norm', 's5_lam_re', 's5_lam_im',
                's5_log_dt', 's5_b_re', 's5_b_im', 's5_c_re', 's5_c_im', 's5_d', 's5_w_glu', 's5_out_norm',
                'gdn_conv', 'gdn_a_log', 'gdn_dt_bias', 'gdn_norm', 'w_out', 'ffn_norm', 'ffn_w_in', 'ffn_conv',
                'ffn_conv_b', 'ffn_w_out')


def kernel(x, attn_norm, w_in, nsa_q_norm, nsa_kc_norm, nsa_ks_norm, nsa_kw_norm, cmp_pe, cmp_k_w1, cmp_k_w2,
           cmp_v_w1, cmp_v_w2, nsa_out_norm, s5_lam_re, s5_lam_im, s5_log_dt, s5_b_re, s5_b_im, s5_c_re, s5_c_im,
           s5_d, s5_w_glu, s5_out_norm, gdn_conv, gdn_a_log, gdn_dt_bias, gdn_norm, w_out, ffn_norm, ffn_w_in,
           ffn_conv, ffn_conv_b, ffn_w_out):
    vals = (attn_norm, w_in, nsa_q_norm, nsa_kc_norm, nsa_ks_norm, nsa_kw_norm, cmp_pe, cmp_k_w1, cmp_k_w2,
            cmp_v_w1, cmp_v_w2, nsa_out_norm, s5_lam_re, s5_lam_im, s5_log_dt, s5_b_re, s5_b_im, s5_c_re, s5_c_im,
            s5_d, s5_w_glu, s5_out_norm, gdn_conv, gdn_a_log, gdn_dt_bias, gdn_norm, w_out, ffn_norm, ffn_w_in,
            ffn_conv, ffn_conv_b, ffn_w_out)
    return _forward(x, dict(zip(_PARAM_NAMES, vals)))
```

```python
import functools
import math

import jax
import jax.numpy as jnp
import numpy as np
from jax import lax
from jax.experimental import pallas as pl
from jax.experimental.pallas import tpu as pltpu

F32 = jnp.float32
BF16 = jnp.bfloat16
HIGHEST = lax.Precision.HIGHEST

NSA_HEADS = 8
NSA_KV = 2
NSA_G = NSA_HEADS // NSA_KV
NSA_DH = 64
CMP_LEN = 32
CMP_STRIDE = 16
SLC_BLOCK = 64
SLC_TOPK = 16
WINDOW = 512
ROPE_THETA = 500000.0
ROPE_DIMS = NSA_DH // 4
S5_WIDTH = 512
S5_CH = 16
S5_GROUPS = 32
S5_STATE = 64
S5_CHUNK = 32
GDN_HEADS = 8
GDN_DH = 128
GDN_WIDTH = 1024
GDN_CHUNK = 64
D_FF = 5504
D_FF_PAD = 5632
EPS = 1e-6
NEG_BIG = -(2.0 ** 100)
LOG2_E = math.log2(math.e)
V_ROWS = 80
KW_LANES = 128
CMP_CHUNK = 256
SEL_UNROLL = 2

LANES = 128
TQ = 128
TK = 512
VMEM_LIMIT = 56 * 1024 * 1024

C_Q, C_KS, C_KW, C_VS, C_VW = 0, 512, 640, 768, 896
C_Z, C_QKV, C_U, C_KC, C_VC, C_SM = 1024, 2048, 5120, 5632, 5760, 5888
NSA_PREP_W = C_Z
PROJ_COLS = 6144
SM_GATES, SM_G, SM_BETA, SM_END = 0, 24, 32, 40


def _cparams(sem):
    return pltpu.CompilerParams(dimension_semantics=sem, vmem_limit_bytes=VMEM_LIMIT)


def _dot(a, b, precision=None):
    return jnp.dot(a, b, preferred_element_type=F32, precision=precision)


def _dot_nt(a, b, precision=None):
    return lax.dot_general(a, b, (((1,), (1,)), ((), ())), preferred_element_type=F32, precision=precision)


def _gelu(x):
    return x * (0.5 * (1.0 + jnp.tanh(math.sqrt(2.0 / math.pi) * (x + 0.044715 * (x * x * x)))))


def _sigmoid(x):
    return 1.0 / (1.0 + jnp.exp(-x))


def _rms_matmul_kernel(x_ref, g_ref, w_ref, o_ref, xn_ref):
    @pl.when(pl.program_id(1) == 0)
    def _():
        xf = x_ref[...]
        ms = jnp.mean(xf * xf, axis=-1, keepdims=True)
        xn_ref[...] = (xf * lax.rsqrt(ms + EPS) * g_ref[...]).astype(BF16)

    o_ref[...] = _dot(xn_ref[...], w_ref[...])


def _rms_matmul(x, gain, w, tm=512, tn=2048):
    L, D = x.shape
    N = w.shape[1]
    return pl.pallas_call(
        _rms_matmul_kernel,
        grid=(L // tm, N // tn),
        in_specs=[
            pl.BlockSpec((tm, D), lambda i, j: (i, 0)),
            pl.BlockSpec((1, D), lambda i, j: (0, 0)),
            pl.BlockSpec((D, tn), lambda i, j: (0, j)),
        ],
        out_specs=pl.BlockSpec((tm, tn), lambda i, j: (i, j)),
        out_shape=jax.ShapeDtypeStruct((L, N), F32),
        scratch_shapes=[pltpu.VMEM((tm, D), BF16)],
        compiler_params=_cparams(("arbitrary", "arbitrary")),
        name="rms_proj",
    )(x, gain.reshape(1, D), w)


def _rope_slab(x, cos, sin):
    d = lax.broadcasted_iota(jnp.int32, x.shape, 1) & (NSA_DH - 1)
    half = ROPE_DIMS // 2
    partner = jnp.where(d < half, pltpu.roll(x, LANES - half, 1), pltpu.roll(x, half, 1))
    return x * cos + partner * sin


def _nsa_prep_kernel(x_ref, g_ref, cos_ref, sin_ref, bsum_ref, qT_ref, ksaug_ref, kwa_ref, vsx_ref, vwx_ref, *, n_grp):
    i = pl.program_id(0)
    tm = x_ref.shape[0]
    cos = cos_ref[...]
    sin = sin_ref[...]
    bsum = bsum_ref[...]
    lane = lax.broadcasted_iota(jnp.int32, (tm, LANES), 1)
    tiles = range(tm // TQ)

    def normed(s):
        sl = slice(s * LANES, (s + 1) * LANES)
        x = x_ref[:, sl]
        ms = _dot(x * x, bsum, precision=HIGHEST)
        return _rope_slab(x * lax.rsqrt(ms + EPS) * g_ref[:, sl], cos, sin)

    def per_head(y):
        return y, pltpu.roll(y, NSA_DH, 1)

    for s in range((C_KS - C_Q) // LANES):
        y = normed(s) * (NSA_DH ** -0.5 * LOG2_E)
        hk, g0 = divmod(s, NSA_G // 2)
        for t in tiles:
            yT = y[t * TQ:(t + 1) * TQ, :].T.astype(BF16)
            for half in range(2):
                g = 2 * g0 + half
                qT_ref[hk, t, :, g * TQ:(g + 1) * TQ] = yT[half * NSA_DH:(half + 1) * NSA_DH, :]

    blk = ((i * tm + lax.broadcasted_iota(jnp.int32, (tm, LANES), 0)) // SLC_BLOCK) % n_grp
    onehot = jnp.where(lane - NSA_DH == blk, 1.0, 0.0)
    for hk, y in enumerate(per_head(normed(C_KS // LANES))):
        ksaug_ref[hk] = jnp.where(lane < NSA_DH, y, onehot).astype(BF16)
    for hk, y in enumerate(per_head(normed(C_KW // LANES))):
        kwa_ref[hk] = jnp.where(lane < NSA_DH, y, 0.0).astype(BF16)

    ones_rows = jnp.where(lax.broadcasted_iota(jnp.int32, (V_ROWS - NSA_DH, tm), 0) == 0, 1.0, 0.0).astype(BF16)
    for c0, v_ref in ((C_VS, vsx_ref), (C_VW, vwx_ref)):
        x = x_ref[:, c0:c0 + LANES]
        for t in tiles:
            xT = x[t * TQ:(t + 1) * TQ, :].T.astype(BF16)
            for hk in range(NSA_KV):
                v_ref[hk, 0:NSA_DH, t * TQ:(t + 1) * TQ] = xT[hk * NSA_DH:(hk + 1) * NSA_DH, :]
        for hk in range(NSA_KV):
            v_ref[hk, NSA_DH:V_ROWS, :] = ones_rows


def _nsa_prep(proj, gains, cos, sin, n_grp, tm=512):
    L = proj.shape[0]
    W = NSA_PREP_W
    HK = NSA_KV
    GW = NSA_G * TQ
    bsum = jnp.asarray(np.kron(np.eye(2), np.ones((NSA_DH, NSA_DH))) / NSA_DH, F32)
    rows = pl.BlockSpec((HK, tm, LANES), lambda i: (0, i, 0))
    vals = pl.BlockSpec((HK, V_ROWS, tm), lambda i: (0, 0, i))
    return pl.pallas_call(
        functools.partial(_nsa_prep_kernel, n_grp=n_grp),
        grid=(L // tm,),
        in_specs=[
            pl.BlockSpec((tm, W), lambda i: (i, 0)),
            pl.BlockSpec((1, W), lambda i: (0, 0)),
            pl.BlockSpec((tm, LANES), lambda i: (i, 0)),
            pl.BlockSpec((tm, LANES), lambda i: (i, 0)),
            pl.BlockSpec((LANES, LANES), lambda i: (0, 0)),
        ],
        out_specs=[pl.BlockSpec((HK, tm // TQ, NSA_DH, GW), lambda i: (0, i, 0, 0)), rows, rows, vals, vals],
        out_shape=[jax.ShapeDtypeStruct((HK, L // TQ, NSA_DH, GW), BF16),
                   jax.ShapeDtypeStruct((HK, L, LANES), BF16), jax.ShapeDtypeStruct((HK, L, LANES), BF16),
                   jax.ShapeDtypeStruct((HK, V_ROWS, L), BF16), jax.ShapeDtypeStruct((HK, V_ROWS, L), BF16)],
        compiler_params=_cparams(("arbitrary",)),
        name="nsa_prep",
    )(proj, gains, cos, sin, bsum)


def _cmp_kernel(xk_ref, xv_ref, pe_ref, w1k_ref, w2k_ref, w1v_ref, w2v_ref, gk_ref, cos_ref, sin_ref, bsum_ref,
                kc_ref, vcx_ref):
    nb = kc_ref.shape[1]
    half_w = CMP_STRIDE * LANES

    def mlp(x_ref, w1_ref, w2_ref):
        hb = jnp.concatenate([x_ref[pl.ds(l, nb, stride=CMP_STRIDE), :] for l in range(CMP_STRIDE)], axis=1)
        a = (hb + pe_ref[:, 0:half_w]).astype(BF16)
        b = (hb + pe_ref[:, half_w:2 * half_w]).astype(BF16)
        p1 = _dot(a, w1_ref[0:half_w, :])
        p2 = _dot(b, w1_ref[half_w:2 * half_w, :])
        h = p1 + pltpu.roll(p2, nb - 1, 0)
        return _dot(_gelu(h).astype(BF16), w2_ref[...])

    rid = lax.broadcasted_iota(jnp.int32, (nb, LANES), 0)
    real = rid < nb - 1
    kc = jnp.where(real, mlp(xk_ref, w1k_ref, w2k_ref), 0.0)
    vc = jnp.where(real, mlp(xv_ref, w1v_ref, w2v_ref), 0.0)
    ms = _dot(kc * kc, bsum_ref[...], precision=HIGHEST)
    kcn = _rope_slab(kc * lax.rsqrt(ms + EPS) * gk_ref[...], cos_ref[...], sin_ref[...])
    kc_ref[0] = kcn[:, 0:NSA_DH].astype(BF16)
    kc_ref[1] = pltpu.roll(kcn, NSA_DH, 1)[:, 0:NSA_DH].astype(BF16)
    ones_rows = jnp.where(lax.broadcasted_iota(jnp.int32, (V_ROWS - NSA_DH, nb), 0) == 0, 1.0, 0.0).astype(BF16)
    for t in range(nb // LANES):
        vT = vc[t * LANES:(t + 1) * LANES, :].T.astype(BF16)
        for hk in range(NSA_KV):
            vcx_ref[hk, 0:NSA_DH, t * LANES:(t + 1) * LANES] = vT[hk * NSA_DH:(hk + 1) * NSA_DH, :]
    for hk in range(NSA_KV):
        vcx_ref[hk, NSA_DH:V_ROWS, :] = ones_rows


def _nsa_compress(proj, pe, w1k, w2k, w1v, w2v, gk, cos, sin):
    L = proj.shape[0]
    nb = L // CMP_STRIDE
    bsum = jnp.asarray(np.kron(np.eye(2), np.ones((NSA_DH, NSA_DH))) / NSA_DH, F32)
    once = lambda shape, idx: pl.BlockSpec(shape, lambda i: idx, pipeline_mode=pl.Buffered(1))
    full = lambda a: once(a.shape, (0,) * a.ndim)
    return pl.pallas_call(
        _cmp_kernel,
        grid=(1,),
        in_specs=[once((L, LANES), (0, C_KC // LANES)), once((L, LANES), (0, C_VC // LANES)),
                  full(pe), full(w1k), full(w2k), full(w1v), full(w2v), full(gk), full(cos), full(sin), full(bsum)],
        out_specs=[pl.BlockSpec((NSA_KV, nb, NSA_DH), lambda i: (0, 0, 0)),
                   pl.BlockSpec((NSA_KV, V_ROWS, nb), lambda i: (0, 0, 0))],
        out_shape=[jax.ShapeDtypeStruct((NSA_KV, nb, NSA_DH), BF16),
                   jax.ShapeDtypeStruct((NSA_KV, V_ROWS, nb), BF16)],
        compiler_params=_cparams(("arbitrary",)),
        name="nsa_compress",
    )(proj, proj, pe, w1k, w2k, w1v, w2v, gk, cos, sin, bsum)


def _nsa_attn_kernel(qT_ref, kc_ref, vcx_ref, smap_ref, ksaug_ref, vsx_ref, kwp_ref, vwxp_ref, gT_ref,
                     o_ref, qaug_ref, qwin_ref, ss_ref, mt_ref, m_ref, acc_ref, sc_ref, imp_ref, *, n_grp):
    i = pl.program_id(0)
    s0 = i * TQ
    hks = range(qT_ref.shape[0])
    nb = kc_ref.shape[1]
    ns = qaug_ref.shape[1] * n_grp
    GW = NSA_G * TQ
    qT = [qT_ref[h, 0] for h in hks]
    tq = s0 + (lax.broadcasted_iota(jnp.int32, (1, GW), 1) & (TQ - 1))

    cch = min(CMP_CHUNK, nb)
    n_cch = (s0 + TQ - CMP_LEN) // (CMP_STRIDE * cch) + 1
    nthr = (tq - (CMP_LEN - 1)) // CMP_STRIDE

    def cmp_scores(c, mcs):
        r0c = pl.multiple_of(c * cch, cch)
        s = [_dot(kc_ref[h, pl.ds(r0c, cch), :], qT[h]) for h in hks]
        nrow = r0c + lax.broadcasted_iota(jnp.int32, (cch, GW), 0)
        out = []
        for h in hks:
            sh = jnp.where(nrow <= nthr, s[h], NEG_BIG)
            sc_ref[h, pl.ds(r0c, cch), :] = sh
            out.append(jnp.maximum(mcs[h], jnp.max(sh, axis=0, keepdims=True)))
        return tuple(out)

    mc = lax.fori_loop(0, n_cch, cmp_scores, tuple(jnp.full((1, GW), NEG_BIG, F32) for _ in hks))
    imp_ref[...] = jnp.zeros_like(imp_ref)
    ones16 = jnp.ones((16, cch), BF16)
    VX = vcx_ref.shape[1]

    sm = smap_ref[...]
    sm_rows = sm.shape[0]
    blocks_per_chunk = cch * CMP_STRIDE // SLC_BLOCK

    def cmp_accum(c, carry):
        r0c = pl.multiple_of(c * cch, cch)
        j0 = pl.multiple_of(c * blocks_per_chunk, blocks_per_chunk)
        parts = [_split2(jnp.exp2(sc_ref[h, pl.ds(r0c, cch), :] - mc[h])) for h in hks]
        for h in hks:
            imp_ref[h, pl.ds(j0, sm_rows), :] += _dot(sm, parts[h][0]) + _dot(sm, parts[h][1])
        return tuple((carry[h][0] + _dot(vcx_ref[h, :, pl.ds(r0c, cch)], parts[h][0]),
                      carry[h][1] + _dot(ones16, parts[h][1])) for h in hks)

    cacc = lax.fori_loop(0, n_cch, cmp_accum,
                         tuple((jnp.zeros((VX, GW), F32), jnp.zeros((16, GW), F32)) for _ in hks))
    inv_l = [jnp.where(tq >= CMP_LEN - 1, 1.0 / (cacc[h][0][NSA_DH:NSA_DH + 1] + cacc[h][1][0:1]), 0.0) for h in hks]
    ocT = [cacc[h][0][0:NSA_DH] * inv_l[h] for h in hks]

    r0 = pl.multiple_of(s0, TQ)
    rel = lax.broadcasted_iota(jnp.int32, (TQ, GW), 0)
    qrel = tq - s0
    flag_row = lax.broadcasted_iota(jnp.int32, (KW_LANES - NSA_DH, GW), 0) == 0
    for h in hks:
        qwin_ref[h, 0:NSA_DH, :] = qT[h]
        qwin_ref[h, NSA_DH:KW_LANES, :] = jnp.where(flag_row, NEG_BIG, 0.0).astype(BF16)
    qwin = [qwin_ref[h] for h in hks]
    edge = [_dot(kwp_ref[h, pl.ds(r0, TQ), :], qwin[h]) for h in hks]
    mid = [_dot(kwp_ref[h, pl.ds(r0 + TQ, WINDOW - TQ), :], qwin[h]) for h in hks]
    diag = [_dot(kwp_ref[h, pl.ds(r0 + WINDOW, TQ), :], qwin[h]) for h in hks]
    edge16 = [jnp.where(rel > qrel, edge[h], NEG_BIG).astype(BF16) for h in hks]
    diag16 = [jnp.where(rel <= qrel, diag[h], NEG_BIG).astype(BF16) for h in hks]
    mid16 = [mid[h].astype(BF16) for h in hks]
    mw = [jnp.maximum(jnp.maximum(jnp.max(edge16[h], axis=0, keepdims=True), jnp.max(mid16[h], axis=0, keepdims=True)),
                      jnp.max(diag16[h], axis=0, keepdims=True)) for h in hks]
    accw = [(_dot(vwxp_ref[h, :, pl.ds(r0, TQ)], jnp.exp2(edge16[h] - mw[h]))
             + _dot(vwxp_ref[h, :, pl.ds(r0 + TQ, WINDOW - TQ)], jnp.exp2(mid16[h] - mw[h]))
             + _dot(vwxp_ref[h, :, pl.ds(r0 + WINDOW, TQ)], jnp.exp2(diag16[h] - mw[h]))) for h in hks]
    owT = [accw[h][0:NSA_DH] / accw[h][NSA_DH:NSA_DH + 1] for h in hks]
    sd = [_dot(ksaug_ref[h, pl.ds(r0, TQ), 0:NSA_DH], qT[h]) for h in hks]
    sd16 = [jnp.where(rel <= qrel, sd[h], NEG_BIG).astype(BF16) for h in hks]

    imp = []
    for h in hks:
        tot = imp_ref[h, 0:ns, 0:TQ] * inv_l[h][:, 0:TQ]
        for g in range(1, NSA_G):
            tot = tot + imp_ref[h, 0:ns, g * TQ:(g + 1) * TQ] * inv_l[h][:, g * TQ:(g + 1) * TQ]
        imp.append(tot)

    jrow = lax.broadcasted_iota(jnp.int32, (ns, TQ), 0)
    cur = (s0 + lax.broadcasted_iota(jnp.int32, (1, TQ), 1)) // SLC_BLOCK
    forced = jnp.where(jrow == 0, 0.0, jnp.where(jrow == cur, 0.0, jnp.where(jrow == cur - 1, 0.0, NEG_BIG)))
    val = [jnp.where(jrow >= 1, jnp.where(jrow <= cur - 2, imp[h], -jnp.inf), -jnp.inf) for h in hks]
    for _ in range(min(SLC_TOPK, ns) - 3):
        mx = [jnp.max(val[h], axis=0, keepdims=True) for h in hks]
        idx = [jnp.min(jnp.where(val[h] == mx[h], jrow, ns), axis=0, keepdims=True) for h in hks]
        val = [jnp.where(jrow == idx[h], -jnp.inf, val[h]) for h in hks]
    bias = [jnp.where(jrow >= 1, jnp.where(jrow <= cur - 2, jnp.where(val[h] == -jnp.inf, 0.0, NEG_BIG), forced), forced)
            for h in hks]
    own = jrow // 2 == i
    for h in hks:
        bias16 = jnp.where(own, NEG_BIG, bias[h]).astype(BF16)
        for grp in range(ns // n_grp):
            qaug_ref[h, grp, 0:NSA_DH, :] = qT[h]
            for g in range(NSA_G):
                qaug_ref[h, grp, NSA_DH:NSA_DH + n_grp, g * TQ:(g + 1) * TQ] = bias16[grp * n_grp:(grp + 1) * n_grp, :]

    tiles_per_grp = (n_grp * SLC_BLOCK) // TK
    n_body = jnp.maximum((s0 + SEL_UNROLL * TK - 1) // (SEL_UNROLL * TK), 1)

    def scores(kt, slot):
        k0 = pl.multiple_of(kt * TK, TK)
        grp = kt // tiles_per_grp
        s = [_dot(ksaug_ref[h, pl.ds(k0, TK), :], qaug_ref[h, grp]) for h in hks]
        for h in hks:
            sb = s[h].astype(BF16)
            ss_ref[h, slot] = sb
            mt_ref[h, slot] = jnp.max(sb, axis=0, keepdims=True).astype(F32)

    def online_update(sb, mt, v):
        m = [m_ref[h] for h in hks]
        mn = [jnp.maximum(m[h], mt[h]) for h in hks]
        pv = [_dot(v[h], jnp.exp2(sb[h] - mn[h].astype(BF16))) for h in hks]
        for h in hks:
            acc_ref[h] = jnp.exp2(m[h] - mn[h]) * acc_ref[h] + pv[h]
            m_ref[h] = mn[h]

    def accumulate(kt, slot):
        k0 = pl.multiple_of(kt * TK, TK)
        online_update([ss_ref[h, slot] for h in hks], [mt_ref[h, slot] for h in hks],
                      [vsx_ref[h, :, pl.ds(k0, TK)] for h in hks])

    m_ref[...] = jnp.full(m_ref.shape, NEG_BIG, F32)
    acc_ref[...] = jnp.zeros_like(acc_ref)

    scores(0, 0)

    def tile_group(base, last):
        for u in range(SEL_UNROLL):
            if not (last and u == SEL_UNROLL - 1):
                scores(base + u + 1, (u + 1) % 2)
            accumulate(base + u, u % 2)

    def loop_body(j, carry):
        tile_group(SEL_UNROLL * j, last=False)
        return carry

    lax.fori_loop(0, n_body - 1, loop_body, 0)

    tile_group(SEL_UNROLL * (n_body - 1), last=True)

    online_update(sd16, [jnp.max(sd16[h], axis=0, keepdims=True).astype(F32) for h in hks],
                  [vsx_ref[h, :, pl.ds(r0, TQ)] for h in hks])
    osT = [acc_ref[h, 0:NSA_DH, :] / acc_ref[h, NSA_DH:NSA_DH + 1, :] for h in hks]

    for h in hks:
        gates = _sigmoid(gT_ref[h])
        outs = []
        for g in range(NSA_G):
            sl = slice(g * TQ, (g + 1) * TQ)
            outs.append(gates[3 * g:3 * g + 1] * ocT[h][:, sl] + gates[3 * g + 1:3 * g + 2] * osT[h][:, sl]
                        + gates[3 * g + 2:3 * g + 3] * owT[h][:, sl])
        for g in range(0, NSA_G, 2):
            c0 = (h * NSA_G + g) * NSA_DH
            o_ref[:, c0:c0 + 2 * NSA_DH] = jnp.concatenate([outs[g], outs[g + 1]], axis=0).T


def _nsa_attention(qT, kc, vcx, smapT, ksaug, vsx, kwp, vwxp, gT, n_grp):
    HK, nQ, _, GW = qT.shape
    nb = kc.shape[1]
    L = vsx.shape[2]
    ns = L // SLC_BLOCK
    KA = ksaug.shape[2]
    scratch = [pltpu.VMEM((HK, ns // n_grp, KA, GW), BF16),
               pltpu.VMEM((HK, KW_LANES, GW), BF16),
               pltpu.VMEM((HK, 2, TK, GW), BF16),
               pltpu.VMEM((HK, 2, 1, GW), F32),
               pltpu.VMEM((HK, 1, GW), F32),
               pltpu.VMEM((HK, V_ROWS, GW), F32),
               pltpu.VMEM((HK, nb, GW), F32),
               pltpu.VMEM((HK, ns + 8, GW), F32)]
    resident = lambda a: pl.BlockSpec(a.shape, lambda i: (0,) * a.ndim, pipeline_mode=pl.Buffered(1))
    return pl.pallas_call(
        functools.partial(_nsa_attn_kernel, n_grp=n_grp),
        grid=(nQ,),
        in_specs=[
            pl.BlockSpec((HK, 1, NSA_DH, GW), lambda i: (0, i, 0, 0)),
            resident(kc), resident(vcx), resident(smapT), resident(ksaug), resident(vsx), resident(kwp),
            resident(vwxp),
            pl.BlockSpec((HK, 16, TQ), lambda i: (0, 0, i)),
        ],
        out_specs=pl.BlockSpec((TQ, NSA_HEADS * NSA_DH), lambda i: (i, 0)),
        out_shape=jax.ShapeDtypeStruct((nQ * TQ, NSA_HEADS * NSA_DH), F32),
        scratch_shapes=scratch,
        compiler_params=_cparams(("arbitrary",)),
        name="nsa_attn",
    )(qT, kc, vcx, smapT, ksaug, vsx, kwp, vwxp, gT)


def _s5_kernel(u_ref, lbr_ref, lbi_ref, ctr_ref, cti_ref, btr_ref, bti_ref, bsr_ref, bsi_ref, y_ref, lm_ref):
    T = S5_CHUNK
    W = T * S5_CH
    ar = lbr_ref[0]
    ai = lbi_ref[0]
    delta = lax.broadcasted_iota(jnp.int32, (W, S5_STATE), 0) // S5_CH

    def powers(e):
        pr = jnp.ones((W, S5_STATE), F32)
        pi = jnp.zeros((W, S5_STATE), F32)
        fr, fi = ar, ai
        for b in range(T.bit_length() - 1):
            bit = ((e >> b) & 1) == 1
            nr = pr * fr - pi * fi
            ni = pr * fi + pi * fr
            pr = jnp.where(bit, nr, pr)
            pi = jnp.where(bit, ni, pi)
            fr, fi = fr * fr - fi * fi, 2.0 * fr * fi
        return pr, pi, fr, fi

    pwr, pwi, aTr, aTi = powers(delta)
    rvr, rvi, _, _ = powers(T - 1 - delta)
    ctr, cti = ctr_ref[0], cti_ref[0]
    car = ctr * pwr - cti * pwi
    cai = ctr * pwi + cti * pwr

    kw = _dot_nt(bsr_ref[0], car, HIGHEST) - _dot_nt(bsi_ref[0], cai, HIGHEST)
    lane = lax.broadcasted_iota(jnp.int32, (S5_CH, W), 1)
    for tau in range(T):
        sh = S5_CH * tau
        blk = kw if tau == 0 else jnp.where(lane >= sh, pltpu.roll(kw, sh, 1), 0.0)
        lm_ref[sh:sh + S5_CH, :] = blk.astype(BF16)

    u = u_ref[0]
    nc = u.shape[0]
    y = _dot(u, lm_ref[...])

    btr, bti = btr_ref[0], bti_ref[0]
    sr = _dot(u, (rvr * btr - rvi * bti).astype(BF16))
    si = _dot(u, (rvr * bti + rvi * btr).astype(BF16))
    rowc = lax.broadcasted_iota(jnp.int32, (nc, S5_STATE), 0)
    fr, fi = aTr, aTi
    step = 1
    while step < nc:
        shr = jnp.where(rowc >= step, pltpu.roll(sr, step, 0), 0.0)
        shi = jnp.where(rowc >= step, pltpu.roll(si, step, 0), 0.0)
        sr, si = sr + fr * shr - fi * shi, si + fr * shi + fi * shr
        fr, fi = fr * fr - fi * fi, 2.0 * fr * fi
        step *= 2
    xr = jnp.where(rowc >= 1, pltpu.roll(sr, 1, 0), 0.0)
    xi = jnp.where(rowc >= 1, pltpu.roll(si, 1, 0), 0.0)
    c1r = car * ar - cai * ai
    c1i = car * ai + cai * ar
    y = y + _dot_nt(xr.astype(BF16), c1r.astype(BF16)) - _dot_nt(xi.astype(BF16), c1i.astype(BF16))
    y_ref[0] = y


def _s5_scan(ug, lbr, lbi, ctr, cti, btr, bti, bsr, bsi):
    G, nc, W = ug.shape
    grp = lambda shape: pl.BlockSpec((1,) + shape, lambda g: (g, 0, 0))
    return pl.pallas_call(
        _s5_kernel,
        grid=(G,),
        in_specs=[grp((nc, W)), grp((1, S5_STATE)), grp((1, S5_STATE)),
                  grp((W, S5_STATE)), grp((W, S5_STATE)), grp((W, S5_STATE)), grp((W, S5_STATE)),
                  grp((S5_CH, S5_STATE)), grp((S5_CH, S5_STATE))],
        out_specs=grp((nc, W)),
        out_shape=jax.ShapeDtypeStruct((G, nc, W), F32),
        scratch_shapes=[pltpu.VMEM((W, W), BF16)],
        compiler_params=_cparams(("arbitrary",)),
        name="s5_scan",
    )(ug, lbr, lbi, ctr, cti, btr, bti, bsr, bsi)


def _small_kernel(x_ref, p_ref, o_ref):
    x = x_ref[...]
    lane = lax.broadcasted_iota(jnp.int32, x.shape, 1)
    z = x + p_ref[0:1, :]
    softplus = jnp.maximum(z, 0.0) + jnp.log(1.0 + jnp.exp(-jnp.abs(z)))
    gdec = p_ref[1:2, :] * softplus
    beta = _sigmoid(x)
    o_ref[...] = jnp.where(lane < SM_G, x, jnp.where(lane < SM_BETA, gdec, beta))


def _small(proj, params, tm=1024):
    L = proj.shape[0]
    cb = C_SM // LANES
    return pl.pallas_call(
        _small_kernel,
        grid=(L // tm,),
        in_specs=[pl.BlockSpec((tm, LANES), lambda i: (i, cb)), pl.BlockSpec((8, LANES), lambda i: (0, 0))],
        out_specs=pl.BlockSpec((tm, LANES), lambda i: (i, 0)),
        out_shape=jax.ShapeDtypeStruct((L, LANES), F32),
        compiler_params=_cparams(("arbitrary",)),
        name="small_cols",
    )(proj, params)


GDN_CB = 8


def _split2(x):
    hi = x.astype(BF16)
    return hi, (x - hi.astype(F32)).astype(BF16)


def _dot_split(a, b):
    ah, al = _split2(a)
    bh, bl = _split2(b)
    return _dot(ah, bh) + (_dot(ah, bl) + _dot(al, bh))


def _gdn_local_kernel(xq_ref, xk_ref, xv_ref, cwq_ref, cwk_ref, cwv_ref, sm_ref,
                      val_ref, kcum_ref, qg_ref, kdT_ref, qk_ref, gl_ref, carry_ref, hs_ref):
    C = GDN_CHUNK
    R = xq_ref.shape[0]

    @pl.when(pl.program_id(1) == 0)
    def _():
        carry_ref[...] = jnp.zeros_like(carry_ref)

    def conv_silu(x_ref, cw_ref, slot):
        x = x_ref[...]
        hs_ref[slot, 0:8, :] = carry_ref[slot]
        hs_ref[slot, 8:R + 8, :] = x
        carry_ref[slot] = x[R - 8:R, :]
        taps = cw_ref.shape[0]
        y = cw_ref[taps - 1:taps, :] * x
        for back in range(1, taps):
            y = y + cw_ref[taps - 1 - back:taps - back, :] * hs_ref[slot, 8 - back:8 - back + R, :]
        half_y = 0.5 * y
        return half_y * (1.0 + jnp.tanh(half_y))

    def l2norm(y):
        return y * lax.rsqrt(jnp.sum(y * y, axis=-1, keepdims=True) + EPS)

    q_all = l2norm(conv_silu(xq_ref, cwq_ref, 0)) * (GDN_DH ** -0.5)
    k_all = l2norm(conv_silu(xk_ref, cwk_ref, 1))
    v_all = conv_silu(xv_ref, cwv_ref, 2)
    chunks = range(GDN_CB)
    rows = [slice(c * C, (c + 1) * C) for c in chunks]
    ii = lax.broadcasted_iota(jnp.int32, (C, C), 0)
    jj = lax.broadcasted_iota(jnp.int32, (C, C), 1)
    causal = ii >= jj
    strict = ii > jj
    eye = (ii == jj).astype(F32)
    tril16 = causal.astype(BF16)
    row2 = lax.broadcasted_iota(jnp.int32, (C, 2 * C), 0)
    col2 = lax.broadcasted_iota(jnp.int32, (C, 2 * C), 1)
    keep = row2 > jnp.where(col2 < C, -1, col2 - C)

    head = pl.program_id(0)
    sm = sm_ref[...]
    sm_lane = lax.broadcasted_iota(jnp.int32, sm.shape, 1)
    g_col = jnp.sum(jnp.where(sm_lane == SM_G + head, sm, 0.0), axis=1, keepdims=True)
    b_col = jnp.sum(jnp.where(sm_lane == SM_BETA + head, sm, 0.0), axis=1, keepdims=True)

    k = [k_all[rs, :] for rs in rows]
    beta = [b_col[rs, :] for rs in rows]
    res = []
    for rs in rows:
        x = jnp.where(keep, jnp.broadcast_to(g_col[rs, :], (C, 2 * C)), 0.0)
        hi = x.astype(BF16)
        r1 = x - hi.astype(F32)
        mid = r1.astype(BF16)
        lo = (r1 - mid.astype(F32)).astype(BF16)
        res.append(_dot(tril16, hi) + (_dot(tril16, mid) + _dot(tril16, lo)))
    gc = [jnp.broadcast_to(r[:, 0:1], (C, GDN_DH)) for r in res]
    decay = [jnp.where(causal, jnp.exp(r[:, C:2 * C]), 0.0) for r in res]
    kb = [kc * bc for kc, bc in zip(k, beta)]
    k16 = [kc.astype(BF16) for kc in k]
    a = [jnp.where(strict, _dot_nt(kbc.astype(BF16), kc16) * dc, 0.0) for kbc, kc16, dc in zip(kb, k16, decay)]
    t = [eye - ac for ac in a]
    p = a
    for _ in range(int(math.log2(C)) - 1):
        p = [_dot_split(pc, pc) for pc in p]
        t = [tc + _dot_split(tc, pc) for tc, pc in zip(t, p)]
    t16 = [tc.astype(BF16) for tc in t]
    egc = [jnp.exp(gcc) for gcc in gc]
    val = [_dot(tc, (v_all[rs, :] * bc).astype(BF16)) for tc, rs, bc in zip(t16, rows, beta)]
    kcum = [_dot(tc, (kbc * ec).astype(BF16)) for tc, kbc, ec in zip(t16, kb, egc)]
    qk = [_dot_nt(q_all[rs, :].astype(BF16), kc16) * dc for rs, kc16, dc in zip(rows, k16, decay)]
    kd = []
    for c in chunks:
        rs = rows[c]
        val_ref[rs, :] = val[c]
        kcum_ref[rs, :] = kcum[c].astype(BF16)
        qk_ref[0, rs, :] = qk[c].astype(BF16)
        glast = gc[c][C - 1:C, :]
        qg_ref[rs, :] = (q_all[rs, :] * egc[c]).astype(BF16)
        kd.append(k[c] * jnp.exp(glast - gc[c]))
        gl_ref[0, c:c + 1, :] = jnp.exp(glast)
    for c in range(0, GDN_CB, 2):
        kdT_ref[0, c // 2] = jnp.concatenate([kd[c], kd[c + 1]], axis=0).T.astype(BF16)


def _gdn_local(proj, conv_w, small):
    L = proj.shape[0]
    H = GDN_HEADS
    R = GDN_CB * GDN_CHUNK
    cb = C_QKV // GDN_DH
    taps = conv_w.shape[0]
    col = lambda off: pl.BlockSpec((R, GDN_DH), lambda h, i: (i, cb + off + h))
    cw = lambda off: pl.BlockSpec((taps, GDN_DH), lambda h, i: (0, off + h))
    big = pl.BlockSpec((R, GDN_DH), lambda h, i: (i, h))
    return pl.pallas_call(
        _gdn_local_kernel,
        grid=(H, L // R),
        in_specs=[col(0), col(H), col(2 * H), cw(0), cw(H), cw(2 * H),
                  pl.BlockSpec((R, LANES), lambda h, i: (i, 0))],
        out_specs=[big, big, big,
                   pl.BlockSpec((1, GDN_CB // 2, GDN_DH, 2 * GDN_CHUNK), lambda h, i: (h, i, 0, 0)),
                   pl.BlockSpec((1, R, GDN_CHUNK), lambda h, i: (h, i, 0)),
                   pl.BlockSpec((1, GDN_CB, GDN_DH), lambda h, i: (h, i, 0))],
        out_shape=[jax.ShapeDtypeStruct((L, GDN_WIDTH), F32),
                   jax.ShapeDtypeStruct((L, GDN_WIDTH), BF16),
                   jax.ShapeDtypeStruct((L, GDN_WIDTH), BF16),
                   jax.ShapeDtypeStruct((H, L // (2 * GDN_CHUNK), GDN_DH, 2 * GDN_CHUNK), BF16),
                   jax.ShapeDtypeStruct((H, L, GDN_CHUNK), BF16),
                   jax.ShapeDtypeStruct((H, L // GDN_CHUNK, GDN_DH), F32)],
        scratch_shapes=[pltpu.VMEM((3, 8, GDN_DH), F32), pltpu.VMEM((3, R + 8, GDN_DH), F32)],
        compiler_params=_cparams(("arbitrary", "arbitrary")),
        name="gdn_local",
    )(proj, proj, proj, conv_w, conv_w, conv_w, small)


def _gdn_scan_kernel(val_ref, kcum_ref, qg_ref, qk_ref, kdT_ref, gl_ref, z_ref, gn_ref, o_ref, st_ref):
    @pl.when(pl.program_id(0) == 0)
    def _():
        st_ref[...] = jnp.zeros_like(st_ref)

    C = GDN_CHUNK
    gn = gn_ref[...]
    heads = range(GDN_HEADS)
    cols = [slice(h * GDN_DH, (h + 1) * GDN_DH) for h in heads]
    state = [st_ref[h] for h in heads]
    for c in range(GDN_CB):
        rs = slice(c * C, (c + 1) * C)
        sb = [s.astype(BF16) for s in state]
        kcs = [_dot(kcum_ref[rs, cs], s) for cs, s in zip(cols, sb)]
        qgs = [_dot(qg_ref[rs, cs], s) for cs, s in zip(cols, sb)]
        vb = [(val_ref[rs, cs] - x).astype(BF16) for cs, x in zip(cols, kcs)]
        o = [x + _dot(qk_ref[h, rs, :], v) for h, x, v in zip(heads, qgs, vb)]
        kcols = slice((c % 2) * C, (c % 2 + 1) * C)
        state = [s * gl_ref[h, c:c + 1, :] + _dot(kdT_ref[h, c // 2, :, kcols], v)
                 for h, s, v in zip(heads, state, vb)]
        for h in heads:
            on = o[h] * lax.rsqrt(jnp.mean(o[h] * o[h], axis=-1, keepdims=True) + EPS) * gn
            z = z_ref[rs, cols[h]]
            o_ref[rs, cols[h]] = on * (z * _sigmoid(z))
    for h in heads:
        st_ref[h] = state[h]


def _gdn_scan(val, kcum, qg, qk, kdT, gl, proj, gnorm):
    L = val.shape[0]
    H = GDN_HEADS
    R = GDN_CB * GDN_CHUNK
    row = pl.BlockSpec((R, GDN_WIDTH), lambda i: (i, 0))
    zb = C_Z // GDN_WIDTH
    return pl.pallas_call(
        _gdn_scan_kernel,
        grid=(L // R,),
        in_specs=[row, row, row,
                  pl.BlockSpec((H, R, GDN_CHUNK), lambda i: (0, i, 0)),
                  pl.BlockSpec((H, GDN_CB // 2, GDN_DH, 2 * GDN_CHUNK), lambda i: (0, i, 0, 0)),
                  pl.BlockSpec((H, GDN_CB, GDN_DH), lambda i: (0, i, 0)),
                  pl.BlockSpec((R, GDN_WIDTH), lambda i: (i, zb)),
                  pl.BlockSpec((1, GDN_DH), lambda i: (0, 0))],
        out_specs=row,
        out_shape=jax.ShapeDtypeStruct((L, GDN_WIDTH), F32),
        scratch_shapes=[pltpu.VMEM((H, GDN_DH, GDN_DH), F32)],
        compiler_params=_cparams(("arbitrary",)),
        name="gdn_scan",
    )(val, kcum, qg, qk, kdT, gl, proj, gnorm)


def _mix_kernel(ya_ref, ys_ref, u_ref, yc_ref, x_ref, ga_ref, gb_ref, d_ref, wglu_ref, wout_ref, o_ref):
    def rms(y, g):
        return y * lax.rsqrt(jnp.mean(y * y, axis=-1, keepdims=True) + EPS) * g

    a = rms(ya_ref[...], ga_ref[...]).astype(BF16)
    yb = _gelu(ys_ref[...] + d_ref[...] * u_ref[...])
    yb = yb * _sigmoid(_dot(yb.astype(BF16), wglu_ref[...]))
    b = rms(yb, gb_ref[...]).astype(BF16)
    na, nb = a.shape[1], b.shape[1]
    acc = x_ref[...] + _dot(a, wout_ref[0:na, :])
    acc = acc + _dot(b, wout_ref[na:na + nb, :])
    o_ref[...] = acc + _dot(yc_ref[...].astype(BF16), wout_ref[na + nb:, :])


def _mix(ya, ys, proj, yc, x, ga, gb, d, wglu, wout, tm=256):
    L, D = x.shape
    ub = C_U // S5_WIDTH
    full = lambda a: pl.BlockSpec(a.shape, lambda i: (0, 0))
    return pl.pallas_call(
        _mix_kernel,
        grid=(L // tm,),
        in_specs=[pl.BlockSpec((tm, ya.shape[1]), lambda i: (i, 0)),
                  pl.BlockSpec((tm, S5_WIDTH), lambda i: (i, 0)),
                  pl.BlockSpec((tm, S5_WIDTH), lambda i: (i, ub)),
                  pl.BlockSpec((tm, GDN_WIDTH), lambda i: (i, 0)),
                  pl.BlockSpec((tm, D), lambda i: (i, 0)),
                  full(ga), full(gb), full(d), full(wglu), full(wout)],
        out_specs=pl.BlockSpec((tm, D), lambda i: (i, 0)),
        out_shape=jax.ShapeDtypeStruct((L, D), F32),
        compiler_params=_cparams(("arbitrary",)),
        name="mix_out",
    )(ya, ys, proj, yc, x, ga, gb, d, wglu, wout)


def _ffn_in_kernel(x_ref, g_ref, wg_ref, wu_ref, cg_ref, cu_ref, bg_ref, bu_ref, o_ref, xn_ref, carry_ref, hs_ref):
    i = pl.program_id(0)
    j = pl.program_id(1)

    @pl.when(j == 0)
    def _():
        xf = x_ref[...]
        ms = jnp.mean(xf * xf, axis=-1, keepdims=True)
        xn_ref[...] = (xf * lax.rsqrt(ms + EPS) * g_ref[...]).astype(BF16)

    @pl.when(i == 0)
    def _():
        carry_ref[j] = jnp.zeros(carry_ref.shape[1:], F32)

    xn = xn_ref[...]
    tm = xn.shape[0]

    def conv(h, slot, cw_ref, cb_ref):
        hs_ref[slot, 0:8, :] = carry_ref[j, slot]
        hs_ref[slot, 8:tm + 8, :] = h
        carry_ref[j, slot] = h[tm - 8:tm, :]
        taps = cw_ref.shape[0]
        y = cw_ref[taps - 1:taps, :] * h + cb_ref[...]
        for back in range(1, taps):
            y = y + cw_ref[taps - 1 - back:taps - back, :] * hs_ref[slot, 8 - back:8 - back + tm, :]
        return y

    gate = conv(_dot(xn, wg_ref[...]), 0, cg_ref, bg_ref)
    up = conv(_dot(xn, wu_ref[...]), 1, cu_ref, bu_ref)
    half_gate = 0.5 * gate
    o_ref[...] = (half_gate * (1.0 + jnp.tanh(half_gate)) * up).astype(BF16)


def _ffn_in(x, gain, w, cw, cb, tm=1024, tn=512):
    L, D = x.shape
    N = w.shape[1] // 2
    nj = N // tn
    taps = cw.shape[0]
    halves = lambda rows: (pl.BlockSpec((rows, tn), lambda i, j: (0, j)), pl.BlockSpec((rows, tn), lambda i, j: (0, nj + j)))
    wg, wu = halves(D)
    cg, cu = halves(taps)
    bg, bu = halves(1)
    return pl.pallas_call(
        _ffn_in_kernel,
        grid=(L // tm, nj),
        in_specs=[pl.BlockSpec((tm, D), lambda i, j: (i, 0)),
                  pl.BlockSpec((1, D), lambda i, j: (0, 0)),
                  wg, wu, cg, cu, bg, bu],
        out_specs=pl.BlockSpec((tm, tn), lambda i, j: (i, j)),
        out_shape=jax.ShapeDtypeStruct((L, N), BF16),
        scratch_shapes=[pltpu.VMEM((tm, D), BF16), pltpu.VMEM((nj, 2, 8, tn), F32),
                        pltpu.VMEM((2, tm + 8, tn), F32)],
        compiler_params=_cparams(("arbitrary", "arbitrary")),
        name="ffn_in",
    )(x, gain.reshape(1, D), w, w, cw, cw, cb, cb)


def _ffn_out_kernel(a_ref, w_ref, x_ref, o_ref):
    @pl.when(pl.program_id(1) == 0)
    def _():
        o_ref[...] = x_ref[...]

    o_ref[...] += _dot(a_ref[...], w_ref[...])


def _ffn_out(act, w, x, tm=1024, tk=512):
    L, D = x.shape
    K = act.shape[1]
    return pl.pallas_call(
        _ffn_out_kernel,
        grid=(L // tm, K // tk),
        in_specs=[pl.BlockSpec((tm, tk), lambda i, k: (i, k)),
                  pl.BlockSpec((tk, D), lambda i, k: (k, 0)),
                  pl.BlockSpec((tm, D), lambda i, k: (i, 0))],
        out_specs=pl.BlockSpec((tm, D), lambda i, k: (i, 0)),
        out_shape=jax.ShapeDtypeStruct((L, D), F32),
        compiler_params=_cparams(("arbitrary", "arbitrary")),
        name="ffn_out",
    )(act, w, x)


def _rope_tables(pos):
    half = ROPE_DIMS // 2
    inv = ROPE_THETA ** (-jnp.arange(half, dtype=F32) / half)
    ang = pos.astype(F32)[:, None] * inv[None, :]
    n = pos.shape[0]
    cos = jnp.concatenate([jnp.cos(ang), jnp.cos(ang), jnp.ones((n, NSA_DH - ROPE_DIMS), F32)], axis=1)
    sin = jnp.concatenate([-jnp.sin(ang), jnp.sin(ang), jnp.zeros((n, NSA_DH - ROPE_DIMS), F32)], axis=1)
    return jnp.tile(cos, (1, 2)), jnp.tile(sin, (1, 2))


def _permute_w_in(w):
    sp = np.cumsum([0, 512, 128, 128, 128, 128, 128, 128, 24, 512, 3072, 1024, 8, 8])
    q, kc, vc, ks, vs, kw, vw, gates, u, qkv, z, a, b = [w[:, sp[n]:sp[n + 1]] for n in range(13)]
    pad = jnp.zeros((w.shape[0], PROJ_COLS - C_SM - SM_END), w.dtype)
    return jnp.concatenate([q, ks, kw, vs, vw, z, qkv, u, kc, vc, gates, a, b, pad], axis=1)


def _nsa_mixer(proj, small, p, l):
    L = proj.shape[0]
    HK, G, DH = NSA_KV, NSA_G, NSA_DH
    ns = L // SLC_BLOCK
    nb = L // CMP_STRIDE
    n_grp = LANES - DH
    assert ns % n_grp == 0 and (n_grp * SLC_BLOCK) % (SEL_UNROLL * TK) == 0 and SEL_UNROLL % 2 == 0
    cos, sin = _rope_tables(jnp.arange(L))
    gains = jnp.concatenate([jnp.tile(p['nsa_q_norm'][l], NSA_HEADS), jnp.tile(p['nsa_ks_norm'][l], HK),
                             jnp.tile(p['nsa_kw_norm'][l], HK), jnp.ones((2 * HK * DH,), F32)]).reshape(1, -1)
    qT, ksaug, kwa, vsx, vwx = _nsa_prep(proj, gains, cos, sin, n_grp)
    kpad = jnp.zeros((HK, WINDOW, KW_LANES), BF16).at[:, :, DH].set(1.0)
    kwp = jnp.concatenate([kpad, kwa], axis=1)
    vwxp = jnp.pad(vwx, ((0, 0), (0, 0), (WINDOW, 0)))

    eye = jnp.eye(HK, dtype=F32)
    pe = jnp.tile(p['cmp_pe'][l][:, None, :], (1, HK, 1)).reshape(1, CMP_LEN * HK * DH)
    w1 = lambda name: jnp.einsum('ldf,hg->lhdgf', p[name][l], eye).reshape(CMP_LEN * HK * DH, -1).astype(BF16)
    w2 = lambda name: jnp.einsum('fd,hg->hfgd', p[name][l], eye).reshape(-1, HK * DH).astype(BF16)
    ccos, csin = _rope_tables(jnp.arange(nb) * CMP_STRIDE + CMP_LEN // 2)
    gk = jnp.tile(p['nsa_kc_norm'][l], HK).reshape(1, LANES)
    kc, vcx = _nsa_compress(proj, pe, w1('cmp_k_w1'), w2('cmp_k_w2'), w1('cmp_v_w1'), w2('cmp_v_w2'), gk, ccos, csin)

    cch = min(CMP_CHUNK, nb)
    cmp_start = np.arange(cch) * CMP_STRIDE
    slc_start = np.arange(cch * CMP_STRIDE // SLC_BLOCK + 8) * SLC_BLOCK
    smapT = jnp.asarray((cmp_start[None, :] <= slc_start[:, None] + SLC_BLOCK - 1)
                        & (cmp_start[None, :] + CMP_LEN - 1 >= slc_start[:, None]), BF16)

    gT = jnp.pad(small[:, :HK * G * 3].reshape(L, HK, G * 3).transpose(1, 2, 0), ((0, 0), (0, 16 - G * 3), (0, 0)))
    return _nsa_attention(qT, kc, vcx, smapT, ksaug, vsx, kwp, vwxp, gT, n_grp)


def _s5_mixer(proj, p, l):
    L = proj.shape[0]
    T = S5_CHUNK
    nc = L // T
    G, H, P = S5_GROUPS, S5_CH, S5_STATE
    lam_re, lam_im = p['s5_lam_re'][l], p['s5_lam_im'][l]
    dt = jnp.exp(p['s5_log_dt'][l])[:, None]
    mag = jnp.exp(lam_re * dt)
    lb_re = mag * jnp.cos(lam_im * dt)
    lb_im = mag * jnp.sin(lam_im * dt)
    den = lam_re * lam_re + lam_im * lam_im
    nr, ni = lb_re - 1.0, lb_im
    coef_re = (nr * lam_re + ni * lam_im) / den
    coef_im = (ni * lam_re - nr * lam_im) / den
    b_re, b_im = p['s5_b_re'][l], p['s5_b_im'][l]
    bb_re = coef_re[..., None] * b_re - coef_im[..., None] * b_im
    bb_im = coef_re[..., None] * b_im + coef_im[..., None] * b_re
    bsr, bsi = bb_re.transpose(0, 2, 1), bb_im.transpose(0, 2, 1)
    tile_rows = lambda a: jnp.tile(a, (1, T, 1))
    u = proj[:, C_U:C_U + S5_WIDTH]
    ug = u.astype(BF16).reshape(nc, T, G, H).transpose(2, 0, 1, 3).reshape(G, nc, T * H)
    y = _s5_scan(ug, lb_re[:, None, :], lb_im[:, None, :], tile_rows(p['s5_c_re'][l]), tile_rows(p['s5_c_im'][l]),
                 tile_rows(bsr), tile_rows(bsi), bsr, bsi)
    return y.reshape(G, nc, T, H).transpose(1, 2, 0, 3).reshape(L, S5_WIDTH)


def _gdn_mixer(proj, small, p, l):
    val, kcum, qg, kdT, qk, gl = _gdn_local(proj, p['gdn_conv'][l], small)
    return _gdn_scan(val, kcum, qg, qk, kdT, gl, proj, p['gdn_norm'][l].reshape(1, GDN_DH))


def _pad_halves(a):
    pad = jnp.zeros((a.shape[0], D_FF_PAD - D_FF), a.dtype)
    return jnp.concatenate([a[:, :D_FF], pad, a[:, D_FF:], pad], axis=1)


def _forward(x3, p):
    x = x3.reshape(x3.shape[1:])
    depth = p['w_in'].shape[0]
    for l in range(depth):
        w_in = _permute_w_in(p['w_in'][l].astype(BF16))
        proj = _rms_matmul(x, p['attn_norm'][l], w_in)
        sm_par = jnp.zeros((8, LANES), F32)
        sm_par = sm_par.at[0, SM_G:SM_BETA].set(p['gdn_dt_bias'][l]).at[1, SM_G:SM_BETA].set(-jnp.exp(p['gdn_a_log'][l]))
        small = _small(proj, sm_par)
        y_a = _nsa_mixer(proj, small, p, l)
        y_s = _s5_mixer(proj, p, l)
        y_c = _gdn_mixer(proj, small, p, l)
        x = _mix(y_a, y_s, proj, y_c, x, p['nsa_out_norm'][l].reshape(1, -1), p['s5_out_norm'][l].reshape(1, -1),
                 p['s5_d'][l].reshape(1, -1), p['s5_w_glu'][l].astype(BF16), p['w_out'][l].astype(BF16))
        act = _ffn_in(x, p['ffn_norm'][l], _pad_halves(p['ffn_w_in'][l].astype(BF16)), _pad_halves(p['ffn_conv'][l]),
                      _pad_halves(p['ffn_conv_b'][l].reshape(1, -1)))
        w_o = jnp.pad(p['ffn_w_out'][l].astype(BF16), ((0, D_FF_PAD - D_FF), (0, 0)))
        x = _ffn_out(act, w_o, x)
    return x.reshape(x3.shape)


_PARAM_NAMES = ('attn_norm', 'w_in', 'nsa_q_norm', 'nsa_kc_norm', 'nsa_ks_norm', 'nsa_kw_norm', 'cmp_pe',
                'cmp_k_w1', 'cmp_k_w2', 'cmp_v_w1', 'cmp_v_w2', 'nsa_out_norm', 's5_lam_re', 's5_lam_im',
                's5_log_dt', 's5_b_re', 's5_b_im', 's5_c_re', 's5_c_im', 's5_d', 's5_w_glu', 's5_out_norm',
                'gdn_conv', 'gdn_a_log', 'gdn_dt_bias', 'gdn_norm', 'w_out', 'ffn_norm', 'ffn_w_in', 'ffn_conv',
                'ffn_conv_b', 'ffn_w_out')


def kernel(x, attn_norm, w_in, nsa_q_norm, nsa_kc_norm, nsa_ks_norm, nsa_kw_norm, cmp_pe, cmp_k_w1, cmp_k_w2,
           cmp_v_w1, cmp_v_w2, nsa_out_norm, s5_lam_re, s5_lam_im, s5_log_dt, s5_b_re, s5_b_im, s5_c_re, s5_c_im,
           s5_d, s5_w_glu, s5_out_norm, gdn_conv, gdn_a_log, gdn_dt_bias, gdn_norm, w_out, ffn_norm, ffn_w_in,
           ffn_conv, ffn_conv_b, ffn_w_out):
    vals = (attn_norm, w_in, nsa_q_norm, nsa_kc_norm, nsa_ks_norm, nsa_kw_norm, cmp_pe, cmp_k_w1, cmp_k_w2,
            cmp_v_w1, cmp_v_w2, nsa_out_norm, s5_lam_re, s5_lam_im, s5_log_dt, s5_b_re, s5_b_im, s5_c_re, s5_c_im,
            s5_d, s5_w_glu, s5_out_norm, gdn_conv, gdn_a_log, gdn_dt_bias, gdn_norm, w_out, ffn_norm, ffn_w_in,
            ffn_conv, ffn_conv_b, ffn_w_out)
    return _forward(x, dict(zip(_PARAM_NAMES, vals)))
```

```python
import functools
import math

import jax
import jax.numpy as jnp
import numpy as np
from jax import lax
from jax.experimental import pallas as pl
from jax.experimental.pallas import tpu as pltpu

F32 = jnp.float32
BF16 = jnp.bfloat16
HIGHEST = lax.Precision.HIGHEST

NSA_HEADS = 8
NSA_KV = 2
NSA_G = NSA_HEADS // NSA_KV
NSA_DH = 64
CMP_LEN = 32
CMP_STRIDE = 16
SLC_BLOCK = 64
SLC_TOPK = 16
WINDOW = 512
ROPE_THETA = 500000.0
ROPE_DIMS = NSA_DH // 4
S5_WIDTH = 512
S5_CH = 16
S5_GROUPS = 32
S5_STATE = 64
S5_CHUNK = 32
GDN_HEADS = 8
GDN_DH = 128
GDN_WIDTH = 1024
GDN_CHUNK = 64
D_FF = 5504
D_FF_PAD = 5632
EPS = 1e-6
NEG_BIG = -(2.0 ** 100)
LOG2_E = math.log2(math.e)
V_ROWS = 80
KW_LANES = 128
CMP_CHUNK = 256
SEL_UNROLL = 2

LANES = 128
TQ = 128
TK = 512
VMEM_LIMIT = 56 * 1024 * 1024

C_Q, C_KS, C_KW, C_VS, C_VW = 0, 512, 640, 768, 896
C_Z, C_QKV, C_U, C_KC, C_VC, C_SM = 1024, 2048, 5120, 5632, 5760, 5888
NSA_PREP_W = C_Z
PROJ_COLS = 6144
SM_GATES, SM_G, SM_BETA, SM_END = 0, 24, 32, 40


def _cparams(sem):
    return pltpu.CompilerParams(dimension_semantics=sem, vmem_limit_bytes=VMEM_LIMIT)


def _dot(a, b, precision=None):
    return jnp.dot(a, b, preferred_element_type=F32, precision=precision)


def _dot_nt(a, b, precision=None):
    return lax.dot_general(a, b, (((1,), (1,)), ((), ())), preferred_element_type=F32, precision=precision)


def _gelu(x):
    return x * (0.5 * (1.0 + jnp.tanh(math.sqrt(2.0 / math.pi) * (x + 0.044715 * (x * x * x)))))


def _sigmoid(x):
    return 1.0 / (1.0 + jnp.exp(-x))


def _rms_matmul_kernel(x_ref, g_ref, w_ref, o_ref, xn_ref):
    @pl.when(pl.program_id(1) == 0)
    def _():
        xf = x_ref[...]
        ms = jnp.mean(xf * xf, axis=-1, keepdims=True)
        xn_ref[...] = (xf * lax.rsqrt(ms + EPS) * g_ref[...]).astype(BF16)

    o_ref[...] = _dot(xn_ref[...], w_ref[...])


def _rms_matmul(x, gain, w, tm=512, tn=2048):
    L, D = x.shape
    N = w.shape[1]
    return pl.pallas_call(
        _rms_matmul_kernel,
        grid=(L // tm, N // tn),
        in_specs=[
            pl.BlockSpec((tm, D), lambda i, j: (i, 0)),
            pl.BlockSpec((1, D), lambda i, j: (0, 0)),
            pl.BlockSpec((D, tn), lambda i, j: (0, j)),
        ],
        out_specs=pl.BlockSpec((tm, tn), lambda i, j: (i, j)),
        out_shape=jax.ShapeDtypeStruct((L, N), F32),
        scratch_shapes=[pltpu.VMEM((tm, D), BF16)],
        compiler_params=_cparams(("arbitrary", "arbitrary")),
        name="rms_proj",
    )(x, gain.reshape(1, D), w)


def _rope_slab(x, cos, sin):
    d = lax.broadcasted_iota(jnp.int32, x.shape, 1) & (NSA_DH - 1)
    half = ROPE_DIMS // 2
    partner = jnp.where(d < half, pltpu.roll(x, LANES - half, 1), pltpu.roll(x, half, 1))
    return x * cos + partner * sin


def _nsa_prep_kernel(x_ref, g_ref, cos_ref, sin_ref, bsum_ref, qT_ref, ksaug_ref, kwa_ref, vsx_ref, vwx_ref, *, n_grp):
    i = pl.program_id(0)
    tm = x_ref.shape[0]
    cos = cos_ref[...]
    sin = sin_ref[...]
    bsum = bsum_ref[...]
    lane = lax.broadcasted_iota(jnp.int32, (tm, LANES), 1)
    tiles = range(tm // TQ)

    def normed(s):
        sl = slice(s * LANES, (s + 1) * LANES)
        x = x_ref[:, sl]
        ms = _dot(x * x, bsum, precision=HIGHEST)
        return _rope_slab(x * lax.rsqrt(ms + EPS) * g_ref[:, sl], cos, sin)

    def per_head(y):
        return y, pltpu.roll(y, NSA_DH, 1)

    for s in range((C_KS - C_Q) // LANES):
        y = normed(s) * (NSA_DH ** -0.5 * LOG2_E)
        hk, g0 = divmod(s, NSA_G // 2)
        for t in tiles:
            yT = y[t * TQ:(t + 1) * TQ, :].T.astype(BF16)
            for half in range(2):
                g = 2 * g0 + half
                qT_ref[hk, t, :, g * TQ:(g + 1) * TQ] = yT[half * NSA_DH:(half + 1) * NSA_DH, :]

    blk = ((i * tm + lax.broadcasted_iota(jnp.int32, (tm, LANES), 0)) // SLC_BLOCK) % n_grp
    onehot = jnp.where(lane - NSA_DH == blk, 1.0, 0.0)
    for hk, y in enumerate(per_head(normed(C_KS // LANES))):
        ksaug_ref[hk] = jnp.where(lane < NSA_DH, y, onehot).astype(BF16)
    for hk, y in enumerate(per_head(normed(C_KW // LANES))):
        kwa_ref[hk] = jnp.where(lane < NSA_DH, y, 0.0).astype(BF16)

    ones_rows = jnp.where(lax.broadcasted_iota(jnp.int32, (V_ROWS - NSA_DH, tm), 0) == 0, 1.0, 0.0).astype(BF16)
    for c0, v_ref in ((C_VS, vsx_ref), (C_VW, vwx_ref)):
        x = x_ref[:, c0:c0 + LANES]
        for t in tiles:
            xT = x[t * TQ:(t + 1) * TQ, :].T.astype(BF16)
            for hk in range(NSA_KV):
                v_ref[hk, 0:NSA_DH, t * TQ:(t + 1) * TQ] = xT[hk * NSA_DH:(hk + 1) * NSA_DH, :]
        for hk in range(NSA_KV):
            v_ref[hk, NSA_DH:V_ROWS, :] = ones_rows


def _nsa_prep(proj, gains, cos, sin, n_grp, tm=512):
    L = proj.shape[0]
    W = NSA_PREP_W
    HK = NSA_KV
    GW = NSA_G * TQ
    bsum = jnp.asarray(np.kron(np.eye(2), np.ones((NSA_DH, NSA_DH))) / NSA_DH, F32)
    rows = pl.BlockSpec((HK, tm, LANES), lambda i: (0, i, 0))
    vals = pl.BlockSpec((HK, V_ROWS, tm), lambda i: (0, 0, i))
    return pl.pallas_call(
        functools.partial(_nsa_prep_kernel, n_grp=n_grp),
        grid=(L // tm,),
        in_specs=[
            pl.BlockSpec((tm, W), lambda i: (i, 0)),
            pl.BlockSpec((1, W), lambda i: (0, 0)),
            pl.BlockSpec((tm, LANES), lambda i: (i, 0)),
            pl.BlockSpec((tm, LANES), lambda i: (i, 0)),
            pl.BlockSpec((LANES, LANES), lambda i: (0, 0)),
        ],
        out_specs=[pl.BlockSpec((HK, tm // TQ, NSA_DH, GW), lambda i: (0, i, 0, 0)), rows, rows, vals, vals],
        out_shape=[jax.ShapeDtypeStruct((HK, L // TQ, NSA_DH, GW), BF16),
                   jax.ShapeDtypeStruct((HK, L, LANES), BF16), jax.ShapeDtypeStruct((HK, L, LANES), BF16),
                   jax.ShapeDtypeStruct((HK, V_ROWS, L), BF16), jax.ShapeDtypeStruct((HK, V_ROWS, L), BF16)],
        compiler_params=_cparams(("arbitrary",)),
        name="nsa_prep",
    )(proj, gains, cos, sin, bsum)


def _cmp_kernel(xk_ref, xv_ref, pe_ref, w1k_ref, w2k_ref, w1v_ref, w2v_ref, gk_ref, cos_ref, sin_ref, bsum_ref,
                kc_ref, vcx_ref):
    nb = kc_ref.shape[1]
    half_w = CMP_STRIDE * LANES

    def mlp(x_ref, w1_ref, w2_ref):
        hb = jnp.concatenate([x_ref[pl.ds(l, nb, stride=CMP_STRIDE), :] for l in range(CMP_STRIDE)], axis=1)
        a = (hb + pe_ref[:, 0:half_w]).astype(BF16)
        b = (hb + pe_ref[:, half_w:2 * half_w]).astype(BF16)
        p1 = _dot(a, w1_ref[0:half_w, :])
        p2 = _dot(b, w1_ref[half_w:2 * half_w, :])
        h = p1 + pltpu.roll(p2, nb - 1, 0)
        return _dot(_gelu(h).astype(BF16), w2_ref[...])

    rid = lax.broadcasted_iota(jnp.int32, (nb, LANES), 0)
    real = rid < nb - 1
    kc = jnp.where(real, mlp(xk_ref, w1k_ref, w2k_ref), 0.0)
    vc = jnp.where(real, mlp(xv_ref, w1v_ref, w2v_ref), 0.0)
    ms = _dot(kc * kc, bsum_ref[...], precision=HIGHEST)
    kcn = _rope_slab(kc * lax.rsqrt(ms + EPS) * gk_ref[...], cos_ref[...], sin_ref[...])
    kc_ref[0] = kcn[:, 0:NSA_DH].astype(BF16)
    kc_ref[1] = pltpu.roll(kcn, NSA_DH, 1)[:, 0:NSA_DH].astype(BF16)
    ones_rows = jnp.where(lax.broadcasted_iota(jnp.int32, (V_ROWS - NSA_DH, nb), 0) == 0, 1.0, 0.0).astype(BF16)
    for t in range(nb // LANES):
        vT = vc[t * LANES:(t + 1) * LANES, :].T.astype(BF16)
        for hk in range(NSA_KV):
            vcx_ref[hk, 0:NSA_DH, t * LANES:(t + 1) * LANES] = vT[hk * NSA_DH:(hk + 1) * NSA_DH, :]
    for hk in range(NSA_KV):
        vcx_ref[hk, NSA_DH:V_ROWS, :] = ones_rows


def _nsa_compress(proj, pe, w1k, w2k, w1v, w2v, gk, cos, sin):
    L = proj.shape[0]
    nb = L // CMP_STRIDE
    bsum = jnp.asarray(np.kron(np.eye(2), np.ones((NSA_DH, NSA_DH))) / NSA_DH, F32)
    once = lambda shape, idx: pl.BlockSpec(shape, lambda i: idx, pipeline_mode=pl.Buffered(1))
    full = lambda a: once(a.shape, (0,) * a.ndim)
    return pl.pallas_call(
        _cmp_kernel,
        grid=(1,),
        in_specs=[once((L, LANES), (0, C_KC // LANES)), once((L, LANES), (0, C_VC // LANES)),
                  full(pe), full(w1k), full(w2k), full(w1v), full(w2v), full(gk), full(cos), full(sin), full(bsum)],
        out_specs=[pl.BlockSpec((NSA_KV, nb, NSA_DH), lambda i: (0, 0, 0)),
                   pl.BlockSpec((NSA_KV, V_ROWS, nb), lambda i: (0, 0, 0))],
        out_shape=[jax.ShapeDtypeStruct((NSA_KV, nb, NSA_DH), BF16),
                   jax.ShapeDtypeStruct((NSA_KV, V_ROWS, nb), BF16)],
        compiler_params=_cparams(("arbitrary",)),
        name="nsa_compress",
    )(proj, proj, pe, w1k, w2k, w1v, w2v, gk, cos, sin, bsum)


def _nsa_attn_kernel(qT_ref, kc_ref, vcx_ref, smap_ref, ksaug_ref, vsx_ref, kwp_ref, vwxp_ref, gT_ref,
                     o_ref, qaug_ref, qwin_ref, ss_ref, mt_ref, m_ref, acc_ref, sc_ref, imp_ref, *, n_grp):
    i = pl.program_id(0)
    s0 = i * TQ
    hks = range(qT_ref.shape[0])
    nb = kc_ref.shape[1]
    ns = qaug_ref.shape[1] * n_grp
    GW = NSA_G * TQ
    qT = [qT_ref[h, 0] for h in hks]
    tq = s0 + (lax.broadcasted_iota(jnp.int32, (1, GW), 1) & (TQ - 1))

    cch = min(CMP_CHUNK, nb)
    n_cch = (s0 + TQ - CMP_LEN) // (CMP_STRIDE * cch) + 1
    nthr = (tq - (CMP_LEN - 1)) // CMP_STRIDE

    def cmp_scores(c, mcs):
        r0c = pl.multiple_of(c * cch, cch)
        s = [_dot(kc_ref[h, pl.ds(r0c, cch), :], qT[h]) for h in hks]
        nrow = r0c + lax.broadcasted_iota(jnp.int32, (cch, GW), 0)
        out = []
        for h in hks:
            sh = jnp.where(nrow <= nthr, s[h], NEG_BIG)
            sc_ref[h, pl.ds(r0c, cch), :] = sh
            out.append(jnp.maximum(mcs[h], jnp.max(sh, axis=0, keepdims=True)))
        return tuple(out)

    mc = lax.fori_loop(0, n_cch, cmp_scores, tuple(jnp.full((1, GW), NEG_BIG, F32) for _ in hks))
    imp_ref[...] = jnp.zeros_like(imp_ref)
    ones16 = jnp.ones((16, cch), BF16)
    VX = vcx_ref.shape[1]

    sm = smap_ref[...]
    sm_rows = sm.shape[0]
    blocks_per_chunk = cch * CMP_STRIDE // SLC_BLOCK

    def cmp_accum(c, carry):
        r0c = pl.multiple_of(c * cch, cch)
        j0 = pl.multiple_of(c * blocks_per_chunk, blocks_per_chunk)
        parts = [_split2(jnp.exp2(sc_ref[h, pl.ds(r0c, cch), :] - mc[h])) for h in hks]
        for h in hks:
            imp_ref[h, pl.ds(j0, sm_rows), :] += _dot(sm, parts[h][0]) + _dot(sm, parts[h][1])
        return tuple((carry[h][0] + _dot(vcx_ref[h, :, pl.ds(r0c, cch)], parts[h][0]),
                      carry[h][1] + _dot(ones16, parts[h][1])) for h in hks)

    cacc = lax.fori_loop(0, n_cch, cmp_accum,
                         tuple((jnp.zeros((VX, GW), F32), jnp.zeros((16, GW), F32)) for _ in hks))
    inv_l = [jnp.where(tq >= CMP_LEN - 1, 1.0 / (cacc[h][0][NSA_DH:NSA_DH + 1] + cacc[h][1][0:1]), 0.0) for h in hks]
    ocT = [cacc[h][0][0:NSA_DH] * inv_l[h] for h in hks]

    r0 = pl.multiple_of(s0, TQ)
    rel = lax.broadcasted_iota(jnp.int32, (TQ, GW), 0)
    qrel = tq - s0
    flag_row = lax.broadcasted_iota(jnp.int32, (KW_LANES - NSA_DH, GW), 0) == 0
    for h in hks:
        qwin_ref[h, 0:NSA_DH, :] = qT[h]
        qwin_ref[h, NSA_DH:KW_LANES, :] = jnp.where(flag_row, NEG_BIG, 0.0).astype(BF16)
    qwin = [qwin_ref[h] for h in hks]
    edge = [_dot(kwp_ref[h, pl.ds(r0, TQ), :], qwin[h]) for h in hks]
    mid = [_dot(kwp_ref[h, pl.ds(r0 + TQ, WINDOW - TQ), :], qwin[h]) for h in hks]
    diag = [_dot(kwp_ref[h, pl.ds(r0 + WINDOW, TQ), :], qwin[h]) for h in hks]
    edge16 = [jnp.where(rel > qrel, edge[h], NEG_BIG).astype(BF16) for h in hks]
    diag16 = [jnp.where(rel <= qrel, diag[h], NEG_BIG).astype(BF16) for h in hks]
    mid16 = [mid[h].astype(BF16) for h in hks]
    mw = [jnp.maximum(jnp.maximum(jnp.max(edge16[h], axis=0, keepdims=True), jnp.max(mid16[h], axis=0, keepdims=True)),
                      jnp.max(diag16[h], axis=0, keepdims=True)) for h in hks]
    accw = [(_dot(vwxp_ref[h, :, pl.ds(r0, TQ)], jnp.exp2(edge16[h] - mw[h]))
             + _dot(vwxp_ref[h, :, pl.ds(r0 + TQ, WINDOW - TQ)], jnp.exp2(mid16[h] - mw[h]))
             + _dot(vwxp_ref[h, :, pl.ds(r0 + WINDOW, TQ)], jnp.exp2(diag16[h] - mw[h]))) for h in hks]
    owT = [accw[h][0:NSA_DH] / accw[h][NSA_DH:NSA_DH + 1] for h in hks]
    sd = [_dot(ksaug_ref[h, pl.ds(r0, TQ), 0:NSA_DH], qT[h]) for h in hks]
    sd16 = [jnp.where(rel <= qrel, sd[h], NEG_BIG).astype(BF16) for h in hks]

    imp = []
    for h in hks:
        tot = imp_ref[h, 0:ns, 0:TQ] * inv_l[h][:, 0:TQ]
        for g in range(1, NSA_G):
            tot = tot + imp_ref[h, 0:ns, g * TQ:(g + 1) * TQ] * inv_l[h][:, g * TQ:(g + 1) * TQ]
        imp.append(tot)

    jrow = lax.broadcasted_iota(jnp.int32, (ns, TQ), 0)
    cur = (s0 + lax.broadcasted_iota(jnp.int32, (1, TQ), 1)) // SLC_BLOCK
    forced = jnp.where(jrow == 0, 0.0, jnp.where(jrow == cur, 0.0, jnp.where(jrow == cur - 1, 0.0, NEG_BIG)))
    val = [jnp.where(jrow >= 1, jnp.where(jrow <= cur - 2, imp[h], -jnp.inf), -jnp.inf) for h in hks]
    for _ in range(min(SLC_TOPK, ns) - 3):
        mx = [jnp.max(val[h], axis=0, keepdims=True) for h in hks]
        idx = [jnp.min(jnp.where(val[h] == mx[h], jrow, ns), axis=0, keepdims=True) for h in hks]
        val = [jnp.where(jrow == idx[h], -jnp.inf, val[h]) for h in hks]
    bias = [jnp.where(jrow >= 1, jnp.where(jrow <= cur - 2, jnp.where(val[h] == -jnp.inf, 0.0, NEG_BIG), forced), forced)
            for h in hks]
    own = jrow // 2 == i
    for h in hks:
        bias16 = jnp.where(own, NEG_BIG, bias[h]).astype(BF16)
        for grp in range(ns // n_grp):
            qaug_ref[h, grp, 0:NSA_DH, :] = qT[h]
            for g in range(NSA_G):
                qaug_ref[h, grp, NSA_DH:NSA_DH + n_grp, g * TQ:(g + 1) * TQ] = bias16[grp * n_grp:(grp + 1) * n_grp, :]

    tiles_per_grp = (n_grp * SLC_BLOCK) // TK
    n_body = jnp.maximum((s0 + SEL_UNROLL * TK - 1) // (SEL_UNROLL * TK), 1)

    def scores(kt, slot):
        k0 = pl.multiple_of(kt * TK, TK)
        grp = kt // tiles_per_grp
        s = [_dot(ksaug_ref[h, pl.ds(k0, TK), :], qaug_ref[h, grp]) for h in hks]
        for h in hks:
            sb = s[h].astype(BF16)
            ss_ref[h, slot] = sb
            mt_ref[h, slot] = jnp.max(sb, axis=0, keepdims=True).astype(F32)

    def online_update(sb, mt, v):
        m = [m_ref[h] for h in hks]
        mn = [jnp.maximum(m[h], mt[h]) for h in hks]
        pv = [_dot(v[h], jnp.exp2(sb[h] - mn[h].astype(BF16))) for h in hks]
        for h in hks:
            acc_ref[h] = jnp.exp2(m[h] - mn[h]) * acc_ref[h] + pv[h]
            m_ref[h] = mn[h]

    def accumulate(kt, slot):
        k0 = pl.multiple_of(kt * TK, TK)
        online_update([ss_ref[h, slot] for h in hks], [mt_ref[h, slot] for h in hks],
                      [vsx_ref[h, :, pl.ds(k0, TK)] for h in hks])

    m_ref[...] = jnp.full(m_ref.shape, NEG_BIG, F32)
    acc_ref[...] = jnp.zeros_like(acc_ref)

    scores(0, 0)

    def tile_group(base, last):
        for u in range(SEL_UNROLL):
            if not (last and u == SEL_UNROLL - 1):
                scores(base + u + 1, (u + 1) % 2)
            accumulate(base + u, u % 2)

    def loop_body(j, carry):
        tile_group(SEL_UNROLL * j, last=False)
        return carry

    lax.fori_loop(0, n_body - 1, loop_body, 0)

    tile_group(SEL_UNROLL * (n_body - 1), last=True)

    online_update(sd16, [jnp.max(sd16[h], axis=0, keepdims=True).astype(F32) for h in hks],
                  [vsx_ref[h, :, pl.ds(r0, TQ)] for h in hks])
    osT = [acc_ref[h, 0:NSA_DH, :] / acc_ref[h, NSA_DH:NSA_DH + 1, :] for h in hks]

    for h in hks:
        gates = _sigmoid(gT_ref[h])
        outs = []
        for g in range(NSA_G):
            sl = slice(g * TQ, (g + 1) * TQ)
            outs.append(gates[3 * g:3 * g + 1] * ocT[h][:, sl] + gates[3 * g + 1:3 * g + 2] * osT[h][:, sl]
                        + gates[3 * g + 2:3 * g + 3] * owT[h][:, sl])
        for g in range(0, NSA_G, 2):
            c0 = (h * NSA_G + g) * NSA_DH
            o_ref[:, c0:c0 + 2 * NSA_DH] = jnp.concatenate([outs[g], outs[g + 1]], axis=0).T


def _nsa_attention(qT, kc, vcx, smapT, ksaug, vsx, kwp, vwxp, gT, n_grp):
    HK, nQ, _, GW = qT.shape
    nb = kc.shape[1]
    L = vsx.shape[2]
    ns = L // SLC_BLOCK
    KA = ksaug.shape[2]
    scratch = [pltpu.VMEM((HK, ns // n_grp, KA, GW), BF16),
               pltpu.VMEM((HK, KW_LANES, GW), BF16),
               pltpu.VMEM((HK, 2, TK, GW), BF16),
               pltpu.VMEM((HK, 2, 1, GW), F32),
               pltpu.VMEM((HK, 1, GW), F32),
               pltpu.VMEM((HK, V_ROWS, GW), F32),
               pltpu.VMEM((HK, nb, GW), F32),
               pltpu.VMEM((HK, ns + 8, GW), F32)]
    resident = lambda a: pl.BlockSpec(a.shape, lambda i: (0,) * a.ndim, pipeline_mode=pl.Buffered(1))
    return pl.pallas_call(
        functools.partial(_nsa_attn_kernel, n_grp=n_grp),
        grid=(nQ,),
        in_specs=[
            pl.BlockSpec((HK, 1, NSA_DH, GW), lambda i: (0, i, 0, 0)),
            resident(kc), resident(vcx), resident(smapT), resident(ksaug), resident(vsx), resident(kwp),
            resident(vwxp),
            pl.BlockSpec((HK, 16, TQ), lambda i: (0, 0, i)),
        ],
        out_specs=pl.BlockSpec((TQ, NSA_HEADS * NSA_DH), lambda i: (i, 0)),
        out_shape=jax.ShapeDtypeStruct((nQ * TQ, NSA_HEADS * NSA_DH), F32),
        scratch_shapes=scratch,
        compiler_params=_cparams(("arbitrary",)),
        name="nsa_attn",
    )(qT, kc, vcx, smapT, ksaug, vsx, kwp, vwxp, gT)


def _s5_kernel(u_ref, lbr_ref, lbi_ref, ctr_ref, cti_ref, btr_ref, bti_ref, bsr_ref, bsi_ref, y_ref, lm_ref):
    T = S5_CHUNK
    W = T * S5_CH
    ar = lbr_ref[0]
    ai = lbi_ref[0]
    delta = lax.broadcasted_iota(jnp.int32, (W, S5_STATE), 0) // S5_CH

    def powers(e):
        pr = jnp.ones((W, S5_STATE), F32)
        pi = jnp.zeros((W, S5_STATE), F32)
        fr, fi = ar, ai
        for b in range(T.bit_length() - 1):
            bit = ((e >> b) & 1) == 1
            nr = pr * fr - pi * fi
            ni = pr * fi + pi * fr
            pr = jnp.where(bit, nr, pr)
            pi = jnp.where(bit, ni, pi)
            fr, fi = fr * fr - fi * fi, 2.0 * fr * fi
        return pr, pi, fr, fi

    pwr, pwi, aTr, aTi = powers(delta)
    rvr, rvi, _, _ = powers(T - 1 - delta)
    ctr, cti = ctr_ref[0], cti_ref[0]
    car = ctr * pwr - cti * pwi
    cai = ctr * pwi + cti * pwr

    kw = _dot_nt(bsr_ref[0], car, HIGHEST) - _dot_nt(bsi_ref[0], cai, HIGHEST)
    lane = lax.broadcasted_iota(jnp.int32, (S5_CH, W), 1)
    for tau in range(T):
        sh = S5_CH * tau
        blk = kw if tau == 0 else jnp.where(lane >= sh, pltpu.roll(kw, sh, 1), 0.0)
        lm_ref[sh:sh + S5_CH, :] = blk.astype(BF16)

    u = u_ref[0]
    nc = u.shape[0]
    y = _dot(u, lm_ref[...])

    btr, bti = btr_ref[0], bti_ref[0]
    sr = _dot(u, (rvr * btr - rvi * bti).astype(BF16))
    si = _dot(u, (rvr * bti + rvi * btr).astype(BF16))
    rowc = lax.broadcasted_iota(jnp.int32, (nc, S5_STATE), 0)
    fr, fi = aTr, aTi
    step = 1
    while step < nc:
        shr = jnp.where(rowc >= step, pltpu.roll(sr, step, 0), 0.0)
        shi = jnp.where(rowc >= step, pltpu.roll(si, step, 0), 0.0)
        sr, si = sr + fr * shr - fi * shi, si + fr * shi + fi * shr
        fr, fi = fr * fr - fi * fi, 2.0 * fr * fi
        step *= 2
    xr = jnp.where(rowc >= 1, pltpu.roll(sr, 1, 0), 0.0)
    xi = jnp.where(rowc >= 1, pltpu.roll(si, 1, 0), 0.0)
    c1r = car * ar - cai * ai
    c1i = car * ai + cai * ar
    y = y + _dot_nt(xr.astype(BF16), c1r.astype(BF16)) - _dot_nt(xi.astype(BF16), c1i.astype(BF16))
    y_ref[0] = y


def _s5_scan(ug, lbr, lbi, ctr, cti, btr, bti, bsr, bsi):
    G, nc, W = ug.shape
    grp = lambda shape: pl.BlockSpec((1,) + shape, lambda g: (g, 0, 0))
    return pl.pallas_call(
        _s5_kernel,
        grid=(G,),
        in_specs=[grp((nc, W)), grp((1, S5_STATE)), grp((1, S5_STATE)),
                  grp((W, S5_STATE)), grp((W, S5_STATE)), grp((W, S5_STATE)), grp((W, S5_STATE)),
                  grp((S5_CH, S5_STATE)), grp((S5_CH, S5_STATE))],
        out_specs=grp((nc, W)),
        out_shape=jax.ShapeDtypeStruct((G, nc, W), F32),
        scratch_shapes=[pltpu.VMEM((W, W), BF16)],
        compiler_params=_cparams(("arbitrary",)),
        name="s5_scan",
    )(ug, lbr, lbi, ctr, cti, btr, bti, bsr, bsi)


def _small_kernel(x_ref, p_ref, o_ref):
    x = x_ref[...]
    lane = lax.broadcasted_iota(jnp.int32, x.shape, 1)
    z = x + p_ref[0:1, :]
    softplus = jnp.maximum(z, 0.0) + jnp.log(1.0 + jnp.exp(-jnp.abs(z)))
    gdec = p_ref[1:2, :] * softplus
    beta = _sigmoid(x)
    o_ref[...] = jnp.where(lane < SM_G, x, jnp.where(lane < SM_BETA, gdec, beta))


def _small(proj, params, tm=1024):
    L = proj.shape[0]
    cb = C_SM // LANES
    return pl.pallas_call(
        _small_kernel,
        grid=(L // tm,),
        in_specs=[pl.BlockSpec((tm, LANES), lambda i: (i, cb)), pl.BlockSpec((8, LANES), lambda i: (0, 0))],
        out_specs=pl.BlockSpec((tm, LANES), lambda i: (i, 0)),
        out_shape=jax.ShapeDtypeStruct((L, LANES), F32),
        compiler_params=_cparams(("arbitrary",)),
        name="small_cols",
    )(proj, params)


GDN_CB = 8


def _split2(x):
    hi = x.astype(BF16)
    return hi, (x - hi.astype(F32)).astype(BF16)


def _dot_split(a, b):
    ah, al = _split2(a)
    bh, bl = _split2(b)
    return _dot(ah, bh) + (_dot(ah, bl) + _dot(al, bh))


def _gdn_local_kernel(xq_ref, xk_ref, xv_ref, cwq_ref, cwk_ref, cwv_ref, sm_ref,
                      val_ref, kcum_ref, qg_ref, kdT_ref, qk_ref, gl_ref, carry_ref, hs_ref):
    C = GDN_CHUNK
    R = xq_ref.shape[0]

    @pl.when(pl.program_id(1) == 0)
    def _():
        carry_ref[...] = jnp.zeros_like(carry_ref)

    def conv_silu(x_ref, cw_ref, slot):
        x = x_ref[...]
        hs_ref[slot, 0:8, :] = carry_ref[slot]
        hs_ref[slot, 8:R + 8, :] = x
        carry_ref[slot] = x[R - 8:R, :]
        taps = cw_ref.shape[0]
        y = cw_ref[taps - 1:taps, :] * x
        for back in range(1, taps):
            y = y + cw_ref[taps - 1 - back:taps - back, :] * hs_ref[slot, 8 - back:8 - back + R, :]
        half_y = 0.5 * y
        return half_y * (1.0 + jnp.tanh(half_y))

    def l2norm(y):
        return y * lax.rsqrt(jnp.sum(y * y, axis=-1, keepdims=True) + EPS)

    q_all = l2norm(conv_silu(xq_ref, cwq_ref, 0)) * (GDN_DH ** -0.5)
    k_all = l2norm(conv_silu(xk_ref, cwk_ref, 1))
    v_all = conv_silu(xv_ref, cwv_ref, 2)
    chunks = range(GDN_CB)
    rows = [slice(c * C, (c + 1) * C) for c in chunks]
    ii = lax.broadcasted_iota(jnp.int32, (C, C), 0)
    jj = lax.broadcasted_iota(jnp.int32, (C, C), 1)
    causal = ii >= jj
    strict = ii > jj
    eye = (ii == jj).astype(F32)
    tril16 = causal.astype(BF16)
    row2 = lax.broadcasted_iota(jnp.int32, (C, 2 * C), 0)
    col2 = lax.broadcasted_iota(jnp.int32, (C, 2 * C), 1)
    keep = row2 > jnp.where(col2 < C, -1, col2 - C)

    head = pl.program_id(0)
    sm = sm_ref[...]
    sm_lane = lax.broadcasted_iota(jnp.int32, sm.shape, 1)
    g_col = jnp.sum(jnp.where(sm_lane == SM_G + head, sm, 0.0), axis=1, keepdims=True)
    b_col = jnp.sum(jnp.where(sm_lane == SM_BETA + head, sm, 0.0), axis=1, keepdims=True)

    k = [k_all[rs, :] for rs in rows]
    beta = [b_col[rs, :] for rs in rows]
    res = []
    for rs in rows:
        x = jnp.where(keep, jnp.broadcast_to(g_col[rs, :], (C, 2 * C)), 0.0)
        hi = x.astype(BF16)
        r1 = x - hi.astype(F32)
        mid = r1.astype(BF16)
        lo = (r1 - mid.astype(F32)).astype(BF16)
        res.append(_dot(tril16, hi) + (_dot(tril16, mid) + _dot(tril16, lo)))
    gc = [jnp.broadcast_to(r[:, 0:1], (C, GDN_DH)) for r in res]
    decay = [jnp.where(causal, jnp.exp(r[:, C:2 * C]), 0.0) for r in res]
    kb = [kc * bc for kc, bc in zip(k, beta)]
    k16 = [kc.astype(BF16) for kc in k]
    a = [jnp.where(strict, _dot_nt(kbc.astype(BF16), kc16) * dc, 0.0) for kbc, kc16, dc in zip(kb, k16, decay)]
    t = [eye - ac for ac in a]
    p = a
    for _ in range(int(math.log2(C)) - 1):
        p = [_dot_split(pc, pc) for pc in p]
        t = [tc + _dot_split(tc, pc) for tc, pc in zip(t, p)]
    t16 = [tc.astype(BF16) for tc in t]
    egc = [jnp.exp(gcc) for gcc in gc]
    val = [_dot(tc, (v_all[rs, :] * bc).astype(BF16)) for tc, rs, bc in zip(t16, rows, beta)]
    kcum = [_dot(tc, (kbc * ec).astype(BF16)) for tc, kbc, ec in zip(t16, kb, egc)]
    qk = [_dot_nt(q_all[rs, :].astype(BF16), kc16) * dc for rs, kc16, dc in zip(rows, k16, decay)]
    kd = []
    for c in chunks:
        rs = rows[c]
        val_ref[rs, :] = val[c]
        kcum_ref[rs, :] = kcum[c].astype(BF16)
        qk_ref[0, rs, :] = qk[c].astype(BF16)
        glast = gc[c][C - 1:C, :]
        qg_ref[rs, :] = (q_all[rs, :] * egc[c]).astype(BF16)
        kd.append(k[c] * jnp.exp(glast - gc[c]))
        gl_ref[0, c:c + 1, :] = jnp.exp(glast)
    for c in range(0, GDN_CB, 2):
        kdT_ref[0, c // 2] = jnp.concatenate([kd[c], kd[c + 1]], axis=0).T.astype(BF16)


def _gdn_local(proj, conv_w, small):
    L = proj.shape[0]
    H = GDN_HEADS
    R = GDN_CB * GDN_CHUNK
    cb = C_QKV // GDN_DH
    taps = conv_w.shape[0]
    col = lambda off: pl.BlockSpec((R, GDN_DH), lambda h, i: (i, cb + off + h))
    cw = lambda off: pl.BlockSpec((taps, GDN_DH), lambda h, i: (0, off + h))
    big = pl.BlockSpec((R, GDN_DH), lambda h, i: (i, h))
    return pl.pallas_call(
        _gdn_local_kernel,
        grid=(H, L // R),
        in_specs=[col(0), col(H), col(2 * H), cw(0), cw(H), cw(2 * H),
                  pl.BlockSpec((R, LANES), lambda h, i: (i, 0))],
        out_specs=[big, big, big,
                   pl.BlockSpec((1, GDN_CB // 2, GDN_DH, 2 * GDN_CHUNK), lambda h, i: (h, i, 0, 0)),
                   pl.BlockSpec((1, R, GDN_CHUNK), lambda h, i: (h, i, 0)),
                   pl.BlockSpec((1, GDN_CB, GDN_DH), lambda h, i: (h, i, 0))],
        out_shape=[jax.ShapeDtypeStruct((L, GDN_WIDTH), F32),
                   jax.ShapeDtypeStruct((L, GDN_WIDTH), BF16),
                   jax.ShapeDtypeStruct((L, GDN_WIDTH), BF16),
                   jax.ShapeDtypeStruct((H, L // (2 * GDN_CHUNK), GDN_DH, 2 * GDN_CHUNK), BF16),
                   jax.ShapeDtypeStruct((H, L, GDN_CHUNK), BF16),
                   jax.ShapeDtypeStruct((H, L // GDN_CHUNK, GDN_DH), F32)],
        scratch_shapes=[pltpu.VMEM((3, 8, GDN_DH), F32), pltpu.VMEM((3, R + 8, GDN_DH), F32)],
        compiler_params=_cparams(("arbitrary", "arbitrary")),
        name="gdn_local",
    )(proj, proj, proj, conv_w, conv_w, conv_w, small)


def _gdn_scan_kernel(val_ref, kcum_ref, qg_ref, qk_ref, kdT_ref, gl_ref, z_ref, gn_ref, o_ref, st_ref):
    @pl.when(pl.program_id(0) == 0)
    def _():
        st_ref[...] = jnp.zeros_like(st_ref)

    C = GDN_CHUNK
    gn = gn_ref[...]
    heads = range(GDN_HEADS)
    cols = [slice(h * GDN_DH, (h + 1) * GDN_DH) for h in heads]
    state = [st_ref[h] for h in heads]
    for c in range(GDN_CB):
        rs = slice(c * C, (c + 1) * C)
        sb = [s.astype(BF16) for s in state]
        kcs = [_dot(kcum_ref[rs, cs], s) for cs, s in zip(cols, sb)]
        qgs = [_dot(qg_ref[rs, cs], s) for cs, s in zip(cols, sb)]
        vb = [(val_ref[rs, cs] - x).astype(BF16) for cs, x in zip(cols, kcs)]
        o = [x + _dot(qk_ref[h, rs, :], v) for h, x, v in zip(heads, qgs, vb)]
        kcols = slice((c % 2) * C, (c % 2 + 1) * C)
        state = [s * gl_ref[h, c:c + 1, :] + _dot(kdT_ref[h, c // 2, :, kcols], v)
                 for h, s, v in zip(heads, state, vb)]
        for h in heads:
            on = o[h] * lax.rsqrt(jnp.mean(o[h] * o[h], axis=-1, keepdims=True) + EPS) * gn
            z = z_ref[rs, cols[h]]
            o_ref[rs, cols[h]] = on * (z * _sigmoid(z))
    for h in heads:
        st_ref[h] = state[h]


def _gdn_scan(val, kcum, qg, qk, kdT, gl, proj, gnorm):
    L = val.shape[0]
    H = GDN_HEADS
    R = GDN_CB * GDN_CHUNK
    row = pl.BlockSpec((R, GDN_WIDTH), lambda i: (i, 0))
    zb = C_Z // GDN_WIDTH
    return pl.pallas_call(
        _gdn_scan_kernel,
        grid=(L // R,),
        in_specs=[row, row, row,
                  pl.BlockSpec((H, R, GDN_CHUNK), lambda i: (0, i, 0)),
                  pl.BlockSpec((H, GDN_CB // 2, GDN_DH, 2 * GDN_CHUNK), lambda i: (0, i, 0, 0)),
                  pl.BlockSpec((H, GDN_CB, GDN_DH), lambda i: (0, i, 0)),
                  pl.BlockSpec((R, GDN_WIDTH), lambda i: (i, zb)),
                  pl.BlockSpec((1, GDN_DH), lambda i: (0, 0))],
        out_specs=row,
        out_shape=jax.ShapeDtypeStruct((L, GDN_WIDTH), F32),
        scratch_shapes=[pltpu.VMEM((H, GDN_DH, GDN_DH), F32)],
        compiler_params=_cparams(("arbitrary",)),
        name="gdn_scan",
    )(val, kcum, qg, qk, kdT, gl, proj, gnorm)


def _mix_kernel(ya_ref, ys_ref, u_ref, yc_ref, x_ref, ga_ref, gb_ref, d_ref, wglu_ref, wout_ref, o_ref):
    def rms(y, g):
        return y * lax.rsqrt(jnp.mean(y * y, axis=-1, keepdims=True) + EPS) * g

    a = rms(ya_ref[...], ga_ref[...]).astype(BF16)
    yb = _gelu(ys_ref[...] + d_ref[...] * u_ref[...])
    yb = yb * _sigmoid(_dot(yb.astype(BF16), wglu_ref[...]))
    b = rms(yb, gb_ref[...]).astype(BF16)
    na, nb = a.shape[1], b.shape[1]
    acc = x_ref[...] + _dot(a, wout_ref[0:na, :])
    acc = acc + _dot(b, wout_ref[na:na + nb, :])
    o_ref[...] = acc + _dot(yc_ref[...].astype(BF16), wout_ref[na + nb:, :])


def _mix(ya, ys, proj, yc, x, ga, gb, d, wglu, wout, tm=256):
    L, D = x.shape
    ub = C_U // S5_WIDTH
    full = lambda a: pl.BlockSpec(a.shape, lambda i: (0, 0))
    return pl.pallas_call(
        _mix_kernel,
        grid=(L // tm,),
        in_specs=[pl.BlockSpec((tm, ya.shape[1]), lambda i: (i, 0)),
                  pl.BlockSpec((tm, S5_WIDTH), lambda i: (i, 0)),
                  pl.BlockSpec((tm, S5_WIDTH), lambda i: (i, ub)),
                  pl.BlockSpec((tm, GDN_WIDTH), lambda i: (i, 0)),
                  pl.BlockSpec((tm, D), lambda i: (i, 0)),
                  full(ga), full(gb), full(d), full(wglu), full(wout)],
        out_specs=pl.BlockSpec((tm, D), lambda i: (i, 0)),
        out_shape=jax.ShapeDtypeStruct((L, D), F32),
        compiler_params=_cparams(("arbitrary",)),
        name="mix_out",
    )(ya, ys, proj, yc, x, ga, gb, d, wglu, wout)


def _ffn_in_kernel(x_ref, g_ref, wg_ref, wu_ref, cg_ref, cu_ref, bg_ref, bu_ref, o_ref, xn_ref, carry_ref, hs_ref,
                   *, n_valid):
    i = pl.program_id(0)
    j = pl.program_id(1)

    @pl.when(j == 0)
    def _():
        xf = x_ref[...]
        ms = jnp.mean(xf * xf, axis=-1, keepdims=True)
        xn_ref[...] = (xf * lax.rsqrt(ms + EPS) * g_ref[...]).astype(BF16)

    @pl.when(i == 0)
    def _():
        carry_ref[j] = jnp.zeros(carry_ref.shape[1:], F32)

    xn = xn_ref[...]
    tm = xn.shape[0]

    def conv(h, slot, cw_ref, cb_ref):
        hs_ref[slot, 0:8, :] = carry_ref[j, slot]
        hs_ref[slot, 8:tm + 8, :] = h
        carry_ref[j, slot] = h[tm - 8:tm, :]
        taps = cw_ref.shape[0]
        y = cw_ref[taps - 1:taps, :] * h + cb_ref[...]
        for back in range(1, taps):
            y = y + cw_ref[taps - 1 - back:taps - back, :] * hs_ref[slot, 8 - back:8 - back + tm, :]
        return y

    gate = conv(_dot(xn, wg_ref[...]), 0, cg_ref, bg_ref)
    up = conv(_dot(xn, wu_ref[...]), 1, cu_ref, bu_ref)
    half_gate = 0.5 * gate
    act = half_gate * (1.0 + jnp.tanh(half_gate)) * up
    o_ref[...] = act.astype(BF16)

    @pl.when(j == pl.num_programs(1) - 1)
    def _():
        col = j * act.shape[1] + lax.broadcasted_iota(jnp.int32, act.shape, 1)
        o_ref[...] = jnp.where(col < n_valid, act, 0.0).astype(BF16)


def _ffn_in(x, gain, wg, wu, cg, cu, bg, bu, tm=1024, tn=512):
    L, D = x.shape
    n_valid = wg.shape[1]
    nj = pl.cdiv(n_valid, tn)
    taps = cg.shape[0]
    cols = lambda rows: pl.BlockSpec((rows, tn), lambda i, j: (0, j))
    return pl.pallas_call(
        functools.partial(_ffn_in_kernel, n_valid=n_valid),
        grid=(L // tm, nj),
        in_specs=[pl.BlockSpec((tm, D), lambda i, j: (i, 0)),
                  pl.BlockSpec((1, D), lambda i, j: (0, 0)),
                  cols(D), cols(D), cols(taps), cols(taps), cols(1), cols(1)],
        out_specs=pl.BlockSpec((tm, tn), lambda i, j: (i, j)),
        out_shape=jax.ShapeDtypeStruct((L, nj * tn), BF16),
        scratch_shapes=[pltpu.VMEM((tm, D), BF16), pltpu.VMEM((nj, 2, 8, tn), F32),
                        pltpu.VMEM((2, tm + 8, tn), F32)],
        compiler_params=_cparams(("arbitrary", "arbitrary")),
        name="ffn_in",
    )(x, gain.reshape(1, D), wg, wu, cg, cu, bg, bu)


def _ffn_out_kernel(a_ref, w_ref, x_ref, o_ref):
    @pl.when(pl.program_id(1) == 0)
    def _():
        o_ref[...] = x_ref[...]

    o_ref[...] += _dot(a_ref[...], w_ref[...])


def _ffn_out(act, w, x, tm=1024, tk=512):
    L, D = x.shape
    K = act.shape[1]
    return pl.pallas_call(
        _ffn_out_kernel,
        grid=(L // tm, K // tk),
        in_specs=[pl.BlockSpec((tm, tk), lambda i, k: (i, k)),
                  pl.BlockSpec((tk, D), lambda i, k: (k, 0)),
                  pl.BlockSpec((tm, D), lambda i, k: (i, 0))],
        out_specs=pl.BlockSpec((tm, D), lambda i, k: (i, 0)),
        out_shape=jax.ShapeDtypeStruct((L, D), F32),
        compiler_params=_cparams(("arbitrary", "arbitrary")),
        name="ffn_out",
    )(act, w, x)


def _rope_tables(pos):
    half = ROPE_DIMS // 2
    inv = ROPE_THETA ** (-jnp.arange(half, dtype=F32) / half)
    ang = pos.astype(F32)[:, None] * inv[None, :]
    n = pos.shape[0]
    cos = jnp.concatenate([jnp.cos(ang), jnp.cos(ang), jnp.ones((n, NSA_DH - ROPE_DIMS), F32)], axis=1)
    sin = jnp.concatenate([-jnp.sin(ang), jnp.sin(ang), jnp.zeros((n, NSA_DH - ROPE_DIMS), F32)], axis=1)
    return jnp.tile(cos, (1, 2)), jnp.tile(sin, (1, 2))


def _permute_w_in(w):
    sp = np.cumsum([0, 512, 128, 128, 128, 128, 128, 128, 24, 512, 3072, 1024, 8, 8])
    q, kc, vc, ks, vs, kw, vw, gates, u, qkv, z, a, b = [w[:, sp[n]:sp[n + 1]] for n in range(13)]
    pad = jnp.zeros((w.shape[0], PROJ_COLS - C_SM - SM_END), w.dtype)
    return jnp.concatenate([q, ks, kw, vs, vw, z, qkv, u, kc, vc, gates, a, b, pad], axis=1)


def _nsa_mixer(proj, small, p, l):
    L = proj.shape[0]
    HK, G, DH = NSA_KV, NSA_G, NSA_DH
    ns = L // SLC_BLOCK
    nb = L // CMP_STRIDE
    n_grp = LANES - DH
    assert ns % n_grp == 0 and (n_grp * SLC_BLOCK) % (SEL_UNROLL * TK) == 0 and SEL_UNROLL % 2 == 0
    cos, sin = _rope_tables(jnp.arange(L))
    gains = jnp.concatenate([jnp.tile(p['nsa_q_norm'][l], NSA_HEADS), jnp.tile(p['nsa_ks_norm'][l], HK),
                             jnp.tile(p['nsa_kw_norm'][l], HK), jnp.ones((2 * HK * DH,), F32)]).reshape(1, -1)
    qT, ksaug, kwa, vsx, vwx = _nsa_prep(proj, gains, cos, sin, n_grp)
    kpad = jnp.zeros((HK, WINDOW, KW_LANES), BF16).at[:, :, DH].set(1.0)
    kwp = jnp.concatenate([kpad, kwa], axis=1)
    vwxp = jnp.pad(vwx, ((0, 0), (0, 0), (WINDOW, 0)))

    eye = jnp.eye(HK, dtype=F32)
    pe = jnp.tile(p['cmp_pe'][l][:, None, :], (1, HK, 1)).reshape(1, CMP_LEN * HK * DH)
    w1 = lambda name: jnp.einsum('ldf,hg->lhdgf', p[name][l], eye).reshape(CMP_LEN * HK * DH, -1).astype(BF16)
    w2 = lambda name: jnp.einsum('fd,hg->hfgd', p[name][l], eye).reshape(-1, HK * DH).astype(BF16)
    ccos, csin = _rope_tables(jnp.arange(nb) * CMP_STRIDE + CMP_LEN // 2)
    gk = jnp.tile(p['nsa_kc_norm'][l], HK).reshape(1, LANES)
    kc, vcx = _nsa_compress(proj, pe, w1('cmp_k_w1'), w2('cmp_k_w2'), w1('cmp_v_w1'), w2('cmp_v_w2'), gk, ccos, csin)

    cch = min(CMP_CHUNK, nb)
    cmp_start = np.arange(cch) * CMP_STRIDE
    slc_start = np.arange(cch * CMP_STRIDE // SLC_BLOCK + 8) * SLC_BLOCK
    smapT = jnp.asarray((cmp_start[None, :] <= slc_start[:, None] + SLC_BLOCK - 1)
                        & (cmp_start[None, :] + CMP_LEN - 1 >= slc_start[:, None]), BF16)

    gT = jnp.pad(small[:, :HK * G * 3].reshape(L, HK, G * 3).transpose(1, 2, 0), ((0, 0), (0, 16 - G * 3), (0, 0)))
    return _nsa_attention(qT, kc, vcx, smapT, ksaug, vsx, kwp, vwxp, gT, n_grp)


def _s5_mixer(proj, p, l):
    L = proj.shape[0]
    T = S5_CHUNK
    nc = L // T
    G, H, P = S5_GROUPS, S5_CH, S5_STATE
    lam_re, lam_im = p['s5_lam_re'][l], p['s5_lam_im'][l]
    dt = jnp.exp(p['s5_log_dt'][l])[:, None]
    mag = jnp.exp(lam_re * dt)
    lb_re = mag * jnp.cos(lam_im * dt)
    lb_im = mag * jnp.sin(lam_im * dt)
    den = lam_re * lam_re + lam_im * lam_im
    nr, ni = lb_re - 1.0, lb_im
    coef_re = (nr * lam_re + ni * lam_im) / den
    coef_im = (ni * lam_re - nr * lam_im) / den
    b_re, b_im = p['s5_b_re'][l], p['s5_b_im'][l]
    bb_re = coef_re[..., None] * b_re - coef_im[..., None] * b_im
    bb_im = coef_re[..., None] * b_im + coef_im[..., None] * b_re
    bsr, bsi = bb_re.transpose(0, 2, 1), bb_im.transpose(0, 2, 1)
    tile_rows = lambda a: jnp.tile(a, (1, T, 1))
    u = proj[:, C_U:C_U + S5_WIDTH]
    ug = u.astype(BF16).reshape(nc, T, G, H).transpose(2, 0, 1, 3).reshape(G, nc, T * H)
    y = _s5_scan(ug, lb_re[:, None, :], lb_im[:, None, :], tile_rows(p['s5_c_re'][l]), tile_rows(p['s5_c_im'][l]),
                 tile_rows(bsr), tile_rows(bsi), bsr, bsi)
    return y.reshape(G, nc, T, H).transpose(1, 2, 0, 3).reshape(L, S5_WIDTH)


def _gdn_mixer(proj, small, p, l):
    val, kcum, qg, kdT, qk, gl = _gdn_local(proj, p['gdn_conv'][l], small)
    return _gdn_scan(val, kcum, qg, qk, kdT, gl, proj, p['gdn_norm'][l].reshape(1, GDN_DH))


def _forward(x3, p):
    x = x3.reshape(x3.shape[1:])
    depth = p['w_in'].shape[0]
    for l in range(depth):
        w_in = _permute_w_in(p['w_in'][l].astype(BF16))
        proj = _rms_matmul(x, p['attn_norm'][l], w_in)
        sm_par = jnp.zeros((8, LANES), F32)
        sm_par = sm_par.at[0, SM_G:SM_BETA].set(p['gdn_dt_bias'][l]).at[1, SM_G:SM_BETA].set(-jnp.exp(p['gdn_a_log'][l]))
        small = _small(proj, sm_par)
        y_a = _nsa_mixer(proj, small, p, l)
        y_s = _s5_mixer(proj, p, l)
        y_c = _gdn_mixer(proj, small, p, l)
        x = _mix(y_a, y_s, proj, y_c, x, p['nsa_out_norm'][l].reshape(1, -1), p['s5_out_norm'][l].reshape(1, -1),
                 p['s5_d'][l].reshape(1, -1), p['s5_w_glu'][l].astype(BF16), p['w_out'][l].astype(BF16))
        wf, cf, bf = p['ffn_w_in'][l], p['ffn_conv'][l], p['ffn_conv_b'][l].reshape(1, -1)
        act = _ffn_in(x, p['ffn_norm'][l], wf[:, :D_FF].astype(BF16), wf[:, D_FF:].astype(BF16),
                      cf[:, :D_FF], cf[:, D_FF:], bf[:, :D_FF], bf[:, D_FF:])
        w_o = jnp.pad(p['ffn_w_out'][l].astype(BF16), ((0, D_FF_PAD - D_FF), (0, 0)))
        x = _ffn_out(act, w_o, x)
    return x.reshape(x3.shape)


_PARAM_NAMES = ('attn_norm', 'w_in', 'nsa_q_norm', 'nsa_kc_norm', 'nsa_ks_norm', 'nsa_kw_norm', 'cmp_pe',
                'cmp_k_w1', 'cmp_k_w2', 'cmp_v_w1', 'cmp_v_w2', 'nsa_out_norm', 's5_lam_re', 's5_lam_im',
                's5_log_dt', 's5_b_re', 's5_b_im', 's5_c_re', 's5_c_im', 's5_d', 's5_w_glu', 's5_out_norm',
                'gdn_conv', 'gdn_a_log', 'gdn_dt_bias', 'gdn_norm', 'w_out', 'ffn_norm', 'ffn_w_in', 'ffn_conv',
                'ffn_conv_b', 'ffn_w_out')


def kernel(x, attn_norm, w_in, nsa_q_norm, nsa_kc_norm, nsa_ks_norm, nsa_kw_norm, cmp_pe, cmp_k_w1, cmp_k_w2,
           cmp_v_w1, cmp_v_w2, nsa_out_norm, s5_lam_re, s5_lam_im, s5_log_dt, s5_b_re, s5_b_im, s5_c_re, s5_c_im,
           s5_d, s5_w_glu, s5_out_norm, gdn_conv, gdn_a_log, gdn_dt_bias, gdn_norm, w_out, ffn_norm, ffn_w_in,
           ffn_conv, ffn_conv_b, ffn_w_out):
    vals = (attn_norm, w_in, nsa_q_norm, nsa_kc_norm, nsa_ks_norm, nsa_kw_norm, cmp_pe, cmp_k_w1, cmp_k_w2,
            cmp_v_w1, cmp_v_w2, nsa_out_norm, s5_lam_re, s5_lam_im, s5_log_dt, s5_b_re, s5_b_im, s5_c_re, s5_c_im,
            s5_d, s5_w_glu, s5_out_norm, gdn_conv, gdn_a_log, gdn_dt_bias, gdn_norm, w_out, ffn_norm, ffn_w_in,
            ffn_conv, ffn_conv_b, ffn_w_out)
    return _forward(x, dict(zip(_PARAM_NAMES, vals)))
```

```python
import functools
import math

import jax
import jax.numpy as jnp
import numpy as np
from jax import lax
from jax.experimental import pallas as pl
from jax.experimental.pallas import tpu as pltpu

F32 = jnp.float32
BF16 = jnp.bfloat16
HIGHEST = lax.Precision.HIGHEST

NSA_HEADS = 8
NSA_KV = 2
NSA_G = NSA_HEADS // NSA_KV
NSA_DH = 64
CMP_LEN = 32
CMP_STRIDE = 16
SLC_BLOCK = 64
SLC_TOPK = 16
WINDOW = 512
ROPE_THETA = 500000.0
ROPE_DIMS = NSA_DH // 4
S5_WIDTH = 512
S5_CH = 16
S5_GROUPS = 32
S5_STATE = 64
S5_CHUNK = 32
S5_SLAB = 8
GDN_HEADS = 8
GDN_DH = 128
GDN_WIDTH = 1024
GDN_CHUNK = 64
D_FF = 5504
D_FF_PAD = 5632
EPS = 1e-6
NEG_BIG = -(2.0 ** 100)
LOG2_E = math.log2(math.e)
V_ROWS = 80
KW_LANES = 128
CMP_CHUNK = 256
SEL_UNROLL = 2

LANES = 128
TQ = 128
TK = 512
VMEM_LIMIT = 56 * 1024 * 1024

C_Q, C_KS, C_KW, C_VS, C_VW = 0, 512, 640, 768, 896
C_Z, C_QKV, C_U, C_KC, C_VC, C_SM = 1024, 2048, 5120, 5632, 5760, 5888
NSA_PREP_W = C_Z
PROJ_COLS = 6144
SM_GATES, SM_G, SM_BETA, SM_END = 0, 24, 32, 40


def _cparams(sem):
    return pltpu.CompilerParams(dimension_semantics=sem, vmem_limit_bytes=VMEM_LIMIT)


def _dot(a, b, precision=None):
    return jnp.dot(a, b, preferred_element_type=F32, precision=precision)


def _dot_nt(a, b, precision=None):
    return lax.dot_general(a, b, (((1,), (1,)), ((), ())), preferred_element_type=F32, precision=precision)


def _gelu(x):
    return x * (0.5 * (1.0 + jnp.tanh(math.sqrt(2.0 / math.pi) * (x + 0.044715 * (x * x * x)))))


def _sigmoid(x):
    return 1.0 / (1.0 + jnp.exp(-x))


def _rms_matmul_kernel(x_ref, g_ref, w_ref, o_ref, xn_ref):
    @pl.when(pl.program_id(1) == 0)
    def _():
        xf = x_ref[...]
        ms = jnp.mean(xf * xf, axis=-1, keepdims=True)
        xn_ref[...] = (xf * lax.rsqrt(ms + EPS) * g_ref[...]).astype(BF16)

    o_ref[...] = _dot(xn_ref[...], w_ref[...])


def _rms_matmul(x, gain, w, tm=512, tn=2048):
    L, D = x.shape
    N = w.shape[1]
    return pl.pallas_call(
        _rms_matmul_kernel,
        grid=(L // tm, N // tn),
        in_specs=[
            pl.BlockSpec((tm, D), lambda i, j: (i, 0)),
            pl.BlockSpec((1, D), lambda i, j: (0, 0)),
            pl.BlockSpec((D, tn), lambda i, j: (0, j)),
        ],
        out_specs=pl.BlockSpec((tm, tn), lambda i, j: (i, j)),
        out_shape=jax.ShapeDtypeStruct((L, N), F32),
        scratch_shapes=[pltpu.VMEM((tm, D), BF16)],
        compiler_params=_cparams(("arbitrary", "arbitrary")),
        name="rms_proj",
    )(x, gain.reshape(1, D), w)


def _rope_slab(x, cos, sin):
    d = lax.broadcasted_iota(jnp.int32, x.shape, 1) & (NSA_DH - 1)
    half = ROPE_DIMS // 2
    partner = jnp.where(d < half, pltpu.roll(x, LANES - half, 1), pltpu.roll(x, half, 1))
    return x * cos + partner * sin


def _nsa_prep_kernel(x_ref, g_ref, cos_ref, sin_ref, bsum_ref, qT_ref, ksaug_ref, kwa_ref, vsx_ref, vwx_ref, *, n_grp):
    i = pl.program_id(0)
    tm = x_ref.shape[0]
    cos = cos_ref[...]
    sin = sin_ref[...]
    bsum = bsum_ref[...]
    lane = lax.broadcasted_iota(jnp.int32, (tm, LANES), 1)
    tiles = range(tm // TQ)

    def normed(s):
        sl = slice(s * LANES, (s + 1) * LANES)
        x = x_ref[:, sl]
        ms = _dot(x * x, bsum, precision=HIGHEST)
        return _rope_slab(x * lax.rsqrt(ms + EPS) * g_ref[:, sl], cos, sin)

    def per_head(y):
        return y, pltpu.roll(y, NSA_DH, 1)

    for s in range((C_KS - C_Q) // LANES):
        y = normed(s) * (NSA_DH ** -0.5 * LOG2_E)
        hk, g0 = divmod(s, NSA_G // 2)
        for t in tiles:
            yT = y[t * TQ:(t + 1) * TQ, :].T.astype(BF16)
            for half in range(2):
                g = 2 * g0 + half
                qT_ref[hk, t, :, g * TQ:(g + 1) * TQ] = yT[half * NSA_DH:(half + 1) * NSA_DH, :]

    blk = ((i * tm + lax.broadcasted_iota(jnp.int32, (tm, LANES), 0)) // SLC_BLOCK) % n_grp
    onehot = jnp.where(lane - NSA_DH == blk, 1.0, 0.0)
    for hk, y in enumerate(per_head(normed(C_KS // LANES))):
        ksaug_ref[hk] = jnp.where(lane < NSA_DH, y, onehot).astype(BF16)
    for hk, y in enumerate(per_head(normed(C_KW // LANES))):
        kwa_ref[hk] = jnp.where(lane < NSA_DH, y, 0.0).astype(BF16)

    ones_rows = jnp.where(lax.broadcasted_iota(jnp.int32, (V_ROWS - NSA_DH, tm), 0) == 0, 1.0, 0.0).astype(BF16)
    for c0, v_ref in ((C_VS, vsx_ref), (C_VW, vwx_ref)):
        x = x_ref[:, c0:c0 + LANES]
        for t in tiles:
            xT = x[t * TQ:(t + 1) * TQ, :].T.astype(BF16)
            for hk in range(NSA_KV):
                v_ref[hk, 0:NSA_DH, t * TQ:(t + 1) * TQ] = xT[hk * NSA_DH:(hk + 1) * NSA_DH, :]
        for hk in range(NSA_KV):
            v_ref[hk, NSA_DH:V_ROWS, :] = ones_rows


def _nsa_prep(proj, gains, cos, sin, n_grp, tm=512):
    L = proj.shape[0]
    W = NSA_PREP_W
    HK = NSA_KV
    GW = NSA_G * TQ
    bsum = jnp.asarray(np.kron(np.eye(2), np.ones((NSA_DH, NSA_DH))) / NSA_DH, F32)
    rows = pl.BlockSpec((HK, tm, LANES), lambda i: (0, i, 0))
    vals = pl.BlockSpec((HK, V_ROWS, tm), lambda i: (0, 0, i))
    return pl.pallas_call(
        functools.partial(_nsa_prep_kernel, n_grp=n_grp),
        grid=(L // tm,),
        in_specs=[
            pl.BlockSpec((tm, W), lambda i: (i, 0)),
            pl.BlockSpec((1, W), lambda i: (0, 0)),
            pl.BlockSpec((tm, LANES), lambda i: (i, 0)),
            pl.BlockSpec((tm, LANES), lambda i: (i, 0)),
            pl.BlockSpec((LANES, LANES), lambda i: (0, 0)),
        ],
        out_specs=[pl.BlockSpec((HK, tm // TQ, NSA_DH, GW), lambda i: (0, i, 0, 0)), rows, rows, vals, vals],
        out_shape=[jax.ShapeDtypeStruct((HK, L // TQ, NSA_DH, GW), BF16),
                   jax.ShapeDtypeStruct((HK, L, LANES), BF16), jax.ShapeDtypeStruct((HK, L, LANES), BF16),
                   jax.ShapeDtypeStruct((HK, V_ROWS, L), BF16), jax.ShapeDtypeStruct((HK, V_ROWS, L), BF16)],
        compiler_params=_cparams(("arbitrary",)),
        name="nsa_prep",
    )(proj, gains, cos, sin, bsum)


def _cmp_kernel(xk_ref, xv_ref, pe_ref, w1k_ref, w2k_ref, w1v_ref, w2v_ref, gk_ref, cos_ref, sin_ref, bsum_ref,
                kc_ref, vcx_ref):
    nb = kc_ref.shape[1]
    half_w = CMP_STRIDE * LANES

    def mlp(x_ref, w1_ref, w2_ref):
        hb = jnp.concatenate([x_ref[pl.ds(l, nb, stride=CMP_STRIDE), :] for l in range(CMP_STRIDE)], axis=1)
        a = (hb + pe_ref[:, 0:half_w]).astype(BF16)
        b = (hb + pe_ref[:, half_w:2 * half_w]).astype(BF16)
        p1 = _dot(a, w1_ref[0:half_w, :])
        p2 = _dot(b, w1_ref[half_w:2 * half_w, :])
        h = p1 + pltpu.roll(p2, nb - 1, 0)
        return _dot(_gelu(h).astype(BF16), w2_ref[...])

    rid = lax.broadcasted_iota(jnp.int32, (nb, LANES), 0)
    real = rid < nb - 1
    kc = jnp.where(real, mlp(xk_ref, w1k_ref, w2k_ref), 0.0)
    vc = jnp.where(real, mlp(xv_ref, w1v_ref, w2v_ref), 0.0)
    ms = _dot(kc * kc, bsum_ref[...], precision=HIGHEST)
    kcn = _rope_slab(kc * lax.rsqrt(ms + EPS) * gk_ref[...], cos_ref[...], sin_ref[...])
    kc_ref[0] = kcn[:, 0:NSA_DH].astype(BF16)
    kc_ref[1] = pltpu.roll(kcn, NSA_DH, 1)[:, 0:NSA_DH].astype(BF16)
    ones_rows = jnp.where(lax.broadcasted_iota(jnp.int32, (V_ROWS - NSA_DH, nb), 0) == 0, 1.0, 0.0).astype(BF16)
    for t in range(nb // LANES):
        vT = vc[t * LANES:(t + 1) * LANES, :].T.astype(BF16)
        for hk in range(NSA_KV):
            vcx_ref[hk, 0:NSA_DH, t * LANES:(t + 1) * LANES] = vT[hk * NSA_DH:(hk + 1) * NSA_DH, :]
    for hk in range(NSA_KV):
        vcx_ref[hk, NSA_DH:V_ROWS, :] = ones_rows


def _nsa_compress(proj, pe, w1k, w2k, w1v, w2v, gk, cos, sin):
    L = proj.shape[0]
    nb = L // CMP_STRIDE
    bsum = jnp.asarray(np.kron(np.eye(2), np.ones((NSA_DH, NSA_DH))) / NSA_DH, F32)
    once = lambda shape, idx: pl.BlockSpec(shape, lambda i: idx, pipeline_mode=pl.Buffered(1))
    full = lambda a: once(a.shape, (0,) * a.ndim)
    return pl.pallas_call(
        _cmp_kernel,
        grid=(1,),
        in_specs=[once((L, LANES), (0, C_KC // LANES)), once((L, LANES), (0, C_VC // LANES)),
                  full(pe), full(w1k), full(w2k), full(w1v), full(w2v), full(gk), full(cos), full(sin), full(bsum)],
        out_specs=[pl.BlockSpec((NSA_KV, nb, NSA_DH), lambda i: (0, 0, 0)),
                   pl.BlockSpec((NSA_KV, V_ROWS, nb), lambda i: (0, 0, 0))],
        out_shape=[jax.ShapeDtypeStruct((NSA_KV, nb, NSA_DH), BF16),
                   jax.ShapeDtypeStruct((NSA_KV, V_ROWS, nb), BF16)],
        compiler_params=_cparams(("arbitrary",)),
        name="nsa_compress",
    )(proj, proj, pe, w1k, w2k, w1v, w2v, gk, cos, sin, bsum)


def _nsa_attn_kernel(qT_ref, kc_ref, vcx_ref, smap_ref, ksaug_ref, vsx_ref, kwp_ref, vwxp_ref, gT_ref,
                     o_ref, qaug_ref, qwin_ref, ss_ref, mt_ref, m_ref, acc_ref, sc_ref, imp_ref, *, n_grp):
    i = pl.program_id(0)
    s0 = i * TQ
    hks = range(qT_ref.shape[0])
    nb = kc_ref.shape[1]
    ns = qaug_ref.shape[1] * n_grp
    GW = NSA_G * TQ
    qT = [qT_ref[h, 0] for h in hks]
    tq = s0 + (lax.broadcasted_iota(jnp.int32, (1, GW), 1) & (TQ - 1))

    cch = min(CMP_CHUNK, nb)
    n_cch = (s0 + TQ - CMP_LEN) // (CMP_STRIDE * cch) + 1
    nthr = (tq - (CMP_LEN - 1)) // CMP_STRIDE

    def cmp_scores(c, mcs):
        r0c = pl.multiple_of(c * cch, cch)
        s = [_dot(kc_ref[h, pl.ds(r0c, cch), :], qT[h]) for h in hks]
        nrow = r0c + lax.broadcasted_iota(jnp.int32, (cch, GW), 0)
        out = []
        for h in hks:
            sh = jnp.where(nrow <= nthr, s[h], NEG_BIG)
            sc_ref[h, pl.ds(r0c, cch), :] = sh
            out.append(jnp.maximum(mcs[h], jnp.max(sh, axis=0, keepdims=True)))
        return tuple(out)

    mc = lax.fori_loop(0, n_cch, cmp_scores, tuple(jnp.full((1, GW), NEG_BIG, F32) for _ in hks))
    imp_ref[...] = jnp.zeros_like(imp_ref)
    ones16 = jnp.ones((16, cch), BF16)
    VX = vcx_ref.shape[1]

    sm = smap_ref[...]
    sm_rows = sm.shape[0]
    blocks_per_chunk = cch * CMP_STRIDE // SLC_BLOCK

    def cmp_accum(c, carry):
        r0c = pl.multiple_of(c * cch, cch)
        j0 = pl.multiple_of(c * blocks_per_chunk, blocks_per_chunk)
        parts = [_split2(jnp.exp2(sc_ref[h, pl.ds(r0c, cch), :] - mc[h])) for h in hks]
        for h in hks:
            imp_ref[h, pl.ds(j0, sm_rows), :] += _dot(sm, parts[h][0]) + _dot(sm, parts[h][1])
        return tuple((carry[h][0] + _dot(vcx_ref[h, :, pl.ds(r0c, cch)], parts[h][0]),
                      carry[h][1] + _dot(ones16, parts[h][1])) for h in hks)

    cacc = lax.fori_loop(0, n_cch, cmp_accum,
                         tuple((jnp.zeros((VX, GW), F32), jnp.zeros((16, GW), F32)) for _ in hks))
    inv_l = [jnp.where(tq >= CMP_LEN - 1, 1.0 / (cacc[h][0][NSA_DH:NSA_DH + 1] + cacc[h][1][0:1]), 0.0) for h in hks]
    ocT = [cacc[h][0][0:NSA_DH] * inv_l[h] for h in hks]

    r0 = pl.multiple_of(s0, TQ)
    rel = lax.broadcasted_iota(jnp.int32, (TQ, GW), 0)
    qrel = tq - s0
    flag_row = lax.broadcasted_iota(jnp.int32, (KW_LANES - NSA_DH, GW), 0) == 0
    for h in hks:
        qwin_ref[h, 0:NSA_DH, :] = qT[h]
        qwin_ref[h, NSA_DH:KW_LANES, :] = jnp.where(flag_row, NEG_BIG, 0.0).astype(BF16)
    qwin = [qwin_ref[h] for h in hks]
    edge = [_dot(kwp_ref[h, pl.ds(r0, TQ), :], qwin[h]) for h in hks]
    mid = [_dot(kwp_ref[h, pl.ds(r0 + TQ, WINDOW - TQ), :], qwin[h]) for h in hks]
    diag = [_dot(kwp_ref[h, pl.ds(r0 + WINDOW, TQ), :], qwin[h]) for h in hks]
    edge16 = [jnp.where(rel > qrel, edge[h], NEG_BIG).astype(BF16) for h in hks]
    diag16 = [jnp.where(rel <= qrel, diag[h], NEG_BIG).astype(BF16) for h in hks]
    mid16 = [mid[h].astype(BF16) for h in hks]
    mw = [jnp.maximum(jnp.maximum(jnp.max(edge16[h], axis=0, keepdims=True), jnp.max(mid16[h], axis=0, keepdims=True)),
                      jnp.max(diag16[h], axis=0, keepdims=True)) for h in hks]
    accw = [(_dot(vwxp_ref[h, :, pl.ds(r0, TQ)], jnp.exp2(edge16[h] - mw[h]))
             + _dot(vwxp_ref[h, :, pl.ds(r0 + TQ, WINDOW - TQ)], jnp.exp2(mid16[h] - mw[h]))
             + _dot(vwxp_ref[h, :, pl.ds(r0 + WINDOW, TQ)], jnp.exp2(diag16[h] - mw[h]))) for h in hks]
    owT = [accw[h][0:NSA_DH] / accw[h][NSA_DH:NSA_DH + 1] for h in hks]
    sd = [_dot(ksaug_ref[h, pl.ds(r0, TQ), 0:NSA_DH], qT[h]) for h in hks]
    sd16 = [jnp.where(rel <= qrel, sd[h], NEG_BIG).astype(BF16) for h in hks]

    imp = []
    for h in hks:
        tot = imp_ref[h, 0:ns, 0:TQ] * inv_l[h][:, 0:TQ]
        for g in range(1, NSA_G):
            tot = tot + imp_ref[h, 0:ns, g * TQ:(g + 1) * TQ] * inv_l[h][:, g * TQ:(g + 1) * TQ]
        imp.append(tot)

    jrow = lax.broadcasted_iota(jnp.int32, (ns, TQ), 0)
    cur = (s0 + lax.broadcasted_iota(jnp.int32, (1, TQ), 1)) // SLC_BLOCK
    forced = jnp.where(jrow == 0, 0.0, jnp.where(jrow == cur, 0.0, jnp.where(jrow == cur - 1, 0.0, NEG_BIG)))
    val = [jnp.where(jrow >= 1, jnp.where(jrow <= cur - 2, imp[h], -jnp.inf), -jnp.inf) for h in hks]
    for _ in range(min(SLC_TOPK, ns) - 3):
        mx = [jnp.max(val[h], axis=0, keepdims=True) for h in hks]
        idx = [jnp.min(jnp.where(val[h] == mx[h], jrow, ns), axis=0, keepdims=True) for h in hks]
        val = [jnp.where(jrow == idx[h], -jnp.inf, val[h]) for h in hks]
    bias = [jnp.where(jrow >= 1, jnp.where(jrow <= cur - 2, jnp.where(val[h] == -jnp.inf, 0.0, NEG_BIG), forced), forced)
            for h in hks]
    own = jrow // 2 == i
    for h in hks:
        bias16 = jnp.where(own, NEG_BIG, bias[h]).astype(BF16)
        for grp in range(ns // n_grp):
            qaug_ref[h, grp, 0:NSA_DH, :] = qT[h]
            for g in range(NSA_G):
                qaug_ref[h, grp, NSA_DH:NSA_DH + n_grp, g * TQ:(g + 1) * TQ] = bias16[grp * n_grp:(grp + 1) * n_grp, :]

    tiles_per_grp = (n_grp * SLC_BLOCK) // TK
    n_body = jnp.maximum((s0 + SEL_UNROLL * TK - 1) // (SEL_UNROLL * TK), 1)

    def scores(kt, slot):
        k0 = pl.multiple_of(kt * TK, TK)
        grp = kt // tiles_per_grp
        s = [_dot(ksaug_ref[h, pl.ds(k0, TK), :], qaug_ref[h, grp]) for h in hks]
        for h in hks:
            sb = s[h].astype(BF16)
            ss_ref[h, slot] = sb
            mt_ref[h, slot] = jnp.max(sb, axis=0, keepdims=True).astype(F32)

    def online_update(sb, mt, v):
        m = [m_ref[h] for h in hks]
        mn = [jnp.maximum(m[h], mt[h]) for h in hks]
        pv = [_dot(v[h], jnp.exp2(sb[h] - mn[h].astype(BF16))) for h in hks]
        for h in hks:
            acc_ref[h] = jnp.exp2(m[h] - mn[h]) * acc_ref[h] + pv[h]
            m_ref[h] = mn[h]

    def accumulate(kt, slot):
        k0 = pl.multiple_of(kt * TK, TK)
        online_update([ss_ref[h, slot] for h in hks], [mt_ref[h, slot] for h in hks],
                      [vsx_ref[h, :, pl.ds(k0, TK)] for h in hks])

    m_ref[...] = jnp.full(m_ref.shape, NEG_BIG, F32)
    acc_ref[...] = jnp.zeros_like(acc_ref)

    scores(0, 0)

    def tile_group(base, last):
        for u in range(SEL_UNROLL):
            if not (last and u == SEL_UNROLL - 1):
                scores(base + u + 1, (u + 1) % 2)
            accumulate(base + u, u % 2)

    def loop_body(j, carry):
        tile_group(SEL_UNROLL * j, last=False)
        return carry

    lax.fori_loop(0, n_body - 1, loop_body, 0)

    tile_group(SEL_UNROLL * (n_body - 1), last=True)

    online_update(sd16, [jnp.max(sd16[h], axis=0, keepdims=True).astype(F32) for h in hks],
                  [vsx_ref[h, :, pl.ds(r0, TQ)] for h in hks])
    osT = [acc_ref[h, 0:NSA_DH, :] / acc_ref[h, NSA_DH:NSA_DH + 1, :] for h in hks]

    for h in hks:
        gates = _sigmoid(gT_ref[h])
        outs = []
        for g in range(NSA_G):
            sl = slice(g * TQ, (g + 1) * TQ)
            outs.append(gates[3 * g:3 * g + 1] * ocT[h][:, sl] + gates[3 * g + 1:3 * g + 2] * osT[h][:, sl]
                        + gates[3 * g + 2:3 * g + 3] * owT[h][:, sl])
        for g in range(0, NSA_G, 2):
            c0 = (h * NSA_G + g) * NSA_DH
            o_ref[:, c0:c0 + 2 * NSA_DH] = jnp.concatenate([outs[g], outs[g + 1]], axis=0).T


def _nsa_attention(qT, kc, vcx, smapT, ksaug, vsx, kwp, vwxp, gT, n_grp):
    HK, nQ, _, GW = qT.shape
    nb = kc.shape[1]
    L = vsx.shape[2]
    ns = L // SLC_BLOCK
    KA = ksaug.shape[2]
    scratch = [pltpu.VMEM((HK, ns // n_grp, KA, GW), BF16),
               pltpu.VMEM((HK, KW_LANES, GW), BF16),
               pltpu.VMEM((HK, 2, TK, GW), BF16),
               pltpu.VMEM((HK, 2, 1, GW), F32),
               pltpu.VMEM((HK, 1, GW), F32),
               pltpu.VMEM((HK, V_ROWS, GW), F32),
               pltpu.VMEM((HK, nb, GW), F32),
               pltpu.VMEM((HK, ns + 8, GW), F32)]
    resident = lambda a: pl.BlockSpec(a.shape, lambda i: (0,) * a.ndim, pipeline_mode=pl.Buffered(1))
    return pl.pallas_call(
        functools.partial(_nsa_attn_kernel, n_grp=n_grp),
        grid=(nQ,),
        in_specs=[
            pl.BlockSpec((HK, 1, NSA_DH, GW), lambda i: (0, i, 0, 0)),
            resident(kc), resident(vcx), resident(smapT), resident(ksaug), resident(vsx), resident(kwp),
            resident(vwxp),
            pl.BlockSpec((HK, 16, TQ), lambda i: (0, 0, i)),
        ],
        out_specs=pl.BlockSpec((TQ, NSA_HEADS * NSA_DH), lambda i: (i, 0)),
        out_shape=jax.ShapeDtypeStruct((nQ * TQ, NSA_HEADS * NSA_DH), F32),
        scratch_shapes=scratch,
        compiler_params=_cparams(("arbitrary",)),
        name="nsa_attn",
    )(qT, kc, vcx, smapT, ksaug, vsx, kwp, vwxp, gT)


def _s5_kernel(x_ref, lbr_ref, lbi_ref, cr_ref, ci_ref, bsr_ref, bsi_ref, y_ref, lm_ref, yg_ref):
    T = S5_CHUNK
    nc = x_ref.shape[0] // T
    u_tau = [x_ref[pl.ds(tau, nc, stride=T), :] for tau in range(T)]
    for gi in range(S5_SLAB):
        lanes = slice(gi * S5_CH, (gi + 1) * S5_CH)
        u = jnp.concatenate([ut[:, lanes] for ut in u_tau], axis=1).astype(BF16)
        yg_ref[gi] = _s5_group(u, lbr_ref[gi], lbi_ref[gi], cr_ref[gi], ci_ref[gi], bsr_ref[gi], bsi_ref[gi], lm_ref)
    for tau in range(T):
        cols = slice(tau * S5_CH, (tau + 1) * S5_CH)
        y_ref[pl.ds(tau, nc, stride=T), :] = jnp.concatenate([yg_ref[gi, :, cols] for gi in range(S5_SLAB)], axis=1)


def _s5_group(u, ar, ai, cr, ci, bsr, bsi, lm_ref):
    T = S5_CHUNK
    W = T * S5_CH
    tile_rows = lambda a: jnp.concatenate([a] * T, axis=0)
    ctr, cti, btr, bti = tile_rows(cr), tile_rows(ci), tile_rows(bsr), tile_rows(bsi)
    delta = lax.broadcasted_iota(jnp.int32, (W, S5_STATE), 0) // S5_CH

    def powers(e):
        pr = jnp.ones((W, S5_STATE), F32)
        pi = jnp.zeros((W, S5_STATE), F32)
        fr, fi = ar, ai
        for b in range(T.bit_length() - 1):
            bit = ((e >> b) & 1) == 1
            nr = pr * fr - pi * fi
            ni = pr * fi + pi * fr
            pr = jnp.where(bit, nr, pr)
            pi = jnp.where(bit, ni, pi)
            fr, fi = fr * fr - fi * fi, 2.0 * fr * fi
        return pr, pi, fr, fi

    pwr, pwi, aTr, aTi = powers(delta)
    rvr, rvi, _, _ = powers(T - 1 - delta)
    car = ctr * pwr - cti * pwi
    cai = ctr * pwi + cti * pwr

    kw = _dot_nt(bsr, car, HIGHEST) - _dot_nt(bsi, cai, HIGHEST)
    lane = lax.broadcasted_iota(jnp.int32, (S5_CH, W), 1)
    for tau in range(T):
        sh = S5_CH * tau
        blk = kw if tau == 0 else jnp.where(lane >= sh, pltpu.roll(kw, sh, 1), 0.0)
        lm_ref[sh:sh + S5_CH, :] = blk.astype(BF16)

    nc = u.shape[0]
    y = _dot(u, lm_ref[...])

    sr = _dot(u, (rvr * btr - rvi * bti).astype(BF16))
    si = _dot(u, (rvr * bti + rvi * btr).astype(BF16))
    rowc = lax.broadcasted_iota(jnp.int32, (nc, S5_STATE), 0)
    fr, fi = aTr, aTi
    step = 1
    while step < nc:
        shr = jnp.where(rowc >= step, pltpu.roll(sr, step, 0), 0.0)
        shi = jnp.where(rowc >= step, pltpu.roll(si, step, 0), 0.0)
        sr, si = sr + fr * shr - fi * shi, si + fr * shi + fi * shr
        fr, fi = fr * fr - fi * fi, 2.0 * fr * fi
        step *= 2
    xr = jnp.where(rowc >= 1, pltpu.roll(sr, 1, 0), 0.0)
    xi = jnp.where(rowc >= 1, pltpu.roll(si, 1, 0), 0.0)
    c1r = car * ar - cai * ai
    c1i = car * ai + cai * ar
    return y + _dot_nt(xr.astype(BF16), c1r.astype(BF16)) - _dot_nt(xi.astype(BF16), c1i.astype(BF16))


def _s5_scan(proj, lbr, lbi, cr, ci, bsr, bsi):
    L = proj.shape[0]
    W = S5_CHUNK * S5_CH
    nc = L // S5_CHUNK
    G = lbr.shape[0]
    slab = lambda shape: pl.BlockSpec((S5_SLAB,) + shape, lambda s: (s, 0, 0))
    return pl.pallas_call(
        _s5_kernel,
        grid=(G // S5_SLAB,),
        in_specs=[pl.BlockSpec((L, LANES), lambda s: (0, C_U // LANES + s)),
                  slab((1, S5_STATE)), slab((1, S5_STATE)),
                  slab((S5_CH, S5_STATE)), slab((S5_CH, S5_STATE)), slab((S5_CH, S5_STATE)), slab((S5_CH, S5_STATE))],
        out_specs=pl.BlockSpec((L, LANES), lambda s: (0, s)),
        out_shape=jax.ShapeDtypeStruct((L, G * S5_CH), F32),
        scratch_shapes=[pltpu.VMEM((W, W), BF16), pltpu.VMEM((S5_SLAB, nc, W), F32)],
        compiler_params=_cparams(("arbitrary",)),
        name="s5_scan",
    )(proj, lbr, lbi, cr, ci, bsr, bsi)


def _small_kernel(x_ref, p_ref, o_ref):
    x = x_ref[...]
    lane = lax.broadcasted_iota(jnp.int32, x.shape, 1)
    z = x + p_ref[0:1, :]
    softplus = jnp.maximum(z, 0.0) + jnp.log(1.0 + jnp.exp(-jnp.abs(z)))
    gdec = p_ref[1:2, :] * softplus
    beta = _sigmoid(x)
    o_ref[...] = jnp.where(lane < SM_G, x, jnp.where(lane < SM_BETA, gdec, beta))


def _small(proj, params, tm=1024):
    L = proj.shape[0]
    cb = C_SM // LANES
    return pl.pallas_call(
        _small_kernel,
        grid=(L // tm,),
        in_specs=[pl.BlockSpec((tm, LANES), lambda i: (i, cb)), pl.BlockSpec((8, LANES), lambda i: (0, 0))],
        out_specs=pl.BlockSpec((tm, LANES), lambda i: (i, 0)),
        out_shape=jax.ShapeDtypeStruct((L, LANES), F32),
        compiler_params=_cparams(("arbitrary",)),
        name="small_cols",
    )(proj, params)


GDN_CB = 8


def _split2(x):
    hi = x.astype(BF16)
    return hi, (x - hi.astype(F32)).astype(BF16)


def _dot_split(a, b):
    ah, al = _split2(a)
    bh, bl = _split2(b)
    return _dot(ah, bh) + (_dot(ah, bl) + _dot(al, bh))


def _gdn_local_kernel(xq_ref, xk_ref, xv_ref, cwq_ref, cwk_ref, cwv_ref, sm_ref,
                      val_ref, kcum_ref, qg_ref, kdT_ref, qk_ref, gl_ref, carry_ref, hs_ref):
    C = GDN_CHUNK
    R = xq_ref.shape[0]

    @pl.when(pl.program_id(1) == 0)
    def _():
        carry_ref[...] = jnp.zeros_like(carry_ref)

    def conv_silu(x_ref, cw_ref, slot):
        x = x_ref[...]
        hs_ref[slot, 0:8, :] = carry_ref[slot]
        hs_ref[slot, 8:R + 8, :] = x
        carry_ref[slot] = x[R - 8:R, :]
        taps = cw_ref.shape[0]
        y = cw_ref[taps - 1:taps, :] * x
        for back in range(1, taps):
            y = y + cw_ref[taps - 1 - back:taps - back, :] * hs_ref[slot, 8 - back:8 - back + R, :]
        half_y = 0.5 * y
        return half_y * (1.0 + jnp.tanh(half_y))

    def l2norm(y):
        return y * lax.rsqrt(jnp.sum(y * y, axis=-1, keepdims=True) + EPS)

    q_all = l2norm(conv_silu(xq_ref, cwq_ref, 0)) * (GDN_DH ** -0.5)
    k_all = l2norm(conv_silu(xk_ref, cwk_ref, 1))
    v_all = conv_silu(xv_ref, cwv_ref, 2)
    chunks = range(GDN_CB)
    rows = [slice(c * C, (c + 1) * C) for c in chunks]
    ii = lax.broadcasted_iota(jnp.int32, (C, C), 0)
    jj = lax.broadcasted_iota(jnp.int32, (C, C), 1)
    causal = ii >= jj
    strict = ii > jj
    eye = (ii == jj).astype(F32)
    tril16 = causal.astype(BF16)
    row2 = lax.broadcasted_iota(jnp.int32, (C, 2 * C), 0)
    col2 = lax.broadcasted_iota(jnp.int32, (C, 2 * C), 1)
    keep = row2 > jnp.where(col2 < C, -1, col2 - C)

    head = pl.program_id(0)
    sm = sm_ref[...]
    sm_lane = lax.broadcasted_iota(jnp.int32, sm.shape, 1)
    g_col = jnp.sum(jnp.where(sm_lane == SM_G + head, sm, 0.0), axis=1, keepdims=True)
    b_col = jnp.sum(jnp.where(sm_lane == SM_BETA + head, sm, 0.0), axis=1, keepdims=True)

    k = [k_all[rs, :] for rs in rows]
    beta = [b_col[rs, :] for rs in rows]
    res = []
    for rs in rows:
        x = jnp.where(keep, jnp.broadcast_to(g_col[rs, :], (C, 2 * C)), 0.0)
        hi = x.astype(BF16)
        r1 = x - hi.astype(F32)
        mid = r1.astype(BF16)
        lo = (r1 - mid.astype(F32)).astype(BF16)
        res.append(_dot(tril16, hi) + (_dot(tril16, mid) + _dot(tril16, lo)))
    gc = [jnp.broadcast_to(r[:, 0:1], (C, GDN_DH)) for r in res]
    decay = [jnp.where(causal, jnp.exp(r[:, C:2 * C]), 0.0) for r in res]
    kb = [kc * bc for kc, bc in zip(k, beta)]
    k16 = [kc.astype(BF16) for kc in k]
    a = [jnp.where(strict, _dot_nt(kbc.astype(BF16), kc16) * dc, 0.0) for kbc, kc16, dc in zip(kb, k16, decay)]
    t = [eye - ac for ac in a]
    p = a
    for _ in range(int(math.log2(C)) - 1):
        p = [_dot_split(pc, pc) for pc in p]
        t = [tc + _dot_split(tc, pc) for tc, pc in zip(t, p)]
    t16 = [tc.astype(BF16) for tc in t]
    egc = [jnp.exp(gcc) for gcc in gc]
    val = [_dot(tc, (v_all[rs, :] * bc).astype(BF16)) for tc, rs, bc in zip(t16, rows, beta)]
    kcum = [_dot(tc, (kbc * ec).astype(BF16)) for tc, kbc, ec in zip(t16, kb, egc)]
    qk = [_dot_nt(q_all[rs, :].astype(BF16), kc16) * dc for rs, kc16, dc in zip(rows, k16, decay)]
    kd = []
    for c in chunks:
        rs = rows[c]
        val_ref[rs, :] = val[c]
        kcum_ref[rs, :] = kcum[c].astype(BF16)
        qk_ref[0, rs, :] = qk[c].astype(BF16)
        glast = gc[c][C - 1:C, :]
        qg_ref[rs, :] = (q_all[rs, :] * egc[c]).astype(BF16)
        kd.append(k[c] * jnp.exp(glast - gc[c]))
        gl_ref[0, c:c + 1, :] = jnp.exp(glast)
    for c in range(0, GDN_CB, 2):
        kdT_ref[0, c // 2] = jnp.concatenate([kd[c], kd[c + 1]], axis=0).T.astype(BF16)


def _gdn_local(proj, conv_w, small):
    L = proj.shape[0]
    H = GDN_HEADS
    R = GDN_CB * GDN_CHUNK
    cb = C_QKV // GDN_DH
    taps = conv_w.shape[0]
    col = lambda off: pl.BlockSpec((R, GDN_DH), lambda h, i: (i, cb + off + h))
    cw = lambda off: pl.BlockSpec((taps, GDN_DH), lambda h, i: (0, off + h))
    big = pl.BlockSpec((R, GDN_DH), lambda h, i: (i, h))
    return pl.pallas_call(
        _gdn_local_kernel,
        grid=(H, L // R),
        in_specs=[col(0), col(H), col(2 * H), cw(0), cw(H), cw(2 * H),
                  pl.BlockSpec((R, LANES), lambda h, i: (i, 0))],
        out_specs=[big, big, big,
                   pl.BlockSpec((1, GDN_CB // 2, GDN_DH, 2 * GDN_CHUNK), lambda h, i: (h, i, 0, 0)),
                   pl.BlockSpec((1, R, GDN_CHUNK), lambda h, i: (h, i, 0)),
                   pl.BlockSpec((1, GDN_CB, GDN_DH), lambda h, i: (h, i, 0))],
        out_shape=[jax.ShapeDtypeStruct((L, GDN_WIDTH), F32),
                   jax.ShapeDtypeStruct((L, GDN_WIDTH), BF16),
                   jax.ShapeDtypeStruct((L, GDN_WIDTH), BF16),
                   jax.ShapeDtypeStruct((H, L // (2 * GDN_CHUNK), GDN_DH, 2 * GDN_CHUNK), BF16),
                   jax.ShapeDtypeStruct((H, L, GDN_CHUNK), BF16),
                   jax.ShapeDtypeStruct((H, L // GDN_CHUNK, GDN_DH), F32)],
        scratch_shapes=[pltpu.VMEM((3, 8, GDN_DH), F32), pltpu.VMEM((3, R + 8, GDN_DH), F32)],
        compiler_params=_cparams(("arbitrary", "arbitrary")),
        name="gdn_local",
    )(proj, proj, proj, conv_w, conv_w, conv_w, small)


def _gdn_scan_kernel(val_ref, kcum_ref, qg_ref, qk_ref, kdT_ref, gl_ref, z_ref, gn_ref, o_ref, st_ref):
    @pl.when(pl.program_id(0) == 0)
    def _():
        st_ref[...] = jnp.zeros_like(st_ref)

    C = GDN_CHUNK
    gn = gn_ref[...]
    heads = range(GDN_HEADS)
    cols = [slice(h * GDN_DH, (h + 1) * GDN_DH) for h in heads]
    state = [st_ref[h] for h in heads]
    for c in range(GDN_CB):
        rs = slice(c * C, (c + 1) * C)
        sb = [s.astype(BF16) for s in state]
        kcs = [_dot(kcum_ref[rs, cs], s) for cs, s in zip(cols, sb)]
        qgs = [_dot(qg_ref[rs, cs], s) for cs, s in zip(cols, sb)]
        vb = [(val_ref[rs, cs] - x).astype(BF16) for cs, x in zip(cols, kcs)]
        o = [x + _dot(qk_ref[h, rs, :], v) for h, x, v in zip(heads, qgs, vb)]
        kcols = slice((c % 2) * C, (c % 2 + 1) * C)
        state = [s * gl_ref[h, c:c + 1, :] + _dot(kdT_ref[h, c // 2, :, kcols], v)
                 for h, s, v in zip(heads, state, vb)]
        for h in heads:
            on = o[h] * lax.rsqrt(jnp.mean(o[h] * o[h], axis=-1, keepdims=True) + EPS) * gn
            z = z_ref[rs, cols[h]]
            o_ref[rs, cols[h]] = on * (z * _sigmoid(z))
    for h in heads:
        st_ref[h] = state[h]


def _gdn_scan(val, kcum, qg, qk, kdT, gl, proj, gnorm):
    L = val.shape[0]
    H = GDN_HEADS
    R = GDN_CB * GDN_CHUNK
    row = pl.BlockSpec((R, GDN_WIDTH), lambda i: (i, 0))
    zb = C_Z // GDN_WIDTH
    return pl.pallas_call(
        _gdn_scan_kernel,
        grid=(L // R,),
        in_specs=[row, row, row,
                  pl.BlockSpec((H, R, GDN_CHUNK), lambda i: (0, i, 0)),
                  pl.BlockSpec((H, GDN_CB // 2, GDN_DH, 2 * GDN_CHUNK), lambda i: (0, i, 0, 0)),
                  pl.BlockSpec((H, GDN_CB, GDN_DH), lambda i: (0, i, 0)),
                  pl.BlockSpec((R, GDN_WIDTH), lambda i: (i, zb)),
                  pl.BlockSpec((1, GDN_DH), lambda i: (0, 0))],
        out_specs=row,
        out_shape=jax.ShapeDtypeStruct((L, GDN_WIDTH), F32),
        scratch_shapes=[pltpu.VMEM((H, GDN_DH, GDN_DH), F32)],
        compiler_params=_cparams(("arbitrary",)),
        name="gdn_scan",
    )(val, kcum, qg, qk, kdT, gl, proj, gnorm)


def _mix_kernel(ya_ref, ys_ref, u_ref, yc_ref, x_ref, ga_ref, gb_ref, d_ref, wglu_ref, wout_ref, o_ref):
    def rms(y, g):
        return y * lax.rsqrt(jnp.mean(y * y, axis=-1, keepdims=True) + EPS) * g

    a = rms(ya_ref[...], ga_ref[...]).astype(BF16)
    yb = _gelu(ys_ref[...] + d_ref[...] * u_ref[...])
    yb = yb * _sigmoid(_dot(yb.astype(BF16), wglu_ref[...]))
    b = rms(yb, gb_ref[...]).astype(BF16)
    na, nb = a.shape[1], b.shape[1]
    acc = x_ref[...] + _dot(a, wout_ref[0:na, :])
    acc = acc + _dot(b, wout_ref[na:na + nb, :])
    o_ref[...] = acc + _dot(yc_ref[...].astype(BF16), wout_ref[na + nb:, :])


def _mix(ya, ys, proj, yc, x, ga, gb, d, wglu, wout, tm=256):
    L, D = x.shape
    ub = C_U // S5_WIDTH
    full = lambda a: pl.BlockSpec(a.shape, lambda i: (0, 0))
    return pl.pallas_call(
        _mix_kernel,
        grid=(L // tm,),
        in_specs=[pl.BlockSpec((tm, ya.shape[1]), lambda i: (i, 0)),
                  pl.BlockSpec((tm, S5_WIDTH), lambda i: (i, 0)),
                  pl.BlockSpec((tm, S5_WIDTH), lambda i: (i, ub)),
                  pl.BlockSpec((tm, GDN_WIDTH), lambda i: (i, 0)),
                  pl.BlockSpec((tm, D), lambda i: (i, 0)),
                  full(ga), full(gb), full(d), full(wglu), full(wout)],
        out_specs=pl.BlockSpec((tm, D), lambda i: (i, 0)),
        out_shape=jax.ShapeDtypeStruct((L, D), F32),
        compiler_params=_cparams(("arbitrary",)),
        name="mix_out",
    )(ya, ys, proj, yc, x, ga, gb, d, wglu, wout)


def _ffn_in_kernel(x_ref, g_ref, wg_ref, wu_ref, cg_ref, cu_ref, bg_ref, bu_ref, o_ref, xn_ref, carry_ref, hs_ref,
                   *, n_valid):
    i = pl.program_id(0)
    j = pl.program_id(1)

    @pl.when(j == 0)
    def _():
        xf = x_ref[...]
        ms = jnp.mean(xf * xf, axis=-1, keepdims=True)
        xn_ref[...] = (xf * lax.rsqrt(ms + EPS) * g_ref[...]).astype(BF16)

    @pl.when(i == 0)
    def _():
        carry_ref[j] = jnp.zeros(carry_ref.shape[1:], F32)

    xn = xn_ref[...]
    tm = xn.shape[0]

    def conv(h, slot, cw_ref, cb_ref):
        hs_ref[slot, 0:8, :] = carry_ref[j, slot]
        hs_ref[slot, 8:tm + 8, :] = h
        carry_ref[j, slot] = h[tm - 8:tm, :]
        taps = cw_ref.shape[0]
        y = cw_ref[taps - 1:taps, :] * h + cb_ref[...]
        for back in range(1, taps):
            y = y + cw_ref[taps - 1 - back:taps - back, :] * hs_ref[slot, 8 - back:8 - back + tm, :]
        return y

    gate = conv(_dot(xn, wg_ref[...]), 0, cg_ref, bg_ref)
    up = conv(_dot(xn, wu_ref[...]), 1, cu_ref, bu_ref)
    half_gate = 0.5 * gate
    act = half_gate * (1.0 + jnp.tanh(half_gate)) * up
    o_ref[...] = act.astype(BF16)

    @pl.when(j == pl.num_programs(1) - 1)
    def _():
        col = j * act.shape[1] + lax.broadcasted_iota(jnp.int32, act.shape, 1)
        o_ref[...] = jnp.where(col < n_valid, act, 0.0).astype(BF16)


def _ffn_in(x, gain, wg, wu, cg, cu, bg, bu, tm=1024, tn=512):
    L, D = x.shape
    n_valid = wg.shape[1]
    nj = pl.cdiv(n_valid, tn)
    taps = cg.shape[0]
    cols = lambda rows: pl.BlockSpec((rows, tn), lambda i, j: (0, j))
    return pl.pallas_call(
        functools.partial(_ffn_in_kernel, n_valid=n_valid),
        grid=(L // tm, nj),
        in_specs=[pl.BlockSpec((tm, D), lambda i, j: (i, 0)),
                  pl.BlockSpec((1, D), lambda i, j: (0, 0)),
                  cols(D), cols(D), cols(taps), cols(taps), cols(1), cols(1)],
        out_specs=pl.BlockSpec((tm, tn), lambda i, j: (i, j)),
        out_shape=jax.ShapeDtypeStruct((L, nj * tn), BF16),
        scratch_shapes=[pltpu.VMEM((tm, D), BF16), pltpu.VMEM((nj, 2, 8, tn), F32),
                        pltpu.VMEM((2, tm + 8, tn), F32)],
        compiler_params=_cparams(("arbitrary", "arbitrary")),
        name="ffn_in",
    )(x, gain.reshape(1, D), wg, wu, cg, cu, bg, bu)


def _ffn_out_kernel(a_ref, w_ref, x_ref, o_ref):
    @pl.when(pl.program_id(1) == 0)
    def _():
        o_ref[...] = x_ref[...]

    o_ref[...] += _dot(a_ref[...], w_ref[...])


def _ffn_out(act, w, x, tm=1024, tk=512):
    L, D = x.shape
    K = act.shape[1]
    return pl.pallas_call(
        _ffn_out_kernel,
        grid=(L // tm, K // tk),
        in_specs=[pl.BlockSpec((tm, tk), lambda i, k: (i, k)),
                  pl.BlockSpec((tk, D), lambda i, k: (k, 0)),
                  pl.BlockSpec((tm, D), lambda i, k: (i, 0))],
        out_specs=pl.BlockSpec((tm, D), lambda i, k: (i, 0)),
        out_shape=jax.ShapeDtypeStruct((L, D), F32),
        compiler_params=_cparams(("arbitrary", "arbitrary")),
        name="ffn_out",
    )(act, w, x)


def _rope_tables(pos):
    half = ROPE_DIMS // 2
    inv = ROPE_THETA ** (-jnp.arange(half, dtype=F32) / half)
    ang = pos.astype(F32)[:, None] * inv[None, :]
    n = pos.shape[0]
    cos = jnp.concatenate([jnp.cos(ang), jnp.cos(ang), jnp.ones((n, NSA_DH - ROPE_DIMS), F32)], axis=1)
    sin = jnp.concatenate([-jnp.sin(ang), jnp.sin(ang), jnp.zeros((n, NSA_DH - ROPE_DIMS), F32)], axis=1)
    return jnp.tile(cos, (1, 2)), jnp.tile(sin, (1, 2))


def _permute_w_in(w):
    sp = np.cumsum([0, 512, 128, 128, 128, 128, 128, 128, 24, 512, 3072, 1024, 8, 8])
    q, kc, vc, ks, vs, kw, vw, gates, u, qkv, z, a, b = [w[:, sp[n]:sp[n + 1]] for n in range(13)]
    pad = jnp.zeros((w.shape[0], PROJ_COLS - C_SM - SM_END), w.dtype)
    return jnp.concatenate([q, ks, kw, vs, vw, z, qkv, u, kc, vc, gates, a, b, pad], axis=1)


def _nsa_mixer(proj, small, p, l):
    L = proj.shape[0]
    HK, G, DH = NSA_KV, NSA_G, NSA_DH
    ns = L // SLC_BLOCK
    nb = L // CMP_STRIDE
    n_grp = LANES - DH
    assert ns % n_grp == 0 and (n_grp * SLC_BLOCK) % (SEL_UNROLL * TK) == 0 and SEL_UNROLL % 2 == 0
    cos, sin = _rope_tables(jnp.arange(L))
    gains = jnp.concatenate([jnp.tile(p['nsa_q_norm'][l], NSA_HEADS), jnp.tile(p['nsa_ks_norm'][l], HK),
                             jnp.tile(p['nsa_kw_norm'][l], HK), jnp.ones((2 * HK * DH,), F32)]).reshape(1, -1)
    qT, ksaug, kwa, vsx, vwx = _nsa_prep(proj, gains, cos, sin, n_grp)
    kpad = jnp.zeros((HK, WINDOW, KW_LANES), BF16).at[:, :, DH].set(1.0)
    kwp = jnp.concatenate([kpad, kwa], axis=1)
    vwxp = jnp.pad(vwx, ((0, 0), (0, 0), (WINDOW, 0)))

    eye = jnp.eye(HK, dtype=F32)
    pe = jnp.tile(p['cmp_pe'][l][:, None, :], (1, HK, 1)).reshape(1, CMP_LEN * HK * DH)
    w1 = lambda name: jnp.einsum('ldf,hg->lhdgf', p[name][l], eye).reshape(CMP_LEN * HK * DH, -1).astype(BF16)
    w2 = lambda name: jnp.einsum('fd,hg->hfgd', p[name][l], eye).reshape(-1, HK * DH).astype(BF16)
    ccos, csin = _rope_tables(jnp.arange(nb) * CMP_STRIDE + CMP_LEN // 2)
    gk = jnp.tile(p['nsa_kc_norm'][l], HK).reshape(1, LANES)
    kc, vcx = _nsa_compress(proj, pe, w1('cmp_k_w1'), w2('cmp_k_w2'), w1('cmp_v_w1'), w2('cmp_v_w2'), gk, ccos, csin)

    cch = min(CMP_CHUNK, nb)
    cmp_start = np.arange(cch) * CMP_STRIDE
    slc_start = np.arange(cch * CMP_STRIDE // SLC_BLOCK + 8) * SLC_BLOCK
    smapT = jnp.asarray((cmp_start[None, :] <= slc_start[:, None] + SLC_BLOCK - 1)
                        & (cmp_start[None, :] + CMP_LEN - 1 >= slc_start[:, None]), BF16)

    gT = jnp.pad(small[:, :HK * G * 3].reshape(L, HK, G * 3).transpose(1, 2, 0), ((0, 0), (0, 16 - G * 3), (0, 0)))
    return _nsa_attention(qT, kc, vcx, smapT, ksaug, vsx, kwp, vwxp, gT, n_grp)


def _s5_mixer(proj, p, l):
    L = proj.shape[0]
    T = S5_CHUNK
    nc = L // T
    G, H, P = S5_GROUPS, S5_CH, S5_STATE
    lam_re, lam_im = p['s5_lam_re'][l], p['s5_lam_im'][l]
    dt = jnp.exp(p['s5_log_dt'][l])[:, None]
    mag = jnp.exp(lam_re * dt)
    lb_re = mag * jnp.cos(lam_im * dt)
    lb_im = mag * jnp.sin(lam_im * dt)
    den = lam_re * lam_re + lam_im * lam_im
    nr, ni = lb_re - 1.0, lb_im
    coef_re = (nr * lam_re + ni * lam_im) / den
    coef_im = (ni * lam_re - nr * lam_im) / den
    b_re, b_im = p['s5_b_re'][l], p['s5_b_im'][l]
    bb_re = coef_re[..., None] * b_re - coef_im[..., None] * b_im
    bb_im = coef_re[..., None] * b_im + coef_im[..., None] * b_re
    bsr, bsi = bb_re.transpose(0, 2, 1), bb_im.transpose(0, 2, 1)
    return _s5_scan(proj, lb_re[:, None, :], lb_im[:, None, :], p['s5_c_re'][l], p['s5_c_im'][l], bsr, bsi)


def _gdn_mixer(proj, small, p, l):
    val, kcum, qg, kdT, qk, gl = _gdn_local(proj, p['gdn_conv'][l], small)
    return _gdn_scan(val, kcum, qg, qk, kdT, gl, proj, p['gdn_norm'][l].reshape(1, GDN_DH))


def _forward(x3, p):
    x = x3.reshape(x3.shape[1:])
    depth = p['w_in'].shape[0]
    for l in range(depth):
        w_in = _permute_w_in(p['w_in'][l].astype(BF16))
        proj = _rms_matmul(x, p['attn_norm'][l], w_in)
        sm_par = jnp.zeros((8, LANES), F32)
        sm_par = sm_par.at[0, SM_G:SM_BETA].set(p['gdn_dt_bias'][l]).at[1, SM_G:SM_BETA].set(-jnp.exp(p['gdn_a_log'][l]))
        small = _small(proj, sm_par)
        y_a = _nsa_mixer(proj, small, p, l)
        y_s = _s5_mixer(proj, p, l)
        y_c = _gdn_mixer(proj, small, p, l)
        x = _mix(y_a, y_s, proj, y_c, x, p['nsa_out_norm'][l].reshape(1, -1), p['s5_out_norm'][l].reshape(1, -1),
                 p['s5_d'][l].reshape(1, -1), p['s5_w_glu'][l].astype(BF16), p['w_out'][l].astype(BF16))
        wf, cf, bf = p['ffn_w_in'][l], p['ffn_conv'][l], p['ffn_conv_b'][l].reshape(1, -1)
        act = _ffn_in(x, p['ffn_norm'][l], wf[:, :D_FF].astype(BF16), wf[:, D_FF:].astype(BF16),
                      cf[:, :D_FF], cf[:, D_FF:], bf[:, :D_FF], bf[:, D_FF:])
        w_o = jnp.pad(p['ffn_w_out'][l].astype(BF16), ((0, D_FF_PAD - D_FF), (0, 0)))
        x = _ffn_out(act, w_o, x)
    return x.reshape(x3.shape)


_PARAM_NAMES = ('attn_norm', 'w_in', 'nsa_q_norm', 'nsa_kc_norm', 'nsa_ks_norm', 'nsa_kw_norm', 'cmp_pe',
                'cmp_k_w1', 'cmp_k_w2', 'cmp_v_w1', 'cmp_v_w2', 'nsa_out_norm', 's5_lam_re', 's5_lam_im',
                's5_log_dt', 's5_b_re', 's5_b_im', 's5_c_re', 's5_c_im', 's5_d', 's5_w_glu', 's5_out_norm',
                'gdn_conv', 'gdn_a_log', 'gdn_dt_bias', 'gdn_norm', 'w_out', 'ffn_norm', 'ffn_w_in', 'ffn_conv',
                'ffn_conv_b', 'ffn_w_out')


def kernel(x, attn_norm, w_in, nsa_q_norm, nsa_kc_norm, nsa_ks_norm, nsa_kw_norm, cmp_pe, cmp_k_w1, cmp_k_w2,
           cmp_v_w1, cmp_v_w2, nsa_out_norm, s5_lam_re, s5_lam_im, s5_log_dt, s5_b_re, s5_b_im, s5_c_re, s5_c_im,
           s5_d, s5_w_glu, s5_out_norm, gdn_conv, gdn_a_log, gdn_dt_bias, gdn_norm, w_out, ffn_norm, ffn_w_in,
           ffn_conv, ffn_conv_b, ffn_w_out):
    vals = (attn_norm, w_in, nsa_q_norm, nsa_kc_norm, nsa_ks_norm, nsa_kw_norm, cmp_pe, cmp_k_w1, cmp_k_w2,
            cmp_v_w1, cmp_v_w2, nsa_out_norm, s5_lam_re, s5_lam_im, s5_log_dt, s5_b_re, s5_b_im, s5_c_re, s5_c_im,
            s5_d, s5_w_glu, s5_out_norm, gdn_conv, gdn_a_log, gdn_dt_bias, gdn_norm, w_out, ffn_norm, ffn_w_in,
            ffn_conv, ffn_conv_b, ffn_w_out)
    return _forward(x, dict(zip(_PARAM_NAMES, vals)))
```

```python
import functools
import math

import jax
import jax.numpy as jnp
import numpy as np
from jax import lax
from jax.experimental import pallas as pl
from jax.experimental.pallas import tpu as pltpu

F32 = jnp.float32
BF16 = jnp.bfloat16
HIGHEST = lax.Precision.HIGHEST

NSA_HEADS = 8
NSA_KV = 2
NSA_G = NSA_HEADS // NSA_KV
NSA_DH = 64
CMP_LEN = 32
CMP_STRIDE = 16
SLC_BLOCK = 64
SLC_TOPK = 16
WINDOW = 512
ROPE_THETA = 500000.0
ROPE_DIMS = NSA_DH // 4
S5_WIDTH = 512
S5_CH = 16
S5_GROUPS = 32
S5_STATE = 64
S5_CHUNK = 32
S5_SLAB = 8
GDN_HEADS = 8
GDN_DH = 128
GDN_WIDTH = 1024
GDN_CHUNK = 64
D_FF = 5504
D_FF_PAD = 5632
EPS = 1e-6
NEG_BIG = -(2.0 ** 100)
LOG2_E = math.log2(math.e)
V_ROWS = 80
KW_LANES = 128
CMP_CHUNK = 256
SEL_UNROLL = 2

LANES = 128
TQ = 128
TK = 512
VMEM_LIMIT = 56 * 1024 * 1024

C_Q, C_KS, C_KW, C_VS, C_VW = 0, 512, 640, 768, 896
C_Z, C_QKV, C_U, C_KC, C_VC, C_SM = 1024, 2048, 5120, 5632, 5760, 5888
NSA_PREP_W = C_Z
PROJ_COLS = 6144
SM_GATES, SM_G, SM_BETA, SM_END = 0, 24, 32, 40


def _cparams(sem):
    return pltpu.CompilerParams(dimension_semantics=sem, vmem_limit_bytes=VMEM_LIMIT)


def _dot(a, b, precision=None):
    return jnp.dot(a, b, preferred_element_type=F32, precision=precision)


def _dot_nt(a, b, precision=None):
    return lax.dot_general(a, b, (((1,), (1,)), ((), ())), preferred_element_type=F32, precision=precision)


def _gelu(x):
    return x * (0.5 * (1.0 + jnp.tanh(math.sqrt(2.0 / math.pi) * (x + 0.044715 * (x * x * x)))))


def _sigmoid(x):
    return 1.0 / (1.0 + jnp.exp(-x))


def _rms_matmul_kernel(x_ref, g_ref, w_ref, o_ref, xn_ref):
    @pl.when(pl.program_id(1) == 0)
    def _():
        xf = x_ref[...]
        ms = jnp.mean(xf * xf, axis=-1, keepdims=True)
        xn_ref[...] = (xf * lax.rsqrt(ms + EPS) * g_ref[...]).astype(BF16)

    o_ref[...] = _dot(xn_ref[...], w_ref[...])


def _rms_matmul(x, gain, w, tm=512, tn=2048):
    L, D = x.shape
    N = w.shape[1]
    return pl.pallas_call(
        _rms_matmul_kernel,
        grid=(L // tm, N // tn),
        in_specs=[
            pl.BlockSpec((tm, D), lambda i, j: (i, 0)),
            pl.BlockSpec((1, D), lambda i, j: (0, 0)),
            pl.BlockSpec((D, tn), lambda i, j: (0, j)),
        ],
        out_specs=pl.BlockSpec((tm, tn), lambda i, j: (i, j)),
        out_shape=jax.ShapeDtypeStruct((L, N), F32),
        scratch_shapes=[pltpu.VMEM((tm, D), BF16)],
        compiler_params=_cparams(("arbitrary", "arbitrary")),
        name="rms_proj",
    )(x, gain.reshape(1, D), w)


def _rope_slab(x, cos, sin):
    d = lax.broadcasted_iota(jnp.int32, x.shape, 1) & (NSA_DH - 1)
    half = ROPE_DIMS // 2
    partner = jnp.where(d < half, pltpu.roll(x, LANES - half, 1), pltpu.roll(x, half, 1))
    return x * cos + partner * sin


def _nsa_prep_kernel(x_ref, g_ref, cos_ref, sin_ref, bsum_ref, qT_ref, ksaug_ref, kwp_ref, vsx_ref, vwxp_ref, *, n_grp):
    step = pl.program_id(0)
    is_pad = step == 0
    i = jnp.maximum(step - 1, 0)
    tm = x_ref.shape[0]
    cos = cos_ref[...]
    sin = sin_ref[...]
    bsum = bsum_ref[...]
    lane = lax.broadcasted_iota(jnp.int32, (tm, LANES), 1)
    tiles = range(tm // TQ)

    def normed(s):
        sl = slice(s * LANES, (s + 1) * LANES)
        x = x_ref[:, sl]
        ms = _dot(x * x, bsum, precision=HIGHEST)
        return _rope_slab(x * lax.rsqrt(ms + EPS) * g_ref[:, sl], cos, sin)

    def per_head(y):
        return y, pltpu.roll(y, NSA_DH, 1)

    for s in range((C_KS - C_Q) // LANES):
        y = normed(s) * (NSA_DH ** -0.5 * LOG2_E)
        hk, g0 = divmod(s, NSA_G // 2)
        for t in tiles:
            yT = y[t * TQ:(t + 1) * TQ, :].T.astype(BF16)
            for half in range(2):
                g = 2 * g0 + half
                qT_ref[hk, t, :, g * TQ:(g + 1) * TQ] = yT[half * NSA_DH:(half + 1) * NSA_DH, :]

    blk = ((i * tm + lax.broadcasted_iota(jnp.int32, (tm, LANES), 0)) // SLC_BLOCK) % n_grp
    onehot = jnp.where(lane - NSA_DH == blk, 1.0, 0.0)
    for hk, y in enumerate(per_head(normed(C_KS // LANES))):
        ksaug_ref[hk] = jnp.where(lane < NSA_DH, y, onehot).astype(BF16)
    pad_rows = jnp.where(lane == NSA_DH, 1.0, 0.0)
    for hk, y in enumerate(per_head(normed(C_KW // LANES))):
        kwp_ref[hk] = jnp.where(is_pad, pad_rows, jnp.where(lane < NSA_DH, y, 0.0)).astype(BF16)

    ones_rows = jnp.where(lax.broadcasted_iota(jnp.int32, (V_ROWS - NSA_DH, tm), 0) == 0, 1.0, 0.0).astype(BF16)
    for c0, v_ref, padded in ((C_VS, vsx_ref, False), (C_VW, vwxp_ref, True)):
        x = x_ref[:, c0:c0 + LANES]
        if padded:
            x = jnp.where(is_pad, 0.0, x)
        for t in tiles:
            xT = x[t * TQ:(t + 1) * TQ, :].T.astype(BF16)
            for hk in range(NSA_KV):
                v_ref[hk, 0:NSA_DH, t * TQ:(t + 1) * TQ] = xT[hk * NSA_DH:(hk + 1) * NSA_DH, :]
        for hk in range(NSA_KV):
            v_ref[hk, NSA_DH:V_ROWS, :] = ones_rows


def _nsa_prep(proj, gains, cos, sin, n_grp, tm=512):
    L = proj.shape[0]
    W = NSA_PREP_W
    HK = NSA_KV
    GW = NSA_G * TQ
    bsum = jnp.asarray(np.kron(np.eye(2), np.ones((NSA_DH, NSA_DH))) / NSA_DH, F32)
    assert tm == WINDOW
    tile = lambda s: jnp.maximum(s - 1, 0)
    return pl.pallas_call(
        functools.partial(_nsa_prep_kernel, n_grp=n_grp),
        grid=(L // tm + 1,),
        in_specs=[
            pl.BlockSpec((tm, W), lambda s: (tile(s), 0)),
            pl.BlockSpec((1, W), lambda s: (0, 0)),
            pl.BlockSpec((tm, LANES), lambda s: (tile(s), 0)),
            pl.BlockSpec((tm, LANES), lambda s: (tile(s), 0)),
            pl.BlockSpec((LANES, LANES), lambda s: (0, 0)),
        ],
        out_specs=[pl.BlockSpec((HK, tm // TQ, NSA_DH, GW), lambda s: (0, tile(s), 0, 0)),
                   pl.BlockSpec((HK, tm, LANES), lambda s: (0, tile(s), 0)),
                   pl.BlockSpec((HK, tm, LANES), lambda s: (0, s, 0)),
                   pl.BlockSpec((HK, V_ROWS, tm), lambda s: (0, 0, tile(s))),
                   pl.BlockSpec((HK, V_ROWS, tm), lambda s: (0, 0, s))],
        out_shape=[jax.ShapeDtypeStruct((HK, L // TQ, NSA_DH, GW), BF16),
                   jax.ShapeDtypeStruct((HK, L, LANES), BF16), jax.ShapeDtypeStruct((HK, L + WINDOW, KW_LANES), BF16),
                   jax.ShapeDtypeStruct((HK, V_ROWS, L), BF16), jax.ShapeDtypeStruct((HK, V_ROWS, L + WINDOW), BF16)],
        compiler_params=_cparams(("arbitrary",)),
        name="nsa_prep",
    )(proj, gains, cos, sin, bsum)


def _cmp_kernel(xk_ref, xv_ref, pe_ref, w1k_ref, w2k_ref, w1v_ref, w2v_ref, gk_ref, cos_ref, sin_ref, bsum_ref,
                kc_ref, vcx_ref):
    nb = kc_ref.shape[1]
    half_w = CMP_STRIDE * LANES

    def mlp(x_ref, w1_ref, w2_ref):
        hb = jnp.concatenate([x_ref[pl.ds(l, nb, stride=CMP_STRIDE), :] for l in range(CMP_STRIDE)], axis=1)
        a = (hb + pe_ref[:, 0:half_w]).astype(BF16)
        b = (hb + pe_ref[:, half_w:2 * half_w]).astype(BF16)
        p1 = _dot(a, w1_ref[0:half_w, :])
        p2 = _dot(b, w1_ref[half_w:2 * half_w, :])
        h = p1 + pltpu.roll(p2, nb - 1, 0)
        return _dot(_gelu(h).astype(BF16), w2_ref[...])

    rid = lax.broadcasted_iota(jnp.int32, (nb, LANES), 0)
    real = rid < nb - 1
    kc = jnp.where(real, mlp(xk_ref, w1k_ref, w2k_ref), 0.0)
    vc = jnp.where(real, mlp(xv_ref, w1v_ref, w2v_ref), 0.0)
    ms = _dot(kc * kc, bsum_ref[...], precision=HIGHEST)
    kcn = _rope_slab(kc * lax.rsqrt(ms + EPS) * gk_ref[...], cos_ref[...], sin_ref[...])
    kc_ref[0] = kcn[:, 0:NSA_DH].astype(BF16)
    kc_ref[1] = pltpu.roll(kcn, NSA_DH, 1)[:, 0:NSA_DH].astype(BF16)
    ones_rows = jnp.where(lax.broadcasted_iota(jnp.int32, (V_ROWS - NSA_DH, nb), 0) == 0, 1.0, 0.0).astype(BF16)
    for t in range(nb // LANES):
        vT = vc[t * LANES:(t + 1) * LANES, :].T.astype(BF16)
        for hk in range(NSA_KV):
            vcx_ref[hk, 0:NSA_DH, t * LANES:(t + 1) * LANES] = vT[hk * NSA_DH:(hk + 1) * NSA_DH, :]
    for hk in range(NSA_KV):
        vcx_ref[hk, NSA_DH:V_ROWS, :] = ones_rows


def _nsa_compress(proj, pe, w1k, w2k, w1v, w2v, gk, cos, sin):
    L = proj.shape[0]
    nb = L // CMP_STRIDE
    bsum = jnp.asarray(np.kron(np.eye(2), np.ones((NSA_DH, NSA_DH))) / NSA_DH, F32)
    once = lambda shape, idx: pl.BlockSpec(shape, lambda i: idx, pipeline_mode=pl.Buffered(1))
    full = lambda a: once(a.shape, (0,) * a.ndim)
    return pl.pallas_call(
        _cmp_kernel,
        grid=(1,),
        in_specs=[once((L, LANES), (0, C_KC // LANES)), once((L, LANES), (0, C_VC // LANES)),
                  full(pe), full(w1k), full(w2k), full(w1v), full(w2v), full(gk), full(cos), full(sin), full(bsum)],
        out_specs=[pl.BlockSpec((NSA_KV, nb, NSA_DH), lambda i: (0, 0, 0)),
                   pl.BlockSpec((NSA_KV, V_ROWS, nb), lambda i: (0, 0, 0))],
        out_shape=[jax.ShapeDtypeStruct((NSA_KV, nb, NSA_DH), BF16),
                   jax.ShapeDtypeStruct((NSA_KV, V_ROWS, nb), BF16)],
        compiler_params=_cparams(("arbitrary",)),
        name="nsa_compress",
    )(proj, proj, pe, w1k, w2k, w1v, w2v, gk, cos, sin, bsum)


def _nsa_attn_kernel(qT_ref, kc_ref, vcx_ref, smap_ref, ksaug_ref, vsx_ref, kwp_ref, vwxp_ref, gT_ref,
                     o_ref, qaug_ref, qwin_ref, ss_ref, mt_ref, m_ref, acc_ref, sc_ref, imp_ref, *, n_grp):
    i = pl.program_id(0)
    s0 = i * TQ
    hks = range(qT_ref.shape[0])
    nb = kc_ref.shape[1]
    ns = qaug_ref.shape[1] * n_grp
    GW = NSA_G * TQ
    qT = [qT_ref[h, 0] for h in hks]
    tq = s0 + (lax.broadcasted_iota(jnp.int32, (1, GW), 1) & (TQ - 1))

    cch = min(CMP_CHUNK, nb)
    n_cch = (s0 + TQ - CMP_LEN) // (CMP_STRIDE * cch) + 1
    nthr = (tq - (CMP_LEN - 1)) // CMP_STRIDE

    def cmp_scores(c, mcs):
        r0c = pl.multiple_of(c * cch, cch)
        s = [_dot(kc_ref[h, pl.ds(r0c, cch), :], qT[h]) for h in hks]
        nrow = r0c + lax.broadcasted_iota(jnp.int32, (cch, GW), 0)
        out = []
        for h in hks:
            sh = jnp.where(nrow <= nthr, s[h], NEG_BIG)
            sc_ref[h, pl.ds(r0c, cch), :] = sh
            out.append(jnp.maximum(mcs[h], jnp.max(sh, axis=0, keepdims=True)))
        return tuple(out)

    mc = lax.fori_loop(0, n_cch, cmp_scores, tuple(jnp.full((1, GW), NEG_BIG, F32) for _ in hks))
    imp_ref[...] = jnp.zeros_like(imp_ref)
    ones16 = jnp.ones((16, cch), BF16)
    VX = vcx_ref.shape[1]

    sm = smap_ref[...]
    sm_rows = sm.shape[0]
    blocks_per_chunk = cch * CMP_STRIDE // SLC_BLOCK

    def cmp_accum(c, carry):
        r0c = pl.multiple_of(c * cch, cch)
        j0 = pl.multiple_of(c * blocks_per_chunk, blocks_per_chunk)
        parts = [_split2(jnp.exp2(sc_ref[h, pl.ds(r0c, cch), :] - mc[h])) for h in hks]
        for h in hks:
            imp_ref[h, pl.ds(j0, sm_rows), :] += _dot(sm, parts[h][0]) + _dot(sm, parts[h][1])
        return tuple((carry[h][0] + _dot(vcx_ref[h, :, pl.ds(r0c, cch)], parts[h][0]),
                      carry[h][1] + _dot(ones16, parts[h][1])) for h in hks)

    cacc = lax.fori_loop(0, n_cch, cmp_accum,
                         tuple((jnp.zeros((VX, GW), F32), jnp.zeros((16, GW), F32)) for _ in hks))
    inv_l = [jnp.where(tq >= CMP_LEN - 1, 1.0 / (cacc[h][0][NSA_DH:NSA_DH + 1] + cacc[h][1][0:1]), 0.0) for h in hks]
    ocT = [cacc[h][0][0:NSA_DH] * inv_l[h] for h in hks]

    r0 = pl.multiple_of(s0, TQ)
    rel = lax.broadcasted_iota(jnp.int32, (TQ, GW), 0)
    qrel = tq - s0
    flag_row = lax.broadcasted_iota(jnp.int32, (KW_LANES - NSA_DH, GW), 0) == 0
    for h in hks:
        qwin_ref[h, 0:NSA_DH, :] = qT[h]
        qwin_ref[h, NSA_DH:KW_LANES, :] = jnp.where(flag_row, NEG_BIG, 0.0).astype(BF16)
    qwin = [qwin_ref[h] for h in hks]
    edge = [_dot(kwp_ref[h, pl.ds(r0, TQ), :], qwin[h]) for h in hks]
    mid = [_dot(kwp_ref[h, pl.ds(r0 + TQ, WINDOW - TQ), :], qwin[h]) for h in hks]
    diag = [_dot(kwp_ref[h, pl.ds(r0 + WINDOW, TQ), :], qwin[h]) for h in hks]
    edge16 = [jnp.where(rel > qrel, edge[h], NEG_BIG).astype(BF16) for h in hks]
    diag16 = [jnp.where(rel <= qrel, diag[h], NEG_BIG).astype(BF16) for h in hks]
    mid16 = [mid[h].astype(BF16) for h in hks]
    mw = [jnp.maximum(jnp.maximum(jnp.max(edge16[h], axis=0, keepdims=True), jnp.max(mid16[h], axis=0, keepdims=True)),
                      jnp.max(diag16[h], axis=0, keepdims=True)) for h in hks]
    accw = [(_dot(vwxp_ref[h, :, pl.ds(r0, TQ)], jnp.exp2(edge16[h] - mw[h]))
             + _dot(vwxp_ref[h, :, pl.ds(r0 + TQ, WINDOW - TQ)], jnp.exp2(mid16[h] - mw[h]))
             + _dot(vwxp_ref[h, :, pl.ds(r0 + WINDOW, TQ)], jnp.exp2(diag16[h] - mw[h]))) for h in hks]
    owT = [accw[h][0:NSA_DH] / accw[h][NSA_DH:NSA_DH + 1] for h in hks]
    sd = [_dot(ksaug_ref[h, pl.ds(r0, TQ), 0:NSA_DH], qT[h]) for h in hks]
    sd16 = [jnp.where(rel <= qrel, sd[h], NEG_BIG).astype(BF16) for h in hks]

    imp = []
    for h in hks:
        tot = imp_ref[h, 0:ns, 0:TQ] * inv_l[h][:, 0:TQ]
        for g in range(1, NSA_G):
            tot = tot + imp_ref[h, 0:ns, g * TQ:(g + 1) * TQ] * inv_l[h][:, g * TQ:(g + 1) * TQ]
        imp.append(tot)

    jrow = lax.broadcasted_iota(jnp.int32, (ns, TQ), 0)
    cur = (s0 + lax.broadcasted_iota(jnp.int32, (1, TQ), 1)) // SLC_BLOCK
    forced = jnp.where(jrow == 0, 0.0, jnp.where(jrow == cur, 0.0, jnp.where(jrow == cur - 1, 0.0, NEG_BIG)))
    val = [jnp.where(jrow >= 1, jnp.where(jrow <= cur - 2, imp[h], -jnp.inf), -jnp.inf) for h in hks]
    for _ in range(min(SLC_TOPK, ns) - 3):
        mx = [jnp.max(val[h], axis=0, keepdims=True) for h in hks]
        idx = [jnp.min(jnp.where(val[h] == mx[h], jrow, ns), axis=0, keepdims=True) for h in hks]
        val = [jnp.where(jrow == idx[h], -jnp.inf, val[h]) for h in hks]
    bias = [jnp.where(jrow >= 1, jnp.where(jrow <= cur - 2, jnp.where(val[h] == -jnp.inf, 0.0, NEG_BIG), forced), forced)
            for h in hks]
    own = jrow // 2 == i
    for h in hks:
        bias16 = jnp.where(own, NEG_BIG, bias[h]).astype(BF16)
        for grp in range(ns // n_grp):
            qaug_ref[h, grp, 0:NSA_DH, :] = qT[h]
            for g in range(NSA_G):
                qaug_ref[h, grp, NSA_DH:NSA_DH + n_grp, g * TQ:(g + 1) * TQ] = bias16[grp * n_grp:(grp + 1) * n_grp, :]

    tiles_per_grp = (n_grp * SLC_BLOCK) // TK
    n_body = jnp.maximum((s0 + SEL_UNROLL * TK - 1) // (SEL_UNROLL * TK), 1)

    def scores(kt, slot):
        k0 = pl.multiple_of(kt * TK, TK)
        grp = kt // tiles_per_grp
        s = [_dot(ksaug_ref[h, pl.ds(k0, TK), :], qaug_ref[h, grp]) for h in hks]
        for h in hks:
            sb = s[h].astype(BF16)
            ss_ref[h, slot] = sb
            mt_ref[h, slot] = jnp.max(sb, axis=0, keepdims=True).astype(F32)

    def online_update(sb, mt, v):
        m = [m_ref[h] for h in hks]
        mn = [jnp.maximum(m[h], mt[h]) for h in hks]
        pv = [_dot(v[h], jnp.exp2(sb[h] - mn[h].astype(BF16))) for h in hks]
        for h in hks:
            acc_ref[h] = jnp.exp2(m[h] - mn[h]) * acc_ref[h] + pv[h]
            m_ref[h] = mn[h]

    def accumulate(kt, slot):
        k0 = pl.multiple_of(kt * TK, TK)
        online_update([ss_ref[h, slot] for h in hks], [mt_ref[h, slot] for h in hks],
                      [vsx_ref[h, :, pl.ds(k0, TK)] for h in hks])

    m_ref[...] = jnp.full(m_ref.shape, NEG_BIG, F32)
    acc_ref[...] = jnp.zeros_like(acc_ref)

    scores(0, 0)

    def tile_group(base, last):
        for u in range(SEL_UNROLL):
            if not (last and u == SEL_UNROLL - 1):
                scores(base + u + 1, (u + 1) % 2)
            accumulate(base + u, u % 2)

    def loop_body(j, carry):
        tile_group(SEL_UNROLL * j, last=False)
        return carry

    lax.fori_loop(0, n_body - 1, loop_body, 0)

    tile_group(SEL_UNROLL * (n_body - 1), last=True)

    online_update(sd16, [jnp.max(sd16[h], axis=0, keepdims=True).astype(F32) for h in hks],
                  [vsx_ref[h, :, pl.ds(r0, TQ)] for h in hks])
    osT = [acc_ref[h, 0:NSA_DH, :] / acc_ref[h, NSA_DH:NSA_DH + 1, :] for h in hks]

    for h in hks:
        gates = _sigmoid(gT_ref[h])
        outs = []
        for g in range(NSA_G):
            sl = slice(g * TQ, (g + 1) * TQ)
            outs.append(gates[3 * g:3 * g + 1] * ocT[h][:, sl] + gates[3 * g + 1:3 * g + 2] * osT[h][:, sl]
                        + gates[3 * g + 2:3 * g + 3] * owT[h][:, sl])
        for g in range(0, NSA_G, 2):
            c0 = (h * NSA_G + g) * NSA_DH
            o_ref[:, c0:c0 + 2 * NSA_DH] = jnp.concatenate([outs[g], outs[g + 1]], axis=0).T


def _nsa_attention(qT, kc, vcx, smapT, ksaug, vsx, kwp, vwxp, gT, n_grp):
    HK, nQ, _, GW = qT.shape
    nb = kc.shape[1]
    L = vsx.shape[2]
    ns = L // SLC_BLOCK
    KA = ksaug.shape[2]
    scratch = [pltpu.VMEM((HK, ns // n_grp, KA, GW), BF16),
               pltpu.VMEM((HK, KW_LANES, GW), BF16),
               pltpu.VMEM((HK, 2, TK, GW), BF16),
               pltpu.VMEM((HK, 2, 1, GW), F32),
               pltpu.VMEM((HK, 1, GW), F32),
               pltpu.VMEM((HK, V_ROWS, GW), F32),
               pltpu.VMEM((HK, nb, GW), F32),
               pltpu.VMEM((HK, ns + 8, GW), F32)]
    resident = lambda a: pl.BlockSpec(a.shape, lambda i: (0,) * a.ndim, pipeline_mode=pl.Buffered(1))
    return pl.pallas_call(
        functools.partial(_nsa_attn_kernel, n_grp=n_grp),
        grid=(nQ,),
        in_specs=[
            pl.BlockSpec((HK, 1, NSA_DH, GW), lambda i: (0, i, 0, 0)),
            resident(kc), resident(vcx), resident(smapT), resident(ksaug), resident(vsx), resident(kwp),
            resident(vwxp),
            pl.BlockSpec((HK, 16, TQ), lambda i: (0, 0, i)),
        ],
        out_specs=pl.BlockSpec((TQ, NSA_HEADS * NSA_DH), lambda i: (i, 0)),
        out_shape=jax.ShapeDtypeStruct((nQ * TQ, NSA_HEADS * NSA_DH), F32),
        scratch_shapes=scratch,
        compiler_params=_cparams(("arbitrary",)),
        name="nsa_attn",
    )(qT, kc, vcx, smapT, ksaug, vsx, kwp, vwxp, gT)


def _s5_kernel(x_ref, lbr_ref, lbi_ref, cr_ref, ci_ref, bsr_ref, bsi_ref, y_ref, lm_ref, yg_ref):
    T = S5_CHUNK
    nc = x_ref.shape[0] // T
    u_tau = [x_ref[pl.ds(tau, nc, stride=T), :] for tau in range(T)]
    for gi in range(S5_SLAB):
        lanes = slice(gi * S5_CH, (gi + 1) * S5_CH)
        u = jnp.concatenate([ut[:, lanes] for ut in u_tau], axis=1).astype(BF16)
        yg_ref[gi] = _s5_group(u, lbr_ref[gi], lbi_ref[gi], cr_ref[gi], ci_ref[gi], bsr_ref[gi], bsi_ref[gi], lm_ref)
    for tau in range(T):
        cols = slice(tau * S5_CH, (tau + 1) * S5_CH)
        y_ref[pl.ds(tau, nc, stride=T), :] = jnp.concatenate([yg_ref[gi, :, cols] for gi in range(S5_SLAB)], axis=1)


def _s5_group(u, ar, ai, cr, ci, bsr, bsi, lm_ref):
    T = S5_CHUNK
    W = T * S5_CH
    tile_rows = lambda a: jnp.concatenate([a] * T, axis=0)
    ctr, cti, btr, bti = tile_rows(cr), tile_rows(ci), tile_rows(bsr), tile_rows(bsi)
    delta = lax.broadcasted_iota(jnp.int32, (W, S5_STATE), 0) // S5_CH

    def powers(e):
        pr = jnp.ones((W, S5_STATE), F32)
        pi = jnp.zeros((W, S5_STATE), F32)
        fr, fi = ar, ai
        for b in range(T.bit_length() - 1):
            bit = ((e >> b) & 1) == 1
            nr = pr * fr - pi * fi
            ni = pr * fi + pi * fr
            pr = jnp.where(bit, nr, pr)
            pi = jnp.where(bit, ni, pi)
            fr, fi = fr * fr - fi * fi, 2.0 * fr * fi
        return pr, pi, fr, fi

    pwr, pwi, aTr, aTi = powers(delta)
    rvr, rvi, _, _ = powers(T - 1 - delta)
    car = ctr * pwr - cti * pwi
    cai = ctr * pwi + cti * pwr

    kw = _dot_nt(bsr, car, HIGHEST) - _dot_nt(bsi, cai, HIGHEST)
    lane = lax.broadcasted_iota(jnp.int32, (S5_CH, W), 1)
    for tau in range(T):
        sh = S5_CH * tau
        blk = kw if tau == 0 else jnp.where(lane >= sh, pltpu.roll(kw, sh, 1), 0.0)
        lm_ref[sh:sh + S5_CH, :] = blk.astype(BF16)

    nc = u.shape[0]
    y = _dot(u, lm_ref[...])

    sr = _dot(u, (rvr * btr - rvi * bti).astype(BF16))
    si = _dot(u, (rvr * bti + rvi * btr).astype(BF16))
    rowc = lax.broadcasted_iota(jnp.int32, (nc, S5_STATE), 0)
    fr, fi = aTr, aTi
    step = 1
    while step < nc:
        shr = jnp.where(rowc >= step, pltpu.roll(sr, step, 0), 0.0)
        shi = jnp.where(rowc >= step, pltpu.roll(si, step, 0), 0.0)
        sr, si = sr + fr * shr - fi * shi, si + fr * shi + fi * shr
        fr, fi = fr * fr - fi * fi, 2.0 * fr * fi
        step *= 2
    xr = jnp.where(rowc >= 1, pltpu.roll(sr, 1, 0), 0.0)
    xi = jnp.where(rowc >= 1, pltpu.roll(si, 1, 0), 0.0)
    c1r = car * ar - cai * ai
    c1i = car * ai + cai * ar
    return y + _dot_nt(xr.astype(BF16), c1r.astype(BF16)) - _dot_nt(xi.astype(BF16), c1i.astype(BF16))


def _s5_scan(proj, lbr, lbi, cr, ci, bsr, bsi):
    L = proj.shape[0]
    W = S5_CHUNK * S5_CH
    nc = L // S5_CHUNK
    G = lbr.shape[0]
    slab = lambda shape: pl.BlockSpec((S5_SLAB,) + shape, lambda s: (s, 0, 0))
    return pl.pallas_call(
        _s5_kernel,
        grid=(G // S5_SLAB,),
        in_specs=[pl.BlockSpec((L, LANES), lambda s: (0, C_U // LANES + s)),
                  slab((1, S5_STATE)), slab((1, S5_STATE)),
                  slab((S5_CH, S5_STATE)), slab((S5_CH, S5_STATE)), slab((S5_CH, S5_STATE)), slab((S5_CH, S5_STATE))],
        out_specs=pl.BlockSpec((L, LANES), lambda s: (0, s)),
        out_shape=jax.ShapeDtypeStruct((L, G * S5_CH), F32),
        scratch_shapes=[pltpu.VMEM((W, W), BF16), pltpu.VMEM((S5_SLAB, nc, W), F32)],
        compiler_params=_cparams(("arbitrary",)),
        name="s5_scan",
    )(proj, lbr, lbi, cr, ci, bsr, bsi)


def _small_kernel(x_ref, p_ref, o_ref):
    x = x_ref[...]
    lane = lax.broadcasted_iota(jnp.int32, x.shape, 1)
    z = x + p_ref[0:1, :]
    softplus = jnp.maximum(z, 0.0) + jnp.log(1.0 + jnp.exp(-jnp.abs(z)))
    gdec = p_ref[1:2, :] * softplus
    beta = _sigmoid(x)
    o_ref[...] = jnp.where(lane < SM_G, x, jnp.where(lane < SM_BETA, gdec, beta))


def _small(proj, params, tm=1024):
    L = proj.shape[0]
    cb = C_SM // LANES
    return pl.pallas_call(
        _small_kernel,
        grid=(L // tm,),
        in_specs=[pl.BlockSpec((tm, LANES), lambda i: (i, cb)), pl.BlockSpec((8, LANES), lambda i: (0, 0))],
        out_specs=pl.BlockSpec((tm, LANES), lambda i: (i, 0)),
        out_shape=jax.ShapeDtypeStruct((L, LANES), F32),
        compiler_params=_cparams(("arbitrary",)),
        name="small_cols",
    )(proj, params)


GDN_CB = 8


def _split2(x):
    hi = x.astype(BF16)
    return hi, (x - hi.astype(F32)).astype(BF16)


def _dot_split(a, b):
    ah, al = _split2(a)
    bh, bl = _split2(b)
    return _dot(ah, bh) + (_dot(ah, bl) + _dot(al, bh))


def _gdn_local_kernel(xq_ref, xk_ref, xv_ref, cwq_ref, cwk_ref, cwv_ref, sm_ref,
                      val_ref, kcum_ref, qg_ref, kdT_ref, qk_ref, gl_ref, carry_ref, hs_ref):
    C = GDN_CHUNK
    R = xq_ref.shape[0]

    @pl.when(pl.program_id(1) == 0)
    def _():
        carry_ref[...] = jnp.zeros_like(carry_ref)

    def conv_silu(x_ref, cw_ref, slot):
        x = x_ref[...]
        hs_ref[slot, 0:8, :] = carry_ref[slot]
        hs_ref[slot, 8:R + 8, :] = x
        carry_ref[slot] = x[R - 8:R, :]
        taps = cw_ref.shape[0]
        y = cw_ref[taps - 1:taps, :] * x
        for back in range(1, taps):
            y = y + cw_ref[taps - 1 - back:taps - back, :] * hs_ref[slot, 8 - back:8 - back + R, :]
        half_y = 0.5 * y
        return half_y * (1.0 + jnp.tanh(half_y))

    def l2norm(y):
        return y * lax.rsqrt(jnp.sum(y * y, axis=-1, keepdims=True) + EPS)

    q_all = l2norm(conv_silu(xq_ref, cwq_ref, 0)) * (GDN_DH ** -0.5)
    k_all = l2norm(conv_silu(xk_ref, cwk_ref, 1))
    v_all = conv_silu(xv_ref, cwv_ref, 2)
    chunks = range(GDN_CB)
    rows = [slice(c * C, (c + 1) * C) for c in chunks]
    ii = lax.broadcasted_iota(jnp.int32, (C, C), 0)
    jj = lax.broadcasted_iota(jnp.int32, (C, C), 1)
    causal = ii >= jj
    strict = ii > jj
    eye = (ii == jj).astype(F32)
    tril16 = causal.astype(BF16)
    row2 = lax.broadcasted_iota(jnp.int32, (C, 2 * C), 0)
    col2 = lax.broadcasted_iota(jnp.int32, (C, 2 * C), 1)
    keep = row2 > jnp.where(col2 < C, -1, col2 - C)

    head = pl.program_id(0)
    sm = sm_ref[...]
    sm_lane = lax.broadcasted_iota(jnp.int32, sm.shape, 1)
    g_col = jnp.sum(jnp.where(sm_lane == SM_G + head, sm, 0.0), axis=1, keepdims=True)
    b_col = jnp.sum(jnp.where(sm_lane == SM_BETA + head, sm, 0.0), axis=1, keepdims=True)

    k = [k_all[rs, :] for rs in rows]
    beta = [b_col[rs, :] for rs in rows]
    res = []
    for rs in rows:
        x = jnp.where(keep, jnp.broadcast_to(g_col[rs, :], (C, 2 * C)), 0.0)
        hi = x.astype(BF16)
        r1 = x - hi.astype(F32)
        mid = r1.astype(BF16)
        lo = (r1 - mid.astype(F32)).astype(BF16)
        res.append(_dot(tril16, hi) + (_dot(tril16, mid) + _dot(tril16, lo)))
    gc = [jnp.broadcast_to(r[:, 0:1], (C, GDN_DH)) for r in res]
    decay = [jnp.where(causal, jnp.exp(r[:, C:2 * C]), 0.0) for r in res]
    kb = [kc * bc for kc, bc in zip(k, beta)]
    k16 = [kc.astype(BF16) for kc in k]
    a = [jnp.where(strict, _dot_nt(kbc.astype(BF16), kc16) * dc, 0.0) for kbc, kc16, dc in zip(kb, k16, decay)]
    t = [eye - ac for ac in a]
    p = a
    for _ in range(int(math.log2(C)) - 1):
        p = [_dot_split(pc, pc) for pc in p]
        t = [tc + _dot_split(tc, pc) for tc, pc in zip(t, p)]
    t16 = [tc.astype(BF16) for tc in t]
    egc = [jnp.exp(gcc) for gcc in gc]
    val = [_dot(tc, (v_all[rs, :] * bc).astype(BF16)) for tc, rs, bc in zip(t16, rows, beta)]
    kcum = [_dot(tc, (kbc * ec).astype(BF16)) for tc, kbc, ec in zip(t16, kb, egc)]
    qk = [_dot_nt(q_all[rs, :].astype(BF16), kc16) * dc for rs, kc16, dc in zip(rows, k16, decay)]
    kd = []
    for c in chunks:
        rs = rows[c]
        val_ref[rs, :] = val[c]
        kcum_ref[rs, :] = kcum[c].astype(BF16)
        qk_ref[0, rs, :] = qk[c].astype(BF16)
        glast = gc[c][C - 1:C, :]
        qg_ref[rs, :] = (q_all[rs, :] * egc[c]).astype(BF16)
        kd.append(k[c] * jnp.exp(glast - gc[c]))
        gl_ref[0, c:c + 1, :] = jnp.exp(glast)
    for c in range(0, GDN_CB, 2):
        kdT_ref[0, c // 2] = jnp.concatenate([kd[c], kd[c + 1]], axis=0).T.astype(BF16)


def _gdn_local(proj, conv_w, small):
    L = proj.shape[0]
    H = GDN_HEADS
    R = GDN_CB * GDN_CHUNK
    cb = C_QKV // GDN_DH
    taps = conv_w.shape[0]
    col = lambda off: pl.BlockSpec((R, GDN_DH), lambda h, i: (i, cb + off + h))
    cw = lambda off: pl.BlockSpec((taps, GDN_DH), lambda h, i: (0, off + h))
    big = pl.BlockSpec((R, GDN_DH), lambda h, i: (i, h))
    return pl.pallas_call(
        _gdn_local_kernel,
        grid=(H, L // R),
        in_specs=[col(0), col(H), col(2 * H), cw(0), cw(H), cw(2 * H),
                  pl.BlockSpec((R, LANES), lambda h, i: (i, 0))],
        out_specs=[big, big, big,
                   pl.BlockSpec((1, GDN_CB // 2, GDN_DH, 2 * GDN_CHUNK), lambda h, i: (h, i, 0, 0)),
                   pl.BlockSpec((1, R, GDN_CHUNK), lambda h, i: (h, i, 0)),
                   pl.BlockSpec((1, GDN_CB, GDN_DH), lambda h, i: (h, i, 0))],
        out_shape=[jax.ShapeDtypeStruct((L, GDN_WIDTH), F32),
                   jax.ShapeDtypeStruct((L, GDN_WIDTH), BF16),
                   jax.ShapeDtypeStruct((L, GDN_WIDTH), BF16),
                   jax.ShapeDtypeStruct((H, L // (2 * GDN_CHUNK), GDN_DH, 2 * GDN_CHUNK), BF16),
                   jax.ShapeDtypeStruct((H, L, GDN_CHUNK), BF16),
                   jax.ShapeDtypeStruct((H, L // GDN_CHUNK, GDN_DH), F32)],
        scratch_shapes=[pltpu.VMEM((3, 8, GDN_DH), F32), pltpu.VMEM((3, R + 8, GDN_DH), F32)],
        compiler_params=_cparams(("arbitrary", "arbitrary")),
        name="gdn_local",
    )(proj, proj, proj, conv_w, conv_w, conv_w, small)


def _gdn_scan_kernel(val_ref, kcum_ref, qg_ref, qk_ref, kdT_ref, gl_ref, z_ref, gn_ref, o_ref, st_ref):
    @pl.when(pl.program_id(0) == 0)
    def _():
        st_ref[...] = jnp.zeros_like(st_ref)

    C = GDN_CHUNK
    gn = gn_ref[...]
    heads = range(GDN_HEADS)
    cols = [slice(h * GDN_DH, (h + 1) * GDN_DH) for h in heads]
    state = [st_ref[h] for h in heads]
    for c in range(GDN_CB):
        rs = slice(c * C, (c + 1) * C)
        sb = [s.astype(BF16) for s in state]
        kcs = [_dot(kcum_ref[rs, cs], s) for cs, s in zip(cols, sb)]
        qgs = [_dot(qg_ref[rs, cs], s) for cs, s in zip(cols, sb)]
        vb = [(val_ref[rs, cs] - x).astype(BF16) for cs, x in zip(cols, kcs)]
        o = [x + _dot(qk_ref[h, rs, :], v) for h, x, v in zip(heads, qgs, vb)]
        kcols = slice((c % 2) * C, (c % 2 + 1) * C)
        state = [s * gl_ref[h, c:c + 1, :] + _dot(kdT_ref[h, c // 2, :, kcols], v)
                 for h, s, v in zip(heads, state, vb)]
        for h in heads:
            on = o[h] * lax.rsqrt(jnp.mean(o[h] * o[h], axis=-1, keepdims=True) + EPS) * gn
            z = z_ref[rs, cols[h]]
            o_ref[rs, cols[h]] = on * (z * _sigmoid(z))
    for h in heads:
        st_ref[h] = state[h]


def _gdn_scan(val, kcum, qg, qk, kdT, gl, proj, gnorm):
    L = val.shape[0]
    H = GDN_HEADS
    R = GDN_CB * GDN_CHUNK
    row = pl.BlockSpec((R, GDN_WIDTH), lambda i: (i, 0))
    zb = C_Z // GDN_WIDTH
    return pl.pallas_call(
        _gdn_scan_kernel,
        grid=(L // R,),
        in_specs=[row, row, row,
                  pl.BlockSpec((H, R, GDN_CHUNK), lambda i: (0, i, 0)),
                  pl.BlockSpec((H, GDN_CB // 2, GDN_DH, 2 * GDN_CHUNK), lambda i: (0, i, 0, 0)),
                  pl.BlockSpec((H, GDN_CB, GDN_DH), lambda i: (0, i, 0)),
                  pl.BlockSpec((R, GDN_WIDTH), lambda i: (i, zb)),
                  pl.BlockSpec((1, GDN_DH), lambda i: (0, 0))],
        out_specs=row,
        out_shape=jax.ShapeDtypeStruct((L, GDN_WIDTH), F32),
        scratch_shapes=[pltpu.VMEM((H, GDN_DH, GDN_DH), F32)],
        compiler_params=_cparams(("arbitrary",)),
        name="gdn_scan",
    )(val, kcum, qg, qk, kdT, gl, proj, gnorm)


def _mix_kernel(ya_ref, ys_ref, u_ref, yc_ref, x_ref, ga_ref, gb_ref, d_ref, wglu_ref, wout_ref, o_ref):
    def rms(y, g):
        return y * lax.rsqrt(jnp.mean(y * y, axis=-1, keepdims=True) + EPS) * g

    a = rms(ya_ref[...], ga_ref[...]).astype(BF16)
    yb = _gelu(ys_ref[...] + d_ref[...] * u_ref[...])
    yb = yb * _sigmoid(_dot(yb.astype(BF16), wglu_ref[...]))
    b = rms(yb, gb_ref[...]).astype(BF16)
    na, nb = a.shape[1], b.shape[1]
    acc = x_ref[...] + _dot(a, wout_ref[0:na, :])
    acc = acc + _dot(b, wout_ref[na:na + nb, :])
    o_ref[...] = acc + _dot(yc_ref[...].astype(BF16), wout_ref[na + nb:, :])


def _mix(ya, ys, proj, yc, x, ga, gb, d, wglu, wout, tm=256):
    L, D = x.shape
    ub = C_U // S5_WIDTH
    full = lambda a: pl.BlockSpec(a.shape, lambda i: (0, 0))
    return pl.pallas_call(
        _mix_kernel,
        grid=(L // tm,),
        in_specs=[pl.BlockSpec((tm, ya.shape[1]), lambda i: (i, 0)),
                  pl.BlockSpec((tm, S5_WIDTH), lambda i: (i, 0)),
                  pl.BlockSpec((tm, S5_WIDTH), lambda i: (i, ub)),
                  pl.BlockSpec((tm, GDN_WIDTH), lambda i: (i, 0)),
                  pl.BlockSpec((tm, D), lambda i: (i, 0)),
                  full(ga), full(gb), full(d), full(wglu), full(wout)],
        out_specs=pl.BlockSpec((tm, D), lambda i: (i, 0)),
        out_shape=jax.ShapeDtypeStruct((L, D), F32),
        compiler_params=_cparams(("arbitrary",)),
        name="mix_out",
    )(ya, ys, proj, yc, x, ga, gb, d, wglu, wout)


def _ffn_in_kernel(x_ref, g_ref, wg_ref, wu_ref, cg_ref, cu_ref, bg_ref, bu_ref, o_ref, xn_ref, carry_ref, hs_ref,
                   *, n_valid):
    i = pl.program_id(0)
    j = pl.program_id(1)

    @pl.when(j == 0)
    def _():
        xf = x_ref[...]
        ms = jnp.mean(xf * xf, axis=-1, keepdims=True)
        xn_ref[...] = (xf * lax.rsqrt(ms + EPS) * g_ref[...]).astype(BF16)

    @pl.when(i == 0)
    def _():
        carry_ref[j] = jnp.zeros(carry_ref.shape[1:], F32)

    xn = xn_ref[...]
    tm = xn.shape[0]

    def conv(h, slot, cw_ref, cb_ref):
        hs_ref[slot, 0:8, :] = carry_ref[j, slot]
        hs_ref[slot, 8:tm + 8, :] = h
        carry_ref[j, slot] = h[tm - 8:tm, :]
        taps = cw_ref.shape[0]
        y = cw_ref[taps - 1:taps, :] * h + cb_ref[...]
        for back in range(1, taps):
            y = y + cw_ref[taps - 1 - back:taps - back, :] * hs_ref[slot, 8 - back:8 - back + tm, :]
        return y

    gate = conv(_dot(xn, wg_ref[...]), 0, cg_ref, bg_ref)
    up = conv(_dot(xn, wu_ref[...]), 1, cu_ref, bu_ref)
    half_gate = 0.5 * gate
    act = half_gate * (1.0 + jnp.tanh(half_gate)) * up
    o_ref[...] = act.astype(BF16)

    @pl.when(j == pl.num_programs(1) - 1)
    def _():
        col = j * act.shape[1] + lax.broadcasted_iota(jnp.int32, act.shape, 1)
        o_ref[...] = jnp.where(col < n_valid, act, 0.0).astype(BF16)


def _ffn_in(x, gain, wg, wu, cg, cu, bg, bu, tm=1024, tn=512):
    L, D = x.shape
    n_valid = wg.shape[1]
    nj = pl.cdiv(n_valid, tn)
    taps = cg.shape[0]
    cols = lambda rows: pl.BlockSpec((rows, tn), lambda i, j: (0, j))
    return pl.pallas_call(
        functools.partial(_ffn_in_kernel, n_valid=n_valid),
        grid=(L // tm, nj),
        in_specs=[pl.BlockSpec((tm, D), lambda i, j: (i, 0)),
                  pl.BlockSpec((1, D), lambda i, j: (0, 0)),
                  cols(D), cols(D), cols(taps), cols(taps), cols(1), cols(1)],
        out_specs=pl.BlockSpec((tm, tn), lambda i, j: (i, j)),
        out_shape=jax.ShapeDtypeStruct((L, nj * tn), BF16),
        scratch_shapes=[pltpu.VMEM((tm, D), BF16), pltpu.VMEM((nj, 2, 8, tn), F32),
                        pltpu.VMEM((2, tm + 8, tn), F32)],
        compiler_params=_cparams(("arbitrary", "arbitrary")),
        name="ffn_in",
    )(x, gain.reshape(1, D), wg, wu, cg, cu, bg, bu)


def _ffn_out_kernel(a_ref, w_ref, x_ref, o_ref):
    @pl.when(pl.program_id(1) == 0)
    def _():
        o_ref[...] = x_ref[...]

    o_ref[...] += _dot(a_ref[...], w_ref[...])


def _ffn_out(act, w, x, tm=1024, tk=512):
    L, D = x.shape
    K = act.shape[1]
    return pl.pallas_call(
        _ffn_out_kernel,
        grid=(L // tm, K // tk),
        in_specs=[pl.BlockSpec((tm, tk), lambda i, k: (i, k)),
                  pl.BlockSpec((tk, D), lambda i, k: (k, 0)),
                  pl.BlockSpec((tm, D), lambda i, k: (i, 0))],
        out_specs=pl.BlockSpec((tm, D), lambda i, k: (i, 0)),
        out_shape=jax.ShapeDtypeStruct((L, D), F32),
        compiler_params=_cparams(("arbitrary", "arbitrary")),
        name="ffn_out",
    )(act, w, x)


def _rope_tables(pos):
    half = ROPE_DIMS // 2
    inv = ROPE_THETA ** (-jnp.arange(half, dtype=F32) / half)
    ang = pos.astype(F32)[:, None] * inv[None, :]
    n = pos.shape[0]
    cos = jnp.concatenate([jnp.cos(ang), jnp.cos(ang), jnp.ones((n, NSA_DH - ROPE_DIMS), F32)], axis=1)
    sin = jnp.concatenate([-jnp.sin(ang), jnp.sin(ang), jnp.zeros((n, NSA_DH - ROPE_DIMS), F32)], axis=1)
    return jnp.tile(cos, (1, 2)), jnp.tile(sin, (1, 2))


def _permute_w_in(w):
    sp = np.cumsum([0, 512, 128, 128, 128, 128, 128, 128, 24, 512, 3072, 1024, 8, 8])
    q, kc, vc, ks, vs, kw, vw, gates, u, qkv, z, a, b = [w[:, sp[n]:sp[n + 1]] for n in range(13)]
    pad = jnp.zeros((w.shape[0], PROJ_COLS - C_SM - SM_END), w.dtype)
    return jnp.concatenate([q, ks, kw, vs, vw, z, qkv, u, kc, vc, gates, a, b, pad], axis=1)


def _nsa_mixer(proj, small, p, l):
    L = proj.shape[0]
    HK, G, DH = NSA_KV, NSA_G, NSA_DH
    ns = L // SLC_BLOCK
    nb = L // CMP_STRIDE
    n_grp = LANES - DH
    assert ns % n_grp == 0 and (n_grp * SLC_BLOCK) % (SEL_UNROLL * TK) == 0 and SEL_UNROLL % 2 == 0
    cos, sin = _rope_tables(jnp.arange(L))
    gains = jnp.concatenate([jnp.tile(p['nsa_q_norm'][l], NSA_HEADS), jnp.tile(p['nsa_ks_norm'][l], HK),
                             jnp.tile(p['nsa_kw_norm'][l], HK), jnp.ones((2 * HK * DH,), F32)]).reshape(1, -1)
    qT, ksaug, kwp, vsx, vwxp = _nsa_prep(proj, gains, cos, sin, n_grp)

    eye = jnp.eye(HK, dtype=F32)
    pe = jnp.tile(p['cmp_pe'][l][:, None, :], (1, HK, 1)).reshape(1, CMP_LEN * HK * DH)
    w1 = lambda name: jnp.einsum('ldf,hg->lhdgf', p[name][l], eye).reshape(CMP_LEN * HK * DH, -1).astype(BF16)
    w2 = lambda name: jnp.einsum('fd,hg->hfgd', p[name][l], eye).reshape(-1, HK * DH).astype(BF16)
    ccos, csin = _rope_tables(jnp.arange(nb) * CMP_STRIDE + CMP_LEN // 2)
    gk = jnp.tile(p['nsa_kc_norm'][l], HK).reshape(1, LANES)
    kc, vcx = _nsa_compress(proj, pe, w1('cmp_k_w1'), w2('cmp_k_w2'), w1('cmp_v_w1'), w2('cmp_v_w2'), gk, ccos, csin)

    cch = min(CMP_CHUNK, nb)
    cmp_start = np.arange(cch) * CMP_STRIDE
    slc_start = np.arange(cch * CMP_STRIDE // SLC_BLOCK + 8) * SLC_BLOCK
    smapT = jnp.asarray((cmp_start[None, :] <= slc_start[:, None] + SLC_BLOCK - 1)
                        & (cmp_start[None, :] + CMP_LEN - 1 >= slc_start[:, None]), BF16)

    gT = jnp.pad(small[:, :HK * G * 3].reshape(L, HK, G * 3).transpose(1, 2, 0), ((0, 0), (0, 16 - G * 3), (0, 0)))
    return _nsa_attention(qT, kc, vcx, smapT, ksaug, vsx, kwp, vwxp, gT, n_grp)


def _s5_mixer(proj, p, l):
    L = proj.shape[0]
    T = S5_CHUNK
    nc = L // T
    G, H, P = S5_GROUPS, S5_CH, S5_STATE
    lam_re, lam_im = p['s5_lam_re'][l], p['s5_lam_im'][l]
    dt = jnp.exp(p['s5_log_dt'][l])[:, None]
    mag = jnp.exp(lam_re * dt)
    lb_re = mag * jnp.cos(lam_im * dt)
    lb_im = mag * jnp.sin(lam_im * dt)
    den = lam_re * lam_re + lam_im * lam_im
    nr, ni = lb_re - 1.0, lb_im
    coef_re = (nr * lam_re + ni * lam_im) / den
    coef_im = (ni * lam_re - nr * lam_im) / den
    b_re, b_im = p['s5_b_re'][l], p['s5_b_im'][l]
    bb_re = coef_re[..., None] * b_re - coef_im[..., None] * b_im
    bb_im = coef_re[..., None] * b_im + coef_im[..., None] * b_re
    bsr, bsi = bb_re.transpose(0, 2, 1), bb_im.transpose(0, 2, 1)
    return _s5_scan(proj, lb_re[:, None, :], lb_im[:, None, :], p['s5_c_re'][l], p['s5_c_im'][l], bsr, bsi)


def _gdn_mixer(proj, small, p, l):
    val, kcum, qg, kdT, qk, gl = _gdn_local(proj, p['gdn_conv'][l], small)
    return _gdn_scan(val, kcum, qg, qk, kdT, gl, proj, p['gdn_norm'][l].reshape(1, GDN_DH))


def _forward(x3, p):
    x = x3.reshape(x3.shape[1:])
    depth = p['w_in'].shape[0]
    for l in range(depth):
        w_in = _permute_w_in(p['w_in'][l].astype(BF16))
        proj = _rms_matmul(x, p['attn_norm'][l], w_in)
        sm_par = jnp.zeros((8, LANES), F32)
        sm_par = sm_par.at[0, SM_G:SM_BETA].set(p['gdn_dt_bias'][l]).at[1, SM_G:SM_BETA].set(-jnp.exp(p['gdn_a_log'][l]))
        small = _small(proj, sm_par)
        y_a = _nsa_mixer(proj, small, p, l)
        y_s = _s5_mixer(proj, p, l)
        y_c = _gdn_mixer(proj, small, p, l)
        x = _mix(y_a, y_s, proj, y_c, x, p['nsa_out_norm'][l].reshape(1, -1), p['s5_out_norm'][l].reshape(1, -1),
                 p['s5_d'][l].reshape(1, -1), p['s5_w_glu'][l].astype(BF16), p['w_out'][l].astype(BF16))
        wf, cf, bf = p['ffn_w_in'][l], p['ffn_conv'][l], p['ffn_conv_b'][l].reshape(1, -1)
        act = _ffn_in(x, p['ffn_norm'][l], wf[:, :D_FF].astype(BF16), wf[:, D_FF:].astype(BF16),
                      cf[:, :D_FF], cf[:, D_FF:], bf[:, :D_FF], bf[:, D_FF:])
        w_o = jnp.pad(p['ffn_w_out'][l].astype(BF16), ((0, D_FF_PAD - D_FF), (0, 0)))
        x = _ffn_out(act, w_o, x)
    return x.reshape(x3.shape)


_PARAM_NAMES = ('attn_norm', 'w_in', 'nsa_q_norm', 'nsa_kc_norm', 'nsa_ks_norm', 'nsa_kw_norm', 'cmp_pe',
                'cmp_k_w1', 'cmp_k_w2', 'cmp_v_w1', 'cmp_v_w2', 'nsa_out_norm', 's5_lam_re', 's5_lam_im',
                's5_log_dt', 's5_b_re', 's5_b_im', 's5_c_re', 's5_c_im', 's5_d', 's5_w_glu', 's5_out_norm',
                'gdn_conv', 'gdn_a_log', 'gdn_dt_bias', 'gdn_norm', 'w_out', 'ffn_norm', 'ffn_w_in', 'ffn_conv',
                'ffn_conv_b', 'ffn_w_out')


def kernel(x, attn_norm, w_in, nsa_q_norm, nsa_kc_norm, nsa_ks_norm, nsa_kw_norm, cmp_pe, cmp_k_w1, cmp_k_w2,
           cmp_v_w1, cmp_v_w2, nsa_out_norm, s5_lam_re, s5_lam_im, s5_log_dt, s5_b_re, s5_b_im, s5_c_re, s5_c_im,
           s5_d, s5_w_glu, s5_out_norm, gdn_conv, gdn_a_log, gdn_dt_bias, gdn_norm, w_out, ffn_norm, ffn_w_in,
           ffn_conv, ffn_conv_b, ffn_w_out):
    vals = (attn_norm, w_in, nsa_q_norm, nsa_kc_norm, nsa_ks_norm, nsa_kw_norm, cmp_pe, cmp_k_w1, cmp_k_w2,
            cmp_v_w1, cmp_v_w2, nsa_out_norm, s5_lam_re, s5_lam_im, s5_log_dt, s5_b_re, s5_b_im, s5_c_re, s5_c_im,
            s5_d, s5_w_glu, s5_out_norm, gdn_conv, gdn_a_log, gdn_dt_bias, gdn_norm, w_out, ffn_norm, ffn_w_in,
            ffn_conv, ffn_conv_b, ffn_w_out)
    return _forward(x, dict(zip(_PARAM_NAMES, vals)))
```

```python
import functools
import math

import jax
import jax.numpy as jnp
import numpy as np
from jax import lax
from jax.experimental import pallas as pl
from jax.experimental.pallas import tpu as pltpu

F32 = jnp.float32
BF16 = jnp.bfloat16
HIGHEST = lax.Precision.HIGHEST

NSA_HEADS = 8
NSA_KV = 2
NSA_G = NSA_HEADS // NSA_KV
NSA_DH = 64
CMP_LEN = 32
CMP_STRIDE = 16
SLC_BLOCK = 64
SLC_TOPK = 16
WINDOW = 512
ROPE_THETA = 500000.0
ROPE_DIMS = NSA_DH // 4
S5_WIDTH = 512
S5_CH = 16
S5_GROUPS = 32
S5_STATE = 64
S5_CHUNK = 32
S5_SLAB = 8
GDN_HEADS = 8
GDN_DH = 128
GDN_WIDTH = 1024
GDN_CHUNK = 64
D_FF = 5504
D_FF_PAD = 5632
EPS = 1e-6
NEG_BIG = -(2.0 ** 100)
LOG2_E = math.log2(math.e)
V_ROWS = 80
KW_LANES = 128
CMP_CHUNK = 256
SEL_UNROLL = 2

LANES = 128
TQ = 128
TK = 256
VMEM_LIMIT = 56 * 1024 * 1024

C_Q, C_KS, C_KW, C_VS, C_VW = 0, 512, 640, 768, 896
C_Z, C_QKV, C_U, C_KC, C_VC, C_SM = 1024, 2048, 5120, 5632, 5760, 5888
NSA_PREP_W = C_Z
PROJ_COLS = 6144
SM_GATES, SM_G, SM_BETA, SM_END = 0, 24, 32, 40


def _cparams(sem):
    return pltpu.CompilerParams(dimension_semantics=sem, vmem_limit_bytes=VMEM_LIMIT)


def _dot(a, b, precision=None):
    return jnp.dot(a, b, preferred_element_type=F32, precision=precision)


def _dot_nt(a, b, precision=None):
    return lax.dot_general(a, b, (((1,), (1,)), ((), ())), preferred_element_type=F32, precision=precision)


def _gelu(x):
    return x * (0.5 * (1.0 + jnp.tanh(math.sqrt(2.0 / math.pi) * (x + 0.044715 * (x * x * x)))))


def _sigmoid(x):
    return 1.0 / (1.0 + jnp.exp(-x))


def _rms_matmul_kernel(x_ref, g_ref, w_ref, o_ref, xn_ref):
    @pl.when(pl.program_id(1) == 0)
    def _():
        xf = x_ref[...]
        ms = jnp.mean(xf * xf, axis=-1, keepdims=True)
        xn_ref[...] = (xf * lax.rsqrt(ms + EPS) * g_ref[...]).astype(BF16)

    o_ref[...] = _dot(xn_ref[...], w_ref[...])


def _rms_matmul(x, gain, w, tm=512, tn=2048):
    L, D = x.shape
    N = w.shape[1]
    return pl.pallas_call(
        _rms_matmul_kernel,
        grid=(L // tm, N // tn),
        in_specs=[
            pl.BlockSpec((tm, D), lambda i, j: (i, 0)),
            pl.BlockSpec((1, D), lambda i, j: (0, 0)),
            pl.BlockSpec((D, tn), lambda i, j: (0, j)),
        ],
        out_specs=pl.BlockSpec((tm, tn), lambda i, j: (i, j)),
        out_shape=jax.ShapeDtypeStruct((L, N), F32),
        scratch_shapes=[pltpu.VMEM((tm, D), BF16)],
        compiler_params=_cparams(("arbitrary", "arbitrary")),
        name="rms_proj",
    )(x, gain.reshape(1, D), w)


def _rope_slab(x, cos, sin):
    d = lax.broadcasted_iota(jnp.int32, x.shape, 1) & (NSA_DH - 1)
    half = ROPE_DIMS // 2
    partner = jnp.where(d < half, pltpu.roll(x, LANES - half, 1), pltpu.roll(x, half, 1))
    return x * cos + partner * sin


def _nsa_prep_kernel(x_ref, g_ref, cos_ref, sin_ref, bsum_ref, qT_ref, ksaug_ref, kwa_ref, vsx_ref, vwx_ref, *, n_grp):
    i = pl.program_id(0)
    tm = x_ref.shape[0]
    cos = cos_ref[...]
    sin = sin_ref[...]
    bsum = bsum_ref[...]
    lane = lax.broadcasted_iota(jnp.int32, (tm, LANES), 1)
    tiles = range(tm // TQ)

    def normed(s):
        sl = slice(s * LANES, (s + 1) * LANES)
        x = x_ref[:, sl]
        ms = _dot(x * x, bsum, precision=HIGHEST)
        return _rope_slab(x * lax.rsqrt(ms + EPS) * g_ref[:, sl], cos, sin)

    def per_head(y):
        return y, pltpu.roll(y, NSA_DH, 1)

    for s in range((C_KS - C_Q) // LANES):
        y = normed(s) * (NSA_DH ** -0.5 * LOG2_E)
        hk, g0 = divmod(s, NSA_G // 2)
        for t in tiles:
            yT = y[t * TQ:(t + 1) * TQ, :].T.astype(BF16)
            for half in range(2):
                g = 2 * g0 + half
                qT_ref[hk, t, :, g * TQ:(g + 1) * TQ] = yT[half * NSA_DH:(half + 1) * NSA_DH, :]

    blk = ((i * tm + lax.broadcasted_iota(jnp.int32, (tm, LANES), 0)) // SLC_BLOCK) % n_grp
    onehot = jnp.where(lane - NSA_DH == blk, 1.0, 0.0)
    for hk, y in enumerate(per_head(normed(C_KS // LANES))):
        ksaug_ref[hk] = jnp.where(lane < NSA_DH, y, onehot).astype(BF16)
    for hk, y in enumerate(per_head(normed(C_KW // LANES))):
        kwa_ref[hk] = jnp.where(lane < NSA_DH, y, 0.0).astype(BF16)

    ones_rows = jnp.where(lax.broadcasted_iota(jnp.int32, (V_ROWS - NSA_DH, tm), 0) == 0, 1.0, 0.0).astype(BF16)
    for c0, v_ref in ((C_VS, vsx_ref), (C_VW, vwx_ref)):
        x = x_ref[:, c0:c0 + LANES]
        for t in tiles:
            xT = x[t * TQ:(t + 1) * TQ, :].T.astype(BF16)
            for hk in range(NSA_KV):
                v_ref[hk, 0:NSA_DH, t * TQ:(t + 1) * TQ] = xT[hk * NSA_DH:(hk + 1) * NSA_DH, :]
        for hk in range(NSA_KV):
            v_ref[hk, NSA_DH:V_ROWS, :] = ones_rows


def _nsa_prep(proj, gains, cos, sin, n_grp, tm=512):
    L = proj.shape[0]
    W = NSA_PREP_W
    HK = NSA_KV
    GW = NSA_G * TQ
    bsum = jnp.asarray(np.kron(np.eye(2), np.ones((NSA_DH, NSA_DH))) / NSA_DH, F32)
    rows = pl.BlockSpec((HK, tm, LANES), lambda i: (0, i, 0))
    vals = pl.BlockSpec((HK, V_ROWS, tm), lambda i: (0, 0, i))
    return pl.pallas_call(
        functools.partial(_nsa_prep_kernel, n_grp=n_grp),
        grid=(L // tm,),
        in_specs=[
            pl.BlockSpec((tm, W), lambda i: (i, 0)),
            pl.BlockSpec((1, W), lambda i: (0, 0)),
            pl.BlockSpec((tm, LANES), lambda i: (i, 0)),
            pl.BlockSpec((tm, LANES), lambda i: (i, 0)),
            pl.BlockSpec((LANES, LANES), lambda i: (0, 0)),
        ],
        out_specs=[pl.BlockSpec((HK, tm // TQ, NSA_DH, GW), lambda i: (0, i, 0, 0)), rows, rows, vals, vals],
        out_shape=[jax.ShapeDtypeStruct((HK, L // TQ, NSA_DH, GW), BF16),
                   jax.ShapeDtypeStruct((HK, L, LANES), BF16), jax.ShapeDtypeStruct((HK, L, LANES), BF16),
                   jax.ShapeDtypeStruct((HK, V_ROWS, L), BF16), jax.ShapeDtypeStruct((HK, V_ROWS, L), BF16)],
        compiler_params=_cparams(("arbitrary",)),
        name="nsa_prep",
    )(proj, gains, cos, sin, bsum)


def _cmp_kernel(xk_ref, xv_ref, pe_ref, w1k_ref, w2k_ref, w1v_ref, w2v_ref, gk_ref, cos_ref, sin_ref, bsum_ref,
                kc_ref, vcx_ref):
    nb = kc_ref.shape[1]
    half_w = CMP_STRIDE * LANES

    def mlp(x_ref, w1_ref, w2_ref):
        hb = jnp.concatenate([x_ref[pl.ds(l, nb, stride=CMP_STRIDE), :] for l in range(CMP_STRIDE)], axis=1)
        a = (hb + pe_ref[:, 0:half_w]).astype(BF16)
        b = (hb + pe_ref[:, half_w:2 * half_w]).astype(BF16)
        p1 = _dot(a, w1_ref[0:half_w, :])
        p2 = _dot(b, w1_ref[half_w:2 * half_w, :])
        h = p1 + pltpu.roll(p2, nb - 1, 0)
        return _dot(_gelu(h).astype(BF16), w2_ref[...])

    rid = lax.broadcasted_iota(jnp.int32, (nb, LANES), 0)
    real = rid < nb - 1
    kc = jnp.where(real, mlp(xk_ref, w1k_ref, w2k_ref), 0.0)
    vc = jnp.where(real, mlp(xv_ref, w1v_ref, w2v_ref), 0.0)
    ms = _dot(kc * kc, bsum_ref[...], precision=HIGHEST)
    kcn = _rope_slab(kc * lax.rsqrt(ms + EPS) * gk_ref[...], cos_ref[...], sin_ref[...])
    kc_ref[0] = kcn[:, 0:NSA_DH].astype(BF16)
    kc_ref[1] = pltpu.roll(kcn, NSA_DH, 1)[:, 0:NSA_DH].astype(BF16)
    ones_rows = jnp.where(lax.broadcasted_iota(jnp.int32, (V_ROWS - NSA_DH, nb), 0) == 0, 1.0, 0.0).astype(BF16)
    for t in range(nb // LANES):
        vT = vc[t * LANES:(t + 1) * LANES, :].T.astype(BF16)
        for hk in range(NSA_KV):
            vcx_ref[hk, 0:NSA_DH, t * LANES:(t + 1) * LANES] = vT[hk * NSA_DH:(hk + 1) * NSA_DH, :]
    for hk in range(NSA_KV):
        vcx_ref[hk, NSA_DH:V_ROWS, :] = ones_rows


def _nsa_compress(proj, pe, w1k, w2k, w1v, w2v, gk, cos, sin):
    L = proj.shape[0]
    nb = L // CMP_STRIDE
    bsum = jnp.asarray(np.kron(np.eye(2), np.ones((NSA_DH, NSA_DH))) / NSA_DH, F32)
    once = lambda shape, idx: pl.BlockSpec(shape, lambda i: idx, pipeline_mode=pl.Buffered(1))
    full = lambda a: once(a.shape, (0,) * a.ndim)
    return pl.pallas_call(
        _cmp_kernel,
        grid=(1,),
        in_specs=[once((L, LANES), (0, C_KC // LANES)), once((L, LANES), (0, C_VC // LANES)),
                  full(pe), full(w1k), full(w2k), full(w1v), full(w2v), full(gk), full(cos), full(sin), full(bsum)],
        out_specs=[pl.BlockSpec((NSA_KV, nb, NSA_DH), lambda i: (0, 0, 0)),
                   pl.BlockSpec((NSA_KV, V_ROWS, nb), lambda i: (0, 0, 0))],
        out_shape=[jax.ShapeDtypeStruct((NSA_KV, nb, NSA_DH), BF16),
                   jax.ShapeDtypeStruct((NSA_KV, V_ROWS, nb), BF16)],
        compiler_params=_cparams(("arbitrary",)),
        name="nsa_compress",
    )(proj, proj, pe, w1k, w2k, w1v, w2v, gk, cos, sin, bsum)


def _nsa_attn_kernel(qT_ref, kc_ref, vcx_ref, smap_ref, ksaug_ref, vsx_ref, kwp_ref, vwxp_ref, gT_ref,
                     o_ref, qaug_ref, qwin_ref, ss_ref, mt_ref, m_ref, acc_ref, sc_ref, imp_ref, *, n_grp):
    i = pl.program_id(0)
    s0 = i * TQ
    hks = range(qT_ref.shape[0])
    nb = kc_ref.shape[1]
    ns = qaug_ref.shape[1] * n_grp
    GW = NSA_G * TQ
    qT = [qT_ref[h, 0] for h in hks]
    tq = s0 + (lax.broadcasted_iota(jnp.int32, (1, GW), 1) & (TQ - 1))

    cch = min(CMP_CHUNK, nb)
    n_cch = (s0 + TQ - CMP_LEN) // (CMP_STRIDE * cch) + 1
    nthr = (tq - (CMP_LEN - 1)) // CMP_STRIDE

    def cmp_scores(c, mcs):
        r0c = pl.multiple_of(c * cch, cch)
        s = [_dot(kc_ref[h, pl.ds(r0c, cch), :], qT[h]) for h in hks]
        nrow = r0c + lax.broadcasted_iota(jnp.int32, (cch, GW), 0)
        out = []
        for h in hks:
            sh = jnp.where(nrow <= nthr, s[h], NEG_BIG)
            sc_ref[h, pl.ds(r0c, cch), :] = sh
            out.append(jnp.maximum(mcs[h], jnp.max(sh, axis=0, keepdims=True)))
        return tuple(out)

    mc = lax.fori_loop(0, n_cch, cmp_scores, tuple(jnp.full((1, GW), NEG_BIG, F32) for _ in hks))
    imp_ref[...] = jnp.zeros_like(imp_ref)
    ones16 = jnp.ones((16, cch), BF16)
    VX = vcx_ref.shape[1]

    sm = smap_ref[...]
    sm_rows = sm.shape[0]
    blocks_per_chunk = cch * CMP_STRIDE // SLC_BLOCK

    def cmp_accum(c, carry):
        r0c = pl.multiple_of(c * cch, cch)
        j0 = pl.multiple_of(c * blocks_per_chunk, blocks_per_chunk)
        parts = [_split2(jnp.exp2(sc_ref[h, pl.ds(r0c, cch), :] - mc[h])) for h in hks]
        for h in hks:
            imp_ref[h, pl.ds(j0, sm_rows), :] += _dot(sm, parts[h][0]) + _dot(sm, parts[h][1])
        return tuple((carry[h][0] + _dot(vcx_ref[h, :, pl.ds(r0c, cch)], parts[h][0]),
                      carry[h][1] + _dot(ones16, parts[h][1])) for h in hks)

    cacc = lax.fori_loop(0, n_cch, cmp_accum,
                         tuple((jnp.zeros((VX, GW), F32), jnp.zeros((16, GW), F32)) for _ in hks))
    inv_l = [jnp.where(tq >= CMP_LEN - 1, 1.0 / (cacc[h][0][NSA_DH:NSA_DH + 1] + cacc[h][1][0:1]), 0.0) for h in hks]
    ocT = [cacc[h][0][0:NSA_DH] * inv_l[h] for h in hks]

    r0 = pl.multiple_of(s0, TQ)
    rel = lax.broadcasted_iota(jnp.int32, (TQ, GW), 0)
    qrel = tq - s0
    flag_row = lax.broadcasted_iota(jnp.int32, (KW_LANES - NSA_DH, GW), 0) == 0
    for h in hks:
        qwin_ref[h, 0:NSA_DH, :] = qT[h]
        qwin_ref[h, NSA_DH:KW_LANES, :] = jnp.where(flag_row, NEG_BIG, 0.0).astype(BF16)
    qwin = [qwin_ref[h] for h in hks]
    edge = [_dot(kwp_ref[h, pl.ds(r0, TQ), :], qwin[h]) for h in hks]
    mid = [_dot(kwp_ref[h, pl.ds(r0 + TQ, WINDOW - TQ), :], qwin[h]) for h in hks]
    diag = [_dot(kwp_ref[h, pl.ds(r0 + WINDOW, TQ), :], qwin[h]) for h in hks]
    edge16 = [jnp.where(rel > qrel, edge[h], NEG_BIG).astype(BF16) for h in hks]
    diag16 = [jnp.where(rel <= qrel, diag[h], NEG_BIG).astype(BF16) for h in hks]
    mid16 = [mid[h].astype(BF16) for h in hks]
    mw = [jnp.maximum(jnp.maximum(jnp.max(edge16[h], axis=0, keepdims=True), jnp.max(mid16[h], axis=0, keepdims=True)),
                      jnp.max(diag16[h], axis=0, keepdims=True)) for h in hks]
    accw = [(_dot(vwxp_ref[h, :, pl.ds(r0, TQ)], jnp.exp2(edge16[h] - mw[h]))
             + _dot(vwxp_ref[h, :, pl.ds(r0 + TQ, WINDOW - TQ)], jnp.exp2(mid16[h] - mw[h]))
             + _dot(vwxp_ref[h, :, pl.ds(r0 + WINDOW, TQ)], jnp.exp2(diag16[h] - mw[h]))) for h in hks]
    owT = [accw[h][0:NSA_DH] / accw[h][NSA_DH:NSA_DH + 1] for h in hks]
    sd = [_dot(ksaug_ref[h, pl.ds(r0, TQ), 0:NSA_DH], qT[h]) for h in hks]
    sd16 = [jnp.where(rel <= qrel, sd[h], NEG_BIG).astype(BF16) for h in hks]

    imp = []
    for h in hks:
        tot = imp_ref[h, 0:ns, 0:TQ] * inv_l[h][:, 0:TQ]
        for g in range(1, NSA_G):
            tot = tot + imp_ref[h, 0:ns, g * TQ:(g + 1) * TQ] * inv_l[h][:, g * TQ:(g + 1) * TQ]
        imp.append(tot)

    jrow = lax.broadcasted_iota(jnp.int32, (ns, TQ), 0)
    cur = (s0 + lax.broadcasted_iota(jnp.int32, (1, TQ), 1)) // SLC_BLOCK
    forced = jnp.where(jrow == 0, 0.0, jnp.where(jrow == cur, 0.0, jnp.where(jrow == cur - 1, 0.0, NEG_BIG)))
    val = [jnp.where(jrow >= 1, jnp.where(jrow <= cur - 2, imp[h], -jnp.inf), -jnp.inf) for h in hks]
    for _ in range(min(SLC_TOPK, ns) - 3):
        mx = [jnp.max(val[h], axis=0, keepdims=True) for h in hks]
        idx = [jnp.min(jnp.where(val[h] == mx[h], jrow, ns), axis=0, keepdims=True) for h in hks]
        val = [jnp.where(jrow == idx[h], -jnp.inf, val[h]) for h in hks]
    bias = [jnp.where(jrow >= 1, jnp.where(jrow <= cur - 2, jnp.where(val[h] == -jnp.inf, 0.0, NEG_BIG), forced), forced)
            for h in hks]
    own = jrow // 2 == i
    for h in hks:
        bias16 = jnp.where(own, NEG_BIG, bias[h]).astype(BF16)
        for grp in range(ns // n_grp):
            qaug_ref[h, grp, 0:NSA_DH, :] = qT[h]
            for g in range(NSA_G):
                qaug_ref[h, grp, NSA_DH:NSA_DH + n_grp, g * TQ:(g + 1) * TQ] = bias16[grp * n_grp:(grp + 1) * n_grp, :]

    tiles_per_grp = (n_grp * SLC_BLOCK) // TK
    n_body = jnp.maximum((s0 + SEL_UNROLL * TK - 1) // (SEL_UNROLL * TK), 1)

    def scores(kt, slot):
        k0 = pl.multiple_of(kt * TK, TK)
        grp = kt // tiles_per_grp
        s = [_dot(ksaug_ref[h, pl.ds(k0, TK), :], qaug_ref[h, grp]) for h in hks]
        for h in hks:
            sb = s[h].astype(BF16)
            ss_ref[h, slot] = sb
            mt_ref[h, slot] = jnp.max(sb, axis=0, keepdims=True).astype(F32)

    def online_update(sb, mt, v):
        m = [m_ref[h] for h in hks]
        mn = [jnp.maximum(m[h], mt[h]) for h in hks]
        pv = [_dot(v[h], jnp.exp2(sb[h] - mn[h].astype(BF16))) for h in hks]
        for h in hks:
            acc_ref[h] = jnp.exp2(m[h] - mn[h]) * acc_ref[h] + pv[h]
            m_ref[h] = mn[h]

    def accumulate(kt, slot):
        k0 = pl.multiple_of(kt * TK, TK)
        online_update([ss_ref[h, slot] for h in hks], [mt_ref[h, slot] for h in hks],
                      [vsx_ref[h, :, pl.ds(k0, TK)] for h in hks])

    m_ref[...] = jnp.full(m_ref.shape, NEG_BIG, F32)
    acc_ref[...] = jnp.zeros_like(acc_ref)

    scores(0, 0)

    def tile_group(base, last):
        for u in range(SEL_UNROLL):
            if not (last and u == SEL_UNROLL - 1):
                scores(base + u + 1, (u + 1) % 2)
            accumulate(base + u, u % 2)

    def loop_body(j, carry):
        tile_group(SEL_UNROLL * j, last=False)
        return carry

    lax.fori_loop(0, n_body - 1, loop_body, 0)

    tile_group(SEL_UNROLL * (n_body - 1), last=True)

    online_update(sd16, [jnp.max(sd16[h], axis=0, keepdims=True).astype(F32) for h in hks],
                  [vsx_ref[h, :, pl.ds(r0, TQ)] for h in hks])
    osT = [acc_ref[h, 0:NSA_DH, :] / acc_ref[h, NSA_DH:NSA_DH + 1, :] for h in hks]

    for h in hks:
        gates = _sigmoid(gT_ref[h])
        outs = []
        for g in range(NSA_G):
            sl = slice(g * TQ, (g + 1) * TQ)
            outs.append(gates[3 * g:3 * g + 1] * ocT[h][:, sl] + gates[3 * g + 1:3 * g + 2] * osT[h][:, sl]
                        + gates[3 * g + 2:3 * g + 3] * owT[h][:, sl])
        for g in range(0, NSA_G, 2):
            c0 = (h * NSA_G + g) * NSA_DH
            o_ref[:, c0:c0 + 2 * NSA_DH] = jnp.concatenate([outs[g], outs[g + 1]], axis=0).T


def _nsa_attention(qT, kc, vcx, smapT, ksaug, vsx, kwp, vwxp, gT, n_grp):
    HK, nQ, _, GW = qT.shape
    nb = kc.shape[1]
    L = vsx.shape[2]
    ns = L // SLC_BLOCK
    KA = ksaug.shape[2]
    scratch = [pltpu.VMEM((HK, ns // n_grp, KA, GW), BF16),
               pltpu.VMEM((HK, KW_LANES, GW), BF16),
               pltpu.VMEM((HK, 2, TK, GW), BF16),
               pltpu.VMEM((HK, 2, 1, GW), F32),
               pltpu.VMEM((HK, 1, GW), F32),
               pltpu.VMEM((HK, V_ROWS, GW), F32),
               pltpu.VMEM((HK, nb, GW), F32),
               pltpu.VMEM((HK, ns + 8, GW), F32)]
    resident = lambda a: pl.BlockSpec(a.shape, lambda i: (0,) * a.ndim, pipeline_mode=pl.Buffered(1))
    return pl.pallas_call(
        functools.partial(_nsa_attn_kernel, n_grp=n_grp),
        grid=(nQ,),
        in_specs=[
            pl.BlockSpec((HK, 1, NSA_DH, GW), lambda i: (0, i, 0, 0)),
            resident(kc), resident(vcx), resident(smapT), resident(ksaug), resident(vsx), resident(kwp),
            resident(vwxp),
            pl.BlockSpec((HK, 16, TQ), lambda i: (0, 0, i)),
        ],
        out_specs=pl.BlockSpec((TQ, NSA_HEADS * NSA_DH), lambda i: (i, 0)),
        out_shape=jax.ShapeDtypeStruct((nQ * TQ, NSA_HEADS * NSA_DH), F32),
        scratch_shapes=scratch,
        compiler_params=_cparams(("arbitrary",)),
        name="nsa_attn",
    )(qT, kc, vcx, smapT, ksaug, vsx, kwp, vwxp, gT)


def _s5_kernel(x_ref, lbr_ref, lbi_ref, cr_ref, ci_ref, bsr_ref, bsi_ref, y_ref, lm_ref, yg_ref):
    T = S5_CHUNK
    nc = x_ref.shape[0] // T
    u_tau = [x_ref[pl.ds(tau, nc, stride=T), :] for tau in range(T)]
    for gi in range(S5_SLAB):
        lanes = slice(gi * S5_CH, (gi + 1) * S5_CH)
        u = jnp.concatenate([ut[:, lanes] for ut in u_tau], axis=1).astype(BF16)
        yg_ref[gi] = _s5_group(u, lbr_ref[gi], lbi_ref[gi], cr_ref[gi], ci_ref[gi], bsr_ref[gi], bsi_ref[gi], lm_ref)
    for tau in range(T):
        cols = slice(tau * S5_CH, (tau + 1) * S5_CH)
        y_ref[pl.ds(tau, nc, stride=T), :] = jnp.concatenate([yg_ref[gi, :, cols] for gi in range(S5_SLAB)], axis=1)


def _s5_group(u, ar, ai, cr, ci, bsr, bsi, lm_ref):
    T = S5_CHUNK
    W = T * S5_CH
    tile_rows = lambda a: jnp.concatenate([a] * T, axis=0)
    ctr, cti, btr, bti = tile_rows(cr), tile_rows(ci), tile_rows(bsr), tile_rows(bsi)
    delta = lax.broadcasted_iota(jnp.int32, (W, S5_STATE), 0) // S5_CH

    def powers(e):
        pr = jnp.ones((W, S5_STATE), F32)
        pi = jnp.zeros((W, S5_STATE), F32)
        fr, fi = ar, ai
        for b in range(T.bit_length() - 1):
            bit = ((e >> b) & 1) == 1
            nr = pr * fr - pi * fi
            ni = pr * fi + pi * fr
            pr = jnp.where(bit, nr, pr)
            pi = jnp.where(bit, ni, pi)
            fr, fi = fr * fr - fi * fi, 2.0 * fr * fi
        return pr, pi, fr, fi

    pwr, pwi, aTr, aTi = powers(delta)
    rvr, rvi, _, _ = powers(T - 1 - delta)
    car = ctr * pwr - cti * pwi
    cai = ctr * pwi + cti * pwr

    kw = _dot_nt(bsr, car, HIGHEST) - _dot_nt(bsi, cai, HIGHEST)
    lane = lax.broadcasted_iota(jnp.int32, (S5_CH, W), 1)
    for tau in range(T):
        sh = S5_CH * tau
        blk = kw if tau == 0 else jnp.where(lane >= sh, pltpu.roll(kw, sh, 1), 0.0)
        lm_ref[sh:sh + S5_CH, :] = blk.astype(BF16)

    nc = u.shape[0]
    y = _dot(u, lm_ref[...])

    sr = _dot(u, (rvr * btr - rvi * bti).astype(BF16))
    si = _dot(u, (rvr * bti + rvi * btr).astype(BF16))
    rowc = lax.broadcasted_iota(jnp.int32, (nc, S5_STATE), 0)
    fr, fi = aTr, aTi
    step = 1
    while step < nc:
        shr = jnp.where(rowc >= step, pltpu.roll(sr, step, 0), 0.0)
        shi = jnp.where(rowc >= step, pltpu.roll(si, step, 0), 0.0)
        sr, si = sr + fr * shr - fi * shi, si + fr * shi + fi * shr
        fr, fi = fr * fr - fi * fi, 2.0 * fr * fi
        step *= 2
    xr = jnp.where(rowc >= 1, pltpu.roll(sr, 1, 0), 0.0)
    xi = jnp.where(rowc >= 1, pltpu.roll(si, 1, 0), 0.0)
    c1r = car * ar - cai * ai
    c1i = car * ai + cai * ar
    return y + _dot_nt(xr.astype(BF16), c1r.astype(BF16)) - _dot_nt(xi.astype(BF16), c1i.astype(BF16))


def _s5_scan(proj, lbr, lbi, cr, ci, bsr, bsi):
    L = proj.shape[0]
    W = S5_CHUNK * S5_CH
    nc = L // S5_CHUNK
    G = lbr.shape[0]
    slab = lambda shape: pl.BlockSpec((S5_SLAB,) + shape, lambda s: (s, 0, 0))
    return pl.pallas_call(
        _s5_kernel,
        grid=(G // S5_SLAB,),
        in_specs=[pl.BlockSpec((L, LANES), lambda s: (0, C_U // LANES + s)),
                  slab((1, S5_STATE)), slab((1, S5_STATE)),
                  slab((S5_CH, S5_STATE)), slab((S5_CH, S5_STATE)), slab((S5_CH, S5_STATE)), slab((S5_CH, S5_STATE))],
        out_specs=pl.BlockSpec((L, LANES), lambda s: (0, s)),
        out_shape=jax.ShapeDtypeStruct((L, G * S5_CH), F32),
        scratch_shapes=[pltpu.VMEM((W, W), BF16), pltpu.VMEM((S5_SLAB, nc, W), F32)],
        compiler_params=_cparams(("arbitrary",)),
        name="s5_scan",
    )(proj, lbr, lbi, cr, ci, bsr, bsi)


def _small_kernel(x_ref, p_ref, o_ref):
    x = x_ref[...]
    lane = lax.broadcasted_iota(jnp.int32, x.shape, 1)
    z = x + p_ref[0:1, :]
    softplus = jnp.maximum(z, 0.0) + jnp.log(1.0 + jnp.exp(-jnp.abs(z)))
    gdec = p_ref[1:2, :] * softplus
    beta = _sigmoid(x)
    o_ref[...] = jnp.where(lane < SM_G, x, jnp.where(lane < SM_BETA, gdec, beta))


def _small(proj, params, tm=1024):
    L = proj.shape[0]
    cb = C_SM // LANES
    return pl.pallas_call(
        _small_kernel,
        grid=(L // tm,),
        in_specs=[pl.BlockSpec((tm, LANES), lambda i: (i, cb)), pl.BlockSpec((8, LANES), lambda i: (0, 0))],
        out_specs=pl.BlockSpec((tm, LANES), lambda i: (i, 0)),
        out_shape=jax.ShapeDtypeStruct((L, LANES), F32),
        compiler_params=_cparams(("arbitrary",)),
        name="small_cols",
    )(proj, params)


GDN_CB = 8


def _split2(x):
    hi = x.astype(BF16)
    return hi, (x - hi.astype(F32)).astype(BF16)


def _dot_split(a, b):
    ah, al = _split2(a)
    bh, bl = _split2(b)
    return _dot(ah, bh) + (_dot(ah, bl) + _dot(al, bh))


def _gdn_local_kernel(xq_ref, xk_ref, xv_ref, cwq_ref, cwk_ref, cwv_ref, sm_ref,
                      val_ref, kcum_ref, qg_ref, kdT_ref, qk_ref, gl_ref, carry_ref, hs_ref):
    C = GDN_CHUNK
    R = xq_ref.shape[0]

    @pl.when(pl.program_id(1) == 0)
    def _():
        carry_ref[...] = jnp.zeros_like(carry_ref)

    def conv_silu(x_ref, cw_ref, slot):
        x = x_ref[...]
        hs_ref[slot, 0:8, :] = carry_ref[slot]
        hs_ref[slot, 8:R + 8, :] = x
        carry_ref[slot] = x[R - 8:R, :]
        taps = cw_ref.shape[0]
        y = cw_ref[taps - 1:taps, :] * x
        for back in range(1, taps):
            y = y + cw_ref[taps - 1 - back:taps - back, :] * hs_ref[slot, 8 - back:8 - back + R, :]
        half_y = 0.5 * y
        return half_y * (1.0 + jnp.tanh(half_y))

    def l2norm(y):
        return y * lax.rsqrt(jnp.sum(y * y, axis=-1, keepdims=True) + EPS)

    q_all = l2norm(conv_silu(xq_ref, cwq_ref, 0)) * (GDN_DH ** -0.5)
    k_all = l2norm(conv_silu(xk_ref, cwk_ref, 1))
    v_all = conv_silu(xv_ref, cwv_ref, 2)
    chunks = range(GDN_CB)
    rows = [slice(c * C, (c + 1) * C) for c in chunks]
    ii = lax.broadcasted_iota(jnp.int32, (C, C), 0)
    jj = lax.broadcasted_iota(jnp.int32, (C, C), 1)
    causal = ii >= jj
    strict = ii > jj
    eye = (ii == jj).astype(F32)
    tril16 = causal.astype(BF16)
    row2 = lax.broadcasted_iota(jnp.int32, (C, 2 * C), 0)
    col2 = lax.broadcasted_iota(jnp.int32, (C, 2 * C), 1)
    keep = row2 > jnp.where(col2 < C, -1, col2 - C)

    head = pl.program_id(0)
    sm = sm_ref[...]
    sm_lane = lax.broadcasted_iota(jnp.int32, sm.shape, 1)
    g_col = jnp.sum(jnp.where(sm_lane == SM_G + head, sm, 0.0), axis=1, keepdims=True)
    b_col = jnp.sum(jnp.where(sm_lane == SM_BETA + head, sm, 0.0), axis=1, keepdims=True)

    k = [k_all[rs, :] for rs in rows]
    beta = [b_col[rs, :] for rs in rows]
    res = []
    for rs in rows:
        x = jnp.where(keep, jnp.broadcast_to(g_col[rs, :], (C, 2 * C)), 0.0)
        hi = x.astype(BF16)
        r1 = x - hi.astype(F32)
        mid = r1.astype(BF16)
        lo = (r1 - mid.astype(F32)).astype(BF16)
        res.append(_dot(tril16, hi) + (_dot(tril16, mid) + _dot(tril16, lo)))
    gc = [jnp.broadcast_to(r[:, 0:1], (C, GDN_DH)) for r in res]
    decay = [jnp.where(causal, jnp.exp(r[:, C:2 * C]), 0.0) for r in res]
    kb = [kc * bc for kc, bc in zip(k, beta)]
    k16 = [kc.astype(BF16) for kc in k]
    a = [jnp.where(strict, _dot_nt(kbc.astype(BF16), kc16) * dc, 0.0) for kbc, kc16, dc in zip(kb, k16, decay)]
    t = [eye - ac for ac in a]
    p = a
    for _ in range(int(math.log2(C)) - 1):
        p = [_dot_split(pc, pc) for pc in p]
        t = [tc + _dot_split(tc, pc) for tc, pc in zip(t, p)]
    t16 = [tc.astype(BF16) for tc in t]
    egc = [jnp.exp(gcc) for gcc in gc]
    val = [_dot(tc, (v_all[rs, :] * bc).astype(BF16)) for tc, rs, bc in zip(t16, rows, beta)]
    kcum = [_dot(tc, (kbc * ec).astype(BF16)) for tc, kbc, ec in zip(t16, kb, egc)]
    qk = [_dot_nt(q_all[rs, :].astype(BF16), kc16) * dc for rs, kc16, dc in zip(rows, k16, decay)]
    kd = []
    for c in chunks:
        rs = rows[c]
        val_ref[rs, :] = val[c]
        kcum_ref[rs, :] = kcum[c].astype(BF16)
        qk_ref[0, rs, :] = qk[c].astype(BF16)
        glast = gc[c][C - 1:C, :]
        qg_ref[rs, :] = (q_all[rs, :] * egc[c]).astype(BF16)
        kd.append(k[c] * jnp.exp(glast - gc[c]))
        gl_ref[0, c:c + 1, :] = jnp.exp(glast)
    for c in range(0, GDN_CB, 2):
        kdT_ref[0, c // 2] = jnp.concatenate([kd[c], kd[c + 1]], axis=0).T.astype(BF16)


def _gdn_local(proj, conv_w, small):
    L = proj.shape[0]
    H = GDN_HEADS
    R = GDN_CB * GDN_CHUNK
    cb = C_QKV // GDN_DH
    taps = conv_w.shape[0]
    col = lambda off: pl.BlockSpec((R, GDN_DH), lambda h, i: (i, cb + off + h))
    cw = lambda off: pl.BlockSpec((taps, GDN_DH), lambda h, i: (0, off + h))
    big = pl.BlockSpec((R, GDN_DH), lambda h, i: (i, h))
    return pl.pallas_call(
        _gdn_local_kernel,
        grid=(H, L // R),
        in_specs=[col(0), col(H), col(2 * H), cw(0), cw(H), cw(2 * H),
                  pl.BlockSpec((R, LANES), lambda h, i: (i, 0))],
        out_specs=[big, big, big,
                   pl.BlockSpec((1, GDN_CB // 2, GDN_DH, 2 * GDN_CHUNK), lambda h, i: (h, i, 0, 0)),
                   pl.BlockSpec((1, R, GDN_CHUNK), lambda h, i: (h, i, 0)),
                   pl.BlockSpec((1, GDN_CB, GDN_DH), lambda h, i: (h, i, 0))],
        out_shape=[jax.ShapeDtypeStruct((L, GDN_WIDTH), F32),
                   jax.ShapeDtypeStruct((L, GDN_WIDTH), BF16),
                   jax.ShapeDtypeStruct((L, GDN_WIDTH), BF16),
                   jax.ShapeDtypeStruct((H, L // (2 * GDN_CHUNK), GDN_DH, 2 * GDN_CHUNK), BF16),
                   jax.ShapeDtypeStruct((H, L, GDN_CHUNK), BF16),
                   jax.ShapeDtypeStruct((H, L // GDN_CHUNK, GDN_DH), F32)],
        scratch_shapes=[pltpu.VMEM((3, 8, GDN_DH), F32), pltpu.VMEM((3, R + 8, GDN_DH), F32)],
        compiler_params=_cparams(("arbitrary", "arbitrary")),
        name="gdn_local",
    )(proj, proj, proj, conv_w, conv_w, conv_w, small)


def _gdn_scan_kernel(val_ref, kcum_ref, qg_ref, qk_ref, kdT_ref, gl_ref, z_ref, gn_ref, o_ref, st_ref):
    @pl.when(pl.program_id(0) == 0)
    def _():
        st_ref[...] = jnp.zeros_like(st_ref)

    C = GDN_CHUNK
    gn = gn_ref[...]
    heads = range(GDN_HEADS)
    cols = [slice(h * GDN_DH, (h + 1) * GDN_DH) for h in heads]
    state = [st_ref[h] for h in heads]
    for c in range(GDN_CB):
        rs = slice(c * C, (c + 1) * C)
        sb = [s.astype(BF16) for s in state]
        kcs = [_dot(kcum_ref[rs, cs], s) for cs, s in zip(cols, sb)]
        qgs = [_dot(qg_ref[rs, cs], s) for cs, s in zip(cols, sb)]
        vb = [(val_ref[rs, cs] - x).astype(BF16) for cs, x in zip(cols, kcs)]
        o = [x + _dot(qk_ref[h, rs, :], v) for h, x, v in zip(heads, qgs, vb)]
        kcols = slice((c % 2) * C, (c % 2 + 1) * C)
        state = [s * gl_ref[h, c:c + 1, :] + _dot(kdT_ref[h, c // 2, :, kcols], v)
                 for h, s, v in zip(heads, state, vb)]
        for h in heads:
            on = o[h] * lax.rsqrt(jnp.mean(o[h] * o[h], axis=-1, keepdims=True) + EPS) * gn
            z = z_ref[rs, cols[h]]
            o_ref[rs, cols[h]] = on * (z * _sigmoid(z))
    for h in heads:
        st_ref[h] = state[h]


def _gdn_scan(val, kcum, qg, qk, kdT, gl, proj, gnorm):
    L = val.shape[0]
    H = GDN_HEADS
    R = GDN_CB * GDN_CHUNK
    row = pl.BlockSpec((R, GDN_WIDTH), lambda i: (i, 0))
    zb = C_Z // GDN_WIDTH
    return pl.pallas_call(
        _gdn_scan_kernel,
        grid=(L // R,),
        in_specs=[row, row, row,
                  pl.BlockSpec((H, R, GDN_CHUNK), lambda i: (0, i, 0)),
                  pl.BlockSpec((H, GDN_CB // 2, GDN_DH, 2 * GDN_CHUNK), lambda i: (0, i, 0, 0)),
                  pl.BlockSpec((H, GDN_CB, GDN_DH), lambda i: (0, i, 0)),
                  pl.BlockSpec((R, GDN_WIDTH), lambda i: (i, zb)),
                  pl.BlockSpec((1, GDN_DH), lambda i: (0, 0))],
        out_specs=row,
        out_shape=jax.ShapeDtypeStruct((L, GDN_WIDTH), F32),
        scratch_shapes=[pltpu.VMEM((H, GDN_DH, GDN_DH), F32)],
        compiler_params=_cparams(("arbitrary",)),
        name="gdn_scan",
    )(val, kcum, qg, qk, kdT, gl, proj, gnorm)


def _mix_kernel(ya_ref, ys_ref, u_ref, yc_ref, x_ref, ga_ref, gb_ref, d_ref, wglu_ref, wout_ref, o_ref):
    def rms(y, g):
        return y * lax.rsqrt(jnp.mean(y * y, axis=-1, keepdims=True) + EPS) * g

    a = rms(ya_ref[...], ga_ref[...]).astype(BF16)
    yb = _gelu(ys_ref[...] + d_ref[...] * u_ref[...])
    yb = yb * _sigmoid(_dot(yb.astype(BF16), wglu_ref[...]))
    b = rms(yb, gb_ref[...]).astype(BF16)
    na, nb = a.shape[1], b.shape[1]
    acc = x_ref[...] + _dot(a, wout_ref[0:na, :])
    acc = acc + _dot(b, wout_ref[na:na + nb, :])
    o_ref[...] = acc + _dot(yc_ref[...].astype(BF16), wout_ref[na + nb:, :])


def _mix(ya, ys, proj, yc, x, ga, gb, d, wglu, wout, tm=256):
    L, D = x.shape
    ub = C_U // S5_WIDTH
    full = lambda a: pl.BlockSpec(a.shape, lambda i: (0, 0))
    return pl.pallas_call(
        _mix_kernel,
        grid=(L // tm,),
        in_specs=[pl.BlockSpec((tm, ya.shape[1]), lambda i: (i, 0)),
                  pl.BlockSpec((tm, S5_WIDTH), lambda i: (i, 0)),
                  pl.BlockSpec((tm, S5_WIDTH), lambda i: (i, ub)),
                  pl.BlockSpec((tm, GDN_WIDTH), lambda i: (i, 0)),
                  pl.BlockSpec((tm, D), lambda i: (i, 0)),
                  full(ga), full(gb), full(d), full(wglu), full(wout)],
        out_specs=pl.BlockSpec((tm, D), lambda i: (i, 0)),
        out_shape=jax.ShapeDtypeStruct((L, D), F32),
        compiler_params=_cparams(("arbitrary",)),
        name="mix_out",
    )(ya, ys, proj, yc, x, ga, gb, d, wglu, wout)


def _ffn_in_kernel(x_ref, g_ref, wg_ref, wu_ref, cg_ref, cu_ref, bg_ref, bu_ref, o_ref, xn_ref, carry_ref, hs_ref,
                   *, n_valid):
    i = pl.program_id(0)
    j = pl.program_id(1)

    @pl.when(j == 0)
    def _():
        xf = x_ref[...]
        ms = jnp.mean(xf * xf, axis=-1, keepdims=True)
        xn_ref[...] = (xf * lax.rsqrt(ms + EPS) * g_ref[...]).astype(BF16)

    @pl.when(i == 0)
    def _():
        carry_ref[j] = jnp.zeros(carry_ref.shape[1:], F32)

    xn = xn_ref[...]
    tm = xn.shape[0]

    def conv(h, slot, cw_ref, cb_ref):
        hs_ref[slot, 0:8, :] = carry_ref[j, slot]
        hs_ref[slot, 8:tm + 8, :] = h
        carry_ref[j, slot] = h[tm - 8:tm, :]
        taps = cw_ref.shape[0]
        y = cw_ref[taps - 1:taps, :] * h + cb_ref[...]
        for back in range(1, taps):
            y = y + cw_ref[taps - 1 - back:taps - back, :] * hs_ref[slot, 8 - back:8 - back + tm, :]
        return y

    gate = conv(_dot(xn, wg_ref[...]), 0, cg_ref, bg_ref)
    up = conv(_dot(xn, wu_ref[...]), 1, cu_ref, bu_ref)
    half_gate = 0.5 * gate
    act = half_gate * (1.0 + jnp.tanh(half_gate)) * up
    o_ref[...] = act.astype(BF16)

    @pl.when(j == pl.num_programs(1) - 1)
    def _():
        col = j * act.shape[1] + lax.broadcasted_iota(jnp.int32, act.shape, 1)
        o_ref[...] = jnp.where(col < n_valid, act, 0.0).astype(BF16)


def _ffn_in(x, gain, wg, wu, cg, cu, bg, bu, tm=1024, tn=512):
    L, D = x.shape
    n_valid = wg.shape[1]
    nj = pl.cdiv(n_valid, tn)
    taps = cg.shape[0]
    cols = lambda rows: pl.BlockSpec((rows, tn), lambda i, j: (0, j))
    return pl.pallas_call(
        functools.partial(_ffn_in_kernel, n_valid=n_valid),
        grid=(L // tm, nj),
        in_specs=[pl.BlockSpec((tm, D), lambda i, j: (i, 0)),
                  pl.BlockSpec((1, D), lambda i, j: (0, 0)),
                  cols(D), cols(D), cols(taps), cols(taps), cols(1), cols(1)],
        out_specs=pl.BlockSpec((tm, tn), lambda i, j: (i, j)),
        out_shape=jax.ShapeDtypeStruct((L, nj * tn), BF16),
        scratch_shapes=[pltpu.VMEM((tm, D), BF16), pltpu.VMEM((nj, 2, 8, tn), F32),
                        pltpu.VMEM((2, tm + 8, tn), F32)],
        compiler_params=_cparams(("arbitrary", "arbitrary")),
        name="ffn_in",
    )(x, gain.reshape(1, D), wg, wu, cg, cu, bg, bu)


def _ffn_out_kernel(a_ref, w_ref, x_ref, o_ref):
    @pl.when(pl.program_id(1) == 0)
    def _():
        o_ref[...] = x_ref[...]

    o_ref[...] += _dot(a_ref[...], w_ref[...])


def _ffn_out(act, w, x, tm=1024, tk=512):
    L, D = x.shape
    K = act.shape[1]
    return pl.pallas_call(
        _ffn_out_kernel,
        grid=(L // tm, K // tk),
        in_specs=[pl.BlockSpec((tm, tk), lambda i, k: (i, k)),
                  pl.BlockSpec((tk, D), lambda i, k: (k, 0)),
                  pl.BlockSpec((tm, D), lambda i, k: (i, 0))],
        out_specs=pl.BlockSpec((tm, D), lambda i, k: (i, 0)),
        out_shape=jax.ShapeDtypeStruct((L, D), F32),
        compiler_params=_cparams(("arbitrary", "arbitrary")),
        name="ffn_out",
    )(act, w, x)


def _rope_tables(pos):
    half = ROPE_DIMS // 2
    inv = ROPE_THETA ** (-jnp.arange(half, dtype=F32) / half)
    ang = pos.astype(F32)[:, None] * inv[None, :]
    n = pos.shape[0]
    cos = jnp.concatenate([jnp.cos(ang), jnp.cos(ang), jnp.ones((n, NSA_DH - ROPE_DIMS), F32)], axis=1)
    sin = jnp.concatenate([-jnp.sin(ang), jnp.sin(ang), jnp.zeros((n, NSA_DH - ROPE_DIMS), F32)], axis=1)
    return jnp.tile(cos, (1, 2)), jnp.tile(sin, (1, 2))


def _permute_w_in(w):
    sp = np.cumsum([0, 512, 128, 128, 128, 128, 128, 128, 24, 512, 3072, 1024, 8, 8])
    q, kc, vc, ks, vs, kw, vw, gates, u, qkv, z, a, b = [w[:, sp[n]:sp[n + 1]] for n in range(13)]
    pad = jnp.zeros((w.shape[0], PROJ_COLS - C_SM - SM_END), w.dtype)
    return jnp.concatenate([q, ks, kw, vs, vw, z, qkv, u, kc, vc, gates, a, b, pad], axis=1)


def _nsa_mixer(proj, small, p, l):
    L = proj.shape[0]
    HK, G, DH = NSA_KV, NSA_G, NSA_DH
    ns = L // SLC_BLOCK
    nb = L // CMP_STRIDE
    n_grp = LANES - DH
    assert ns % n_grp == 0 and (n_grp * SLC_BLOCK) % (SEL_UNROLL * TK) == 0 and SEL_UNROLL % 2 == 0
    cos, sin = _rope_tables(jnp.arange(L))
    gains = jnp.concatenate([jnp.tile(p['nsa_q_norm'][l], NSA_HEADS), jnp.tile(p['nsa_ks_norm'][l], HK),
                             jnp.tile(p['nsa_kw_norm'][l], HK), jnp.ones((2 * HK * DH,), F32)]).reshape(1, -1)
    qT, ksaug, kwa, vsx, vwx = _nsa_prep(proj, gains, cos, sin, n_grp)
    kpad = jnp.zeros((HK, WINDOW, KW_LANES), BF16).at[:, :, DH].set(1.0)
    kwp = jnp.concatenate([kpad, kwa], axis=1)
    vwxp = jnp.pad(vwx, ((0, 0), (0, 0), (WINDOW, 0)))

    eye = jnp.eye(HK, dtype=F32)
    pe = jnp.tile(p['cmp_pe'][l][:, None, :], (1, HK, 1)).reshape(1, CMP_LEN * HK * DH)
    w1 = lambda name: jnp.einsum('ldf,hg->lhdgf', p[name][l], eye).reshape(CMP_LEN * HK * DH, -1).astype(BF16)
    w2 = lambda name: jnp.einsum('fd,hg->hfgd', p[name][l], eye).reshape(-1, HK * DH).astype(BF16)
    ccos, csin = _rope_tables(jnp.arange(nb) * CMP_STRIDE + CMP_LEN // 2)
    gk = jnp.tile(p['nsa_kc_norm'][l], HK).reshape(1, LANES)
    kc, vcx = _nsa_compress(proj, pe, w1('cmp_k_w1'), w2('cmp_k_w2'), w1('cmp_v_w1'), w2('cmp_v_w2'), gk, ccos, csin)

    cch = min(CMP_CHUNK, nb)
    cmp_start = np.arange(cch) * CMP_STRIDE
    slc_start = np.arange(cch * CMP_STRIDE // SLC_BLOCK + 8) * SLC_BLOCK
    smapT = jnp.asarray((cmp_start[None, :] <= slc_start[:, None] + SLC_BLOCK - 1)
                        & (cmp_start[None, :] + CMP_LEN - 1 >= slc_start[:, None]), BF16)

    gT = jnp.pad(small[:, :HK * G * 3].reshape(L, HK, G * 3).transpose(1, 2, 0), ((0, 0), (0, 16 - G * 3), (0, 0)))
    return _nsa_attention(qT, kc, vcx, smapT, ksaug, vsx, kwp, vwxp, gT, n_grp)


def _s5_mixer(proj, p, l):
    L = proj.shape[0]
    T = S5_CHUNK
    nc = L // T
    G, H, P = S5_GROUPS, S5_CH, S5_STATE
    lam_re, lam_im = p['s5_lam_re'][l], p['s5_lam_im'][l]
    dt = jnp.exp(p['s5_log_dt'][l])[:, None]
    mag = jnp.exp(lam_re * dt)
    lb_re = mag * jnp.cos(lam_im * dt)
    lb_im = mag * jnp.sin(lam_im * dt)
    den = lam_re * lam_re + lam_im * lam_im
    nr, ni = lb_re - 1.0, lb_im
    coef_re = (nr * lam_re + ni * lam_im) / den
    coef_im = (ni * lam_re - nr * lam_im) / den
    b_re, b_im = p['s5_b_re'][l], p['s5_b_im'][l]
    bb_re = coef_re[..., None] * b_re - coef_im[..., None] * b_im
    bb_im = coef_re[..., None] * b_im + coef_im[..., None] * b_re
    bsr, bsi = bb_re.transpose(0, 2, 1), bb_im.transpose(0, 2, 1)
    return _s5_scan(proj, lb_re[:, None, :], lb_im[:, None, :], p['s5_c_re'][l], p['s5_c_im'][l], bsr, bsi)


def _gdn_mixer(proj, small, p, l):
    val, kcum, qg, kdT, qk, gl = _gdn_local(proj, p['gdn_conv'][l], small)
    return _gdn_scan(val, kcum, qg, qk, kdT, gl, proj, p['gdn_norm'][l].reshape(1, GDN_DH))


def _forward(x3, p):
    x = x3.reshape(x3.shape[1:])
    depth = p['w_in'].shape[0]
    for l in range(depth):
        w_in = _permute_w_in(p['w_in'][l].astype(BF16))
        proj = _rms_matmul(x, p['attn_norm'][l], w_in)
        sm_par = jnp.zeros((8, LANES), F32)
        sm_par = sm_par.at[0, SM_G:SM_BETA].set(p['gdn_dt_bias'][l]).at[1, SM_G:SM_BETA].set(-jnp.exp(p['gdn_a_log'][l]))
        small = _small(proj, sm_par)
        y_a = _nsa_mixer(proj, small, p, l)
        y_s = _s5_mixer(proj, p, l)
        y_c = _gdn_mixer(proj, small, p, l)
        x = _mix(y_a, y_s, proj, y_c, x, p['nsa_out_norm'][l].reshape(1, -1), p['s5_out_norm'][l].reshape(1, -1),
                 p['s5_d'][l].reshape(1, -1), p['s5_w_glu'][l].astype(BF16), p['w_out'][l].astype(BF16))
        wf, cf, bf = p['ffn_w_in'][l], p['ffn_conv'][l], p['ffn_conv_b'][l].reshape(1, -1)
        act = _ffn_in(x, p['ffn_norm'][l], wf[:, :D_FF].astype(BF16), wf[:, D_FF:].astype(BF16),
                      cf[:, :D_FF], cf[:, D_FF:], bf[:, :D_FF], bf[:, D_FF:])
        w_o = jnp.pad(p['ffn_w_out'][l].astype(BF16), ((0, D_FF_PAD - D_FF), (0, 0)))
        x = _ffn_out(act, w_o, x)
    return x.reshape(x3.shape)


_PARAM_NAMES = ('attn_norm', 'w_in', 'nsa_q_norm', 'nsa_kc_norm', 'nsa_ks_norm', 'nsa_kw_norm', 'cmp_pe',
                'cmp_k_w1', 'cmp_k_w2', 'cmp_v_w1', 'cmp_v_w2', 'nsa_out_norm', 's5_lam_re', 's5_lam_im',
                's5_log_dt', 's5_b_re', 's5_b_im', 's5_c_re', 's5_c_im', 's5_d', 's5_w_glu', 's5_out_norm',
                'gdn_conv', 'gdn_a_log', 'gdn_dt_bias', 'gdn_norm', 'w_out', 'ffn_norm', 'ffn_w_in', 'ffn_conv',
                'ffn_conv_b', 'ffn_w_out')


def kernel(x, attn_norm, w_in, nsa_q_norm, nsa_kc_norm, nsa_ks_norm, nsa_kw_norm, cmp_pe, cmp_k_w1, cmp_k_w2,
           cmp_v_w1, cmp_v_w2, nsa_out_norm, s5_lam_re, s5_lam_im, s5_log_dt, s5_b_re, s5_b_im, s5_c_re, s5_c_im,
           s5_d, s5_w_glu, s5_out_norm, gdn_conv, gdn_a_log, gdn_dt_bias, gdn_norm, w_out, ffn_norm, ffn_w_in,
           ffn_conv, ffn_conv_b, ffn_w_out):
    vals = (attn_norm, w_in, nsa_q_norm, nsa_kc_norm, nsa_ks_norm, nsa_kw_norm, cmp_pe, cmp_k_w1, cmp_k_w2,
            cmp_v_w1, cmp_v_w2, nsa_out_norm, s5_lam_re, s5_lam_im, s5_log_dt, s5_b_re, s5_b_im, s5_c_re, s5_c_im,
            s5_d, s5_w_glu, s5_out_norm, gdn_conv, gdn_a_log, gdn_dt_bias, gdn_norm, w_out, ffn_norm, ffn_w_in,
            ffn_conv, ffn_conv_b, ffn_w_out)
    return _forward(x, dict(zip(_PARAM_NAMES, vals)))
```
